```python
import jax, jax.numpy as jnp
from jax import lax
import numpy as np


D_MODEL = 1024
BATCH = 8
SEQ = 2048
DEPTH = 1
DEC_BATCH = 128
DEC_SEQ = 1
PAST_LEN = 16384
PAGE_SIZE = 128

MAMBA_D_INNER = 2 * D_MODEL
MAMBA_HEAD_DIM = 64
MAMBA_N_HEADS = MAMBA_D_INNER // MAMBA_HEAD_DIM
MAMBA_N_GROUPS = 8
MAMBA_HEADS_PER_GROUP = MAMBA_N_HEADS // MAMBA_N_GROUPS
MAMBA_D_STATE = 128
MAMBA_CONV_DIM = MAMBA_D_INNER + 2 * MAMBA_N_GROUPS * MAMBA_D_STATE
CONV_WIDTH = 4
MLSTM_D_INNER = D_MODEL
MLSTM_N_HEADS = 8
MLSTM_HEAD_DIM = MLSTM_D_INNER // MLSTM_N_HEADS
D_FF = 4 * D_MODEL
CHUNK = 128
EPS = 1e-6
IN_WIDTH = (MAMBA_D_INNER + MAMBA_CONV_DIM + MAMBA_N_HEADS + 3 * MLSTM_D_INNER
            + 2 * MLSTM_N_HEADS + 2 * D_MODEL)

kernel_name = 'hybrid_ssd_mlstm_gated_decoder_step'


def _split_points():
    sizes = (MAMBA_D_INNER, MAMBA_CONV_DIM, MAMBA_N_HEADS, MLSTM_D_INNER, MLSTM_D_INNER,
             MLSTM_D_INNER, MLSTM_N_HEADS, MLSTM_N_HEADS, D_MODEL, D_MODEL)
    pts, acc = [], 0
    for s in sizes[:-1]:
        acc += s
        pts.append(acc)
    return pts


def rmsnorm(x, w):
    xf = x.astype(jnp.float32)
    y = xf * lax.rsqrt(jnp.mean(xf * xf, axis=-1, keepdims=True) + EPS)
    return (y * w.astype(jnp.float32)).astype(x.dtype)


def causal_conv(x, buf, w, b):
    T = x.shape[1]
    xp = jnp.concatenate([buf.astype(x.dtype), x], axis=1)
    y = b
    for j in range(CONV_WIDTH):
        y = y + xp[:, j:j + T] * w[j]
    return y, xp[:, -(CONV_WIDTH - 1):]


def _chunk_setup(T):
    L = min(CHUNK, T)
    nc = -(-T // L)
    return L, nc * L


def _pad_t(a, Tp, value=0.0):
    pad = Tp - a.shape[1]
    return jnp.pad(a, [(0, 0), (0, pad)] + [(0, 0)] * (a.ndim - 2), constant_values=value)


def _to_chunks(a, L):
    return jnp.moveaxis(a.reshape((a.shape[0], a.shape[1] // L, L) + a.shape[2:]), 1, 0)


def _from_chunks(a):
    a = jnp.moveaxis(a, 0, 1)
    return a.reshape((a.shape[0], a.shape[1] * a.shape[2]) + a.shape[3:])


def ssd_scan(x, dt, A, Bm, Cm, h0):
    f32 = jnp.float32
    Bsz, T = x.shape[:2]
    G, R, P, N = MAMBA_N_GROUPS, MAMBA_HEADS_PER_GROUP, MAMBA_HEAD_DIM, MAMBA_D_STATE
    L, Tp = _chunk_setup(T)
    xs = _to_chunks(_pad_t(x.astype(f32), Tp).reshape(Bsz, Tp, G, R, P), L)
    dts = _to_chunks(_pad_t(dt.astype(f32), Tp).reshape(Bsz, Tp, G, R), L)
    Bs = _to_chunks(_pad_t(Bm.astype(f32), Tp), L)
    Cs = _to_chunks(_pad_t(Cm.astype(f32), Tp), L)
    A_gj = A.astype(f32).reshape(G, R)
    causal = jnp.tril(jnp.ones((L, L), dtype=bool))

    def step(h, inp):
        xc, dtc, Bc, Cc = inp
        acum = jnp.cumsum(dtc * A_gj, axis=1)
        seg = acum[:, :, None] - acum[:, None, :]
        decay = jnp.exp(jnp.where(causal[None, :, :, None, None], seg, -jnp.inf))
        cb = jnp.einsum('bsgn,brgn->bsrg', Cc, Bc)
        wgt = cb[..., None] * decay * dtc[:, None]
        y = jnp.einsum('bsrgj,brgjp->bsgjp', wgt, xc)
        y = y + jnp.exp(acum)[..., None] * jnp.einsum('bsgn,bgjpn->bsgjp', Cc, h)
        last = acum[:, -1]
        wr = jnp.exp(last[:, None] - acum) * dtc
        h = jnp.exp(last)[..., None, None] * h + jnp.einsum('brgj,brgjp,brgn->bgjpn', wr, xc, Bc)
        return h, y

    h, ys = lax.scan(step, h0.astype(f32).reshape(Bsz, G, R, P, N), (xs, dts, Bs, Cs))
    y = _from_chunks(ys)[:, :T].reshape(Bsz, T, MAMBA_N_HEADS, P)
    return y, h.reshape(Bsz, MAMBA_N_HEADS, P, N)


def mlstm_scan(q, k, v, ig, logf, C0, n0, m0):
    f32 = jnp.float32
    Bsz, T = q.shape[:2]
    L, Tp = _chunk_setup(T)
    qs = _to_chunks(_pad_t(q.astype(f32), Tp), L)
    ks = _to_chunks(_pad_t(k.astype(f32), Tp), L)
    vs = _to_chunks(_pad_t(v.astype(f32), Tp), L)
    igs = _to_chunks(_pad_t(ig.astype(f32), Tp, -1e30), L)
    lfs = _to_chunks(_pad_t(logf.astype(f32), Tp), L)
    causal = jnp.tril(jnp.ones((L, L), dtype=bool))

    def step(carry, inp):
        C, n, m = carry
        qc, kc, vc, ic, fc = inp
        bcum = jnp.cumsum(fc, axis=1)
        logD = bcum[:, :, None] - bcum[:, None, :] + ic[:, None, :]
        logD = jnp.where(causal[None, :, :, None], logD, -jnp.inf)
        inter = bcum + m[:, None]
        m_s = jnp.maximum(inter, jnp.max(logD, axis=2))
        Dm = jnp.exp(logD - m_s[:, :, None])
        sc = jnp.exp(inter - m_s)
        wgt = jnp.einsum('bshd,brhd->bsrh', qc, kc) * Dm
        num = jnp.einsum('bsrh,brhe->bshe', wgt, vc) + sc[..., None] * jnp.einsum('bshd,bhde->bshe', qc, C)
        den = jnp.sum(wgt, axis=2) + sc * jnp.einsum('bshd,bhd->bsh', qc, n)
        h = num / jnp.maximum(jnp.abs(den), jnp.exp(-m_s))[..., None]
        bl = bcum[:, -1]
        src = bl[:, None] - bcum + ic
        m_new = jnp.maximum(bl + m, jnp.max(src, axis=1))
        wr = jnp.exp(src - m_new[:, None])
        keep = jnp.exp(bl + m - m_new)
        C = keep[..., None, None] * C + jnp.einsum('brh,brhd,brhe->bhde', wr, kc, vc)
        n = keep[..., None] * n + jnp.einsum('brh,brhd->bhd', wr, kc)
        return (C, n, m_new), h

    (C, n, m), hs = lax.scan(step, (C0.astype(f32), n0.astype(f32), m0.astype(f32)), (qs, ks, vs, igs, lfs))
    return _from_chunks(hs)[:, :T], C, n, m


def mixer(xn, st, p):
    (w_in, m_cw, m_cb, dt_bias, A_log, Dskip, m_nw, w_a,
     l_cw, l_cb, wq, wk, i_b, f_b, l_nw, w_b, w_o) = p
    conv_m, ssm0, conv_l, C0, n0, m0 = st
    Bsz, T, _ = xn.shape
    dtype = xn.dtype
    f32 = jnp.float32
    proj = xn @ w_in
    z, xbc, dt_raw, u, v, o_pre, i_pre, f_pre, g_a, g_b = jnp.split(proj, _split_points(), axis=-1)

    xbc, conv_m_new = causal_conv(xbc, conv_m, m_cw, m_cb)
    xbc = jax.nn.silu(xbc)
    xm, Bm, Cm = jnp.split(xbc, [MAMBA_D_INNER, MAMBA_D_INNER + MAMBA_N_GROUPS * MAMBA_D_STATE], axis=-1)
    xm = xm.reshape(Bsz, T, MAMBA_N_HEADS, MAMBA_HEAD_DIM)
    Bm = Bm.reshape(Bsz, T, MAMBA_N_GROUPS, MAMBA_D_STATE)
    Cm = Cm.reshape(Bsz, T, MAMBA_N_GROUPS, MAMBA_D_STATE)
    dt = jax.nn.softplus(dt_raw.astype(f32) + dt_bias.astype(f32))
    A = -jnp.exp(A_log.astype(f32))
    ya, ssm_new = ssd_scan(xm, dt, A, Bm, Cm, ssm0)
    ya = (ya + Dskip.astype(f32)[:, None] * xm.astype(f32)).astype(dtype)
    ya = ya.reshape(Bsz, T, MAMBA_D_INNER) * jax.nn.silu(z)
    ya = rmsnorm(ya.reshape(Bsz, T, MAMBA_N_GROUPS, -1), m_nw.reshape(MAMBA_N_GROUPS, -1))
    y_a = ya.reshape(Bsz, T, MAMBA_D_INNER) @ w_a

    uc, conv_l_new = causal_conv(u, conv_l, l_cw, l_cb)
    uc = jax.nn.silu(uc).reshape(Bsz, T, MLSTM_N_HEADS, MLSTM_HEAD_DIM)
    q = jnp.einsum('bthd,hde->bthe', uc, wq)
    k = jnp.einsum('bthd,hde->bthe', uc, wk) * (MLSTM_HEAD_DIM ** -0.5)
    vh = v.reshape(Bsz, T, MLSTM_N_HEADS, MLSTM_HEAD_DIM)
    ig = i_pre.astype(f32) + i_b.astype(f32)
    logf = jax.nn.log_sigmoid(f_pre.astype(f32) + f_b.astype(f32))
    hb, C_new, n_new, m_new = mlstm_scan(q, k, vh, ig, logf, C0, n0, m0)
    hb = rmsnorm(hb.astype(dtype), l_nw.reshape(MLSTM_N_HEADS, MLSTM_HEAD_DIM))
    hb = jax.nn.sigmoid(o_pre) * hb.reshape(Bsz, T, MLSTM_D_INNER)
    y_b = hb @ w_b

    out = (jax.nn.sigmoid(g_a) * y_a + jax.nn.sigmoid(g_b) * y_b) @ w_o
    new_st = (conv_m_new.astype(dtype), ssm_new.astype(dtype), conv_l_new.astype(dtype),
              C_new.astype(dtype), n_new.astype(dtype), m_new.astype(dtype))
    return out, new_st


def block(x, st, p, norm_mix_w, norm_mlp_w, w_up, w_down):
    mix, new_st = mixer(rmsnorm(x, norm_mix_w), st, p)
    x = x + mix
    h = rmsnorm(x, norm_mlp_w)
    x = x + jnp.square(jax.nn.relu(h @ w_up)) @ w_down
    return x, new_st


def setup_inputs(seed: int = 0) -> dict:
    key = jax.random.key(seed)
    ks = jax.random.split(key, 40)
    f32 = jnp.float32

    def nrm(k, shape, scale):
        return jax.random.normal(k, shape, f32) * scale

    Dd = DEPTH
    dt0 = jnp.exp(jax.random.uniform(ks[12], (Dd, MAMBA_N_HEADS), f32, np.log(1e-3), np.log(1e-1)))
    return {
        'x_prompt': nrm(ks[0], (BATCH, SEQ, D_MODEL), 1.0),
        'x_sample': nrm(ks[1], (DEC_BATCH, DEC_SEQ, D_MODEL), 1.0),
        'state_mamba_conv': nrm(ks[2], (Dd, DEC_BATCH, CONV_WIDTH - 1, MAMBA_CONV_DIM), 1.0),
        'state_mamba_ssm': nrm(ks[3], (Dd, DEC_BATCH, MAMBA_N_HEADS, MAMBA_HEAD_DIM, MAMBA_D_STATE), 0.1),
        'state_mlstm_conv': nrm(ks[4], (Dd, DEC_BATCH, CONV_WIDTH - 1, MLSTM_D_INNER), 1.0),
        'state_mlstm_C': nrm(ks[5], (Dd, DEC_BATCH, MLSTM_N_HEADS, MLSTM_HEAD_DIM, MLSTM_HEAD_DIM), 0.1),
        'state_mlstm_n': nrm(ks[6], (Dd, DEC_BATCH, MLSTM_N_HEADS, MLSTM_HEAD_DIM), 0.1),
        'state_mlstm_m': nrm(ks[7], (Dd, DEC_BATCH, MLSTM_N_HEADS), 1.0),
        'w_in': nrm(ks[8], (Dd, D_MODEL, IN_WIDTH), D_MODEL ** -0.5),
        'mamba_conv_w': nrm(ks[9], (Dd, CONV_WIDTH, MAMBA_CONV_DIM), 0.5),
        'mamba_conv_b': nrm(ks[10], (Dd, MAMBA_CONV_DIM), 0.02),
        'mamba_dt_bias': dt0 + jnp.log(-jnp.expm1(-dt0)),
        'mamba_A_log': jnp.log(jax.random.uniform(ks[11], (Dd, MAMBA_N_HEADS), f32, 1.0, 16.0)),
        'mamba_D': 1.0 + nrm(ks[13], (Dd, MAMBA_N_HEADS), 0.02),
        'mamba_norm_w': 1.0 + nrm(ks[14], (Dd, MAMBA_D_INNER), 0.02),
        'w_branch_a': nrm(ks[15], (Dd, MAMBA_D_INNER, D_MODEL), MAMBA_D_INNER ** -0.5),
        'mlstm_conv_w': nrm(ks[16], (Dd, CONV_WIDTH, MLSTM_D_INNER), 0.5),
        'mlstm_conv_b': nrm(ks[17], (Dd, MLSTM_D_INNER), 0.02),
        'mlstm_wq': nrm(ks[18], (Dd, MLSTM_N_HEADS, MLSTM_HEAD_DIM, MLSTM_HEAD_DIM), MLSTM_HEAD_DIM ** -0.5),
        'mlstm_wk': nrm(ks[19], (Dd, MLSTM_N_HEADS, MLSTM_HEAD_DIM, MLSTM_HEAD_DIM), MLSTM_HEAD_DIM ** -0.5),
        'mlstm_i_bias': nrm(ks[20], (Dd, MLSTM_N_HEADS), 0.1),
        'mlstm_f_bias': 3.0 + jax.random.uniform(ks[21], (Dd, MLSTM_N_HEADS), f32, 0.0, 3.0),
        'mlstm_norm_w': 1.0 + nrm(ks[22], (Dd, MLSTM_D_INNER), 0.02),
        'w_branch_b': nrm(ks[23], (Dd, MLSTM_D_INNER, D_MODEL), MLSTM_D_INNER ** -0.5),
        'w_out': nrm(ks[24], (Dd, D_MODEL, D_MODEL), D_MODEL ** -0.5),
        'norm_mix_w': 1.0 + nrm(ks[25], (Dd, D_MODEL), 0.02),
        'norm_mlp_w': 1.0 + nrm(ks[26], (Dd, D_MODEL), 0.02),
        'w_up': nrm(ks[27], (Dd, D_MODEL, D_FF), D_MODEL ** -0.5),
        'w_down': nrm(ks[28], (Dd, D_FF, D_MODEL), D_FF ** -0.5),
        'final_norm_w': 1.0 + nrm(ks[29], (D_MODEL,), 0.02),
    }


def reference(x_prompt, x_sample, state_mamba_conv, state_mamba_ssm, state_mlstm_conv,
              state_mlstm_C, state_mlstm_n, state_mlstm_m, w_in, mamba_conv_w, mamba_conv_b,
              mamba_dt_bias, mamba_A_log, mamba_D, mamba_norm_w, w_branch_a, mlstm_conv_w,
              mlstm_conv_b, mlstm_wq, mlstm_wk, mlstm_i_bias, mlstm_f_bias, mlstm_norm_w,
              w_branch_b, w_out, norm_mix_w, norm_mlp_w, w_up, w_down, final_norm_w):
    dtype = x_prompt.dtype
    Bp = x_prompt.shape[0]
    zero_state = (
        jnp.zeros((Bp, CONV_WIDTH - 1, MAMBA_CONV_DIM), dtype),
        jnp.zeros((Bp, MAMBA_N_HEADS, MAMBA_HEAD_DIM, MAMBA_D_STATE), dtype),
        jnp.zeros((Bp, CONV_WIDTH - 1, MLSTM_D_INNER), dtype),
        jnp.zeros((Bp, MLSTM_N_HEADS, MLSTM_HEAD_DIM, MLSTM_HEAD_DIM), dtype),
        jnp.zeros((Bp, MLSTM_N_HEADS, MLSTM_HEAD_DIM), dtype),
        jnp.zeros((Bp, MLSTM_N_HEADS), dtype),
    )
    xp, xs = x_prompt, x_sample
    p_states, s_states = [], []
    for l in range(DEPTH):
        p = (w_in[l], mamba_conv_w[l], mamba_conv_b[l], mamba_dt_bias[l], mamba_A_log[l], mamba_D[l],
             mamba_norm_w[l], w_branch_a[l], mlstm_conv_w[l], mlstm_conv_b[l], mlstm_wq[l], mlstm_wk[l],
             mlstm_i_bias[l], mlstm_f_bias[l], mlstm_norm_w[l], w_branch_b[l], w_out[l])
        s_in = (state_mamba_conv[l], state_mamba_ssm[l], state_mlstm_conv[l],
                state_mlstm_C[l], state_mlstm_n[l], state_mlstm_m[l])
        xp, st_p = block(xp, zero_state, p, norm_mix_w[l], norm_mlp_w[l], w_up[l], w_down[l])
        xs, st_s = block(xs, s_in, p, norm_mix_w[l], norm_mlp_w[l], w_up[l], w_down[l])
        p_states.append(st_p)
        s_states.append(st_s)
    y_prompt = rmsnorm(xp, final_norm_w)
    y_sample = rmsnorm(xs, final_norm_w)
    ps = [jnp.stack(z, axis=0) for z in zip(*p_states)]
    ss = [jnp.stack(z, axis=0) for z in zip(*s_states)]
    return (y_prompt, y_sample, ps[0], ps[1], ps[2], ps[3], ps[4], ps[5],
            ss[0], ss[1], ss[2], ss[3], ss[4], ss[5])
```

```python
import functools

import jax
import jax.numpy as jnp
from jax import lax
from jax.experimental import pallas as pl
from jax.experimental.pallas import tpu as pltpu

F32 = jnp.float32
BF16 = jnp.bfloat16

D_MODEL = 1024
M_INNER = 2048
M_HEADS = 32
M_HDIM = 64
M_GROUPS = 8
M_HPG = 4
M_STATE = 128
M_CONV = 4096
L_INNER = 1024
L_HEADS = 8
L_HDIM = 128
D_FF = 4096
CONV_W = 4
CHUNK = 128
EPS = 1e-6

LANES = 128
SUBLANES = 8
PROJ_BLOCK = 1024
N_PROJ_BLOCKS = 11
BIG_WIDTH = PROJ_BLOCK * N_PROJ_BLOCKS
DT_LANE = 0
I_LANE = 32
F_LANE = 40
VMEM_LIMIT = 56 * 1024 * 1024

HIGHEST = lax.Precision.HIGHEST
NT_DIMS = (((1,), (1,)), ((), ()))
TN_DIMS = (((0,), (0,)), ((), ()))


def _params(*sem):
    return pltpu.CompilerParams(dimension_semantics=sem, vmem_limit_bytes=VMEM_LIMIT)


def _silu(x):
    return x * jax.nn.sigmoid(x)


def _softplus(x):
    return jnp.maximum(x, 0.0) + jnp.log1p(jnp.exp(-jnp.abs(x)))


def _rms(x, w):
    return x * lax.rsqrt(jnp.mean(x * x, axis=-1, keepdims=True) + EPS) * w


def _lane_iota(shape):
    return lax.broadcasted_iota(jnp.int32, shape, len(shape) - 1)


def _tri(n):
    r = lax.broadcasted_iota(jnp.int32, (n, n), 0)
    c = lax.broadcasted_iota(jnp.int32, (n, n), 1)
    return r >= c


def _cumsum_rows(tri_f32, a):
    return jnp.dot(tri_f32, a, precision=HIGHEST, preferred_element_type=F32)


def _conv_chunk(x, tail, w_ref, b_ref):
    n = x.shape[0]
    row8 = lax.broadcasted_iota(jnp.int32, tail.shape, 0)
    y = b_ref[...] + w_ref[CONV_W - 1:CONV_W, :] * x
    for back in range(1, CONV_W):
        r = pltpu.roll(x, back, axis=0)
        rt = pltpu.roll(tail, back, axis=0)
        first = jnp.where(row8 < back, rt, r[0:SUBLANES])
        r = jnp.concatenate([first, r[SUBLANES:n]], axis=0)
        y = y + w_ref[CONV_W - 1 - back:CONV_W - back, :] * r
    return y


def _inproj_kernel(x_ref, g_ref, w_ref, ws_ref, o_ref, os_ref, xn_ref):
    @pl.when(pl.program_id(1) == 0)
    def _():
        xn_ref[...] = _rms(x_ref[...], g_ref[...]).astype(BF16)
        os_ref[...] = jnp.dot(xn_ref[...], ws_ref[...], preferred_element_type=F32)

    o_ref[...] = jnp.dot(xn_ref[...], w_ref[...], preferred_element_type=F32)


def _inproj(x, g, w_big, w_small, tm):
    m = x.shape[0]
    tm = min(tm, m)
    return pl.pallas_call(
        _inproj_kernel,
        grid=(m // tm, N_PROJ_BLOCKS),
        in_specs=[
            pl.BlockSpec((tm, D_MODEL), lambda i, j: (i, 0)),
            pl.BlockSpec((1, D_MODEL), lambda i, j: (0, 0)),
            pl.BlockSpec((D_MODEL, PROJ_BLOCK), lambda i, j: (0, j)),
            pl.BlockSpec((D_MODEL, LANES), lambda i, j: (0, 0)),
        ],
        out_specs=[
            pl.BlockSpec((tm, PROJ_BLOCK), lambda i, j: (i, j)),
            pl.BlockSpec((tm, LANES), lambda i, j: (i, 0)),
        ],
        out_shape=[
            jax.ShapeDtypeStruct((m, BIG_WIDTH), F32),
            jax.ShapeDtypeStruct((m, LANES), F32),
        ],
        scratch_shapes=[pltpu.VMEM((tm, D_MODEL), BF16)],
        compiler_params=_params("parallel", "arbitrary"),
        name="in_proj",
    )(x, g, w_big, w_small)


def _ssd_kernel(z_ref, x_ref, bc_ref, sm_ref, cwx_ref, cbx_ref, cwbc_ref, cbbc_ref,
                dtb_ref, alog_ref, dvec_ref, nw_ref,
                ya_ref, h_ref, tailx_ref, tailbc_ref, y_scr):
    n = x_ref.shape[0]

    @pl.when(pl.program_id(1) == 0)
    def _():
        h_ref[...] = jnp.zeros_like(h_ref)
        tailx_ref[...] = jnp.zeros_like(tailx_ref)
        tailbc_ref[...] = jnp.zeros_like(tailbc_ref)

    xraw = x_ref[...]
    bcraw = bc_ref[...]
    xc = _silu(_conv_chunk(xraw, tailx_ref[...], cwx_ref, cbx_ref))
    bcc = _silu(_conv_chunk(bcraw, tailbc_ref[...], cwbc_ref, cbbc_ref))
    tailx_ref[...] = xraw[n - SUBLANES:n]
    tailbc_ref[...] = bcraw[n - SUBLANES:n]

    lane = _lane_iota((n, LANES))
    dt = _softplus(sm_ref[...] + dtb_ref[...])
    da = jnp.where(lane < M_HEADS, dt * (-jnp.exp(alog_ref[...])), 0.0)
    causal = _tri(n)
    acum = _cumsum_rows(causal.astype(F32), da)
    acum_t = acum.T
    dt_t = dt.T
    exp_acum = jnp.exp(acum)
    last = acum[n - 1:n, :]
    wr = jnp.exp(last - acum) * dt
    exp_last = jnp.exp(last)

    for g in range(M_GROUPS):
        bg = bcc[:, g * M_STATE:(g + 1) * M_STATE].astype(BF16)
        cg = bcc[:, (M_GROUPS + g) * M_STATE:(M_GROUPS + g + 1) * M_STATE].astype(BF16)
        cb = lax.dot_general(cg, bg, NT_DIMS, preferred_element_type=F32)
        for j in range(M_HPG):
            h = g * M_HPG + j
            seg = acum[:, h:h + 1] - acum_t[h:h + 1, :]
            decay = jnp.exp(jnp.where(causal, seg, -jnp.inf))
            wgt = (cb * decay * dt_t[h:h + 1, :]).astype(BF16)
            xh = xc[:, h * M_HDIM:(h + 1) * M_HDIM]
            hs = h_ref[0, h]
            y = jnp.dot(wgt, xh.astype(BF16), preferred_element_type=F32)
            y = y + exp_acum[:, h:h + 1] * lax.dot_general(
                cg, hs.astype(BF16), NT_DIMS, preferred_element_type=F32)
            xw = (xh * wr[:, h:h + 1]).astype(BF16)
            h_ref[0, h] = exp_last[:, h:h + 1] * hs + lax.dot_general(
                xw, bg, TN_DIMS, preferred_element_type=F32)
            y_scr[:, h * M_HDIM:(h + 1) * M_HDIM] = y

    y = (y_scr[...] + dvec_ref[...] * xc) * _silu(z_ref[...])
    gw = M_INNER // M_GROUPS
    for g in range(M_GROUPS):
        sl = slice(g * gw, (g + 1) * gw)
        ya_ref[:, sl] = _rms(y[:, sl], nw_ref[:, sl]).astype(BF16)


def _ssd_prompt(pbig, psmall, cwx, cbx, cwbc, cbbc, dtb, alog, dvec, nw, batch, n_chunks):
    m = pbig.shape[0]
    row = lambda b, c: b * n_chunks + c
    const = lambda b, c: (0, 0)
    return pl.pallas_call(
        _ssd_kernel,
        grid=(batch, n_chunks),
        in_specs=[
            pl.BlockSpec((CHUNK, M_INNER), lambda b, c: (row(b, c), 0)),
            pl.BlockSpec((CHUNK, M_INNER), lambda b, c: (row(b, c), 1)),
            pl.BlockSpec((CHUNK, M_INNER), lambda b, c: (row(b, c), 2)),
            pl.BlockSpec((CHUNK, LANES), lambda b, c: (row(b, c), 0)),
            pl.BlockSpec((CONV_W, M_INNER), const),
            pl.BlockSpec((1, M_INNER), const),
            pl.BlockSpec((CONV_W, M_INNER), const),
            pl.BlockSpec((1, M_INNER), const),
            pl.BlockSpec((1, LANES), const),
            pl.BlockSpec((1, LANES), const),
            pl.BlockSpec((1, M_INNER), const),
            pl.BlockSpec((1, M_INNER), const),
        ],
        out_specs=[
            pl.BlockSpec((CHUNK, M_INNER), lambda b, c: (row(b, c), 0)),
            pl.BlockSpec((1, M_HEADS, M_HDIM, M_STATE), lambda b, c: (b, 0, 0, 0)),
        ],
        out_shape=[
            jax.ShapeDtypeStruct((m, M_INNER), BF16),
            jax.ShapeDtypeStruct((batch, M_HEADS, M_HDIM, M_STATE), F32),
        ],
        scratch_shapes=[
            pltpu.VMEM((SUBLANES, M_INNER), F32),
            pltpu.VMEM((SUBLANES, M_INNER), F32),
            pltpu.VMEM((CHUNK, M_INNER), F32),
        ],
        compiler_params=_params("parallel", "arbitrary"),
        name="ssd_prompt",
    )(pbig, pbig, pbig, psmall, cwx, cbx, cwbc, cbbc, dtb, alog, dvec, nw)


def _mlstm_kernel(u_ref, v_ref, o_ref, sm_ref, cw_ref, cb_ref, wq_ref, wk_ref,
                  ib_ref, fb_ref, nw_ref,
                  hb_ref, c_ref, n_ref, m_ref, tail_ref):
    n = u_ref.shape[0]

    @pl.when(pl.program_id(1) == 0)
    def _():
        c_ref[...] = jnp.zeros_like(c_ref)
        n_ref[...] = jnp.zeros_like(n_ref)
        m_ref[...] = jnp.zeros_like(m_ref)
        tail_ref[...] = jnp.zeros_like(tail_ref)

    uraw = u_ref[...]
    uc = _silu(_conv_chunk(uraw, tail_ref[...], cw_ref, cb_ref))
    tail_ref[...] = uraw[n - SUBLANES:n]

    sm = sm_ref[...]
    lane = _lane_iota((n, LANES))
    gate_lanes = (lane >= F_LANE) & (lane < F_LANE + L_HEADS)
    ig = pltpu.roll(sm + ib_ref[...], F_LANE - I_LANE, axis=1)
    logf = jnp.where(gate_lanes, jax.nn.log_sigmoid(sm + fb_ref[...]), 0.0)
    causal = _tri(n)
    bcum = _cumsum_rows(causal.astype(F32), logf)
    m_old = m_ref[0]
    bl = bcum[n - 1:n, :]
    src = bl - bcum + ig
    m_new = jnp.maximum(bl + m_old, jnp.max(src, axis=0, keepdims=True))
    wr = jnp.exp(src - m_new)
    keep = jnp.exp(bl + m_old - m_new)
    inter = bcum + m_old
    bcum_t = bcum.T
    ig_t = ig.T
    scale = L_HDIM ** -0.5

    for h in range(L_HEADS):
        ln = F_LANE + h
        sl = slice(h * L_HDIM, (h + 1) * L_HDIM)
        logd = bcum[:, ln:ln + 1] - bcum_t[ln:ln + 1, :] + ig_t[ln:ln + 1, :]
        logd = jnp.where(causal, logd, -jnp.inf)
        intc = inter[:, ln:ln + 1]
        m_s = jnp.maximum(intc, jnp.max(logd, axis=1, keepdims=True))
        dm = jnp.exp(logd - m_s)
        sc = jnp.exp(intc - m_s)
        ub = uc[:, sl].astype(BF16)
        q = jnp.dot(ub, wq_ref[h], preferred_element_type=F32)
        k = jnp.dot(ub, wk_ref[h], preferred_element_type=F32) * scale
        qb = q.astype(BF16)
        kb = k.astype(BF16)
        vb = v_ref[:, sl].astype(BF16)
        s = lax.dot_general(qb, kb, NT_DIMS, preferred_element_type=F32) * dm
        c_old = c_ref[0, h]
        n_old = n_ref[0, h:h + 1, :]
        num = jnp.dot(s.astype(BF16), vb, preferred_element_type=F32)
        num = num + sc * jnp.dot(qb, c_old.astype(BF16), preferred_element_type=F32)
        den = jnp.sum(s, axis=1, keepdims=True) + sc * jnp.sum(q * n_old, axis=1, keepdims=True)
        hh = num / jnp.maximum(jnp.abs(den), jnp.exp(-m_s))
        hb_ref[:, sl] = (jax.nn.sigmoid(o_ref[:, sl]) * _rms(hh, nw_ref[:, sl])).astype(BF16)
        kw = k * wr[:, ln:ln + 1]
        kp = keep[:, ln:ln + 1]
        c_ref[0, h] = kp * c_old + lax.dot_general(
            kw.astype(BF16), vb, TN_DIMS, preferred_element_type=F32)
        n_ref[0, h:h + 1, :] = kp * n_old + jnp.sum(kw, axis=0, keepdims=True)

    m_ref[0] = m_new


def _mlstm_prompt(pbig, psmall, cw, cb, wq, wk, ib, fb, nw, batch, n_chunks):
    m = pbig.shape[0]
    row = lambda b, c: b * n_chunks + c
    const2 = lambda b, c: (0, 0)
    return pl.pallas_call(
        _mlstm_kernel,
        grid=(batch, n_chunks),
        in_specs=[
            pl.BlockSpec((CHUNK, L_INNER), lambda b, c: (row(b, c), 6)),
            pl.BlockSpec((CHUNK, L_INNER), lambda b, c: (row(b, c), 7)),
            pl.BlockSpec((CHUNK, L_INNER), lambda b, c: (row(b, c), 8)),
            pl.BlockSpec((CHUNK, LANES), lambda b, c: (row(b, c), 0)),
            pl.BlockSpec((CONV_W, L_INNER), const2),
            pl.BlockSpec((1, L_INNER), const2),
            pl.BlockSpec((L_HEADS, L_HDIM, L_HDIM), lambda b, c: (0, 0, 0)),
            pl.BlockSpec((L_HEADS, L_HDIM, L_HDIM), lambda b, c: (0, 0, 0)),
            pl.BlockSpec((1, LANES), const2),
            pl.BlockSpec((1, LANES), const2),
            pl.BlockSpec((1, L_INNER), const2),
        ],
        out_specs=[
            pl.BlockSpec((CHUNK, L_INNER), lambda b, c: (row(b, c), 0)),
            pl.BlockSpec((1, L_HEADS, L_HDIM, L_HDIM), lambda b, c: (b, 0, 0, 0)),
            pl.BlockSpec((1, L_HEADS, L_HDIM), lambda b, c: (b, 0, 0)),
            pl.BlockSpec((1, 1, LANES), lambda b, c: (b, 0, 0)),
        ],
        out_shape=[
            jax.ShapeDtypeStruct((m, L_INNER), BF16),
            jax.ShapeDtypeStruct((batch, L_HEADS, L_HDIM, L_HDIM), F32),
            jax.ShapeDtypeStruct((batch, L_HEADS, L_HDIM), F32),
            jax.ShapeDtypeStruct((batch, 1, LANES), F32),
        ],
        scratch_shapes=[pltpu.VMEM((SUBLANES, L_INNER), F32)],
        compiler_params=_params("parallel", "arbitrary"),
        name="mlstm_prompt",
    )(pbig, pbig, pbig, psmall, cw, cb, wq, wk, ib, fb, nw)


def _tail_kernel(ya_ref, hb_ref, ga_ref, gb_ref, x_ref, wa_ref, wb_ref, wo_ref,
                 nmw_ref, wup_ref, wdn_ref, fw_ref, y_ref):
    a = jnp.dot(ya_ref[...], wa_ref[...], preferred_element_type=F32)
    b = jnp.dot(hb_ref[...], wb_ref[...], preferred_element_type=F32)
    t = jax.nn.sigmoid(ga_ref[...]) * a + jax.nn.sigmoid(gb_ref[...]) * b
    x1 = x_ref[...] + jnp.dot(t.astype(BF16), wo_ref[...], preferred_element_type=F32)
    hn = _rms(x1, nmw_ref[...]).astype(BF16)
    acc = x1
    for c in range(D_FF // PROJ_BLOCK):
        sl = slice(c * PROJ_BLOCK, (c + 1) * PROJ_BLOCK)
        up = jnp.dot(hn, wup_ref[:, sl], preferred_element_type=F32)
        act = jnp.square(jnp.maximum(up, 0.0)).astype(BF16)
        acc = acc + jnp.dot(act, wdn_ref[sl, :], preferred_element_type=F32)
    y_ref[...] = _rms(acc, fw_ref[...])


def _tail(ya, hb, pbig, x, wa, wb, wo, nmw, wup, wdn, fw, tm):
    m = x.shape[0]
    tm = min(tm, m)
    rows = lambda i: (i, 0)
    const = lambda i: (0, 0)

    def resident(shape):
        return pl.BlockSpec(shape, const, pipeline_mode=pl.Buffered(1))

    return pl.pallas_call(
        _tail_kernel,
        grid=(m // tm,),
        in_specs=[
            pl.BlockSpec((tm, M_INNER), rows),
            pl.BlockSpec((tm, L_INNER), rows),
            pl.BlockSpec((tm, PROJ_BLOCK), lambda i: (i, 9)),
            pl.BlockSpec((tm, PROJ_BLOCK), lambda i: (i, 10)),
            pl.BlockSpec((tm, D_MODEL), rows),
            resident((M_INNER, D_MODEL)),
            resident((L_INNER, D_MODEL)),
            resident((D_MODEL, D_MODEL)),
            resident((1, D_MODEL)),
            resident((D_MODEL, D_FF)),
            resident((D_FF, D_MODEL)),
            resident((1, D_MODEL)),
        ],
        out_specs=pl.BlockSpec((tm, D_MODEL), rows),
        out_shape=jax.ShapeDtypeStruct((m, D_MODEL), F32),
        compiler_params=_params("parallel"),
        name="tail",
    )(ya, hb, pbig, pbig, x, wa, wb, wo, nmw, wup, wdn, fw)


def _expand_heads(a, first_lane, n_heads, width):
    rows = a.shape[0]
    return jnp.concatenate(
        [jnp.broadcast_to(a[:, first_lane + h:first_lane + h + 1], (rows, width))
         for h in range(n_heads)], axis=1)


def _sample_pre_kernel(xbc_ref, u_ref, sm_ref, cm0_ref, cm1_ref, cm2_ref, cl0_ref, cl1_ref, cl2_ref,
                       m_ref, cwm_ref, cbm_ref, cwl_ref, cbl_ref, wq_ref, wk_ref,
                       dtb_ref, alog_ref, ib_ref, fb_ref,
                       xc_ref, bcc_ref, q_ref, k_ref, dt_ref, da_ref, wr_ref, keep_ref, mnew_ref,
                       cmn_ref, cln_ref):
    xbc = xbc_ref[...]
    conv_m = (cbm_ref[...] + cwm_ref[0:1, :] * cm0_ref[...] + cwm_ref[1:2, :] * cm1_ref[...]
              + cwm_ref[2:3, :] * cm2_ref[...] + cwm_ref[3:4, :] * xbc)
    act = _silu(conv_m)
    xc_ref[...] = act[:, :M_INNER]
    bcc_ref[...] = act[:, M_INNER:]
    cmn_ref[:, 0:M_CONV] = cm1_ref[...]
    cmn_ref[:, M_CONV:2 * M_CONV] = cm2_ref[...]
    cmn_ref[:, 2 * M_CONV:3 * M_CONV] = xbc

    u = u_ref[...]
    conv_l = (cbl_ref[...] + cwl_ref[0:1, :] * cl0_ref[...] + cwl_ref[1:2, :] * cl1_ref[...]
              + cwl_ref[2:3, :] * cl2_ref[...] + cwl_ref[3:4, :] * u)
    uc = _silu(conv_l)
    cln_ref[:, 0:L_INNER] = cl1_ref[...]
    cln_ref[:, L_INNER:2 * L_INNER] = cl2_ref[...]
    cln_ref[:, 2 * L_INNER:3 * L_INNER] = u
    scale = L_HDIM ** -0.5
    for h in range(L_HEADS):
        sl = slice(h * L_HDIM, (h + 1) * L_HDIM)
        ub = uc[:, sl].astype(BF16)
        q_ref[:, sl] = jnp.dot(ub, wq_ref[h], preferred_element_type=F32)
        k_ref[:, sl] = jnp.dot(ub, wk_ref[h], preferred_element_type=F32) * scale

    sm = sm_ref[...]
    dt = _softplus(sm + dtb_ref[...])
    dt_ref[...] = dt
    da_ref[...] = jnp.exp(dt * (-jnp.exp(alog_ref[...])))
    ig = pltpu.roll(sm + ib_ref[...], F_LANE - I_LANE, axis=1)
    logf = jax.nn.log_sigmoid(sm + fb_ref[...])
    m_old = m_ref[...]
    m_new = jnp.maximum(logf + m_old, ig)
    mnew_ref[...] = m_new
    wr_ref[...] = jnp.exp(ig - m_new)
    keep_ref[...] = jnp.exp(logf + m_old - m_new)


def _sample_pre(pbig, psmall, conv_m, conv_l, m_lanes, cwm, cbm, cwl, cbl, wq, wk, dtb, alog, ib, fb):
    s = pbig.shape[0]
    f = lambda shape: jax.ShapeDtypeStruct(shape, F32)
    full2 = lambda shape: pl.BlockSpec(shape, lambda i: (0, 0))
    full3 = lambda shape: pl.BlockSpec(shape, lambda i: (0, 0, 0))
    state_row = lambda width, j: pl.BlockSpec((s, width), lambda i, j=j: (0, j))
    return pl.pallas_call(
        _sample_pre_kernel,
        grid=(1,),
        in_specs=[
            full2((s, M_CONV)),
            pl.BlockSpec((s, L_INNER), lambda i: (0, 6)),
            full2((s, LANES)),
            state_row(M_CONV, 0), state_row(M_CONV, 1), state_row(M_CONV, 2),
            state_row(L_INNER, 0), state_row(L_INNER, 1), state_row(L_INNER, 2),
            full2((s, LANES)),
            full2((CONV_W, M_CONV)), full2((1, M_CONV)),
            full2((CONV_W, L_INNER)), full2((1, L_INNER)),
            full3((L_HEADS, L_HDIM, L_HDIM)), full3((L_HEADS, L_HDIM, L_HDIM)),
            full2((1, LANES)), full2((1, LANES)), full2((1, LANES)), full2((1, LANES)),
        ],
        out_specs=[
            full2((s, M_INNER)), full2((s, M_INNER)), full2((s, L_INNER)), full2((s, L_INNER)),
            full2((s, LANES)), full2((s, LANES)), full2((s, LANES)), full2((s, LANES)), full2((s, LANES)),
            full2((s, (CONV_W - 1) * M_CONV)), full2((s, (CONV_W - 1) * L_INNER)),
        ],
        out_shape=[
            f((s, M_INNER)), f((s, M_INNER)), f((s, L_INNER)), f((s, L_INNER)),
            f((s, LANES)), f((s, LANES)), f((s, LANES)), f((s, LANES)), f((s, LANES)),
            f((s, (CONV_W - 1) * M_CONV)), f((s, (CONV_W - 1) * L_INNER)),
        ],
        compiler_params=_params("arbitrary"),
        name="sample_pre",
    )(pbig[:, 2 * PROJ_BLOCK:6 * PROJ_BLOCK], pbig, psmall, conv_m, conv_m, conv_m,
      conv_l, conv_l, conv_l, m_lanes, cwm, cbm, cwl, cbl, wq, wk, dtb, alog, ib, fb)


SSD_SB = 8
MLSTM_SB = 16


def _ssd_state_kernel(dt_ref, da_ref, x_ref, b_ref, c_ref, h_ref, hn_ref, y_ref, yt_ref):
    base = pl.program_id(0) * SSD_SB
    xt = x_ref[...].T
    pairs = M_HEADS // 2
    for s in range(SSD_SB):
        for g in range(M_GROUPS):
            brow = b_ref[s, g:g + 1, :]
            crow = c_ref[s, g:g + 1, :]
            for jj in range(M_HPG):
                h = g * M_HPG + jj
                col = s * pairs + h // 2
                j = h % 2
                dt = dt_ref[base + s, h]
                da = da_ref[base + s, h]
                xcol = xt[j * M_HDIM:(j + 1) * M_HDIM, col:col + 1] * dt
                hn = da * h_ref[s, h] + xcol * brow
                hn_ref[s, h] = hn
                yt_ref[j * M_HDIM:(j + 1) * M_HDIM, col:col + 1] = jnp.sum(hn * crow, axis=1, keepdims=True)
    y_ref[...] = yt_ref[...].T


def _ssd_state(dt, da, x_rows, b3, c3, h):
    s = h.shape[0]
    smem = pl.BlockSpec(memory_space=pltpu.SMEM)
    return pl.pallas_call(
        _ssd_state_kernel,
        grid=(s // SSD_SB,),
        in_specs=[
            smem, smem,
            pl.BlockSpec((LANES, LANES), lambda i: (i, 0)),
            pl.BlockSpec((SSD_SB, M_GROUPS, M_STATE), lambda i: (i, 0, 0)),
            pl.BlockSpec((SSD_SB, M_GROUPS, M_STATE), lambda i: (i, 0, 0)),
            pl.BlockSpec((SSD_SB, M_HEADS, M_HDIM, M_STATE), lambda i: (i, 0, 0, 0)),
        ],
        out_specs=[
            pl.BlockSpec((SSD_SB, M_HEADS, M_HDIM, M_STATE), lambda i: (i, 0, 0, 0)),
            pl.BlockSpec((LANES, LANES), lambda i: (i, 0)),
        ],
        out_shape=[
            jax.ShapeDtypeStruct(h.shape, F32),
            jax.ShapeDtypeStruct(x_rows.shape, F32),
        ],
        scratch_shapes=[pltpu.VMEM((LANES, LANES), F32)],
        compiler_params=_params("parallel"),
        name="ssd_state",
    )(dt, da, x_rows, b3, c3, h)


def _mlstm_state_kernel(wr_ref, keep_ref, q_ref, k_ref, v_ref, c_ref, cn_ref, num_ref):
    base = pl.program_id(0) * MLSTM_SB
    qt = q_ref[...].T
    kt = k_ref[...].T
    for s in range(MLSTM_SB):
        for h in range(L_HEADS):
            r = s * L_HEADS + h
            wr = wr_ref[base + s, h]
            keep = keep_ref[base + s, h]
            c_old = c_ref[s, h]
            num_ref[r:r + 1, :] = jnp.sum(qt[:, r:r + 1] * c_old, axis=0, keepdims=True)
            cn_ref[s, h] = keep * c_old + (kt[:, r:r + 1] * wr) * v_ref[r:r + 1, :]


def _mlstm_state(wr, keep, q_rows, k_rows, v_rows, c):
    s = c.shape[0]
    smem = pl.BlockSpec(memory_space=pltpu.SMEM)
    rows = pl.BlockSpec((LANES, LANES), lambda i: (i, 0))
    return pl.pallas_call(
        _mlstm_state_kernel,
        grid=(s // MLSTM_SB,),
        in_specs=[
            smem, smem, rows, rows, rows,
            pl.BlockSpec((MLSTM_SB, L_HEADS, L_HDIM, L_HDIM), lambda i: (i, 0, 0, 0)),
        ],
        out_specs=[
            pl.BlockSpec((MLSTM_SB, L_HEADS, L_HDIM, L_HDIM), lambda i: (i, 0, 0, 0)),
            rows,
        ],
        out_shape=[
            jax.ShapeDtypeStruct(c.shape, F32),
            jax.ShapeDtypeStruct(q_rows.shape, F32),
        ],
        compiler_params=_params("parallel"),
        name="mlstm_state",
    )(wr, keep, q_rows, k_rows, v_rows, c)


def _sample_post_kernel(y_ref, xc_ref, z_ref, dvec_ref, mnw_ref,
                        q_ref, k_ref, v_ref, o_ref, n_ref, num_ref, wr_ref, keep_ref, mnew_ref, lnw_ref,
                        ya_ref, hb_ref, nn_ref):
    y = (y_ref[...] + dvec_ref[...] * xc_ref[...]) * _silu(z_ref[...])
    gw = M_INNER // M_GROUPS
    for g in range(M_GROUPS):
        sl = slice(g * gw, (g + 1) * gw)
        ya_ref[:, sl] = _rms(y[:, sl], mnw_ref[:, sl]).astype(BF16)

    wr = wr_ref[...]
    keep = keep_ref[...]
    floor = jnp.exp(-mnew_ref[...])
    for h in range(L_HEADS):
        ln = F_LANE + h
        sl = slice(h * L_HDIM, (h + 1) * L_HDIM)
        q = q_ref[:, sl]
        k = k_ref[:, sl]
        n_old = n_ref[:, sl]
        wrc = wr[:, ln:ln + 1]
        kpc = keep[:, ln:ln + 1]
        wgt = jnp.sum(q * k, axis=1, keepdims=True) * wrc
        num = wgt * v_ref[:, sl] + kpc * num_ref[:, sl]
        den = wgt + kpc * jnp.sum(q * n_old, axis=1, keepdims=True)
        hh = num / jnp.maximum(jnp.abs(den), floor[:, ln:ln + 1])
        hb_ref[:, sl] = (jax.nn.sigmoid(o_ref[:, sl]) * _rms(hh, lnw_ref[:, sl])).astype(BF16)
        nn_ref[:, sl] = kpc * n_old + wrc * k


def _sample_post(y, xc, pbig, dvec, mnw, q, k, n_rows, num, wr, keep, mnew, lnw):
    s = y.shape[0]
    full = lambda shape: pl.BlockSpec(shape, lambda i: (0, 0))
    blk = lambda width, j: pl.BlockSpec((s, width), lambda i, j=j: (0, j))
    return pl.pallas_call(
        _sample_post_kernel,
        grid=(1,),
        in_specs=[
            full((s, M_INNER)), full((s, M_INNER)), blk(M_INNER, 0),
            full((1, M_INNER)), full((1, M_INNER)),
            full((s, L_INNER)), full((s, L_INNER)), blk(L_INNER, 7), blk(L_INNER, 8),
            full((s, L_INNER)), full((s, L_INNER)),
            full((s, LANES)), full((s, LANES)), full((s, LANES)), full((1, L_INNER)),
        ],
        out_specs=[full((s, M_INNER)), full((s, L_INNER)), full((s, L_INNER))],
        out_shape=[
            jax.ShapeDtypeStruct((s, M_INNER), BF16),
            jax.ShapeDtypeStruct((s, L_INNER), BF16),
            jax.ShapeDtypeStruct((s, L_INNER), F32),
        ],
        compiler_params=_params("arbitrary"),
        name="sample_post",
    )(y, xc, pbig, dvec, mnw, q, k, pbig, pbig, n_rows, num, wr, keep, mnew, lnw)


def _lanes(vec, first_lane):
    n = vec.shape[0]
    return jnp.pad(vec.astype(F32), (first_lane, LANES - first_lane - n)).reshape(1, LANES)


def kernel(x_prompt, x_sample, state_mamba_conv, state_mamba_ssm, state_mlstm_conv, state_mlstm_C, state_mlstm_n, state_mlstm_m, w_in, mamba_conv_w, mamba_conv_b, mamba_dt_bias, mamba_A_log, mamba_D, mamba_norm_w, w_branch_a, mlstm_conv_w, mlstm_conv_b, mlstm_wq, mlstm_wk, mlstm_i_bias, mlstm_f_bias, mlstm_norm_w, w_branch_b, w_out, norm_mix_w, norm_mlp_w, w_up, w_down, final_norm_w):
    depth = w_in.shape[0]
    assert depth == 1
    batch, seq, _ = x_prompt.shape
    n_samp, dec_seq, _ = x_sample.shape
    assert dec_seq == 1 and seq % CHUNK == 0 and seq >= SUBLANES
    assert n_samp % MLSTM_SB == 0 and n_samp % SSD_SB == 0
    n_chunks = seq // CHUNK
    l = 0

    w = w_in[l]
    c0 = M_INNER + M_CONV
    c1 = c0 + M_HEADS
    c2 = c1 + 3 * L_INNER
    c3 = c2 + 2 * L_HEADS
    w_big = jnp.concatenate([w[:, :c0], w[:, c1:c2], w[:, c3:]], axis=1).astype(BF16)
    w_small = jnp.concatenate(
        [w[:, c0:c1], w[:, c2:c3], jnp.zeros((D_MODEL, LANES - M_HEADS - 2 * L_HEADS), F32)],
        axis=1).astype(BF16)
    g_mix = norm_mix_w[l].reshape(1, D_MODEL)
    cwm = mamba_conv_w[l]
    cbm = mamba_conv_b[l].reshape(1, M_CONV)
    cwl = mlstm_conv_w[l]
    cbl = mlstm_conv_b[l].reshape(1, L_INNER)
    dtb = _lanes(mamba_dt_bias[l], DT_LANE)
    alog = _lanes(mamba_A_log[l], DT_LANE)
    ib = _lanes(mlstm_i_bias[l], I_LANE)
    fb = _lanes(mlstm_f_bias[l], F_LANE)
    dvec = jnp.repeat(mamba_D[l].astype(F32), M_HDIM).reshape(1, M_INNER)
    mnw = mamba_norm_w[l].reshape(1, M_INNER)
    lnw = mlstm_norm_w[l].reshape(1, L_INNER)
    wq = mlstm_wq[l].astype(BF16)
    wk = mlstm_wk[l].astype(BF16)
    wa = w_branch_a[l].astype(BF16)
    wb = w_branch_b[l].astype(BF16)
    wo = w_out[l].astype(BF16)
    wup = w_up[l].astype(BF16)
    wdn = w_down[l].astype(BF16)
    nmw = norm_mlp_w[l].reshape(1, D_MODEL)
    fw = final_norm_w.reshape(1, D_MODEL)

    xp = x_prompt.reshape(batch * seq, D_MODEL)
    pbig, psmall = _inproj(xp, g_mix, w_big, w_small, tm=1024)
    ya, p_ssm = _ssd_prompt(pbig, psmall, cwm[:, :M_INNER], cbm[:, :M_INNER], cwm[:, M_INNER:],
                            cbm[:, M_INNER:], dtb, alog, dvec, mnw, batch, n_chunks)
    hb, p_c, p_n, p_m = _mlstm_prompt(pbig, psmall, cwl, cbl, wq, wk, ib, fb, lnw, batch, n_chunks)
    y_prompt = _tail(ya, hb, pbig, xp, wa, wb, wo, nmw, wup, wdn, fw, tm=512)
    p3 = pbig.reshape(batch, seq, BIG_WIDTH)
    p_conv_m = p3[:, seq - (CONV_W - 1):, 2 * PROJ_BLOCK:6 * PROJ_BLOCK]
    p_conv_l = p3[:, seq - (CONV_W - 1):, 6 * PROJ_BLOCK:7 * PROJ_BLOCK]
    p_m = p_m[:, 0, F_LANE:F_LANE + L_HEADS]

    xs = x_sample.reshape(n_samp, D_MODEL)
    sbig, ssmall = _inproj(xs, g_mix, w_big, w_small, tm=n_samp)
    m_lanes = jnp.pad(state_mlstm_m[l], ((0, 0), (F_LANE, LANES - F_LANE - L_HEADS)))
    (xc, bcc, q, k, dt, da, wr, keep, mnew, s_conv_m, s_conv_l) = _sample_pre(
        sbig, ssmall, state_mamba_conv[l].reshape(n_samp, (CONV_W - 1) * M_CONV),
        state_mlstm_conv[l].reshape(n_samp, (CONV_W - 1) * L_INNER), m_lanes,
        cwm, cbm, cwl, cbl, wq, wk, dtb, alog, ib, fb)
    s_ssm, y_rows = _ssd_state(
        dt[:, :M_HEADS], da[:, :M_HEADS], xc.reshape(n_samp * M_HEADS // 2, LANES),
        bcc[:, :M_GROUPS * M_STATE].reshape(n_samp, M_GROUPS, M_STATE),
        bcc[:, M_GROUPS * M_STATE:].reshape(n_samp, M_GROUPS, M_STATE),
        state_mamba_ssm[l])
    s_c, num_rows = _mlstm_state(
        wr[:, F_LANE:F_LANE + L_HEADS], keep[:, F_LANE:F_LANE + L_HEADS],
        q.reshape(n_samp * L_HEADS, L_HDIM), k.reshape(n_samp * L_HEADS, L_HDIM),
        sbig[:, 7 * PROJ_BLOCK:8 * PROJ_BLOCK].reshape(n_samp * L_HEADS, L_HDIM),
        state_mlstm_C[l])
    ya_s, hb_s, s_n = _sample_post(
        y_rows.reshape(n_samp, M_INNER), xc, sbig, dvec, mnw, q, k,
        state_mlstm_n[l].reshape(n_samp, L_INNER), num_rows.reshape(n_samp, L_INNER),
        wr, keep, mnew, lnw)
    y_sample = _tail(ya_s, hb_s, sbig, xs, wa, wb, wo, nmw, wup, wdn, fw, tm=n_samp)

    lead = lambda a: a[None]
    return (
        y_prompt.reshape(batch, seq, D_MODEL),
        y_sample.reshape(n_samp, 1, D_MODEL),
        lead(p_conv_m), lead(p_ssm), lead(p_conv_l), lead(p_c), lead(p_n), lead(p_m),
        lead(s_conv_m.reshape(n_samp, CONV_W - 1, M_CONV)), lead(s_ssm),
        lead(s_conv_l.reshape(n_samp, CONV_W - 1, L_INNER)),
        lead(s_c), lead(s_n.reshape(n_samp, L_HEADS, L_HDIM)), lead(mnew[:, F_LANE:F_LANE + L_HEADS]),
    )
```

```python
import jax
import jax.numpy as jnp
from jax import lax
from jax.experimental import pallas as pl
from jax.experimental.pallas import tpu as pltpu

F32 = jnp.float32
BF16 = jnp.bfloat16

D_MODEL = 1024
M_INNER = 2048
M_HEADS = 32
M_HDIM = 64
M_GROUPS = 8
M_PAIRS = M_HEADS // 2
M_STATE = 128
M_CONV = 4096
L_INNER = 1024
L_HEADS = 8
L_HDIM = 128
D_FF = 4096
CONV_W = 4
CHUNK = 128
EPS = 1e-6

LANES = 128
SUBLANES = 8
PROJ_BLOCK = 1024
N_PROJ_BLOCKS = 11
BIG_WIDTH = PROJ_BLOCK * N_PROJ_BLOCKS
DT_LANE = 0
I_LANE = 32
F_LANE = 40
VMEM_LIMIT = 56 * 1024 * 1024

HIGHEST = lax.Precision.HIGHEST
NT_DIMS = (((1,), (1,)), ((), ()))
TN_DIMS = (((0,), (0,)), ((), ()))


def _params(*sem):
    return pltpu.CompilerParams(dimension_semantics=sem, vmem_limit_bytes=VMEM_LIMIT)


def _sigmoid(x):
    return 0.5 * jnp.tanh(0.5 * x) + 0.5


def _silu(x):
    return x * _sigmoid(x)


def _log1p_exp_neg_abs(x):
    e = jnp.exp(-jnp.abs(x))
    u = 1.0 + e
    return jnp.where(u == 1.0, e, jnp.log(u) * (e / (u - 1.0)))


def _softplus(x):
    return jnp.maximum(x, 0.0) + _log1p_exp_neg_abs(x)


def _log_sigmoid(x):
    return jnp.minimum(x, 0.0) - _log1p_exp_neg_abs(x)


def _rms(x, w):
    return x * lax.rsqrt(jnp.mean(x * x, axis=-1, keepdims=True) + EPS) * w


def _lane_iota(shape):
    return lax.broadcasted_iota(jnp.int32, shape, len(shape) - 1)


def _tri(n):
    r = lax.broadcasted_iota(jnp.int32, (n, n), 0)
    c = lax.broadcasted_iota(jnp.int32, (n, n), 1)
    return r >= c


def _split3(a):
    hi = a.astype(BF16)
    r1 = a - hi.astype(F32)
    mid = r1.astype(BF16)
    lo = (r1 - mid.astype(F32)).astype(BF16)
    return hi, mid, lo


def _cumsum_rows(causal, a):
    tri01 = causal.astype(F32).astype(BF16)
    return jnp.dot(jnp.concatenate([tri01] * 3, axis=1), jnp.concatenate(_split3(a), axis=0),
                   preferred_element_type=F32)


def _spread(a, e2_ref):
    hi, mid, _ = _split3(a)
    return jnp.dot(jnp.concatenate([hi, mid], axis=1), e2_ref[...], preferred_element_type=F32)


CONV_COLS = 512


def _conv_silu(x_ref, win_ref, shift_ref, w_ref, b_ref, out_ref):
    n, width = x_ref.shape
    tile = 16
    win_ref[n:2 * n, :] = x_ref[...]
    for c0 in range(0, width, CONV_COLS):
        cs = slice(c0, c0 + CONV_COLS)
        sh = jnp.dot(shift_ref[...], win_ref[:, cs], preferred_element_type=F32)
        y = b_ref[:, cs] + w_ref[CONV_W - 1:CONV_W, cs] * x_ref[:, cs].astype(F32)
        for back in range(1, CONV_W):
            y = y + w_ref[CONV_W - 1 - back:CONV_W - back, cs] * sh[(back - 1) * n:back * n]
        out_ref[:, cs] = _silu(y).astype(out_ref.dtype)
    win_ref[n - tile:n, :] = x_ref[n - tile:n, :]


def _inproj_kernel(x_ref, g_ref, w_ref, ws_ref, o_ref, os_ref, xn_ref):
    @pl.when(pl.program_id(1) == 0)
    def _():
        xn_ref[...] = _rms(x_ref[...], g_ref[...]).astype(BF16)
        os_ref[...] = jnp.dot(xn_ref[...], ws_ref[...], preferred_element_type=F32)

    o_ref[...] = jnp.dot(xn_ref[...], w_ref[...], preferred_element_type=F32).astype(BF16)


def _inproj(x, g, w_big, w_small, tm):
    m = x.shape[0]
    tm = min(tm, m)
    return pl.pallas_call(
        _inproj_kernel,
        grid=(m // tm, N_PROJ_BLOCKS),
        in_specs=[
            pl.BlockSpec((tm, D_MODEL), lambda i, j: (i, 0)),
            pl.BlockSpec((1, D_MODEL), lambda i, j: (0, 0)),
            pl.BlockSpec((D_MODEL, PROJ_BLOCK), lambda i, j: (0, j)),
            pl.BlockSpec((D_MODEL, LANES), lambda i, j: (0, 0)),
        ],
        out_specs=[
            pl.BlockSpec((tm, PROJ_BLOCK), lambda i, j: (i, j)),
            pl.BlockSpec((tm, LANES), lambda i, j: (i, 0)),
        ],
        out_shape=[
            jax.ShapeDtypeStruct((m, BIG_WIDTH), BF16),
            jax.ShapeDtypeStruct((m, LANES), F32),
        ],
        scratch_shapes=[pltpu.VMEM((tm, D_MODEL), BF16)],
        compiler_params=_params("parallel", "arbitrary"),
        name="in_proj",
    )(x, g, w_big, w_small)


def _ssd_kernel(z_ref, x_ref, bc_ref, sm_ref, cwx_ref, cbx_ref, cwbc_ref, cbbc_ref,
                dtb_ref, alog_ref, dvec_ref, nw_ref, e64_ref, shift_ref,
                ya_ref, h_ref, winx_ref, winbc_ref, xc_scr, bcc_scr, y_scr):
    n = x_ref.shape[0]

    @pl.when(pl.program_id(1) == 0)
    def _():
        h_ref[...] = jnp.zeros_like(h_ref)
        winx_ref[0:n, :] = jnp.zeros((n, M_INNER), BF16)
        winbc_ref[0:n, :] = jnp.zeros((n, M_INNER), BF16)

    _conv_silu(x_ref, winx_ref, shift_ref, cwx_ref, cbx_ref, xc_scr)
    _conv_silu(bc_ref, winbc_ref, shift_ref, cwbc_ref, cbbc_ref, bcc_scr)
    xc = xc_scr[...]
    bcc = bcc_scr[...]

    lane = _lane_iota((n, LANES))
    dt = _softplus(sm_ref[...] + dtb_ref[...])
    da = jnp.where(lane < M_HEADS, dt * (-jnp.exp(alog_ref[...])), 0.0)
    causal = _tri(n)
    acum = _cumsum_rows(causal, da)
    acum_t = acum.T
    dt_t = dt.T
    last = acum[n - 1:n, :]
    exp_last = jnp.exp(last)
    wr_x = _spread(jnp.exp(last - acum) * dt, e64_ref)
    ea_x = _spread(jnp.exp(acum), e64_ref)

    xw = (xc * wr_x).astype(BF16)
    xb = xc.astype(BF16)
    low_half = jnp.bitwise_and(_lane_iota((n, M_INNER)), LANES - 1) < M_HDIM
    zero = jnp.zeros((), BF16)
    x_lo = jnp.where(low_half, xb, zero)
    x_hi = jnp.where(low_half, zero, xb)
    first_rows = lax.broadcasted_iota(jnp.int32, (LANES, LANES), 0) < M_HDIM

    def weights(cb, h):
        seg = acum[:, h:h + 1] - acum_t[h:h + 1, :]
        decay = jnp.exp(jnp.where(causal, seg, -jnp.inf))
        return (cb * decay * dt_t[h:h + 1, :]).astype(BF16)

    for g in range(M_GROUPS):
        bg = bcc[:, g * M_STATE:(g + 1) * M_STATE]
        cg = bcc[:, (M_GROUPS + g) * M_STATE:(M_GROUPS + g + 1) * M_STATE]
        cb = lax.dot_general(cg, bg, NT_DIMS, preferred_element_type=F32)
        for pp in range(M_PAIRS // M_GROUPS):
            hp = g * (M_PAIRS // M_GROUPS) + pp
            h0, h1 = 2 * hp, 2 * hp + 1
            sl = slice(hp * LANES, (hp + 1) * LANES)
            hs = h_ref[0, hp]
            y = jnp.dot(weights(cb, h0), x_lo[:, sl], preferred_element_type=F32)
            y = y + jnp.dot(weights(cb, h1), x_hi[:, sl], preferred_element_type=F32)
            ys = lax.dot_general(cg, hs.astype(BF16), NT_DIMS, preferred_element_type=F32)
            y_scr[:, sl] = y + ea_x[:, sl] * ys
            el = jnp.where(first_rows, exp_last[:, h0:h0 + 1], exp_last[:, h1:h1 + 1])
            h_ref[0, hp] = el * hs + lax.dot_general(xw[:, sl], bg, TN_DIMS, preferred_element_type=F32)

    y = (y_scr[...] + dvec_ref[...] * xc) * _silu(z_ref[...].astype(F32))
    gw = M_INNER // M_GROUPS
    for g in range(M_GROUPS):
        sl = slice(g * gw, (g + 1) * gw)
        ya_ref[:, sl] = _rms(y[:, sl], nw_ref[:, sl]).astype(BF16)


def _ssd_prompt(pbig, psmall, cwx, cbx, cwbc, cbbc, dtb, alog, dvec, nw, e64, shift, batch, n_chunks):
    m = pbig.shape[0]
    row = lambda b, c: b * n_chunks + c
    const = lambda b, c: (0, 0)
    return pl.pallas_call(
        _ssd_kernel,
        grid=(batch, n_chunks),
        in_specs=[
            pl.BlockSpec((CHUNK, M_INNER), lambda b, c: (row(b, c), 0)),
            pl.BlockSpec((CHUNK, M_INNER), lambda b, c: (row(b, c), 1)),
            pl.BlockSpec((CHUNK, M_INNER), lambda b, c: (row(b, c), 2)),
            pl.BlockSpec((CHUNK, LANES), lambda b, c: (row(b, c), 0)),
            pl.BlockSpec((CONV_W, M_INNER), const),
            pl.BlockSpec((1, M_INNER), const),
            pl.BlockSpec((CONV_W, M_INNER), const),
            pl.BlockSpec((1, M_INNER), const),
            pl.BlockSpec((1, LANES), const),
            pl.BlockSpec((1, LANES), const),
            pl.BlockSpec((1, M_INNER), const),
            pl.BlockSpec((1, M_INNER), const),
            pl.BlockSpec((2 * LANES, M_INNER), const),
            pl.BlockSpec(((CONV_W - 1) * CHUNK, 2 * CHUNK), const),
        ],
        out_specs=[
            pl.BlockSpec((CHUNK, M_INNER), lambda b, c: (row(b, c), 0)),
            pl.BlockSpec((1, M_PAIRS, LANES, M_STATE), lambda b, c: (b, 0, 0, 0)),
        ],
        out_shape=[
            jax.ShapeDtypeStruct((m, M_INNER), BF16),
            jax.ShapeDtypeStruct((batch, M_PAIRS, LANES, M_STATE), F32),
        ],
        scratch_shapes=[
            pltpu.VMEM((2 * CHUNK, M_INNER), BF16),
            pltpu.VMEM((2 * CHUNK, M_INNER), BF16),
            pltpu.VMEM((CHUNK, M_INNER), F32),
            pltpu.VMEM((CHUNK, M_INNER), BF16),
            pltpu.VMEM((CHUNK, M_INNER), F32),
        ],
        compiler_params=_params("parallel", "arbitrary"),
        name="ssd_prompt",
    )(pbig, pbig, pbig, psmall, cwx, cbx, cwbc, cbbc, dtb, alog, dvec, nw, e64, shift)


def _mlstm_kernel(u_ref, v_ref, o_ref, sm_ref, cw_ref, cb_ref, wq_ref, wk_ref,
                  ib_ref, fb_ref, nw_ref, el_ref, shift_ref,
                  hb_ref, c_ref, n_ref, m_ref, win_ref, uc_scr):
    n = u_ref.shape[0]

    @pl.when(pl.program_id(1) == 0)
    def _():
        c_ref[...] = jnp.zeros_like(c_ref)
        n_ref[...] = jnp.zeros_like(n_ref)
        m_ref[...] = jnp.zeros_like(m_ref)
        win_ref[0:n, :] = jnp.zeros((n, L_INNER), BF16)

    _conv_silu(u_ref, win_ref, shift_ref, cw_ref, cb_ref, uc_scr)
    uc = uc_scr[...]

    sm = sm_ref[...]
    lane = _lane_iota((n, LANES))
    gate_lanes = (lane >= F_LANE) & (lane < F_LANE + L_HEADS)
    ig = pltpu.roll(sm + ib_ref[...], F_LANE - I_LANE, axis=1)
    logf = jnp.where(gate_lanes, _log_sigmoid(sm + fb_ref[...]), 0.0)
    causal = _tri(n)
    bcum = _cumsum_rows(causal, logf)
    m_old = m_ref[0]
    bl = bcum[n - 1:n, :]
    src = bl - bcum + ig
    m_new = jnp.maximum(bl + m_old, jnp.max(src, axis=0, keepdims=True))
    keep = jnp.exp(bl + m_old - m_new)
    bcum_t = bcum.T
    ig_t = ig.T
    wr_x = _spread(jnp.exp(src - m_new), el_ref)
    scale = L_HDIM ** -0.5
    c_olds = [c_ref[0, h] for h in range(L_HEADS)]
    n_olds = [n_ref[0, h:h + 1, :] for h in range(L_HEADS)]
    c_news, n_news = [], []

    for h in range(L_HEADS):
        ln = F_LANE + h
        sl = slice(h * L_HDIM, (h + 1) * L_HDIM)
        bcol = jnp.broadcast_to(bcum[:, ln:ln + 1], (n, n))
        logd = jnp.where(causal, bcol - bcum_t[ln:ln + 1, :] + ig_t[ln:ln + 1, :], -jnp.inf)
        inter = bcol + m_old[:, ln:ln + 1]
        m_s = jnp.maximum(inter, jnp.max(logd, axis=1, keepdims=True))
        dm = jnp.exp(logd - m_s)
        sc = jnp.exp(inter - m_s)
        ub = uc[:, sl]
        q = jnp.dot(ub, wq_ref[h], preferred_element_type=F32)
        k = jnp.dot(ub, wk_ref[h], preferred_element_type=F32) * scale
        qb = q.astype(BF16)
        kb = k.astype(BF16)
        vb = v_ref[:, sl]
        s = lax.dot_general(qb, kb, NT_DIMS, preferred_element_type=F32) * dm
        c_old = c_olds[h]
        n_old = n_olds[h]
        num = jnp.dot(s.astype(BF16), vb, preferred_element_type=F32)
        num = num + sc * jnp.dot(qb, c_old.astype(BF16), preferred_element_type=F32)
        den = jnp.sum(s, axis=1, keepdims=True) + sc * jnp.sum(q * n_old, axis=1, keepdims=True)
        hh = num / jnp.maximum(jnp.abs(den), jnp.exp(-m_s))
        gate = _sigmoid(o_ref[:, sl].astype(F32))
        hb_ref[:, sl] = (gate * _rms(hh, nw_ref[:, sl])).astype(BF16)
        kw = k * wr_x[:, sl]
        kp = keep[:, ln:ln + 1]
        c_news.append(kp * c_old + lax.dot_general(
            kw.astype(BF16), vb, TN_DIMS, preferred_element_type=F32))
        n_news.append(kp * n_old + jnp.sum(kw, axis=0, keepdims=True))

    for h in range(L_HEADS):
        c_ref[0, h] = c_news[h]
        n_ref[0, h:h + 1, :] = n_news[h]
    m_ref[0] = m_new


def _mlstm_prompt(pbig, psmall, cw, cb, wq, wk, ib, fb, nw, el, shift, batch, n_chunks):
    m = pbig.shape[0]
    row = lambda b, c: b * n_chunks + c
    const2 = lambda b, c: (0, 0)
    return pl.pallas_call(
        _mlstm_kernel,
        grid=(batch, n_chunks),
        in_specs=[
            pl.BlockSpec((CHUNK, L_INNER), lambda b, c: (row(b, c), 6)),
            pl.BlockSpec((CHUNK, L_INNER), lambda b, c: (row(b, c), 7)),
            pl.BlockSpec((CHUNK, L_INNER), lambda b, c: (row(b, c), 8)),
            pl.BlockSpec((CHUNK, LANES), lambda b, c: (row(b, c), 0)),
            pl.BlockSpec((CONV_W, L_INNER), const2),
            pl.BlockSpec((1, L_INNER), const2),
            pl.BlockSpec((L_HEADS, L_HDIM, L_HDIM), lambda b, c: (0, 0, 0)),
            pl.BlockSpec((L_HEADS, L_HDIM, L_HDIM), lambda b, c: (0, 0, 0)),
            pl.BlockSpec((1, LANES), const2),
            pl.BlockSpec((1, LANES), const2),
            pl.BlockSpec((1, L_INNER), const2),
            pl.BlockSpec((2 * LANES, L_INNER), const2),
            pl.BlockSpec(((CONV_W - 1) * CHUNK, 2 * CHUNK), const2),
        ],
        out_specs=[
            pl.BlockSpec((CHUNK, L_INNER), lambda b, c: (row(b, c), 0)),
            pl.BlockSpec((1, L_HEADS, L_HDIM, L_HDIM), lambda b, c: (b, 0, 0, 0)),
            pl.BlockSpec((1, L_HEADS, L_HDIM), lambda b, c: (b, 0, 0)),
            pl.BlockSpec((1, 1, LANES), lambda b, c: (b, 0, 0)),
        ],
        out_shape=[
            jax.ShapeDtypeStruct((m, L_INNER), BF16),
            jax.ShapeDtypeStruct((batch, L_HEADS, L_HDIM, L_HDIM), F32),
            jax.ShapeDtypeStruct((batch, L_HEADS, L_HDIM), F32),
            jax.ShapeDtypeStruct((batch, 1, LANES), F32),
        ],
        scratch_shapes=[
            pltpu.VMEM((2 * CHUNK, L_INNER), BF16),
            pltpu.VMEM((CHUNK, L_INNER), BF16),
        ],
        compiler_params=_params("parallel", "arbitrary"),
        name="mlstm_prompt",
    )(pbig, pbig, pbig, psmall, cw, cb, wq, wk, ib, fb, nw, el, shift)


def _tail_kernel(ya_ref, hb_ref, ga_ref, gb_ref, x_ref, wa_ref, wb_ref, wo_ref,
                 nmw_ref, wup_ref, wdn_ref, fw_ref, y_ref):
    a = jnp.dot(ya_ref[...], wa_ref[...], preferred_element_type=F32)
    b = jnp.dot(hb_ref[...], wb_ref[...], preferred_element_type=F32)
    t = _sigmoid(ga_ref[...].astype(F32)) * a + _sigmoid(gb_ref[...].astype(F32)) * b
    x1 = x_ref[...] + jnp.dot(t.astype(BF16), wo_ref[...], preferred_element_type=F32)
    hn = _rms(x1, nmw_ref[...]).astype(BF16)
    acc = x1
    for c in range(D_FF // PROJ_BLOCK):
        sl = slice(c * PROJ_BLOCK, (c + 1) * PROJ_BLOCK)
        up = jnp.dot(hn, wup_ref[:, sl], preferred_element_type=F32)
        act = jnp.square(jnp.maximum(up, 0.0)).astype(BF16)
        acc = acc + jnp.dot(act, wdn_ref[sl, :], preferred_element_type=F32)
    y_ref[...] = _rms(acc, fw_ref[...])


def _tail(ya, hb, pbig, x, wa, wb, wo, nmw, wup, wdn, fw, tm):
    m = x.shape[0]
    tm = min(tm, m)
    rows = lambda i: (i, 0)
    const = lambda i: (0, 0)

    def resident(shape):
        return pl.BlockSpec(shape, const, pipeline_mode=pl.Buffered(1))

    return pl.pallas_call(
        _tail_kernel,
        grid=(m // tm,),
        in_specs=[
            pl.BlockSpec((tm, M_INNER), rows),
            pl.BlockSpec((tm, L_INNER), rows),
            pl.BlockSpec((tm, PROJ_BLOCK), lambda i: (i, 9)),
            pl.BlockSpec((tm, PROJ_BLOCK), lambda i: (i, 10)),
            pl.BlockSpec((tm, D_MODEL), rows),
            resident((M_INNER, D_MODEL)),
            resident((L_INNER, D_MODEL)),
            resident((D_MODEL, D_MODEL)),
            resident((1, D_MODEL)),
            resident((D_MODEL, D_FF)),
            resident((D_FF, D_MODEL)),
            resident((1, D_MODEL)),
        ],
        out_specs=pl.BlockSpec((tm, D_MODEL), rows),
        out_shape=jax.ShapeDtypeStruct((m, D_MODEL), F32),
        compiler_params=_params("parallel"),
        name="tail",
    )(ya, hb, pbig, pbig, x, wa, wb, wo, nmw, wup, wdn, fw)


def _sample_pre_kernel(xbc_ref, u_ref, sm_ref, cm0_ref, cm1_ref, cm2_ref, cl0_ref, cl1_ref, cl2_ref,
                       m_ref, cwm_ref, cbm_ref, cwl_ref, cbl_ref, wq_ref, wk_ref,
                       dtb_ref, alog_ref, ib_ref, fb_ref,
                       xc_ref, bcc_ref, q_ref, k_ref, dt_ref, da_ref, wr_ref, keep_ref, mnew_ref,
                       cmn_ref, cln_ref):
    xbc = xbc_ref[...].astype(F32)
    conv_m = (cbm_ref[...] + cwm_ref[0:1, :] * cm0_ref[...] + cwm_ref[1:2, :] * cm1_ref[...]
              + cwm_ref[2:3, :] * cm2_ref[...] + cwm_ref[3:4, :] * xbc)
    act = _silu(conv_m)
    xc_ref[...] = act[:, :M_INNER]
    bcc_ref[...] = act[:, M_INNER:]
    cmn_ref[:, 0:M_CONV] = cm1_ref[...]
    cmn_ref[:, M_CONV:2 * M_CONV] = cm2_ref[...]
    cmn_ref[:, 2 * M_CONV:3 * M_CONV] = xbc

    u = u_ref[...].astype(F32)
    conv_l = (cbl_ref[...] + cwl_ref[0:1, :] * cl0_ref[...] + cwl_ref[1:2, :] * cl1_ref[...]
              + cwl_ref[2:3, :] * cl2_ref[...] + cwl_ref[3:4, :] * u)
    uc = _silu(conv_l)
    cln_ref[:, 0:L_INNER] = cl1_ref[...]
    cln_ref[:, L_INNER:2 * L_INNER] = cl2_ref[...]
    cln_ref[:, 2 * L_INNER:3 * L_INNER] = u
    scale = L_HDIM ** -0.5
    for h in range(L_HEADS):
        sl = slice(h * L_HDIM, (h + 1) * L_HDIM)
        ub = uc[:, sl].astype(BF16)
        q_ref[:, sl] = jnp.dot(ub, wq_ref[h], preferred_element_type=F32)
        k_ref[:, sl] = jnp.dot(ub, wk_ref[h], preferred_element_type=F32) * scale

    sm = sm_ref[...]
    dt = _softplus(sm + dtb_ref[...])
    dt_ref[...] = dt
    da_ref[...] = jnp.exp(dt * (-jnp.exp(alog_ref[...])))
    ig = pltpu.roll(sm + ib_ref[...], F_LANE - I_LANE, axis=1)
    logf = _log_sigmoid(sm + fb_ref[...])
    m_old = m_ref[...]
    m_new = jnp.maximum(logf + m_old, ig)
    mnew_ref[...] = m_new
    wr_ref[...] = jnp.exp(ig - m_new)
    keep_ref[...] = jnp.exp(logf + m_old - m_new)


def _sample_pre(pbig, psmall, conv_m, conv_l, m_lanes, cwm, cbm, cwl, cbl, wq, wk, dtb, alog, ib, fb):
    s = pbig.shape[0]
    f = lambda shape: jax.ShapeDtypeStruct(shape, F32)
    full2 = lambda shape: pl.BlockSpec(shape, lambda i: (0, 0))
    full3 = lambda shape: pl.BlockSpec(shape, lambda i: (0, 0, 0))
    state_row = lambda width, j: pl.BlockSpec((s, width), lambda i, j=j: (0, j))
    return pl.pallas_call(
        _sample_pre_kernel,
        grid=(1,),
        in_specs=[
            full2((s, M_CONV)),
            pl.BlockSpec((s, L_INNER), lambda i: (0, 6)),
            full2((s, LANES)),
            state_row(M_CONV, 0), state_row(M_CONV, 1), state_row(M_CONV, 2),
            state_row(L_INNER, 0), state_row(L_INNER, 1), state_row(L_INNER, 2),
            full2((s, LANES)),
            full2((CONV_W, M_CONV)), full2((1, M_CONV)),
            full2((CONV_W, L_INNER)), full2((1, L_INNER)),
            full3((L_HEADS, L_HDIM, L_HDIM)), full3((L_HEADS, L_HDIM, L_HDIM)),
            full2((1, LANES)), full2((1, LANES)), full2((1, LANES)), full2((1, LANES)),
        ],
        out_specs=[
            full2((s, M_INNER)), full2((s, M_INNER)), full2((s, L_INNER)), full2((s, L_INNER)),
            full2((s, LANES)), full2((s, LANES)), full2((s, LANES)), full2((s, LANES)), full2((s, LANES)),
            full2((s, (CONV_W - 1) * M_CONV)), full2((s, (CONV_W - 1) * L_INNER)),
        ],
        out_shape=[
            f((s, M_INNER)), f((s, M_INNER)), f((s, L_INNER)), f((s, L_INNER)),
            f((s, LANES)), f((s, LANES)), f((s, LANES)), f((s, LANES)), f((s, LANES)),
            f((s, (CONV_W - 1) * M_CONV)), f((s, (CONV_W - 1) * L_INNER)),
        ],
        compiler_params=_params("arbitrary"),
        name="sample_pre",
    )(pbig[:, 2 * PROJ_BLOCK:6 * PROJ_BLOCK], pbig, psmall, conv_m, conv_m, conv_m,
      conv_l, conv_l, conv_l, m_lanes, cwm, cbm, cwl, cbl, wq, wk, dtb, alog, ib, fb)


SSD_SB = 8
MLSTM_SB = 16


def _ssd_state_kernel(dt_ref, da_ref, x_ref, b_ref, c_ref, h_ref, hn_ref, y_ref, xs_ref):
    base = pl.program_id(0) * SSD_SB
    rows = lax.broadcasted_iota(jnp.int32, (LANES, LANES), 0)
    lanes = _lane_iota((LANES, LANES))
    diag = rows == lanes
    first_rows = rows < M_HDIM
    ones = jnp.ones((LANES, LANES), BF16)
    pairs_per_group = M_PAIRS // M_GROUPS

    def spread(s):
        for hp in range(M_PAIRS):
            xrow = jnp.broadcast_to(x_ref[s:s + 1, hp * LANES:(hp + 1) * LANES], (LANES, LANES))
            xdiag = jnp.where(diag, xrow, 0.0).astype(BF16)
            xs_ref[s % 2, hp] = jnp.dot(xdiag, ones, preferred_element_type=F32)

    acc = jnp.zeros((LANES, LANES), F32)
    spread(0)
    for s in range(SSD_SB):
        if s + 1 < SSD_SB:
            spread(s + 1)
        for g in range(M_GROUPS):
            brow = b_ref[s, g:g + 1, :]
            crow = c_ref[s, g:g + 1, :]
            for pp in range(pairs_per_group):
                hp = g * pairs_per_group + pp
                h0, h1 = 2 * hp, 2 * hp + 1
                dav = jnp.where(first_rows, da_ref[base + s, h0], da_ref[base + s, h1])
                dtv = jnp.where(first_rows, dt_ref[base + s, h0], dt_ref[base + s, h1])
                hn = dav * h_ref[s, hp] + (dtv * xs_ref[s % 2, hp]) * brow
                hn_ref[s, hp] = hn
                ysum = jnp.dot((hn * crow).astype(BF16), ones, preferred_element_type=F32)
                acc = jnp.where(lanes == s * M_PAIRS + hp, ysum, acc)
    y_ref[...] = acc.T


def _ssd_state(dt, da, xc, b3, c3, h):
    s = h.shape[0]
    smem = pl.BlockSpec(memory_space=pltpu.SMEM)
    return pl.pallas_call(
        _ssd_state_kernel,
        grid=(s // SSD_SB,),
        in_specs=[
            smem, smem,
            pl.BlockSpec((SSD_SB, M_INNER), lambda i: (i, 0)),
            pl.BlockSpec((SSD_SB, M_GROUPS, M_STATE), lambda i: (i, 0, 0)),
            pl.BlockSpec((SSD_SB, M_GROUPS, M_STATE), lambda i: (i, 0, 0)),
            pl.BlockSpec((SSD_SB, M_PAIRS, LANES, M_STATE), lambda i: (i, 0, 0, 0)),
        ],
        out_specs=[
            pl.BlockSpec((SSD_SB, M_PAIRS, LANES, M_STATE), lambda i: (i, 0, 0, 0)),
            pl.BlockSpec((SSD_SB * M_PAIRS, LANES), lambda i: (i, 0)),
        ],
        out_shape=[
            jax.ShapeDtypeStruct(h.shape, F32),
            jax.ShapeDtypeStruct((s * M_PAIRS, LANES), F32),
        ],
        scratch_shapes=[pltpu.VMEM((2, M_PAIRS, LANES, LANES), F32)],
        compiler_params=_params("parallel"),
        name="ssd_state",
    )(dt, da, xc, b3, c3, h)


def _mlstm_state_kernel(wr_ref, keep_ref, q_ref, k_ref, v_ref, c_ref, cn_ref, num_ref):
    base = pl.program_id(0) * MLSTM_SB
    qt = q_ref[...].T
    kt = k_ref[...].T
    for s in range(MLSTM_SB):
        for h in range(L_HEADS):
            r = s * L_HEADS + h
            wr = wr_ref[base + s, h]
            keep = keep_ref[base + s, h]
            c_old = c_ref[s, h]
            num_ref[r:r + 1, :] = jnp.sum(qt[:, r:r + 1] * c_old, axis=0, keepdims=True)
            cn_ref[s, h] = keep * c_old + (kt[:, r:r + 1] * wr) * v_ref[r:r + 1, :]


def _mlstm_state(wr, keep, q_rows, k_rows, v_rows, c):
    s = c.shape[0]
    smem = pl.BlockSpec(memory_space=pltpu.SMEM)
    rows = pl.BlockSpec((LANES, LANES), lambda i: (i, 0))
    return pl.pallas_call(
        _mlstm_state_kernel,
        grid=(s // MLSTM_SB,),
        in_specs=[
            smem, smem, rows, rows, rows,
            pl.BlockSpec((MLSTM_SB, L_HEADS, L_HDIM, L_HDIM), lambda i: (i, 0, 0, 0)),
        ],
        out_specs=[
            pl.BlockSpec((MLSTM_SB, L_HEADS, L_HDIM, L_HDIM), lambda i: (i, 0, 0, 0)),
            rows,
        ],
        out_shape=[
            jax.ShapeDtypeStruct(c.shape, F32),
            jax.ShapeDtypeStruct(q_rows.shape, F32),
        ],
        compiler_params=_params("parallel"),
        name="mlstm_state",
    )(wr, keep, q_rows, k_rows, v_rows, c)


def _sample_post_kernel(y_ref, xc_ref, z_ref, dvec_ref, mnw_ref,
                        q_ref, k_ref, v_ref, o_ref, n_ref, num_ref, wr_ref, keep_ref, mnew_ref, lnw_ref,
                        ya_ref, hb_ref, nn_ref):
    y = (y_ref[...] + dvec_ref[...] * xc_ref[...]) * _silu(z_ref[...].astype(F32))
    gw = M_INNER // M_GROUPS
    for g in range(M_GROUPS):
        sl = slice(g * gw, (g + 1) * gw)
        ya_ref[:, sl] = _rms(y[:, sl], mnw_ref[:, sl]).astype(BF16)

    wr = wr_ref[...]
    keep = keep_ref[...]
    floor = jnp.exp(-mnew_ref[...])
    for h in range(L_HEADS):
        ln = F_LANE + h
        sl = slice(h * L_HDIM, (h + 1) * L_HDIM)
        q = q_ref[:, sl]
        k = k_ref[:, sl]
        n_old = n_ref[:, sl]
        wrc = wr[:, ln:ln + 1]
        kpc = keep[:, ln:ln + 1]
        wgt = jnp.sum(q * k, axis=1, keepdims=True) * wrc
        num = wgt * v_ref[:, sl].astype(F32) + kpc * num_ref[:, sl]
        den = wgt + kpc * jnp.sum(q * n_old, axis=1, keepdims=True)
        hh = num / jnp.maximum(jnp.abs(den), floor[:, ln:ln + 1])
        gate = _sigmoid(o_ref[:, sl].astype(F32))
        hb_ref[:, sl] = (gate * _rms(hh, lnw_ref[:, sl])).astype(BF16)
        nn_ref[:, sl] = kpc * n_old + wrc * k


def _sample_post(y, xc, pbig, dvec, mnw, q, k, n_rows, num, wr, keep, mnew, lnw):
    s = y.shape[0]
    full = lambda shape: pl.BlockSpec(shape, lambda i: (0, 0))
    blk = lambda width, j: pl.BlockSpec((s, width), lambda i, j=j: (0, j))
    return pl.pallas_call(
        _sample_post_kernel,
        grid=(1,),
        in_specs=[
            full((s, M_INNER)), full((s, M_INNER)), blk(M_INNER, 0),
            full((1, M_INNER)), full((1, M_INNER)),
            full((s, L_INNER)), full((s, L_INNER)), blk(L_INNER, 7), blk(L_INNER, 8),
            full((s, L_INNER)), full((s, L_INNER)),
            full((s, LANES)), full((s, LANES)), full((s, LANES)), full((1, L_INNER)),
        ],
        out_specs=[full((s, M_INNER)), full((s, L_INNER)), full((s, L_INNER))],
        out_shape=[
            jax.ShapeDtypeStruct((s, M_INNER), BF16),
            jax.ShapeDtypeStruct((s, L_INNER), BF16),
            jax.ShapeDtypeStruct((s, L_INNER), F32),
        ],
        compiler_params=_params("arbitrary"),
        name="sample_post",
    )(y, xc, pbig, dvec, mnw, q, k, pbig, pbig, n_rows, num, wr, keep, mnew, lnw)


def _lanes(vec, first_lane):
    n = vec.shape[0]
    return jnp.pad(vec.astype(F32), (first_lane, LANES - first_lane - n)).reshape(1, LANES)


def _spread_matrix(first_lane, n_heads, width):
    r = lax.broadcasted_iota(jnp.int32, (2 * LANES, n_heads * width), 0) % LANES
    c = lax.broadcasted_iota(jnp.int32, (2 * LANES, n_heads * width), 1)
    return (r - first_lane == c // width).astype(BF16)


def _shift_matrix(n):
    r = lax.broadcasted_iota(jnp.int32, ((CONV_W - 1) * n, 2 * n), 0)
    c = lax.broadcasted_iota(jnp.int32, ((CONV_W - 1) * n, 2 * n), 1)
    return (c == n + r % n - (r // n + 1)).astype(BF16)


def kernel(x_prompt, x_sample, state_mamba_conv, state_mamba_ssm, state_mlstm_conv, state_mlstm_C, state_mlstm_n, state_mlstm_m, w_in, mamba_conv_w, mamba_conv_b, mamba_dt_bias, mamba_A_log, mamba_D, mamba_norm_w, w_branch_a, mlstm_conv_w, mlstm_conv_b, mlstm_wq, mlstm_wk, mlstm_i_bias, mlstm_f_bias, mlstm_norm_w, w_branch_b, w_out, norm_mix_w, norm_mlp_w, w_up, w_down, final_norm_w):
    depth = w_in.shape[0]
    assert depth == 1
    batch, seq, _ = x_prompt.shape
    n_samp, dec_seq, _ = x_sample.shape
    assert dec_seq == 1 and seq % CHUNK == 0 and seq >= SUBLANES
    assert n_samp % MLSTM_SB == 0 and n_samp % SSD_SB == 0
    n_chunks = seq // CHUNK
    l = 0

    w = w_in[l]
    c0 = M_INNER + M_CONV
    c1 = c0 + M_HEADS
    c2 = c1 + 3 * L_INNER
    c3 = c2 + 2 * L_HEADS
    w_big = jnp.concatenate([w[:, :c0], w[:, c1:c2], w[:, c3:]], axis=1).astype(BF16)
    w_small = jnp.concatenate(
        [w[:, c0:c1], w[:, c2:c3], jnp.zeros((D_MODEL, LANES - M_HEADS - 2 * L_HEADS), F32)],
        axis=1).astype(BF16)
    g_mix = norm_mix_w[l].reshape(1, D_MODEL)
    cwm = mamba_conv_w[l]
    cbm = mamba_conv_b[l].reshape(1, M_CONV)
    cwl = mlstm_conv_w[l]
    cbl = mlstm_conv_b[l].reshape(1, L_INNER)
    dtb = _lanes(mamba_dt_bias[l], DT_LANE)
    alog = _lanes(mamba_A_log[l], DT_LANE)
    ib = _lanes(mlstm_i_bias[l], I_LANE)
    fb = _lanes(mlstm_f_bias[l], F_LANE)
    dvec = jnp.repeat(mamba_D[l].astype(F32), M_HDIM).reshape(1, M_INNER)
    mnw = mamba_norm_w[l].reshape(1, M_INNER)
    lnw = mlstm_norm_w[l].reshape(1, L_INNER)
    wq = mlstm_wq[l].astype(BF16)
    wk = mlstm_wk[l].astype(BF16)
    wa = w_branch_a[l].astype(BF16)
    wb = w_branch_b[l].astype(BF16)
    wo = w_out[l].astype(BF16)
    wup = w_up[l].astype(BF16)
    wdn = w_down[l].astype(BF16)
    nmw = norm_mlp_w[l].reshape(1, D_MODEL)
    fw = final_norm_w.reshape(1, D_MODEL)
    e64 = _spread_matrix(DT_LANE, M_HEADS, M_HDIM)
    el = _spread_matrix(F_LANE, L_HEADS, L_HDIM)
    shift = _shift_matrix(CHUNK)

    xp = x_prompt.reshape(batch * seq, D_MODEL)
    pbig, psmall = _inproj(xp, g_mix, w_big, w_small, tm=2048)
    ya, p_ssm = _ssd_prompt(pbig, psmall, cwm[:, :M_INNER], cbm[:, :M_INNER], cwm[:, M_INNER:],
                            cbm[:, M_INNER:], dtb, alog, dvec, mnw, e64, shift, batch, n_chunks)
    hb, p_c, p_n, p_m = _mlstm_prompt(pbig, psmall, cwl, cbl, wq, wk, ib, fb, lnw, el, shift,
                                      batch, n_chunks)
    y_prompt = _tail(ya, hb, pbig, xp, wa, wb, wo, nmw, wup, wdn, fw, tm=512)
    p3 = pbig.reshape(batch, seq, BIG_WIDTH)
    p_conv_m = p3[:, seq - (CONV_W - 1):, 2 * PROJ_BLOCK:6 * PROJ_BLOCK].astype(F32)
    p_conv_l = p3[:, seq - (CONV_W - 1):, 6 * PROJ_BLOCK:7 * PROJ_BLOCK].astype(F32)
    p_ssm = p_ssm.reshape(batch, M_HEADS, M_HDIM, M_STATE)
    p_m = p_m[:, 0, F_LANE:F_LANE + L_HEADS]

    xs = x_sample.reshape(n_samp, D_MODEL)
    sbig, ssmall = _inproj(xs, g_mix, w_big, w_small, tm=n_samp)
    m_lanes = jnp.pad(state_mlstm_m[l], ((0, 0), (F_LANE, LANES - F_LANE - L_HEADS)))
    (xc, bcc, q, k, dt, da, wr, keep, mnew, s_conv_m, s_conv_l) = _sample_pre(
        sbig, ssmall, state_mamba_conv[l].reshape(n_samp, (CONV_W - 1) * M_CONV),
        state_mlstm_conv[l].reshape(n_samp, (CONV_W - 1) * L_INNER), m_lanes,
        cwm, cbm, cwl, cbl, wq, wk, dtb, alog, ib, fb)
    s_ssm, y_rows = _ssd_state(
        dt[:, :M_HEADS], da[:, :M_HEADS], xc,
        bcc[:, :M_GROUPS * M_STATE].reshape(n_samp, M_GROUPS, M_STATE),
        bcc[:, M_GROUPS * M_STATE:].reshape(n_samp, M_GROUPS, M_STATE),
        state_mamba_ssm[l].reshape(n_samp, M_PAIRS, LANES, M_STATE))
    s_c, num_rows = _mlstm_state(
        wr[:, F_LANE:F_LANE + L_HEADS], keep[:, F_LANE:F_LANE + L_HEADS],
        q.reshape(n_samp * L_HEADS, L_HDIM), k.reshape(n_samp * L_HEADS, L_HDIM),
        sbig[:, 7 * PROJ_BLOCK:8 * PROJ_BLOCK].astype(F32).reshape(n_samp * L_HEADS, L_HDIM),
        state_mlstm_C[l])
    ya_s, hb_s, s_n = _sample_post(
        y_rows.reshape(n_samp, M_INNER), xc, sbig, dvec, mnw, q, k,
        state_mlstm_n[l].reshape(n_samp, L_INNER), num_rows.reshape(n_samp, L_INNER),
        wr, keep, mnew, lnw)
    y_sample = _tail(ya_s, hb_s, sbig, xs, wa, wb, wo, nmw, wup, wdn, fw, tm=n_samp)

    lead = lambda a: a[None]
    return (
        y_prompt.reshape(batch, seq, D_MODEL),
        y_sample.reshape(n_samp, 1, D_MODEL),
        lead(p_conv_m), lead(p_ssm), lead(p_conv_l), lead(p_c), lead(p_n), lead(p_m),
        lead(s_conv_m.reshape(n_samp, CONV_W - 1, M_CONV)),
        lead(s_ssm.reshape(n_samp, M_HEADS, M_HDIM, M_STATE)),
        lead(s_conv_l.reshape(n_samp, CONV_W - 1, L_INNER)),
        lead(s_c), lead(s_n.reshape(n_samp, L_HEADS, L_HDIM)), lead(mnew[:, F_LANE:F_LANE + L_HEADS]),
    )
```

```python
import jax
import jax.numpy as jnp
from jax import lax
from jax.experimental import pallas as pl
from jax.experimental.pallas import tpu as pltpu

F32 = jnp.float32
BF16 = jnp.bfloat16

D_MODEL = 1024
M_INNER = 2048
M_HEADS = 32
M_HDIM = 64
M_GROUPS = 8
M_PAIRS = M_HEADS // 2
M_STATE = 128
M_CONV = 4096
L_INNER = 1024
L_HEADS = 8
L_HDIM = 128
D_FF = 4096
CONV_W = 4
CHUNK = 128
EPS = 1e-6

LANES = 128
SUBLANES = 8
PROJ_BLOCK = 1024
N_PROJ_BLOCKS = 11
BIG_WIDTH = PROJ_BLOCK * N_PROJ_BLOCKS
DT_LANE = 0
I_LANE = 32
F_LANE = 40
VMEM_LIMIT = 56 * 1024 * 1024

HIGHEST = lax.Precision.HIGHEST
LOG2_E = 1.4426950408889634
NT_DIMS = (((1,), (1,)), ((), ()))
TN_DIMS = (((0,), (0,)), ((), ()))


def _params(*sem):
    return pltpu.CompilerParams(dimension_semantics=sem, vmem_limit_bytes=VMEM_LIMIT)


def _sigmoid(x):
    return 0.5 * jnp.tanh(0.5 * x) + 0.5


def _silu(x):
    h = 0.5 * x
    return h + h * jnp.tanh(h)


def _log1p_exp_neg_abs(x):
    e = jnp.exp(-jnp.abs(x))
    u = 1.0 + e
    return jnp.where(u == 1.0, e, jnp.log(u) * (e / (u - 1.0)))


def _softplus(x):
    return jnp.maximum(x, 0.0) + _log1p_exp_neg_abs(x)


def _log_sigmoid(x):
    return jnp.minimum(x, 0.0) - _log1p_exp_neg_abs(x)


def _rms(x, w):
    return x * lax.rsqrt(jnp.mean(x * x, axis=-1, keepdims=True) + EPS) * w


def _lane_iota(shape):
    return lax.broadcasted_iota(jnp.int32, shape, len(shape) - 1)


def _tri(n):
    r = lax.broadcasted_iota(jnp.int32, (n, n), 0)
    c = lax.broadcasted_iota(jnp.int32, (n, n), 1)
    return r >= c


def _split3(a):
    hi = a.astype(BF16)
    r1 = a - hi.astype(F32)
    mid = r1.astype(BF16)
    lo = (r1 - mid.astype(F32)).astype(BF16)
    return hi, mid, lo


def _cumsum_rows(causal, a):
    tri01 = causal.astype(F32).astype(BF16)
    return jnp.dot(jnp.concatenate([tri01] * 3, axis=1), jnp.concatenate(_split3(a), axis=0),
                   preferred_element_type=F32)


def _spread(a, e2_ref):
    hi, mid, _ = _split3(a)
    return jnp.dot(jnp.concatenate([hi, mid], axis=1), e2_ref[...], preferred_element_type=F32)


CONV_COLS = 512


def _conv_silu(x_ref, win_ref, shift_ref, w_ref, b_ref, out_ref):
    n, width = x_ref.shape
    tile = 16
    win_ref[n:2 * n, :] = x_ref[...]
    for c0 in range(0, width, CONV_COLS):
        cs = slice(c0, c0 + CONV_COLS)
        sh = jnp.dot(shift_ref[...], win_ref[:, cs], preferred_element_type=F32)
        y = b_ref[:, cs] + w_ref[CONV_W - 1:CONV_W, cs] * x_ref[:, cs].astype(F32)
        for back in range(1, CONV_W):
            y = y + w_ref[CONV_W - 1 - back:CONV_W - back, cs] * sh[(back - 1) * n:back * n]
        out_ref[:, cs] = _silu(y).astype(out_ref.dtype)
    win_ref[n - tile:n, :] = x_ref[n - tile:n, :]


def _inproj_kernel(x_ref, g_ref, w_ref, ws_ref, o_ref, os_ref, xn_ref):
    @pl.when(pl.program_id(1) == 0)
    def _():
        xn_ref[...] = _rms(x_ref[...], g_ref[...]).astype(BF16)
        os_ref[...] = jnp.dot(xn_ref[...], ws_ref[...], preferred_element_type=F32)

    o_ref[...] = jnp.dot(xn_ref[...], w_ref[...], preferred_element_type=F32).astype(BF16)


def _inproj(x, g, w_big, w_small, tm):
    m = x.shape[0]
    tm = min(tm, m)
    return pl.pallas_call(
        _inproj_kernel,
        grid=(m // tm, N_PROJ_BLOCKS),
        in_specs=[
            pl.BlockSpec((tm, D_MODEL), lambda i, j: (i, 0)),
            pl.BlockSpec((1, D_MODEL), lambda i, j: (0, 0)),
            pl.BlockSpec((D_MODEL, PROJ_BLOCK), lambda i, j: (0, j)),
            pl.BlockSpec((D_MODEL, LANES), lambda i, j: (0, 0)),
        ],
        out_specs=[
            pl.BlockSpec((tm, PROJ_BLOCK), lambda i, j: (i, j)),
            pl.BlockSpec((tm, LANES), lambda i, j: (i, 0)),
        ],
        out_shape=[
            jax.ShapeDtypeStruct((m, BIG_WIDTH), BF16),
            jax.ShapeDtypeStruct((m, LANES), F32),
        ],
        scratch_shapes=[pltpu.VMEM((tm, D_MODEL), BF16)],
        compiler_params=_params("parallel", "arbitrary"),
        name="in_proj",
    )(x, g, w_big, w_small)


def _ssd_kernel(x_ref, bc_ref, sm_ref, cwx_ref, cbx_ref, cwbc_ref, cbbc_ref,
                dtb_ref, alog_ref, dvec_ref, e64_ref, shift_ref,
                y_ref, h_ref, winx_ref, winbc_ref, xc_scr, bcc_scr):
    n = x_ref.shape[0]

    @pl.when(pl.program_id(1) == 0)
    def _():
        h_ref[...] = jnp.zeros_like(h_ref)
        winx_ref[0:n, :] = jnp.zeros((n, M_INNER), BF16)
        winbc_ref[0:n, :] = jnp.zeros((n, M_INNER), BF16)

    _conv_silu(x_ref, winx_ref, shift_ref, cwx_ref, cbx_ref, xc_scr)
    _conv_silu(bc_ref, winbc_ref, shift_ref, cwbc_ref, cbbc_ref, bcc_scr)
    xc = xc_scr[...]
    bcc = bcc_scr[...]

    lane = _lane_iota((n, LANES))
    dt = _softplus(sm_ref[...] + dtb_ref[...])
    da = jnp.where(lane < M_HEADS, dt * (-LOG2_E * jnp.exp(alog_ref[...])), 0.0)
    causal = _tri(n)
    acum = _cumsum_rows(causal, da)
    acum_t = acum.T
    last = acum[n - 1:n, :]
    exp_last = jnp.exp2(last)
    dt_x = _spread(dt, e64_ref)
    wr_x = _spread(jnp.exp2(last - acum), e64_ref)
    ea_x = _spread(jnp.exp2(acum), e64_ref)

    xdt = xc * dt_x
    xw = (xdt * wr_x).astype(BF16)
    xb = xdt.astype(BF16)
    low_half = jnp.bitwise_and(_lane_iota((n, M_INNER)), LANES - 1) < M_HDIM
    zero = jnp.zeros((), BF16)
    x_lo = jnp.where(low_half, xb, zero)
    x_hi = jnp.where(low_half, zero, xb)
    first_rows = lax.broadcasted_iota(jnp.int32, (LANES, LANES), 0) < M_HDIM

    def weights(cb, h):
        seg = acum[:, h:h + 1] - acum_t[h:h + 1, :]
        return (cb * jnp.exp2(jnp.where(causal, seg, -jnp.inf))).astype(BF16)

    for g in range(M_GROUPS):
        bg = bcc[:, g * M_STATE:(g + 1) * M_STATE]
        cg = bcc[:, (M_GROUPS + g) * M_STATE:(M_GROUPS + g + 1) * M_STATE]
        cb = lax.dot_general(cg, bg, NT_DIMS, preferred_element_type=F32)
        for pp in range(M_PAIRS // M_GROUPS):
            hp = g * (M_PAIRS // M_GROUPS) + pp
            h0, h1 = 2 * hp, 2 * hp + 1
            sl = slice(hp * LANES, (hp + 1) * LANES)
            hs = h_ref[0, hp]
            y = jnp.dot(weights(cb, h0), x_lo[:, sl], preferred_element_type=F32)
            y = y + jnp.dot(weights(cb, h1), x_hi[:, sl], preferred_element_type=F32)
            ys = lax.dot_general(cg, hs.astype(BF16), NT_DIMS, preferred_element_type=F32)
            y_ref[:, sl] = (y + ea_x[:, sl] * ys + dvec_ref[:, sl] * xc[:, sl]).astype(BF16)
            el = jnp.where(first_rows, exp_last[:, h0:h0 + 1], exp_last[:, h1:h1 + 1])
            h_ref[0, hp] = el * hs + lax.dot_general(xw[:, sl], bg, TN_DIMS, preferred_element_type=F32)


def _ssd_prompt(pbig, psmall, cwx, cbx, cwbc, cbbc, dtb, alog, dvec, e64, shift, batch, n_chunks):
    m = pbig.shape[0]
    row = lambda b, c: b * n_chunks + c
    const = lambda b, c: (0, 0)
    return pl.pallas_call(
        _ssd_kernel,
        grid=(batch, n_chunks),
        in_specs=[
            pl.BlockSpec((CHUNK, M_INNER), lambda b, c: (row(b, c), 1)),
            pl.BlockSpec((CHUNK, M_INNER), lambda b, c: (row(b, c), 2)),
            pl.BlockSpec((CHUNK, LANES), lambda b, c: (row(b, c), 0)),
            pl.BlockSpec((CONV_W, M_INNER), const),
            pl.BlockSpec((1, M_INNER), const),
            pl.BlockSpec((CONV_W, M_INNER), const),
            pl.BlockSpec((1, M_INNER), const),
            pl.BlockSpec((1, LANES), const),
            pl.BlockSpec((1, LANES), const),
            pl.BlockSpec((1, M_INNER), const),
            pl.BlockSpec((2 * LANES, M_INNER), const),
            pl.BlockSpec(((CONV_W - 1) * CHUNK, 2 * CHUNK), const),
        ],
        out_specs=[
            pl.BlockSpec((CHUNK, M_INNER), lambda b, c: (row(b, c), 0)),
            pl.BlockSpec((1, M_PAIRS, LANES, M_STATE), lambda b, c: (b, 0, 0, 0)),
        ],
        out_shape=[
            jax.ShapeDtypeStruct((m, M_INNER), BF16),
            jax.ShapeDtypeStruct((batch, M_PAIRS, LANES, M_STATE), F32),
        ],
        scratch_shapes=[
            pltpu.VMEM((2 * CHUNK, M_INNER), BF16),
            pltpu.VMEM((2 * CHUNK, M_INNER), BF16),
            pltpu.VMEM((CHUNK, M_INNER), F32),
            pltpu.VMEM((CHUNK, M_INNER), BF16),
        ],
        compiler_params=_params("parallel", "arbitrary"),
        name="ssd_prompt",
    )(pbig, pbig, psmall, cwx, cbx, cwbc, cbbc, dtb, alog, dvec, e64, shift)


def _mlstm_kernel(u_ref, v_ref, sm_ref, cw_ref, cb_ref, wq_ref, wk_ref,
                  ib_ref, fb_ref, el_ref, shift_ref,
                  hh_ref, c_ref, n_ref, m_ref, win_ref, uc_scr):
    n = u_ref.shape[0]

    @pl.when(pl.program_id(1) == 0)
    def _():
        c_ref[...] = jnp.zeros_like(c_ref)
        n_ref[...] = jnp.zeros_like(n_ref)
        m_ref[...] = jnp.zeros_like(m_ref)
        win_ref[0:n, :] = jnp.zeros((n, L_INNER), BF16)

    _conv_silu(u_ref, win_ref, shift_ref, cw_ref, cb_ref, uc_scr)
    uc = uc_scr[...]

    sm = sm_ref[...]
    lane = _lane_iota((n, LANES))
    gate_lanes = (lane >= F_LANE) & (lane < F_LANE + L_HEADS)
    ig = pltpu.roll(sm + ib_ref[...], F_LANE - I_LANE, axis=1)
    logf = jnp.where(gate_lanes, _log_sigmoid(sm + fb_ref[...]), 0.0)
    causal = _tri(n)
    bcum = _cumsum_rows(causal, logf)
    m_old = m_ref[0]
    bl = bcum[n - 1:n, :]
    src = bl - bcum + ig
    m_new = jnp.maximum(bl + m_old, jnp.max(src, axis=0, keepdims=True))
    keep = jnp.exp(bl + m_old - m_new)
    bcum_t = bcum.T
    ig_t = ig.T
    wr_x = _spread(jnp.exp(src - m_new), el_ref)
    scale = L_HDIM ** -0.5
    c_olds = [c_ref[0, h] for h in range(L_HEADS)]
    n_olds = [n_ref[0, h:h + 1, :] for h in range(L_HEADS)]

    heads = range(L_HEADS)
    lanes_of = [slice(h * L_HDIM, (h + 1) * L_HDIM) for h in heads]
    qs, ks, qbs, kbs = [], [], [], []
    for h in heads:
        ub = uc[:, lanes_of[h]]
        qs.append(jnp.dot(ub, wq_ref[h], preferred_element_type=F32))
        ks.append(jnp.dot(ub, wk_ref[h], preferred_element_type=F32) * scale)
    logds, inters, rowmax = [], [], []
    for h in heads:
        ln = F_LANE + h
        bcol = jnp.broadcast_to(bcum[:, ln:ln + 1], (n, n))
        logd = jnp.where(causal, bcol - bcum_t[ln:ln + 1, :] + ig_t[ln:ln + 1, :], -jnp.inf)
        logds.append(logd)
        inters.append(bcol + m_old[:, ln:ln + 1])
        rowmax.append(jnp.max(logd, axis=1, keepdims=True))
    ss, scs, floors = [], [], []
    for h in heads:
        qbs.append(qs[h].astype(BF16))
        kbs.append(ks[h].astype(BF16))
        m_s = jnp.maximum(inters[h], rowmax[h])
        dm = jnp.exp(logds[h] - m_s)
        scs.append(jnp.exp(inters[h] - m_s))
        floors.append(jnp.exp(-m_s))
        ss.append(lax.dot_general(qbs[h], kbs[h], NT_DIMS, preferred_element_type=F32) * dm)
    nums, dens = [], []
    for h in heads:
        vb = v_ref[:, lanes_of[h]]
        num = jnp.dot(ss[h].astype(BF16), vb, preferred_element_type=F32)
        num = num + scs[h] * jnp.dot(qbs[h], c_olds[h].astype(BF16), preferred_element_type=F32)
        nums.append(num)
        dens.append(jnp.sum(ss[h], axis=1, keepdims=True)
                    + scs[h] * jnp.sum(qs[h] * n_olds[h], axis=1, keepdims=True))
    for h in heads:
        hh = nums[h] / jnp.maximum(jnp.abs(dens[h]), floors[h])
        hh_ref[:, lanes_of[h]] = hh.astype(BF16)
    for h in heads:
        ln = F_LANE + h
        kw = ks[h] * wr_x[:, lanes_of[h]]
        kp = keep[:, ln:ln + 1]
        c_ref[0, h] = kp * c_olds[h] + lax.dot_general(
            kw.astype(BF16), v_ref[:, lanes_of[h]], TN_DIMS, preferred_element_type=F32)
        n_ref[0, h:h + 1, :] = kp * n_olds[h] + jnp.sum(kw, axis=0, keepdims=True)
    m_ref[0] = m_new


def _mlstm_prompt(pbig, psmall, cw, cb, wq, wk, ib, fb, el, shift, batch, n_chunks):
    m = pbig.shape[0]
    row = lambda b, c: b * n_chunks + c
    const2 = lambda b, c: (0, 0)
    return pl.pallas_call(
        _mlstm_kernel,
        grid=(batch, n_chunks),
        in_specs=[
            pl.BlockSpec((CHUNK, L_INNER), lambda b, c: (row(b, c), 6)),
            pl.BlockSpec((CHUNK, L_INNER), lambda b, c: (row(b, c), 7)),
            pl.BlockSpec((CHUNK, LANES), lambda b, c: (row(b, c), 0)),
            pl.BlockSpec((CONV_W, L_INNER), const2),
            pl.BlockSpec((1, L_INNER), const2),
            pl.BlockSpec((L_HEADS, L_HDIM, L_HDIM), lambda b, c: (0, 0, 0)),
            pl.BlockSpec((L_HEADS, L_HDIM, L_HDIM), lambda b, c: (0, 0, 0)),
            pl.BlockSpec((1, LANES), const2),
            pl.BlockSpec((1, LANES), const2),
            pl.BlockSpec((2 * LANES, L_INNER), const2),
            pl.BlockSpec(((CONV_W - 1) * CHUNK, 2 * CHUNK), const2),
        ],
        out_specs=[
            pl.BlockSpec((CHUNK, L_INNER), lambda b, c: (row(b, c), 0)),
            pl.BlockSpec((1, L_HEADS, L_HDIM, L_HDIM), lambda b, c: (b, 0, 0, 0)),
            pl.BlockSpec((1, L_HEADS, L_HDIM), lambda b, c: (b, 0, 0)),
            pl.BlockSpec((1, 1, LANES), lambda b, c: (b, 0, 0)),
        ],
        out_shape=[
            jax.ShapeDtypeStruct((m, L_INNER), BF16),
            jax.ShapeDtypeStruct((batch, L_HEADS, L_HDIM, L_HDIM), F32),
            jax.ShapeDtypeStruct((batch, L_HEADS, L_HDIM), F32),
            jax.ShapeDtypeStruct((batch, 1, LANES), F32),
        ],
        scratch_shapes=[
            pltpu.VMEM((2 * CHUNK, L_INNER), BF16),
            pltpu.VMEM((CHUNK, L_INNER), BF16),
        ],
        compiler_params=_params("parallel", "arbitrary"),
        name="mlstm_prompt",
    )(pbig, pbig, psmall, cw, cb, wq, wk, ib, fb, el, shift)


def _tail_kernel(ys_ref, hh_ref, z_ref, o_ref, ga_ref, gb_ref, x_ref, mnw_ref, lnw_ref,
                 wa_ref, wb_ref, wo_ref, nmw_ref, wup_ref, wdn_ref, fw_ref, y_ref, ya_scr, hb_scr):
    gw = M_INNER // M_GROUPS
    for g in range(M_GROUPS):
        sl = slice(g * gw, (g + 1) * gw)
        yg = ys_ref[:, sl].astype(F32) * _silu(z_ref[:, sl].astype(F32))
        ya_scr[:, sl] = _rms(yg, mnw_ref[:, sl]).astype(BF16)
    for h in range(L_HEADS):
        sl = slice(h * L_HDIM, (h + 1) * L_HDIM)
        gate = _sigmoid(o_ref[:, sl].astype(F32))
        hb_scr[:, sl] = (gate * _rms(hh_ref[:, sl].astype(F32), lnw_ref[:, sl])).astype(BF16)
    a = jnp.dot(ya_scr[...], wa_ref[...], preferred_element_type=F32)
    b = jnp.dot(hb_scr[...], wb_ref[...], preferred_element_type=F32)
    t = _sigmoid(ga_ref[...].astype(F32)) * a + _sigmoid(gb_ref[...].astype(F32)) * b
    x1 = x_ref[...] + jnp.dot(t.astype(BF16), wo_ref[...], preferred_element_type=F32)
    hn = _rms(x1, nmw_ref[...]).astype(BF16)
    acc = x1
    for c in range(D_FF // PROJ_BLOCK):
        sl = slice(c * PROJ_BLOCK, (c + 1) * PROJ_BLOCK)
        up = jnp.dot(hn, wup_ref[:, sl], preferred_element_type=F32)
        act = jnp.square(jnp.maximum(up, 0.0)).astype(BF16)
        acc = acc + jnp.dot(act, wdn_ref[sl, :], preferred_element_type=F32)
    y_ref[...] = _rms(acc, fw_ref[...])


def _tail(ys, hh, pbig, x, mnw, lnw, wa, wb, wo, nmw, wup, wdn, fw, tm):
    m = x.shape[0]
    tm = min(tm, m)
    rows = lambda i: (i, 0)
    const = lambda i: (0, 0)

    def resident(shape):
        return pl.BlockSpec(shape, const, pipeline_mode=pl.Buffered(1))

    return pl.pallas_call(
        _tail_kernel,
        grid=(m // tm,),
        in_specs=[
            pl.BlockSpec((tm, M_INNER), rows),
            pl.BlockSpec((tm, L_INNER), rows),
            pl.BlockSpec((tm, M_INNER), rows),
            pl.BlockSpec((tm, PROJ_BLOCK), lambda i: (i, 8)),
            pl.BlockSpec((tm, PROJ_BLOCK), lambda i: (i, 9)),
            pl.BlockSpec((tm, PROJ_BLOCK), lambda i: (i, 10)),
            pl.BlockSpec((tm, D_MODEL), rows),
            resident((1, M_INNER)),
            resident((1, L_INNER)),
            resident((M_INNER, D_MODEL)),
            resident((L_INNER, D_MODEL)),
            resident((D_MODEL, D_MODEL)),
            resident((1, D_MODEL)),
            resident((D_MODEL, D_FF)),
            resident((D_FF, D_MODEL)),
            resident((1, D_MODEL)),
        ],
        out_specs=pl.BlockSpec((tm, D_MODEL), rows),
        out_shape=jax.ShapeDtypeStruct((m, D_MODEL), F32),
        scratch_shapes=[pltpu.VMEM((tm, M_INNER), BF16), pltpu.VMEM((tm, L_INNER), BF16)],
        compiler_params=_params("parallel"),
        name="tail",
    )(ys, hh, pbig, pbig, pbig, pbig, x, mnw, lnw, wa, wb, wo, nmw, wup, wdn, fw)


def _sample_pre_kernel(xbc_ref, u_ref, sm_ref, cm0_ref, cm1_ref, cm2_ref, cl0_ref, cl1_ref, cl2_ref,
                       m_ref, cwm_ref, cbm_ref, cwl_ref, cbl_ref, wq_ref, wk_ref,
                       dtb_ref, alog_ref, ib_ref, fb_ref,
                       xc_ref, bcc_ref, q_ref, k_ref, dt_ref, da_ref, wr_ref, keep_ref, mnew_ref,
                       cmn_ref, cln_ref):
    xbc = xbc_ref[...].astype(F32)
    conv_m = (cbm_ref[...] + cwm_ref[0:1, :] * cm0_ref[...] + cwm_ref[1:2, :] * cm1_ref[...]
              + cwm_ref[2:3, :] * cm2_ref[...] + cwm_ref[3:4, :] * xbc)
    act = _silu(conv_m)
    xc_ref[...] = act[:, :M_INNER]
    bcc_ref[...] = act[:, M_INNER:]
    cmn_ref[:, 0:M_CONV] = cm1_ref[...]
    cmn_ref[:, M_CONV:2 * M_CONV] = cm2_ref[...]
    cmn_ref[:, 2 * M_CONV:3 * M_CONV] = xbc

    u = u_ref[...].astype(F32)
    conv_l = (cbl_ref[...] + cwl_ref[0:1, :] * cl0_ref[...] + cwl_ref[1:2, :] * cl1_ref[...]
              + cwl_ref[2:3, :] * cl2_ref[...] + cwl_ref[3:4, :] * u)
    uc = _silu(conv_l)
    cln_ref[:, 0:L_INNER] = cl1_ref[...]
    cln_ref[:, L_INNER:2 * L_INNER] = cl2_ref[...]
    cln_ref[:, 2 * L_INNER:3 * L_INNER] = u
    scale = L_HDIM ** -0.5
    for h in range(L_HEADS):
        sl = slice(h * L_HDIM, (h + 1) * L_HDIM)
        ub = uc[:, sl].astype(BF16)
        q_ref[:, sl] = jnp.dot(ub, wq_ref[h], preferred_element_type=F32)
        k_ref[:, sl] = jnp.dot(ub, wk_ref[h], preferred_element_type=F32) * scale

    sm = sm_ref[...]
    dt = _softplus(sm + dtb_ref[...])
    dt_ref[...] = dt
    da_ref[...] = jnp.exp(dt * (-jnp.exp(alog_ref[...])))
    ig = pltpu.roll(sm + ib_ref[...], F_LANE - I_LANE, axis=1)
    logf = _log_sigmoid(sm + fb_ref[...])
    m_old = m_ref[...]
    m_new = jnp.maximum(logf + m_old, ig)
    mnew_ref[...] = m_new
    wr_ref[...] = jnp.exp(ig - m_new)
    keep_ref[...] = jnp.exp(logf + m_old - m_new)


def _sample_pre(pbig, psmall, conv_m, conv_l, m_lanes, cwm, cbm, cwl, cbl, wq, wk, dtb, alog, ib, fb):
    s = pbig.shape[0]
    f = lambda shape: jax.ShapeDtypeStruct(shape, F32)
    full2 = lambda shape: pl.BlockSpec(shape, lambda i: (0, 0))
    full3 = lambda shape: pl.BlockSpec(shape, lambda i: (0, 0, 0))
    state_row = lambda width, j: pl.BlockSpec((s, width), lambda i, j=j: (0, j))
    return pl.pallas_call(
        _sample_pre_kernel,
        grid=(1,),
        in_specs=[
            full2((s, M_CONV)),
            pl.BlockSpec((s, L_INNER), lambda i: (0, 6)),
            full2((s, LANES)),
            state_row(M_CONV, 0), state_row(M_CONV, 1), state_row(M_CONV, 2),
            state_row(L_INNER, 0), state_row(L_INNER, 1), state_row(L_INNER, 2),
            full2((s, LANES)),
            full2((CONV_W, M_CONV)), full2((1, M_CONV)),
            full2((CONV_W, L_INNER)), full2((1, L_INNER)),
            full3((L_HEADS, L_HDIM, L_HDIM)), full3((L_HEADS, L_HDIM, L_HDIM)),
            full2((1, LANES)), full2((1, LANES)), full2((1, LANES)), full2((1, LANES)),
        ],
        out_specs=[
            full2((s, M_INNER)), full2((s, M_INNER)), full2((s, L_INNER)), full2((s, L_INNER)),
            full2((s, LANES)), full2((s, LANES)), full2((s, LANES)), full2((s, LANES)), full2((s, LANES)),
            full2((s, (CONV_W - 1) * M_CONV)), full2((s, (CONV_W - 1) * L_INNER)),
        ],
        out_shape=[
            f((s, M_INNER)), f((s, M_INNER)), f((s, L_INNER)), f((s, L_INNER)),
            f((s, LANES)), f((s, LANES)), f((s, LANES)), f((s, LANES)), f((s, LANES)),
            f((s, (CONV_W - 1) * M_CONV)), f((s, (CONV_W - 1) * L_INNER)),
        ],
        compiler_params=_params("arbitrary"),
        name="sample_pre",
    )(pbig[:, 2 * PROJ_BLOCK:6 * PROJ_BLOCK], pbig, psmall, conv_m, conv_m, conv_m,
      conv_l, conv_l, conv_l, m_lanes, cwm, cbm, cwl, cbl, wq, wk, dtb, alog, ib, fb)


SSD_SB = 8
MLSTM_SB = 16


def _ssd_state_kernel(dt_ref, da_ref, x_ref, b_ref, c_ref, h_ref, hn_ref, y_ref, xs_ref):
    base = pl.program_id(0) * SSD_SB
    rows = lax.broadcasted_iota(jnp.int32, (LANES, LANES), 0)
    lanes = _lane_iota((LANES, LANES))
    diag = rows == lanes
    first_rows = rows < M_HDIM
    ones = jnp.ones((LANES, LANES), BF16)
    pairs_per_group = M_PAIRS // M_GROUPS

    def spread(s):
        for hp in range(M_PAIRS):
            xrow = jnp.broadcast_to(x_ref[s:s + 1, hp * LANES:(hp + 1) * LANES], (LANES, LANES))
            xdiag = jnp.where(diag, xrow, 0.0).astype(BF16)
            xs_ref[s % 2, hp] = jnp.dot(xdiag, ones, preferred_element_type=F32)

    acc = jnp.zeros((LANES, LANES), F32)
    spread(0)
    for s in range(SSD_SB):
        if s + 1 < SSD_SB:
            spread(s + 1)
        for g in range(M_GROUPS):
            brow = b_ref[s, g:g + 1, :]
            crow = c_ref[s, g:g + 1, :]
            for pp in range(pairs_per_group):
                hp = g * pairs_per_group + pp
                h0, h1 = 2 * hp, 2 * hp + 1
                dav = jnp.where(first_rows, da_ref[base + s, h0], da_ref[base + s, h1])
                dtv = jnp.where(first_rows, dt_ref[base + s, h0], dt_ref[base + s, h1])
                hn = dav * h_ref[s, hp] + (dtv * xs_ref[s % 2, hp]) * brow
                hn_ref[s, hp] = hn
                ysum = jnp.dot((hn * crow).astype(BF16), ones, preferred_element_type=F32)
                acc = jnp.where(lanes == s * M_PAIRS + hp, ysum, acc)
    y_ref[...] = acc.T


def _ssd_state(dt, da, xc, b3, c3, h):
    s = h.shape[0]
    smem = pl.BlockSpec(memory_space=pltpu.SMEM)
    return pl.pallas_call(
        _ssd_state_kernel,
        grid=(s // SSD_SB,),
        in_specs=[
            smem, smem,
            pl.BlockSpec((SSD_SB, M_INNER), lambda i: (i, 0)),
            pl.BlockSpec((SSD_SB, M_GROUPS, M_STATE), lambda i: (i, 0, 0)),
            pl.BlockSpec((SSD_SB, M_GROUPS, M_STATE), lambda i: (i, 0, 0)),
            pl.BlockSpec((SSD_SB, M_PAIRS, LANES, M_STATE), lambda i: (i, 0, 0, 0)),
        ],
        out_specs=[
            pl.BlockSpec((SSD_SB, M_PAIRS, LANES, M_STATE), lambda i: (i, 0, 0, 0)),
            pl.BlockSpec((SSD_SB * M_PAIRS, LANES), lambda i: (i, 0)),
        ],
        out_shape=[
            jax.ShapeDtypeStruct(h.shape, F32),
            jax.ShapeDtypeStruct((s * M_PAIRS, LANES), F32),
        ],
        scratch_shapes=[pltpu.VMEM((2, M_PAIRS, LANES, LANES), F32)],
        compiler_params=_params("parallel"),
        name="ssd_state",
    )(dt, da, xc, b3, c3, h)


def _mlstm_state_kernel(wr_ref, keep_ref, q_ref, k_ref, v_ref, c_ref, cn_ref, num_ref):
    base = pl.program_id(0) * MLSTM_SB
    qt = q_ref[...].T
    kt = k_ref[...].T
    for s in range(MLSTM_SB):
        for h in range(L_HEADS):
            r = s * L_HEADS + h
            wr = wr_ref[base + s, h]
            keep = keep_ref[base + s, h]
            c_old = c_ref[s, h]
            num_ref[r:r + 1, :] = jnp.sum(qt[:, r:r + 1] * c_old, axis=0, keepdims=True)
            cn_ref[s, h] = keep * c_old + (kt[:, r:r + 1] * wr) * v_ref[r:r + 1, :]


def _mlstm_state(wr, keep, q_rows, k_rows, v_rows, c):
    s = c.shape[0]
    smem = pl.BlockSpec(memory_space=pltpu.SMEM)
    rows = pl.BlockSpec((LANES, LANES), lambda i: (i, 0))
    return pl.pallas_call(
        _mlstm_state_kernel,
        grid=(s // MLSTM_SB,),
        in_specs=[
            smem, smem, rows, rows, rows,
            pl.BlockSpec((MLSTM_SB, L_HEADS, L_HDIM, L_HDIM), lambda i: (i, 0, 0, 0)),
        ],
        out_specs=[
            pl.BlockSpec((MLSTM_SB, L_HEADS, L_HDIM, L_HDIM), lambda i: (i, 0, 0, 0)),
            rows,
        ],
        out_shape=[
            jax.ShapeDtypeStruct(c.shape, F32),
            jax.ShapeDtypeStruct(q_rows.shape, F32),
        ],
        compiler_params=_params("parallel"),
        name="mlstm_state",
    )(wr, keep, q_rows, k_rows, v_rows, c)


def _sample_post_kernel(y_ref, xc_ref, dvec_ref,
                        q_ref, k_ref, v_ref, n_ref, num_ref, wr_ref, keep_ref, mnew_ref,
                        ys_ref, hh_ref, nn_ref):
    ys_ref[...] = (y_ref[...] + dvec_ref[...] * xc_ref[...]).astype(BF16)

    wr = wr_ref[...]
    keep = keep_ref[...]
    floor = jnp.exp(-mnew_ref[...])
    for h in range(L_HEADS):
        ln = F_LANE + h
        sl = slice(h * L_HDIM, (h + 1) * L_HDIM)
        q = q_ref[:, sl]
        k = k_ref[:, sl]
        n_old = n_ref[:, sl]
        wrc = wr[:, ln:ln + 1]
        kpc = keep[:, ln:ln + 1]
        wgt = jnp.sum(q * k, axis=1, keepdims=True) * wrc
        num = wgt * v_ref[:, sl].astype(F32) + kpc * num_ref[:, sl]
        den = wgt + kpc * jnp.sum(q * n_old, axis=1, keepdims=True)
        hh_ref[:, sl] = (num / jnp.maximum(jnp.abs(den), floor[:, ln:ln + 1])).astype(BF16)
        nn_ref[:, sl] = kpc * n_old + wrc * k


def _sample_post(y, xc, pbig, dvec, q, k, n_rows, num, wr, keep, mnew):
    s = y.shape[0]
    full = lambda shape: pl.BlockSpec(shape, lambda i: (0, 0))
    blk = lambda width, j: pl.BlockSpec((s, width), lambda i, j=j: (0, j))
    return pl.pallas_call(
        _sample_post_kernel,
        grid=(1,),
        in_specs=[
            full((s, M_INNER)), full((s, M_INNER)), full((1, M_INNER)),
            full((s, L_INNER)), full((s, L_INNER)), blk(L_INNER, 7),
            full((s, L_INNER)), full((s, L_INNER)),
            full((s, LANES)), full((s, LANES)), full((s, LANES)),
        ],
        out_specs=[full((s, M_INNER)), full((s, L_INNER)), full((s, L_INNER))],
        out_shape=[
            jax.ShapeDtypeStruct((s, M_INNER), BF16),
            jax.ShapeDtypeStruct((s, L_INNER), BF16),
            jax.ShapeDtypeStruct((s, L_INNER), F32),
        ],
        compiler_params=_params("arbitrary"),
        name="sample_post",
    )(y, xc, dvec, q, k, pbig, n_rows, num, wr, keep, mnew)


def _lanes(vec, first_lane):
    n = vec.shape[0]
    return jnp.pad(vec.astype(F32), (first_lane, LANES - first_lane - n)).reshape(1, LANES)


def _spread_matrix(first_lane, n_heads, width):
    r = lax.broadcasted_iota(jnp.int32, (2 * LANES, n_heads * width), 0) % LANES
    c = lax.broadcasted_iota(jnp.int32, (2 * LANES, n_heads * width), 1)
    return (r - first_lane == c // width).astype(BF16)


def _shift_matrix(n):
    r = lax.broadcasted_iota(jnp.int32, ((CONV_W - 1) * n, 2 * n), 0)
    c = lax.broadcasted_iota(jnp.int32, ((CONV_W - 1) * n, 2 * n), 1)
    return (c == n + r % n - (r // n + 1)).astype(BF16)


def kernel(x_prompt, x_sample, state_mamba_conv, state_mamba_ssm, state_mlstm_conv, state_mlstm_C, state_mlstm_n, state_mlstm_m, w_in, mamba_conv_w, mamba_conv_b, mamba_dt_bias, mamba_A_log, mamba_D, mamba_norm_w, w_branch_a, mlstm_conv_w, mlstm_conv_b, mlstm_wq, mlstm_wk, mlstm_i_bias, mlstm_f_bias, mlstm_norm_w, w_branch_b, w_out, norm_mix_w, norm_mlp_w, w_up, w_down, final_norm_w):
    depth = w_in.shape[0]
    assert depth == 1
    batch, seq, _ = x_prompt.shape
    n_samp, dec_seq, _ = x_sample.shape
    assert dec_seq == 1 and seq % CHUNK == 0 and seq >= SUBLANES
    assert n_samp % MLSTM_SB == 0 and n_samp % SSD_SB == 0
    n_chunks = seq // CHUNK
    l = 0

    w = w_in[l]
    c0 = M_INNER + M_CONV
    c1 = c0 + M_HEADS
    c2 = c1 + 3 * L_INNER
    c3 = c2 + 2 * L_HEADS
    w_big = jnp.concatenate([w[:, :c0], w[:, c1:c2], w[:, c3:]], axis=1).astype(BF16)
    w_small = jnp.concatenate(
        [w[:, c0:c1], w[:, c2:c3], jnp.zeros((D_MODEL, LANES - M_HEADS - 2 * L_HEADS), F32)],
        axis=1).astype(BF16)
    g_mix = norm_mix_w[l].reshape(1, D_MODEL)
    cwm = mamba_conv_w[l]
    cbm = mamba_conv_b[l].reshape(1, M_CONV)
    cwl = mlstm_conv_w[l]
    cbl = mlstm_conv_b[l].reshape(1, L_INNER)
    dtb = _lanes(mamba_dt_bias[l], DT_LANE)
    alog = _lanes(mamba_A_log[l], DT_LANE)
    ib = _lanes(mlstm_i_bias[l], I_LANE)
    fb = _lanes(mlstm_f_bias[l], F_LANE)
    dvec = jnp.repeat(mamba_D[l].astype(F32), M_HDIM).reshape(1, M_INNER)
    mnw = mamba_norm_w[l].reshape(1, M_INNER)
    lnw = mlstm_norm_w[l].reshape(1, L_INNER)
    wq = mlstm_wq[l].astype(BF16)
    wk = mlstm_wk[l].astype(BF16)
    wa = w_branch_a[l].astype(BF16)
    wb = w_branch_b[l].astype(BF16)
    wo = w_out[l].astype(BF16)
    wup = w_up[l].astype(BF16)
    wdn = w_down[l].astype(BF16)
    nmw = norm_mlp_w[l].reshape(1, D_MODEL)
    fw = final_norm_w.reshape(1, D_MODEL)
    e64 = _spread_matrix(DT_LANE, M_HEADS, M_HDIM)
    el = _spread_matrix(F_LANE, L_HEADS, L_HDIM)
    shift = _shift_matrix(CHUNK)

    xp = x_prompt.reshape(batch * seq, D_MODEL)
    pbig, psmall = _inproj(xp, g_mix, w_big, w_small, tm=2048)
    ya, p_ssm = _ssd_prompt(pbig, psmall, cwm[:, :M_INNER], cbm[:, :M_INNER], cwm[:, M_INNER:],
                            cbm[:, M_INNER:], dtb, alog, dvec, e64, shift, batch, n_chunks)
    hb, p_c, p_n, p_m = _mlstm_prompt(pbig, psmall, cwl, cbl, wq, wk, ib, fb, el, shift,
                                      batch, n_chunks)
    y_prompt = _tail(ya, hb, pbig, xp, mnw, lnw, wa, wb, wo, nmw, wup, wdn, fw, tm=512)
    p3 = pbig.reshape(batch, seq, BIG_WIDTH)
    p_conv_m = p3[:, seq - (CONV_W - 1):, 2 * PROJ_BLOCK:6 * PROJ_BLOCK].astype(F32)
    p_conv_l = p3[:, seq - (CONV_W - 1):, 6 * PROJ_BLOCK:7 * PROJ_BLOCK].astype(F32)
    p_ssm = p_ssm.reshape(batch, M_HEADS, M_HDIM, M_STATE)
    p_m = p_m[:, 0, F_LANE:F_LANE + L_HEADS]

    xs = x_sample.reshape(n_samp, D_MODEL)
    sbig, ssmall = _inproj(xs, g_mix, w_big, w_small, tm=n_samp)
    m_lanes = jnp.pad(state_mlstm_m[l], ((0, 0), (F_LANE, LANES - F_LANE - L_HEADS)))
    (xc, bcc, q, k, dt, da, wr, keep, mnew, s_conv_m, s_conv_l) = _sample_pre(
        sbig, ssmall, state_mamba_conv[l].reshape(n_samp, (CONV_W - 1) * M_CONV),
        state_mlstm_conv[l].reshape(n_samp, (CONV_W - 1) * L_INNER), m_lanes,
        cwm, cbm, cwl, cbl, wq, wk, dtb, alog, ib, fb)
    s_ssm, y_rows = _ssd_state(
        dt[:, :M_HEADS], da[:, :M_HEADS], xc,
        bcc[:, :M_GROUPS * M_STATE].reshape(n_samp, M_GROUPS, M_STATE),
        bcc[:, M_GROUPS * M_STATE:].reshape(n_samp, M_GROUPS, M_STATE),
        state_mamba_ssm[l].reshape(n_samp, M_PAIRS, LANES, M_STATE))
    s_c, num_rows = _mlstm_state(
        wr[:, F_LANE:F_LANE + L_HEADS], keep[:, F_LANE:F_LANE + L_HEADS],
        q.reshape(n_samp * L_HEADS, L_HDIM), k.reshape(n_samp * L_HEADS, L_HDIM),
        sbig[:, 7 * PROJ_BLOCK:8 * PROJ_BLOCK].astype(F32).reshape(n_samp * L_HEADS, L_HDIM),
        state_mlstm_C[l])
    ya_s, hb_s, s_n = _sample_post(
        y_rows.reshape(n_samp, M_INNER), xc, sbig, dvec, q, k,
        state_mlstm_n[l].reshape(n_samp, L_INNER), num_rows.reshape(n_samp, L_INNER),
        wr, keep, mnew)
    y_sample = _tail(ya_s, hb_s, sbig, xs, mnw, lnw, wa, wb, wo, nmw, wup, wdn, fw, tm=n_samp)

    lead = lambda a: a[None]
    return (
        y_prompt.reshape(batch, seq, D_MODEL),
        y_sample.reshape(n_samp, 1, D_MODEL),
        lead(p_conv_m), lead(p_ssm), lead(p_conv_l), lead(p_c), lead(p_n), lead(p_m),
        lead(s_conv_m.reshape(n_samp, CONV_W - 1, M_CONV)),
        lead(s_ssm.reshape(n_samp, M_HEADS, M_HDIM, M_STATE)),
        lead(s_conv_l.reshape(n_samp, CONV_W - 1, L_INNER)),
        lead(s_c), lead(s_n.reshape(n_samp, L_HEADS, L_HDIM)), lead(mnew[:, F_LANE:F_LANE + L_HEADS]),
    )
```

```python
import jax
import jax.numpy as jnp
from jax import lax
from jax.experimental import pallas as pl
from jax.experimental.pallas import tpu as pltpu

F32 = jnp.float32
BF16 = jnp.bfloat16

D_MODEL = 1024
M_INNER = 2048
M_HEADS = 32
M_HDIM = 64
M_GROUPS = 8
M_PAIRS = M_HEADS // 2
M_STATE = 128
M_CONV = 4096
L_INNER = 1024
L_HEADS = 8
L_HDIM = 128
D_FF = 4096
CONV_W = 4
CHUNK = 128
EPS = 1e-6

LANES = 128
SUBLANES = 8
PROJ_BLOCK = 1024
N_PROJ_BLOCKS = 11
BIG_WIDTH = PROJ_BLOCK * N_PROJ_BLOCKS
DT_LANE = 0
I_LANE = 32
F_LANE = 40
VMEM_LIMIT = 56 * 1024 * 1024

HIGHEST = lax.Precision.HIGHEST
LOG2_E = 1.4426950408889634
NT_DIMS = (((1,), (1,)), ((), ()))
TN_DIMS = (((0,), (0,)), ((), ()))


def _params(*sem):
    return pltpu.CompilerParams(dimension_semantics=sem, vmem_limit_bytes=VMEM_LIMIT)


def _sigmoid(x):
    return 0.5 * jnp.tanh(0.5 * x) + 0.5


def _silu(x):
    h = 0.5 * x
    return h + h * jnp.tanh(h)


def _log1p_exp_neg_abs(x):
    e = jnp.exp(-jnp.abs(x))
    u = 1.0 + e
    return jnp.where(u == 1.0, e, jnp.log(u) * (e / (u - 1.0)))


def _softplus(x):
    return jnp.maximum(x, 0.0) + _log1p_exp_neg_abs(x)


def _log_sigmoid(x):
    return jnp.minimum(x, 0.0) - _log1p_exp_neg_abs(x)


def _rms(x, w):
    return x * lax.rsqrt(jnp.mean(x * x, axis=-1, keepdims=True) + EPS) * w


def _lane_iota(shape):
    return lax.broadcasted_iota(jnp.int32, shape, len(shape) - 1)


def _tri(n):
    r = lax.broadcasted_iota(jnp.int32, (n, n), 0)
    c = lax.broadcasted_iota(jnp.int32, (n, n), 1)
    return r >= c


def _split3(a):
    hi = a.astype(BF16)
    r1 = a - hi.astype(F32)
    mid = r1.astype(BF16)
    lo = (r1 - mid.astype(F32)).astype(BF16)
    return hi, mid, lo


def _cumsum_rows(causal, a):
    tri01 = causal.astype(F32).astype(BF16)
    return jnp.dot(jnp.concatenate([tri01] * 3, axis=1), jnp.concatenate(_split3(a), axis=0),
                   preferred_element_type=F32)


def _spread(a, e2_ref):
    hi, mid, _ = _split3(a)
    return jnp.dot(jnp.concatenate([hi, mid], axis=1), e2_ref[...], preferred_element_type=F32)


CONV_COLS = 512


def _conv_silu(x_ref, win_ref, shift_ref, w_ref, b_ref, out_ref):
    n, width = x_ref.shape
    tile = 16
    win_ref[n:2 * n, :] = x_ref[...]
    for c0 in range(0, width, CONV_COLS):
        cs = slice(c0, c0 + CONV_COLS)
        sh = jnp.dot(shift_ref[...], win_ref[:, cs], preferred_element_type=F32)
        y = b_ref[:, cs] + w_ref[CONV_W - 1:CONV_W, cs] * x_ref[:, cs].astype(F32)
        for back in range(1, CONV_W):
            y = y + w_ref[CONV_W - 1 - back:CONV_W - back, cs] * sh[(back - 1) * n:back * n]
        out_ref[:, cs] = _silu(y).astype(out_ref.dtype)
    win_ref[n - tile:n, :] = x_ref[n - tile:n, :]


def _inproj_kernel(x_ref, g_ref, w_ref, ws_ref, o_ref, os_ref, xn_ref):
    @pl.when(pl.program_id(1) == 0)
    def _():
        xn_ref[...] = _rms(x_ref[...], g_ref[...]).astype(BF16)
        os_ref[...] = jnp.dot(xn_ref[...], ws_ref[...], preferred_element_type=F32)

    o_ref[...] = jnp.dot(xn_ref[...], w_ref[...], preferred_element_type=F32).astype(BF16)


def _inproj(x, g, w_big, w_small, tm):
    m = x.shape[0]
    tm = min(tm, m)
    return pl.pallas_call(
        _inproj_kernel,
        grid=(m // tm, N_PROJ_BLOCKS),
        in_specs=[
            pl.BlockSpec((tm, D_MODEL), lambda i, j: (i, 0)),
            pl.BlockSpec((1, D_MODEL), lambda i, j: (0, 0)),
            pl.BlockSpec((D_MODEL, PROJ_BLOCK), lambda i, j: (0, j)),
            pl.BlockSpec((D_MODEL, LANES), lambda i, j: (0, 0)),
        ],
        out_specs=[
            pl.BlockSpec((tm, PROJ_BLOCK), lambda i, j: (i, j)),
            pl.BlockSpec((tm, LANES), lambda i, j: (i, 0)),
        ],
        out_shape=[
            jax.ShapeDtypeStruct((m, BIG_WIDTH), BF16),
            jax.ShapeDtypeStruct((m, LANES), F32),
        ],
        scratch_shapes=[pltpu.VMEM((tm, D_MODEL), BF16)],
        compiler_params=_params("parallel", "arbitrary"),
        name="in_proj",
    )(x, g, w_big, w_small)


SCAN_NB = 2


def _ssd_chunk(x_ref, bc_ref, sm, cwx_ref, cbx_ref, cwbc_ref, cbbc_ref, dtb_ref, alog_ref, dvec_ref,
               e64_ref, shift_ref, y_ref, h_ref, winx_ref, winbc_ref, xc_scr, bcc_scr):
    n = x_ref.shape[0]
    _conv_silu(x_ref, winx_ref, shift_ref, cwx_ref, cbx_ref, xc_scr)
    _conv_silu(bc_ref, winbc_ref, shift_ref, cwbc_ref, cbbc_ref, bcc_scr)
    xc = xc_scr[...]
    bcc = bcc_scr[...]

    lane = _lane_iota((n, LANES))
    dt = _softplus(sm + dtb_ref[...])
    da = jnp.where(lane < M_HEADS, dt * (-LOG2_E * jnp.exp(alog_ref[...])), 0.0)
    causal = _tri(n)
    acum = _cumsum_rows(causal, da)
    acum_t = acum.T
    last = acum[n - 1:n, :]
    exp_last = jnp.exp2(last)
    dt_x = _spread(dt, e64_ref)
    wr_x = _spread(jnp.exp2(last - acum), e64_ref)
    ea_x = _spread(jnp.exp2(acum), e64_ref)

    xdt = xc * dt_x
    xw = (xdt * wr_x).astype(BF16)
    xb = xdt.astype(BF16)
    low_half = jnp.bitwise_and(_lane_iota((n, M_INNER)), LANES - 1) < M_HDIM
    zero = jnp.zeros((), BF16)
    x_lo = jnp.where(low_half, xb, zero)
    x_hi = jnp.where(low_half, zero, xb)
    first_rows = lax.broadcasted_iota(jnp.int32, (LANES, LANES), 0) < M_HDIM

    def weights(cb, h):
        seg = acum[:, h:h + 1] - acum_t[h:h + 1, :]
        return (cb * jnp.exp2(jnp.where(causal, seg, -jnp.inf))).astype(BF16)

    pairs_per_group = M_PAIRS // M_GROUPS
    for g in range(M_GROUPS):
        bg = bcc[:, g * M_STATE:(g + 1) * M_STATE]
        cg = bcc[:, (M_GROUPS + g) * M_STATE:(M_GROUPS + g + 1) * M_STATE]
        cb = lax.dot_general(cg, bg, NT_DIMS, preferred_element_type=F32)
        for pp in range(pairs_per_group):
            hp = g * pairs_per_group + pp
            h0, h1 = 2 * hp, 2 * hp + 1
            sl = slice(hp * LANES, (hp + 1) * LANES)
            hs = h_ref[hp]
            y = jnp.dot(weights(cb, h0), x_lo[:, sl], preferred_element_type=F32)
            y = y + jnp.dot(weights(cb, h1), x_hi[:, sl], preferred_element_type=F32)
            ys = lax.dot_general(cg, hs.astype(BF16), NT_DIMS, preferred_element_type=F32)
            y_ref[:, sl] = (y + ea_x[:, sl] * ys + dvec_ref[:, sl] * xc[:, sl]).astype(BF16)
            el = jnp.where(first_rows, exp_last[:, h0:h0 + 1], exp_last[:, h1:h1 + 1])
            h_ref[hp] = el * hs + lax.dot_general(xw[:, sl], bg, TN_DIMS, preferred_element_type=F32)


def _mlstm_chunks(seqs, wq_ref, wk_ref, ib_ref, fb_ref, el_ref):
    n = seqs[0][0].shape[0]
    causal = _tri(n)
    lane = _lane_iota((n, LANES))
    gate_lanes = (lane >= F_LANE) & (lane < F_LANE + L_HEADS)
    scale = L_HDIM ** -0.5
    lanes_of = [slice(h * L_HDIM, (h + 1) * L_HDIM) for h in range(L_HEADS)]

    gates = []
    for uc, v_ref, sm, hh_ref, c_ref, n_ref, m_ref in seqs:
        ig = pltpu.roll(sm + ib_ref[...], F_LANE - I_LANE, axis=1)
        logf = jnp.where(gate_lanes, _log_sigmoid(sm + fb_ref[...]), 0.0)
        bcum = _cumsum_rows(causal, logf)
        m_old = m_ref[...]
        bl = bcum[n - 1:n, :]
        src = bl - bcum + ig
        m_new = jnp.maximum(bl + m_old, jnp.max(src, axis=0, keepdims=True))
        gates.append(dict(
            bcum=bcum, bcum_t=bcum.T, ig_t=ig.T, m_old=m_old, m_new=m_new,
            keep=jnp.exp(bl + m_old - m_new),
            wr_x=_spread(jnp.exp(src - m_new), el_ref)))

    work = [(i, h) for i in range(len(seqs)) for h in range(L_HEADS)]
    c_olds = {(i, h): seqs[i][4][h] for i, h in work}
    n_olds = {(i, h): seqs[i][5][h:h + 1, :] for i, h in work}
    qs, ks, qbs, kbs = {}, {}, {}, {}
    for i, h in work:
        ub = seqs[i][0][:, lanes_of[h]]
        qs[i, h] = jnp.dot(ub, wq_ref[h], preferred_element_type=F32)
        ks[i, h] = jnp.dot(ub, wk_ref[h], preferred_element_type=F32) * scale
    logds, inters, rowmax = {}, {}, {}
    for i, h in work:
        g = gates[i]
        ln = F_LANE + h
        bcol = jnp.broadcast_to(g["bcum"][:, ln:ln + 1], (n, n))
        logd = jnp.where(causal, bcol - g["bcum_t"][ln:ln + 1, :] + g["ig_t"][ln:ln + 1, :], -jnp.inf)
        logds[i, h] = logd
        inters[i, h] = bcol + g["m_old"][:, ln:ln + 1]
        rowmax[i, h] = jnp.max(logd, axis=1, keepdims=True)
    ss, scs, floors = {}, {}, {}
    for i, h in work:
        qbs[i, h] = qs[i, h].astype(BF16)
        kbs[i, h] = ks[i, h].astype(BF16)
        m_s = jnp.maximum(inters[i, h], rowmax[i, h])
        dm = jnp.exp(logds[i, h] - m_s)
        scs[i, h] = jnp.exp(inters[i, h] - m_s)
        floors[i, h] = jnp.exp(-m_s)
        ss[i, h] = lax.dot_general(qbs[i, h], kbs[i, h], NT_DIMS, preferred_element_type=F32) * dm
    nums, dens = {}, {}
    for i, h in work:
        vb = seqs[i][1][:, lanes_of[h]]
        num = jnp.dot(ss[i, h].astype(BF16), vb, preferred_element_type=F32)
        nums[i, h] = num + scs[i, h] * jnp.dot(
            qbs[i, h], c_olds[i, h].astype(BF16), preferred_element_type=F32)
        dens[i, h] = (jnp.sum(ss[i, h], axis=1, keepdims=True)
                      + scs[i, h] * jnp.sum(qs[i, h] * n_olds[i, h], axis=1, keepdims=True))
    for i, h in work:
        hh = nums[i, h] / jnp.maximum(jnp.abs(dens[i, h]), floors[i, h])
        seqs[i][3][:, lanes_of[h]] = hh.astype(BF16)
    for i, h in work:
        g = gates[i]
        ln = F_LANE + h
        kw = ks[i, h] * g["wr_x"][:, lanes_of[h]]
        kp = g["keep"][:, ln:ln + 1]
        seqs[i][4][h] = kp * c_olds[i, h] + lax.dot_general(
            kw.astype(BF16), seqs[i][1][:, lanes_of[h]], TN_DIMS, preferred_element_type=F32)
        seqs[i][5][h:h + 1, :] = kp * n_olds[i, h] + jnp.sum(kw, axis=0, keepdims=True)
    for i in range(len(seqs)):
        seqs[i][6][...] = gates[i]["m_new"]


def _scan_kernel(x_ref, bc_ref, u_ref, v_ref, sm_ref,
                 cwx_ref, cbx_ref, cwbc_ref, cbbc_ref, cwl_ref, cbl_ref, dtb_ref, alog_ref, dvec_ref,
                 wq_ref, wk_ref, ib_ref, fb_ref, e64_ref, el_ref, shift_ref,
                 y_ref, hh_ref, h_ref, c_ref, n_ref, m_ref,
                 winx_ref, winbc_ref, winu_ref, xc_scr, bcc_scr, uc_scr):
    n = x_ref.shape[1]

    @pl.when(pl.program_id(1) == 0)
    def _():
        h_ref[...] = jnp.zeros_like(h_ref)
        c_ref[...] = jnp.zeros_like(c_ref)
        n_ref[...] = jnp.zeros_like(n_ref)
        m_ref[...] = jnp.zeros_like(m_ref)
        winx_ref[:, 0:n, :] = jnp.zeros((SCAN_NB, n, M_INNER), BF16)
        winbc_ref[:, 0:n, :] = jnp.zeros((SCAN_NB, n, M_INNER), BF16)
        winu_ref[:, 0:n, :] = jnp.zeros((SCAN_NB, n, L_INNER), BF16)

    seqs = []
    for i in range(SCAN_NB):
        _ssd_chunk(x_ref.at[i], bc_ref.at[i], sm_ref[i], cwx_ref, cbx_ref, cwbc_ref, cbbc_ref,
                   dtb_ref, alog_ref, dvec_ref, e64_ref, shift_ref, y_ref.at[i], h_ref.at[i],
                   winx_ref.at[i], winbc_ref.at[i], xc_scr, bcc_scr)
        _conv_silu(u_ref.at[i], winu_ref.at[i], shift_ref, cwl_ref, cbl_ref, uc_scr.at[i])
        seqs.append((uc_scr.at[i], v_ref.at[i], sm_ref[i], hh_ref.at[i], c_ref.at[i], n_ref.at[i],
                     m_ref.at[i]))
    _mlstm_chunks(seqs, wq_ref, wk_ref, ib_ref, fb_ref, el_ref)


def _scan_prompt(pbig, psmall, cwx, cbx, cwbc, cbbc, cwl, cbl, dtb, alog, dvec, wq, wk, ib, fb,
                 e64, el, shift, batch, seq):
    p3 = pbig.reshape(batch, seq, BIG_WIDTH)
    s3 = psmall.reshape(batch, seq, LANES)
    nb = SCAN_NB
    const2 = lambda b, c: (0, 0)
    const3 = lambda b, c: (0, 0, 0)
    rows = lambda width, col: pl.BlockSpec((nb, CHUNK, width), lambda b, c, col=col: (b, c, col))
    state4 = lambda d1, d2, d3: pl.BlockSpec((nb, d1, d2, d3), lambda b, c: (b, 0, 0, 0))
    return pl.pallas_call(
        _scan_kernel,
        grid=(batch // nb, seq // CHUNK),
        in_specs=[
            rows(M_INNER, 1),
            rows(M_INNER, 2),
            rows(L_INNER, 6),
            rows(L_INNER, 7),
            rows(LANES, 0),
            pl.BlockSpec((CONV_W, M_INNER), const2), pl.BlockSpec((1, M_INNER), const2),
            pl.BlockSpec((CONV_W, M_INNER), const2), pl.BlockSpec((1, M_INNER), const2),
            pl.BlockSpec((CONV_W, L_INNER), const2), pl.BlockSpec((1, L_INNER), const2),
            pl.BlockSpec((1, LANES), const2), pl.BlockSpec((1, LANES), const2),
            pl.BlockSpec((1, M_INNER), const2),
            pl.BlockSpec((L_HEADS, L_HDIM, L_HDIM), const3),
            pl.BlockSpec((L_HEADS, L_HDIM, L_HDIM), const3),
            pl.BlockSpec((1, LANES), const2), pl.BlockSpec((1, LANES), const2),
            pl.BlockSpec((2 * LANES, M_INNER), const2),
            pl.BlockSpec((2 * LANES, L_INNER), const2),
            pl.BlockSpec(((CONV_W - 1) * CHUNK, 2 * CHUNK), const2),
        ],
        out_specs=[
            rows(M_INNER, 0),
            rows(L_INNER, 0),
            state4(M_PAIRS, LANES, M_STATE),
            state4(L_HEADS, L_HDIM, L_HDIM),
            pl.BlockSpec((nb, L_HEADS, L_HDIM), lambda b, c: (b, 0, 0)),
            pl.BlockSpec((nb, 1, LANES), lambda b, c: (b, 0, 0)),
        ],
        out_shape=[
            jax.ShapeDtypeStruct((batch, seq, M_INNER), BF16),
            jax.ShapeDtypeStruct((batch, seq, L_INNER), BF16),
            jax.ShapeDtypeStruct((batch, M_PAIRS, LANES, M_STATE), F32),
            jax.ShapeDtypeStruct((batch, L_HEADS, L_HDIM, L_HDIM), F32),
            jax.ShapeDtypeStruct((batch, L_HEADS, L_HDIM), F32),
            jax.ShapeDtypeStruct((batch, 1, LANES), F32),
        ],
        scratch_shapes=[
            pltpu.VMEM((nb, 2 * CHUNK, M_INNER), BF16),
            pltpu.VMEM((nb, 2 * CHUNK, M_INNER), BF16),
            pltpu.VMEM((nb, 2 * CHUNK, L_INNER), BF16),
            pltpu.VMEM((CHUNK, M_INNER), F32),
            pltpu.VMEM((CHUNK, M_INNER), BF16),
            pltpu.VMEM((nb, CHUNK, L_INNER), BF16),
        ],
        compiler_params=_params("parallel", "arbitrary"),
        name="scan_prompt",
    )(p3, p3, p3, p3, s3, cwx, cbx, cwbc, cbbc, cwl, cbl, dtb, alog, dvec, wq, wk, ib, fb,
      e64, el, shift)


def _ssd_kernel(x_ref, bc_ref, sm_ref, cwx_ref, cbx_ref, cwbc_ref, cbbc_ref,
                dtb_ref, alog_ref, dvec_ref, e64_ref, shift_ref,
                y_ref, h_ref, winx_ref, winbc_ref, xc_scr, bcc_scr):
    n = x_ref.shape[0]

    @pl.when(pl.program_id(1) == 0)
    def _():
        h_ref[...] = jnp.zeros_like(h_ref)
        winx_ref[0:n, :] = jnp.zeros((n, M_INNER), BF16)
        winbc_ref[0:n, :] = jnp.zeros((n, M_INNER), BF16)

    _conv_silu(x_ref, winx_ref, shift_ref, cwx_ref, cbx_ref, xc_scr)
    _conv_silu(bc_ref, winbc_ref, shift_ref, cwbc_ref, cbbc_ref, bcc_scr)
    xc = xc_scr[...]
    bcc = bcc_scr[...]

    lane = _lane_iota((n, LANES))
    dt = _softplus(sm_ref[...] + dtb_ref[...])
    da = jnp.where(lane < M_HEADS, dt * (-LOG2_E * jnp.exp(alog_ref[...])), 0.0)
    causal = _tri(n)
    acum = _cumsum_rows(causal, da)
    acum_t = acum.T
    last = acum[n - 1:n, :]
    exp_last = jnp.exp2(last)
    dt_x = _spread(dt, e64_ref)
    wr_x = _spread(jnp.exp2(last - acum), e64_ref)
    ea_x = _spread(jnp.exp2(acum), e64_ref)

    xdt = xc * dt_x
    xw = (xdt * wr_x).astype(BF16)
    xb = xdt.astype(BF16)
    low_half = jnp.bitwise_and(_lane_iota((n, M_INNER)), LANES - 1) < M_HDIM
    zero = jnp.zeros((), BF16)
    x_lo = jnp.where(low_half, xb, zero)
    x_hi = jnp.where(low_half, zero, xb)
    first_rows = lax.broadcasted_iota(jnp.int32, (LANES, LANES), 0) < M_HDIM

    def weights(cb, h):
        seg = acum[:, h:h + 1] - acum_t[h:h + 1, :]
        return (cb * jnp.exp2(jnp.where(causal, seg, -jnp.inf))).astype(BF16)

    for g in range(M_GROUPS):
        bg = bcc[:, g * M_STATE:(g + 1) * M_STATE]
        cg = bcc[:, (M_GROUPS + g) * M_STATE:(M_GROUPS + g + 1) * M_STATE]
        cb = lax.dot_general(cg, bg, NT_DIMS, preferred_element_type=F32)
        for pp in range(M_PAIRS // M_GROUPS):
            hp = g * (M_PAIRS // M_GROUPS) + pp
            h0, h1 = 2 * hp, 2 * hp + 1
            sl = slice(hp * LANES, (hp + 1) * LANES)
            hs = h_ref[0, hp]
            y = jnp.dot(weights(cb, h0), x_lo[:, sl], preferred_element_type=F32)
            y = y + jnp.dot(weights(cb, h1), x_hi[:, sl], preferred_element_type=F32)
            ys = lax.dot_general(cg, hs.astype(BF16), NT_DIMS, preferred_element_type=F32)
            y_ref[:, sl] = (y + ea_x[:, sl] * ys + dvec_ref[:, sl] * xc[:, sl]).astype(BF16)
            el = jnp.where(first_rows, exp_last[:, h0:h0 + 1], exp_last[:, h1:h1 + 1])
            h_ref[0, hp] = el * hs + lax.dot_general(xw[:, sl], bg, TN_DIMS, preferred_element_type=F32)


def _ssd_prompt(pbig, psmall, cwx, cbx, cwbc, cbbc, dtb, alog, dvec, e64, shift, batch, n_chunks):
    m = pbig.shape[0]
    row = lambda b, c: b * n_chunks + c
    const = lambda b, c: (0, 0)
    return pl.pallas_call(
        _ssd_kernel,
        grid=(batch, n_chunks),
        in_specs=[
            pl.BlockSpec((CHUNK, M_INNER), lambda b, c: (row(b, c), 1)),
            pl.BlockSpec((CHUNK, M_INNER), lambda b, c: (row(b, c), 2)),
            pl.BlockSpec((CHUNK, LANES), lambda b, c: (row(b, c), 0)),
            pl.BlockSpec((CONV_W, M_INNER), const),
            pl.BlockSpec((1, M_INNER), const),
            pl.BlockSpec((CONV_W, M_INNER), const),
            pl.BlockSpec((1, M_INNER), const),
            pl.BlockSpec((1, LANES), const),
            pl.BlockSpec((1, LANES), const),
            pl.BlockSpec((1, M_INNER), const),
            pl.BlockSpec((2 * LANES, M_INNER), const),
            pl.BlockSpec(((CONV_W - 1) * CHUNK, 2 * CHUNK), const),
        ],
        out_specs=[
            pl.BlockSpec((CHUNK, M_INNER), lambda b, c: (row(b, c), 0)),
            pl.BlockSpec((1, M_PAIRS, LANES, M_STATE), lambda b, c: (b, 0, 0, 0)),
        ],
        out_shape=[
            jax.ShapeDtypeStruct((m, M_INNER), BF16),
            jax.ShapeDtypeStruct((batch, M_PAIRS, LANES, M_STATE), F32),
        ],
        scratch_shapes=[
            pltpu.VMEM((2 * CHUNK, M_INNER), BF16),
            pltpu.VMEM((2 * CHUNK, M_INNER), BF16),
            pltpu.VMEM((CHUNK, M_INNER), F32),
            pltpu.VMEM((CHUNK, M_INNER), BF16),
        ],
        compiler_params=_params("parallel", "arbitrary"),
        name="ssd_prompt",
    )(pbig, pbig, psmall, cwx, cbx, cwbc, cbbc, dtb, alog, dvec, e64, shift)


def _mlstm_kernel(u_ref, v_ref, sm_ref, cw_ref, cb_ref, wq_ref, wk_ref,
                  ib_ref, fb_ref, el_ref, shift_ref,
                  hh_ref, c_ref, n_ref, m_ref, win_ref, uc_scr):
    n = u_ref.shape[0]

    @pl.when(pl.program_id(1) == 0)
    def _():
        c_ref[...] = jnp.zeros_like(c_ref)
        n_ref[...] = jnp.zeros_like(n_ref)
        m_ref[...] = jnp.zeros_like(m_ref)
        win_ref[0:n, :] = jnp.zeros((n, L_INNER), BF16)

    _conv_silu(u_ref, win_ref, shift_ref, cw_ref, cb_ref, uc_scr)
    uc = uc_scr[...]

    sm = sm_ref[...]
    lane = _lane_iota((n, LANES))
    gate_lanes = (lane >= F_LANE) & (lane < F_LANE + L_HEADS)
    ig = pltpu.roll(sm + ib_ref[...], F_LANE - I_LANE, axis=1)
    logf = jnp.where(gate_lanes, _log_sigmoid(sm + fb_ref[...]), 0.0)
    causal = _tri(n)
    bcum = _cumsum_rows(causal, logf)
    m_old = m_ref[0]
    bl = bcum[n - 1:n, :]
    src = bl - bcum + ig
    m_new = jnp.maximum(bl + m_old, jnp.max(src, axis=0, keepdims=True))
    keep = jnp.exp(bl + m_old - m_new)
    bcum_t = bcum.T
    ig_t = ig.T
    wr_x = _spread(jnp.exp(src - m_new), el_ref)
    scale = L_HDIM ** -0.5
    c_olds = [c_ref[0, h] for h in range(L_HEADS)]
    n_olds = [n_ref[0, h:h + 1, :] for h in range(L_HEADS)]

    heads = range(L_HEADS)
    lanes_of = [slice(h * L_HDIM, (h + 1) * L_HDIM) for h in heads]
    qs, ks, qbs, kbs = [], [], [], []
    for h in heads:
        ub = uc[:, lanes_of[h]]
        qs.append(jnp.dot(ub, wq_ref[h], preferred_element_type=F32))
        ks.append(jnp.dot(ub, wk_ref[h], preferred_element_type=F32) * scale)
    logds, inters, rowmax = [], [], []
    for h in heads:
        ln = F_LANE + h
        bcol = jnp.broadcast_to(bcum[:, ln:ln + 1], (n, n))
        logd = jnp.where(causal, bcol - bcum_t[ln:ln + 1, :] + ig_t[ln:ln + 1, :], -jnp.inf)
        logds.append(logd)
        inters.append(bcol + m_old[:, ln:ln + 1])
        rowmax.append(jnp.max(logd, axis=1, keepdims=True))
    ss, scs, floors = [], [], []
    for h in heads:
        qbs.append(qs[h].astype(BF16))
        kbs.append(ks[h].astype(BF16))
        m_s = jnp.maximum(inters[h], rowmax[h])
        dm = jnp.exp(logds[h] - m_s)
        scs.append(jnp.exp(inters[h] - m_s))
        floors.append(jnp.exp(-m_s))
        ss.append(lax.dot_general(qbs[h], kbs[h], NT_DIMS, preferred_element_type=F32) * dm)
    nums, dens = [], []
    for h in heads:
        vb = v_ref[:, lanes_of[h]]
        num = jnp.dot(ss[h].astype(BF16), vb, preferred_element_type=F32)
        num = num + scs[h] * jnp.dot(qbs[h], c_olds[h].astype(BF16), preferred_element_type=F32)
        nums.append(num)
        dens.append(jnp.sum(ss[h], axis=1, keepdims=True)
                    + scs[h] * jnp.sum(qs[h] * n_olds[h], axis=1, keepdims=True))
    for h in heads:
        hh = nums[h] / jnp.maximum(jnp.abs(dens[h]), floors[h])
        hh_ref[:, lanes_of[h]] = hh.astype(BF16)
    for h in heads:
        ln = F_LANE + h
        kw = ks[h] * wr_x[:, lanes_of[h]]
        kp = keep[:, ln:ln + 1]
        c_ref[0, h] = kp * c_olds[h] + lax.dot_general(
            kw.astype(BF16), v_ref[:, lanes_of[h]], TN_DIMS, preferred_element_type=F32)
        n_ref[0, h:h + 1, :] = kp * n_olds[h] + jnp.sum(kw, axis=0, keepdims=True)
    m_ref[0] = m_new


def _mlstm_prompt(pbig, psmall, cw, cb, wq, wk, ib, fb, el, shift, batch, n_chunks):
    m = pbig.shape[0]
    row = lambda b, c: b * n_chunks + c
    const2 = lambda b, c: (0, 0)
    return pl.pallas_call(
        _mlstm_kernel,
        grid=(batch, n_chunks),
        in_specs=[
            pl.BlockSpec((CHUNK, L_INNER), lambda b, c: (row(b, c), 6)),
            pl.BlockSpec((CHUNK, L_INNER), lambda b, c: (row(b, c), 7)),
            pl.BlockSpec((CHUNK, LANES), lambda b, c: (row(b, c), 0)),
            pl.BlockSpec((CONV_W, L_INNER), const2),
            pl.BlockSpec((1, L_INNER), const2),
            pl.BlockSpec((L_HEADS, L_HDIM, L_HDIM), lambda b, c: (0, 0, 0)),
            pl.BlockSpec((L_HEADS, L_HDIM, L_HDIM), lambda b, c: (0, 0, 0)),
            pl.BlockSpec((1, LANES), const2),
            pl.BlockSpec((1, LANES), const2),
            pl.BlockSpec((2 * LANES, L_INNER), const2),
            pl.BlockSpec(((CONV_W - 1) * CHUNK, 2 * CHUNK), const2),
        ],
        out_specs=[
            pl.BlockSpec((CHUNK, L_INNER), lambda b, c: (row(b, c), 0)),
            pl.BlockSpec((1, L_HEADS, L_HDIM, L_HDIM), lambda b, c: (b, 0, 0, 0)),
            pl.BlockSpec((1, L_HEADS, L_HDIM), lambda b, c: (b, 0, 0)),
            pl.BlockSpec((1, 1, LANES), lambda b, c: (b, 0, 0)),
        ],
        out_shape=[
            jax.ShapeDtypeStruct((m, L_INNER), BF16),
            jax.ShapeDtypeStruct((batch, L_HEADS, L_HDIM, L_HDIM), F32),
            jax.ShapeDtypeStruct((batch, L_HEADS, L_HDIM), F32),
            jax.ShapeDtypeStruct((batch, 1, LANES), F32),
        ],
        scratch_shapes=[
            pltpu.VMEM((2 * CHUNK, L_INNER), BF16),
            pltpu.VMEM((CHUNK, L_INNER), BF16),
        ],
        compiler_params=_params("parallel", "arbitrary"),
        name="mlstm_prompt",
    )(pbig, pbig, psmall, cw, cb, wq, wk, ib, fb, el, shift)


def _tail_kernel(ys_ref, hh_ref, z_ref, o_ref, ga_ref, gb_ref, x_ref, mnw_ref, lnw_ref,
                 wa_ref, wb_ref, wo_ref, nmw_ref, wup_ref, wdn_ref, fw_ref, y_ref, ya_scr, hb_scr):
    gw = M_INNER // M_GROUPS
    for g in range(M_GROUPS):
        sl = slice(g * gw, (g + 1) * gw)
        yg = ys_ref[:, sl].astype(F32) * _silu(z_ref[:, sl].astype(F32))
        ya_scr[:, sl] = _rms(yg, mnw_ref[:, sl]).astype(BF16)
    for h in range(L_HEADS):
        sl = slice(h * L_HDIM, (h + 1) * L_HDIM)
        gate = _sigmoid(o_ref[:, sl].astype(F32))
        hb_scr[:, sl] = (gate * _rms(hh_ref[:, sl].astype(F32), lnw_ref[:, sl])).astype(BF16)
    a = jnp.dot(ya_scr[...], wa_ref[...], preferred_element_type=F32)
    b = jnp.dot(hb_scr[...], wb_ref[...], preferred_element_type=F32)
    t = _sigmoid(ga_ref[...].astype(F32)) * a + _sigmoid(gb_ref[...].astype(F32)) * b
    x1 = x_ref[...] + jnp.dot(t.astype(BF16), wo_ref[...], preferred_element_type=F32)
    hn = _rms(x1, nmw_ref[...]).astype(BF16)
    acc = x1
    for c in range(D_FF // PROJ_BLOCK):
        sl = slice(c * PROJ_BLOCK, (c + 1) * PROJ_BLOCK)
        up = jnp.dot(hn, wup_ref[:, sl], preferred_element_type=F32)
        act = jnp.square(jnp.maximum(up, 0.0)).astype(BF16)
        acc = acc + jnp.dot(act, wdn_ref[sl, :], preferred_element_type=F32)
    y_ref[...] = _rms(acc, fw_ref[...])


def _tail(ys, hh, pbig, x, mnw, lnw, wa, wb, wo, nmw, wup, wdn, fw, tm):
    m = x.shape[0]
    tm = min(tm, m)
    rows = lambda i: (i, 0)
    const = lambda i: (0, 0)

    def resident(shape):
        return pl.BlockSpec(shape, const, pipeline_mode=pl.Buffered(1))

    return pl.pallas_call(
        _tail_kernel,
        grid=(m // tm,),
        in_specs=[
            pl.BlockSpec((tm, M_INNER), rows),
            pl.BlockSpec((tm, L_INNER), rows),
            pl.BlockSpec((tm, M_INNER), rows),
            pl.BlockSpec((tm, PROJ_BLOCK), lambda i: (i, 8)),
            pl.BlockSpec((tm, PROJ_BLOCK), lambda i: (i, 9)),
            pl.BlockSpec((tm, PROJ_BLOCK), lambda i: (i, 10)),
            pl.BlockSpec((tm, D_MODEL), rows),
            resident((1, M_INNER)),
            resident((1, L_INNER)),
            resident((M_INNER, D_MODEL)),
            resident((L_INNER, D_MODEL)),
            resident((D_MODEL, D_MODEL)),
            resident((1, D_MODEL)),
            resident((D_MODEL, D_FF)),
            resident((D_FF, D_MODEL)),
            resident((1, D_MODEL)),
        ],
        out_specs=pl.BlockSpec((tm, D_MODEL), rows),
        out_shape=jax.ShapeDtypeStruct((m, D_MODEL), F32),
        scratch_shapes=[pltpu.VMEM((tm, M_INNER), BF16), pltpu.VMEM((tm, L_INNER), BF16)],
        compiler_params=_params("parallel"),
        name="tail",
    )(ys, hh, pbig, pbig, pbig, pbig, x, mnw, lnw, wa, wb, wo, nmw, wup, wdn, fw)


def _sample_pre_kernel(xbc_ref, u_ref, sm_ref, cm0_ref, cm1_ref, cm2_ref, cl0_ref, cl1_ref, cl2_ref,
                       m_ref, cwm_ref, cbm_ref, cwl_ref, cbl_ref, wq_ref, wk_ref,
                       dtb_ref, alog_ref, ib_ref, fb_ref,
                       xc_ref, bcc_ref, q_ref, k_ref, dt_ref, da_ref, wr_ref, keep_ref, mnew_ref,
                       cmn_ref, cln_ref):
    xbc = xbc_ref[...].astype(F32)
    conv_m = (cbm_ref[...] + cwm_ref[0:1, :] * cm0_ref[...] + cwm_ref[1:2, :] * cm1_ref[...]
              + cwm_ref[2:3, :] * cm2_ref[...] + cwm_ref[3:4, :] * xbc)
    act = _silu(conv_m)
    xc_ref[...] = act[:, :M_INNER]
    bcc_ref[...] = act[:, M_INNER:]
    cmn_ref[:, 0:M_CONV] = cm1_ref[...]
    cmn_ref[:, M_CONV:2 * M_CONV] = cm2_ref[...]
    cmn_ref[:, 2 * M_CONV:3 * M_CONV] = xbc

    u = u_ref[...].astype(F32)
    conv_l = (cbl_ref[...] + cwl_ref[0:1, :] * cl0_ref[...] + cwl_ref[1:2, :] * cl1_ref[...]
              + cwl_ref[2:3, :] * cl2_ref[...] + cwl_ref[3:4, :] * u)
    uc = _silu(conv_l)
    cln_ref[:, 0:L_INNER] = cl1_ref[...]
    cln_ref[:, L_INNER:2 * L_INNER] = cl2_ref[...]
    cln_ref[:, 2 * L_INNER:3 * L_INNER] = u
    scale = L_HDIM ** -0.5
    for h in range(L_HEADS):
        sl = slice(h * L_HDIM, (h + 1) * L_HDIM)
        ub = uc[:, sl].astype(BF16)
        q_ref[:, sl] = jnp.dot(ub, wq_ref[h], preferred_element_type=F32)
        k_ref[:, sl] = jnp.dot(ub, wk_ref[h], preferred_element_type=F32) * scale

    sm = sm_ref[...]
    dt = _softplus(sm + dtb_ref[...])
    dt_ref[...] = dt
    da_ref[...] = jnp.exp(dt * (-jnp.exp(alog_ref[...])))
    ig = pltpu.roll(sm + ib_ref[...], F_LANE - I_LANE, axis=1)
    logf = _log_sigmoid(sm + fb_ref[...])
    m_old = m_ref[...]
    m_new = jnp.maximum(logf + m_old, ig)
    mnew_ref[...] = m_new
    wr_ref[...] = jnp.exp(ig - m_new)
    keep_ref[...] = jnp.exp(logf + m_old - m_new)


def _sample_pre(pbig, psmall, conv_m, conv_l, m_lanes, cwm, cbm, cwl, cbl, wq, wk, dtb, alog, ib, fb):
    s = pbig.shape[0]
    f = lambda shape: jax.ShapeDtypeStruct(shape, F32)
    full2 = lambda shape: pl.BlockSpec(shape, lambda i: (0, 0))
    full3 = lambda shape: pl.BlockSpec(shape, lambda i: (0, 0, 0))
    state_row = lambda width, j: pl.BlockSpec((s, width), lambda i, j=j: (0, j))
    return pl.pallas_call(
        _sample_pre_kernel,
        grid=(1,),
        in_specs=[
            full2((s, M_CONV)),
            pl.BlockSpec((s, L_INNER), lambda i: (0, 6)),
            full2((s, LANES)),
            state_row(M_CONV, 0), state_row(M_CONV, 1), state_row(M_CONV, 2),
            state_row(L_INNER, 0), state_row(L_INNER, 1), state_row(L_INNER, 2),
            full2((s, LANES)),
            full2((CONV_W, M_CONV)), full2((1, M_CONV)),
            full2((CONV_W, L_INNER)), full2((1, L_INNER)),
            full3((L_HEADS, L_HDIM, L_HDIM)), full3((L_HEADS, L_HDIM, L_HDIM)),
            full2((1, LANES)), full2((1, LANES)), full2((1, LANES)), full2((1, LANES)),
        ],
        out_specs=[
            full2((s, M_INNER)), full2((s, M_INNER)), full2((s, L_INNER)), full2((s, L_INNER)),
            full2((s, LANES)), full2((s, LANES)), full2((s, LANES)), full2((s, LANES)), full2((s, LANES)),
            full2((s, (CONV_W - 1) * M_CONV)), full2((s, (CONV_W - 1) * L_INNER)),
        ],
        out_shape=[
            f((s, M_INNER)), f((s, M_INNER)), f((s, L_INNER)), f((s, L_INNER)),
            f((s, LANES)), f((s, LANES)), f((s, LANES)), f((s, LANES)), f((s, LANES)),
            f((s, (CONV_W - 1) * M_CONV)), f((s, (CONV_W - 1) * L_INNER)),
        ],
        compiler_params=_params("arbitrary"),
        name="sample_pre",
    )(pbig[:, 2 * PROJ_BLOCK:6 * PROJ_BLOCK], pbig, psmall, conv_m, conv_m, conv_m,
      conv_l, conv_l, conv_l, m_lanes, cwm, cbm, cwl, cbl, wq, wk, dtb, alog, ib, fb)


SSD_SB = 8
MLSTM_SB = 16


def _ssd_state_kernel(dt_ref, da_ref, x_ref, b_ref, c_ref, h_ref, hn_ref, y_ref, xs_ref):
    base = pl.program_id(0) * SSD_SB
    rows = lax.broadcasted_iota(jnp.int32, (LANES, LANES), 0)
    lanes = _lane_iota((LANES, LANES))
    diag = rows == lanes
    first_rows = rows < M_HDIM
    ones = jnp.ones((LANES, LANES), BF16)
    pairs_per_group = M_PAIRS // M_GROUPS

    def spread(s):
        for hp in range(M_PAIRS):
            xrow = jnp.broadcast_to(x_ref[s:s + 1, hp * LANES:(hp + 1) * LANES], (LANES, LANES))
            xdiag = jnp.where(diag, xrow, 0.0).astype(BF16)
            xs_ref[s % 2, hp] = jnp.dot(xdiag, ones, preferred_element_type=F32)

    acc = jnp.zeros((LANES, LANES), F32)
    spread(0)
    for s in range(SSD_SB):
        if s + 1 < SSD_SB:
            spread(s + 1)
        for g in range(M_GROUPS):
            brow = b_ref[s, g:g + 1, :]
            crow = c_ref[s, g:g + 1, :]
            for pp in range(pairs_per_group):
                hp = g * pairs_per_group + pp
                h0, h1 = 2 * hp, 2 * hp + 1
                dav = jnp.where(first_rows, da_ref[base + s, h0], da_ref[base + s, h1])
                dtv = jnp.where(first_rows, dt_ref[base + s, h0], dt_ref[base + s, h1])
                hn = dav * h_ref[s, hp] + (dtv * xs_ref[s % 2, hp]) * brow
                hn_ref[s, hp] = hn
                ysum = jnp.dot((hn * crow).astype(BF16), ones, preferred_element_type=F32)
                acc = jnp.where(lanes == s * M_PAIRS + hp, ysum, acc)
    y_ref[...] = acc.T


def _ssd_state(dt, da, xc, b3, c3, h):
    s = h.shape[0]
    smem = pl.BlockSpec(memory_space=pltpu.SMEM)
    return pl.pallas_call(
        _ssd_state_kernel,
        grid=(s // SSD_SB,),
        in_specs=[
            smem, smem,
            pl.BlockSpec((SSD_SB, M_INNER), lambda i: (i, 0)),
            pl.BlockSpec((SSD_SB, M_GROUPS, M_STATE), lambda i: (i, 0, 0)),
            pl.BlockSpec((SSD_SB, M_GROUPS, M_STATE), lambda i: (i, 0, 0)),
            pl.BlockSpec((SSD_SB, M_PAIRS, LANES, M_STATE), lambda i: (i, 0, 0, 0)),
        ],
        out_specs=[
            pl.BlockSpec((SSD_SB, M_PAIRS, LANES, M_STATE), lambda i: (i, 0, 0, 0)),
            pl.BlockSpec((SSD_SB * M_PAIRS, LANES), lambda i: (i, 0)),
        ],
        out_shape=[
            jax.ShapeDtypeStruct(h.shape, F32),
            jax.ShapeDtypeStruct((s * M_PAIRS, LANES), F32),
        ],
        scratch_shapes=[pltpu.VMEM((2, M_PAIRS, LANES, LANES), F32)],
        compiler_params=_params("parallel"),
        name="ssd_state",
    )(dt, da, xc, b3, c3, h)


def _mlstm_state_kernel(wr_ref, keep_ref, q_ref, k_ref, v_ref, c_ref, cn_ref, num_ref):
    base = pl.program_id(0) * MLSTM_SB
    qt = q_ref[...].T
    kt = k_ref[...].T
    for s in range(MLSTM_SB):
        for h in range(L_HEADS):
            r = s * L_HEADS + h
            wr = wr_ref[base + s, h]
            keep = keep_ref[base + s, h]
            c_old = c_ref[s, h]
            num_ref[r:r + 1, :] = jnp.sum(qt[:, r:r + 1] * c_old, axis=0, keepdims=True)
            cn_ref[s, h] = keep * c_old + (kt[:, r:r + 1] * wr) * v_ref[r:r + 1, :]


def _mlstm_state(wr, keep, q_rows, k_rows, v_rows, c):
    s = c.shape[0]
    smem = pl.BlockSpec(memory_space=pltpu.SMEM)
    rows = pl.BlockSpec((LANES, LANES), lambda i: (i, 0))
    return pl.pallas_call(
        _mlstm_state_kernel,
        grid=(s // MLSTM_SB,),
        in_specs=[
            smem, smem, rows, rows, rows,
            pl.BlockSpec((MLSTM_SB, L_HEADS, L_HDIM, L_HDIM), lambda i: (i, 0, 0, 0)),
        ],
        out_specs=[
            pl.BlockSpec((MLSTM_SB, L_HEADS, L_HDIM, L_HDIM), lambda i: (i, 0, 0, 0)),
            rows,
        ],
        out_shape=[
            jax.ShapeDtypeStruct(c.shape, F32),
            jax.ShapeDtypeStruct(q_rows.shape, F32),
        ],
        compiler_params=_params("parallel"),
        name="mlstm_state",
    )(wr, keep, q_rows, k_rows, v_rows, c)


def _sample_post_kernel(y_ref, xc_ref, dvec_ref,
                        q_ref, k_ref, v_ref, n_ref, num_ref, wr_ref, keep_ref, mnew_ref,
                        ys_ref, hh_ref, nn_ref):
    ys_ref[...] = (y_ref[...] + dvec_ref[...] * xc_ref[...]).astype(BF16)

    wr = wr_ref[...]
    keep = keep_ref[...]
    floor = jnp.exp(-mnew_ref[...])
    for h in range(L_HEADS):
        ln = F_LANE + h
        sl = slice(h * L_HDIM, (h + 1) * L_HDIM)
        q = q_ref[:, sl]
        k = k_ref[:, sl]
        n_old = n_ref[:, sl]
        wrc = wr[:, ln:ln + 1]
        kpc = keep[:, ln:ln + 1]
        wgt = jnp.sum(q * k, axis=1, keepdims=True) * wrc
        num = wgt * v_ref[:, sl].astype(F32) + kpc * num_ref[:, sl]
        den = wgt + kpc * jnp.sum(q * n_old, axis=1, keepdims=True)
        hh_ref[:, sl] = (num / jnp.maximum(jnp.abs(den), floor[:, ln:ln + 1])).astype(BF16)
        nn_ref[:, sl] = kpc * n_old + wrc * k


def _sample_post(y, xc, pbig, dvec, q, k, n_rows, num, wr, keep, mnew):
    s = y.shape[0]
    full = lambda shape: pl.BlockSpec(shape, lambda i: (0, 0))
    blk = lambda width, j: pl.BlockSpec((s, width), lambda i, j=j: (0, j))
    return pl.pallas_call(
        _sample_post_kernel,
        grid=(1,),
        in_specs=[
            full((s, M_INNER)), full((s, M_INNER)), full((1, M_INNER)),
            full((s, L_INNER)), full((s, L_INNER)), blk(L_INNER, 7),
            full((s, L_INNER)), full((s, L_INNER)),
            full((s, LANES)), full((s, LANES)), full((s, LANES)),
        ],
        out_specs=[full((s, M_INNER)), full((s, L_INNER)), full((s, L_INNER))],
        out_shape=[
            jax.ShapeDtypeStruct((s, M_INNER), BF16),
            jax.ShapeDtypeStruct((s, L_INNER), BF16),
            jax.ShapeDtypeStruct((s, L_INNER), F32),
        ],
        compiler_params=_params("arbitrary"),
        name="sample_post",
    )(y, xc, dvec, q, k, pbig, n_rows, num, wr, keep, mnew)


def _lanes(vec, first_lane):
    n = vec.shape[0]
    return jnp.pad(vec.astype(F32), (first_lane, LANES - first_lane - n)).reshape(1, LANES)


def _spread_matrix(first_lane, n_heads, width):
    r = lax.broadcasted_iota(jnp.int32, (2 * LANES, n_heads * width), 0) % LANES
    c = lax.broadcasted_iota(jnp.int32, (2 * LANES, n_heads * width), 1)
    return (r - first_lane == c // width).astype(BF16)


def _shift_matrix(n):
    r = lax.broadcasted_iota(jnp.int32, ((CONV_W - 1) * n, 2 * n), 0)
    c = lax.broadcasted_iota(jnp.int32, ((CONV_W - 1) * n, 2 * n), 1)
    return (c == n + r % n - (r // n + 1)).astype(BF16)


def kernel(x_prompt, x_sample, state_mamba_conv, state_mamba_ssm, state_mlstm_conv, state_mlstm_C, state_mlstm_n, state_mlstm_m, w_in, mamba_conv_w, mamba_conv_b, mamba_dt_bias, mamba_A_log, mamba_D, mamba_norm_w, w_branch_a, mlstm_conv_w, mlstm_conv_b, mlstm_wq, mlstm_wk, mlstm_i_bias, mlstm_f_bias, mlstm_norm_w, w_branch_b, w_out, norm_mix_w, norm_mlp_w, w_up, w_down, final_norm_w):
    depth = w_in.shape[0]
    assert depth == 1
    batch, seq, _ = x_prompt.shape
    n_samp, dec_seq, _ = x_sample.shape
    assert dec_seq == 1 and seq % CHUNK == 0 and seq >= SUBLANES
    assert n_samp % MLSTM_SB == 0 and n_samp % SSD_SB == 0 and batch % SCAN_NB == 0
    l = 0

    w = w_in[l]
    c0 = M_INNER + M_CONV
    c1 = c0 + M_HEADS
    c2 = c1 + 3 * L_INNER
    c3 = c2 + 2 * L_HEADS
    w_big = jnp.concatenate([w[:, :c0], w[:, c1:c2], w[:, c3:]], axis=1).astype(BF16)
    w_small = jnp.concatenate(
        [w[:, c0:c1], w[:, c2:c3], jnp.zeros((D_MODEL, LANES - M_HEADS - 2 * L_HEADS), F32)],
        axis=1).astype(BF16)
    g_mix = norm_mix_w[l].reshape(1, D_MODEL)
    cwm = mamba_conv_w[l]
    cbm = mamba_conv_b[l].reshape(1, M_CONV)
    cwl = mlstm_conv_w[l]
    cbl = mlstm_conv_b[l].reshape(1, L_INNER)
    dtb = _lanes(mamba_dt_bias[l], DT_LANE)
    alog = _lanes(mamba_A_log[l], DT_LANE)
    ib = _lanes(mlstm_i_bias[l], I_LANE)
    fb = _lanes(mlstm_f_bias[l], F_LANE)
    dvec = jnp.repeat(mamba_D[l].astype(F32), M_HDIM).reshape(1, M_INNER)
    mnw = mamba_norm_w[l].reshape(1, M_INNER)
    lnw = mlstm_norm_w[l].reshape(1, L_INNER)
    wq = mlstm_wq[l].astype(BF16)
    wk = mlstm_wk[l].astype(BF16)
    wa = w_branch_a[l].astype(BF16)
    wb = w_branch_b[l].astype(BF16)
    wo = w_out[l].astype(BF16)
    wup = w_up[l].astype(BF16)
    wdn = w_down[l].astype(BF16)
    nmw = norm_mlp_w[l].reshape(1, D_MODEL)
    fw = final_norm_w.reshape(1, D_MODEL)
    e64 = _spread_matrix(DT_LANE, M_HEADS, M_HDIM)
    el = _spread_matrix(F_LANE, L_HEADS, L_HDIM)
    shift = _shift_matrix(CHUNK)

    xp = x_prompt.reshape(batch * seq, D_MODEL)
    pbig, psmall = _inproj(xp, g_mix, w_big, w_small, tm=2048)
    ya, hb, p_ssm, p_c, p_n, p_m = _scan_prompt(
        pbig, psmall, cwm[:, :M_INNER], cbm[:, :M_INNER], cwm[:, M_INNER:], cbm[:, M_INNER:],
        cwl, cbl, dtb, alog, dvec, wq, wk, ib, fb, e64, el, shift, batch, seq)
    y_prompt = _tail(ya.reshape(batch * seq, M_INNER), hb.reshape(batch * seq, L_INNER), pbig, xp,
                     mnw, lnw, wa, wb, wo, nmw, wup, wdn, fw, tm=512)
    p3 = pbig.reshape(batch, seq, BIG_WIDTH)
    p_conv_m = p3[:, seq - (CONV_W - 1):, 2 * PROJ_BLOCK:6 * PROJ_BLOCK].astype(F32)
    p_conv_l = p3[:, seq - (CONV_W - 1):, 6 * PROJ_BLOCK:7 * PROJ_BLOCK].astype(F32)
    p_ssm = p_ssm.reshape(batch, M_HEADS, M_HDIM, M_STATE)
    p_m = p_m[:, 0, F_LANE:F_LANE + L_HEADS]

    xs = x_sample.reshape(n_samp, D_MODEL)
    sbig, ssmall = _inproj(xs, g_mix, w_big, w_small, tm=n_samp)
    m_lanes = jnp.pad(state_mlstm_m[l], ((0, 0), (F_LANE, LANES - F_LANE - L_HEADS)))
    (xc, bcc, q, k, dt, da, wr, keep, mnew, s_conv_m, s_conv_l) = _sample_pre(
        sbig, ssmall, state_mamba_conv[l].reshape(n_samp, (CONV_W - 1) * M_CONV),
        state_mlstm_conv[l].reshape(n_samp, (CONV_W - 1) * L_INNER), m_lanes,
        cwm, cbm, cwl, cbl, wq, wk, dtb, alog, ib, fb)
    s_ssm, y_rows = _ssd_state(
        dt[:, :M_HEADS], da[:, :M_HEADS], xc,
        bcc[:, :M_GROUPS * M_STATE].reshape(n_samp, M_GROUPS, M_STATE),
        bcc[:, M_GROUPS * M_STATE:].reshape(n_samp, M_GROUPS, M_STATE),
        state_mamba_ssm[l].reshape(n_samp, M_PAIRS, LANES, M_STATE))
    s_c, num_rows = _mlstm_state(
        wr[:, F_LANE:F_LANE + L_HEADS], keep[:, F_LANE:F_LANE + L_HEADS],
        q.reshape(n_samp * L_HEADS, L_HDIM), k.reshape(n_samp * L_HEADS, L_HDIM),
        sbig[:, 7 * PROJ_BLOCK:8 * PROJ_BLOCK].astype(F32).reshape(n_samp * L_HEADS, L_HDIM),
        state_mlstm_C[l])
    ya_s, hb_s, s_n = _sample_post(
        y_rows.reshape(n_samp, M_INNER), xc, sbig, dvec, q, k,
        state_mlstm_n[l].reshape(n_samp, L_INNER), num_rows.reshape(n_samp, L_INNER),
        wr, keep, mnew)
    y_sample = _tail(ya_s, hb_s, sbig, xs, mnw, lnw, wa, wb, wo, nmw, wup, wdn, fw, tm=n_samp)

    lead = lambda a: a[None]
    return (
        y_prompt.reshape(batch, seq, D_MODEL),
        y_sample.reshape(n_samp, 1, D_MODEL),
        lead(p_conv_m), lead(p_ssm), lead(p_conv_l), lead(p_c), lead(p_n), lead(p_m),
        lead(s_conv_m.reshape(n_samp, CONV_W - 1, M_CONV)),
        lead(s_ssm.reshape(n_samp, M_HEADS, M_HDIM, M_STATE)),
        lead(s_conv_l.reshape(n_samp, CONV_W - 1, L_INNER)),
        lead(s_c), lead(s_n.reshape(n_samp, L_HEADS, L_HDIM)), lead(mnew[:, F_LANE:F_LANE + L_HEADS]),
    )
```

```python
import jax
import jax.numpy as jnp
from jax import lax
from jax.experimental import pallas as pl
from jax.experimental.pallas import tpu as pltpu

F32 = jnp.float32
BF16 = jnp.bfloat16

D_MODEL = 1024
M_INNER = 2048
M_HEADS = 32
M_HDIM = 64
M_GROUPS = 8
M_PAIRS = M_HEADS // 2
M_STATE = 128
M_CONV = 4096
L_INNER = 1024
L_HEADS = 8
L_HDIM = 128
D_FF = 4096
CONV_W = 4
CHUNK = 128
EPS = 1e-6

LANES = 128
SUBLANES = 8
PROJ_BLOCK = 1024
N_PROJ_BLOCKS = 11
BIG_WIDTH = PROJ_BLOCK * N_PROJ_BLOCKS
DT_LANE = 0
I_LANE = 32
F_LANE = 40
VMEM_LIMIT = 56 * 1024 * 1024

HIGHEST = lax.Precision.HIGHEST
LOG2_E = 1.4426950408889634
NT_DIMS = (((1,), (1,)), ((), ()))
TN_DIMS = (((0,), (0,)), ((), ()))


def _params(*sem):
    return pltpu.CompilerParams(dimension_semantics=sem, vmem_limit_bytes=VMEM_LIMIT)


def _sigmoid(x):
    return 0.5 * jnp.tanh(0.5 * x) + 0.5


def _silu(x):
    h = 0.5 * x
    return h + h * jnp.tanh(h)


def _log1p_exp_neg_abs(x):
    e = jnp.exp(-jnp.abs(x))
    u = 1.0 + e
    return jnp.where(u == 1.0, e, jnp.log(u) * (e / (u - 1.0)))


def _softplus(x):
    return jnp.maximum(x, 0.0) + _log1p_exp_neg_abs(x)


def _log_sigmoid(x):
    return jnp.minimum(x, 0.0) - _log1p_exp_neg_abs(x)


def _rms(x, w):
    return x * lax.rsqrt(jnp.mean(x * x, axis=-1, keepdims=True) + EPS) * w


def _lane_iota(shape):
    return lax.broadcasted_iota(jnp.int32, shape, len(shape) - 1)


def _tri(n):
    r = lax.broadcasted_iota(jnp.int32, (n, n), 0)
    c = lax.broadcasted_iota(jnp.int32, (n, n), 1)
    return r >= c


def _split3(a):
    hi = a.astype(BF16)
    r1 = a - hi.astype(F32)
    mid = r1.astype(BF16)
    lo = (r1 - mid.astype(F32)).astype(BF16)
    return hi, mid, lo


def _cumsum_rows(causal, a):
    tri01 = causal.astype(F32).astype(BF16)
    return jnp.dot(jnp.concatenate([tri01] * 3, axis=1), jnp.concatenate(_split3(a), axis=0),
                   preferred_element_type=F32)


def _spread(a, e2_ref):
    hi, mid, _ = _split3(a)
    return jnp.dot(jnp.concatenate([hi, mid], axis=1), e2_ref[...], preferred_element_type=F32)


CONV_COLS = 512


def _conv_silu(x_ref, win_ref, shift_ref, w_ref, b_ref, out_ref):
    n, width = x_ref.shape
    tile = 16
    win_ref[n:2 * n, :] = x_ref[...]
    for c0 in range(0, width, CONV_COLS):
        cs = slice(c0, c0 + CONV_COLS)
        sh = jnp.dot(shift_ref[...], win_ref[:, cs], preferred_element_type=F32)
        y = b_ref[:, cs] + w_ref[CONV_W - 1:CONV_W, cs] * x_ref[:, cs].astype(F32)
        for back in range(1, CONV_W):
            y = y + w_ref[CONV_W - 1 - back:CONV_W - back, cs] * sh[(back - 1) * n:back * n]
        out_ref[:, cs] = _silu(y).astype(out_ref.dtype)
    win_ref[n - tile:n, :] = x_ref[n - tile:n, :]


PIECE_BLOCKS = (6, 3, 2)
assert sum(PIECE_BLOCKS) == N_PROJ_BLOCKS


def _inproj_kernel(x_ref, g_ref, w0_ref, w1_ref, w2_ref, ws_ref, o_ref, os_ref, xn_ref):
    j = pl.program_id(1)

    @pl.when(j == 0)
    def _():
        xn_ref[...] = _rms(x_ref[...], g_ref[...]).astype(BF16)
        os_ref[...] = jnp.dot(xn_ref[...], ws_ref[...], preferred_element_type=F32)

    def project(w_ref):
        o_ref[...] = jnp.dot(xn_ref[...], w_ref[...], preferred_element_type=F32).astype(BF16)

    first1 = PIECE_BLOCKS[0]
    first2 = PIECE_BLOCKS[0] + PIECE_BLOCKS[1]
    pl.when(j < first1)(lambda: project(w0_ref))
    pl.when((j >= first1) & (j < first2))(lambda: project(w1_ref))
    pl.when(j >= first2)(lambda: project(w2_ref))


def _inproj(x, g, w_pieces, w_small, tm):
    m = x.shape[0]
    tm = min(tm, m)
    first1 = PIECE_BLOCKS[0]
    first2 = PIECE_BLOCKS[0] + PIECE_BLOCKS[1]
    piece_maps = (
        lambda i, j: (0, jnp.minimum(j, first1 - 1)),
        lambda i, j: (0, jnp.clip(j - first1, 0, PIECE_BLOCKS[1] - 1)),
        lambda i, j: (0, jnp.maximum(j - first2, 0)),
    )
    return pl.pallas_call(
        _inproj_kernel,
        grid=(m // tm, N_PROJ_BLOCKS),
        in_specs=[
            pl.BlockSpec((tm, D_MODEL), lambda i, j: (i, 0)),
            pl.BlockSpec((1, D_MODEL), lambda i, j: (0, 0)),
            pl.BlockSpec((D_MODEL, PROJ_BLOCK), piece_maps[0]),
            pl.BlockSpec((D_MODEL, PROJ_BLOCK), piece_maps[1]),
            pl.BlockSpec((D_MODEL, PROJ_BLOCK), piece_maps[2]),
            pl.BlockSpec((D_MODEL, LANES), lambda i, j: (0, 0)),
        ],
        out_specs=[
            pl.BlockSpec((tm, PROJ_BLOCK), lambda i, j: (i, j)),
            pl.BlockSpec((tm, LANES), lambda i, j: (i, 0)),
        ],
        out_shape=[
            jax.ShapeDtypeStruct((m, BIG_WIDTH), BF16),
            jax.ShapeDtypeStruct((m, LANES), F32),
        ],
        scratch_shapes=[pltpu.VMEM((tm, D_MODEL), BF16)],
        compiler_params=_params("parallel", "arbitrary"),
        name="in_proj",
    )(x, g, *w_pieces, w_small)


SCAN_NB = 2


def _ssd_chunk(x_ref, bc_ref, sm, cwx_ref, cbx_ref, cwbc_ref, cbbc_ref, dtb_ref, alog_ref, dvec_ref,
               e64_ref, shift_ref, y_ref, h_ref, winx_ref, winbc_ref, xc_scr, bcc_scr):
    n = x_ref.shape[0]
    _conv_silu(x_ref, winx_ref, shift_ref, cwx_ref, cbx_ref, xc_scr)
    _conv_silu(bc_ref, winbc_ref, shift_ref, cwbc_ref, cbbc_ref, bcc_scr)
    xc = xc_scr[...]
    bcc = bcc_scr[...]

    lane = _lane_iota((n, LANES))
    dt = _softplus(sm + dtb_ref[...])
    da = jnp.where(lane < M_HEADS, dt * (-LOG2_E * jnp.exp(alog_ref[...])), 0.0)
    causal = _tri(n)
    acum = _cumsum_rows(causal, da)
    acum_t = acum.T
    last = acum[n - 1:n, :]
    exp_last = jnp.exp2(last)
    dt_x = _spread(dt, e64_ref)
    wr_x = _spread(jnp.exp2(last - acum), e64_ref)
    ea_x = _spread(jnp.exp2(acum), e64_ref)

    xdt = xc * dt_x
    xw = (xdt * wr_x).astype(BF16)
    xb = xdt.astype(BF16)
    low_half = jnp.bitwise_and(_lane_iota((n, M_INNER)), LANES - 1) < M_HDIM
    zero = jnp.zeros((), BF16)
    x_lo = jnp.where(low_half, xb, zero)
    x_hi = jnp.where(low_half, zero, xb)
    first_rows = lax.broadcasted_iota(jnp.int32, (LANES, LANES), 0) < M_HDIM

    def weights(cb, h):
        seg = acum[:, h:h + 1] - acum_t[h:h + 1, :]
        return (cb * jnp.exp2(jnp.where(causal, seg, -jnp.inf))).astype(BF16)

    pairs_per_group = M_PAIRS // M_GROUPS
    for g in range(M_GROUPS):
        bg = bcc[:, g * M_STATE:(g + 1) * M_STATE]
        cg = bcc[:, (M_GROUPS + g) * M_STATE:(M_GROUPS + g + 1) * M_STATE]
        cb = lax.dot_general(cg, bg, NT_DIMS, preferred_element_type=F32)
        for pp in range(pairs_per_group):
            hp = g * pairs_per_group + pp
            h0, h1 = 2 * hp, 2 * hp + 1
            sl = slice(hp * LANES, (hp + 1) * LANES)
            hs = h_ref[hp]
            y = jnp.dot(weights(cb, h0), x_lo[:, sl], preferred_element_type=F32)
            y = y + jnp.dot(weights(cb, h1), x_hi[:, sl], preferred_element_type=F32)
            ys = lax.dot_general(cg, hs.astype(BF16), NT_DIMS, preferred_element_type=F32)
            y_ref[:, sl] = (y + ea_x[:, sl] * ys + dvec_ref[:, sl] * xc[:, sl]).astype(BF16)
            el = jnp.where(first_rows, exp_last[:, h0:h0 + 1], exp_last[:, h1:h1 + 1])
            h_ref[hp] = el * hs + lax.dot_general(xw[:, sl], bg, TN_DIMS, preferred_element_type=F32)


def _mlstm_chunks(seqs, wq_ref, wk_ref, ib_ref, fb_ref, el_ref):
    n = seqs[0][0].shape[0]
    causal = _tri(n)
    lane = _lane_iota((n, LANES))
    gate_lanes = (lane >= F_LANE) & (lane < F_LANE + L_HEADS)
    scale = L_HDIM ** -0.5
    lanes_of = [slice(h * L_HDIM, (h + 1) * L_HDIM) for h in range(L_HEADS)]

    gates = []
    for uc, v_ref, sm, hh_ref, c_ref, n_ref, m_ref in seqs:
        ig = pltpu.roll(sm + ib_ref[...], F_LANE - I_LANE, axis=1)
        logf = jnp.where(gate_lanes, _log_sigmoid(sm + fb_ref[...]), 0.0)
        bcum = _cumsum_rows(causal, logf)
        m_old = m_ref[...]
        bl = bcum[n - 1:n, :]
        src = bl - bcum + ig
        m_new = jnp.maximum(bl + m_old, jnp.max(src, axis=0, keepdims=True))
        gates.append(dict(
            bcum=bcum, bcum_t=bcum.T, ig_t=ig.T, m_old=m_old, m_new=m_new,
            keep=jnp.exp(bl + m_old - m_new),
            wr_x=_spread(jnp.exp(src - m_new), el_ref)))

    work = [(i, h) for i in range(len(seqs)) for h in range(L_HEADS)]
    c_olds = {(i, h): seqs[i][4][h] for i, h in work}
    n_olds = {(i, h): seqs[i][5][h:h + 1, :] for i, h in work}
    qs, ks, qbs, kbs = {}, {}, {}, {}
    for i, h in work:
        ub = seqs[i][0][:, lanes_of[h]]
        qs[i, h] = jnp.dot(ub, wq_ref[h], preferred_element_type=F32)
        ks[i, h] = jnp.dot(ub, wk_ref[h], preferred_element_type=F32) * scale
    logds, inters, rowmax = {}, {}, {}
    for i, h in work:
        g = gates[i]
        ln = F_LANE + h
        bcol = jnp.broadcast_to(g["bcum"][:, ln:ln + 1], (n, n))
        logd = jnp.where(causal, bcol - g["bcum_t"][ln:ln + 1, :] + g["ig_t"][ln:ln + 1, :], -jnp.inf)
        logds[i, h] = logd
        inters[i, h] = bcol + g["m_old"][:, ln:ln + 1]
        rowmax[i, h] = jnp.max(logd, axis=1, keepdims=True)
    ss, scs, floors = {}, {}, {}
    for i, h in work:
        qbs[i, h] = qs[i, h].astype(BF16)
        kbs[i, h] = ks[i, h].astype(BF16)
        m_s = jnp.maximum(inters[i, h], rowmax[i, h])
        dm = jnp.exp(logds[i, h] - m_s)
        scs[i, h] = jnp.exp(inters[i, h] - m_s)
        floors[i, h] = jnp.exp(-m_s)
        ss[i, h] = lax.dot_general(qbs[i, h], kbs[i, h], NT_DIMS, preferred_element_type=F32) * dm
    nums, dens = {}, {}
    for i, h in work:
        vb = seqs[i][1][:, lanes_of[h]]
        num = jnp.dot(ss[i, h].astype(BF16), vb, preferred_element_type=F32)
        nums[i, h] = num + scs[i, h] * jnp.dot(
            qbs[i, h], c_olds[i, h].astype(BF16), preferred_element_type=F32)
        dens[i, h] = (jnp.sum(ss[i, h], axis=1, keepdims=True)
                      + scs[i, h] * jnp.sum(qs[i, h] * n_olds[i, h], axis=1, keepdims=True))
    for i, h in work:
        hh = nums[i, h] / jnp.maximum(jnp.abs(dens[i, h]), floors[i, h])
        seqs[i][3][:, lanes_of[h]] = hh.astype(BF16)
    for i, h in work:
        g = gates[i]
        ln = F_LANE + h
        kw = ks[i, h] * g["wr_x"][:, lanes_of[h]]
        kp = g["keep"][:, ln:ln + 1]
        seqs[i][4][h] = kp * c_olds[i, h] + lax.dot_general(
            kw.astype(BF16), seqs[i][1][:, lanes_of[h]], TN_DIMS, preferred_element_type=F32)
        seqs[i][5][h:h + 1, :] = kp * n_olds[i, h] + jnp.sum(kw, axis=0, keepdims=True)
    for i in range(len(seqs)):
        seqs[i][6][...] = gates[i]["m_new"]


def _scan_kernel(x_ref, bc_ref, u_ref, v_ref, sm_ref,
                 cwx_ref, cbx_ref, cwbc_ref, cbbc_ref, cwl_ref, cbl_ref, dtb_ref, alog_ref, dvec_ref,
                 wq_ref, wk_ref, ib_ref, fb_ref, e64_ref, el_ref, shift_ref,
                 y_ref, hh_ref, h_ref, c_ref, n_ref, m_ref,
                 winx_ref, winbc_ref, winu_ref, xc_scr, bcc_scr, uc_scr):
    n = x_ref.shape[1]

    @pl.when(pl.program_id(1) == 0)
    def _():
        h_ref[...] = jnp.zeros_like(h_ref)
        c_ref[...] = jnp.zeros_like(c_ref)
        n_ref[...] = jnp.zeros_like(n_ref)
        m_ref[...] = jnp.zeros_like(m_ref)
        winx_ref[:, 0:n, :] = jnp.zeros((SCAN_NB, n, M_INNER), BF16)
        winbc_ref[:, 0:n, :] = jnp.zeros((SCAN_NB, n, M_INNER), BF16)
        winu_ref[:, 0:n, :] = jnp.zeros((SCAN_NB, n, L_INNER), BF16)

    seqs = []
    for i in range(SCAN_NB):
        _ssd_chunk(x_ref.at[i], bc_ref.at[i], sm_ref[i], cwx_ref, cbx_ref, cwbc_ref, cbbc_ref,
                   dtb_ref, alog_ref, dvec_ref, e64_ref, shift_ref, y_ref.at[i], h_ref.at[i],
                   winx_ref.at[i], winbc_ref.at[i], xc_scr, bcc_scr)
        _conv_silu(u_ref.at[i], winu_ref.at[i], shift_ref, cwl_ref, cbl_ref, uc_scr.at[i])
        seqs.append((uc_scr.at[i], v_ref.at[i], sm_ref[i], hh_ref.at[i], c_ref.at[i], n_ref.at[i],
                     m_ref.at[i]))
    _mlstm_chunks(seqs, wq_ref, wk_ref, ib_ref, fb_ref, el_ref)


def _scan_prompt(pbig, psmall, cwx, cbx, cwbc, cbbc, cwl, cbl, dtb, alog, dvec, wq, wk, ib, fb,
                 e64, el, shift, batch, seq):
    p3 = pbig.reshape(batch, seq, BIG_WIDTH)
    s3 = psmall.reshape(batch, seq, LANES)
    nb = SCAN_NB
    const2 = lambda b, c: (0, 0)
    const3 = lambda b, c: (0, 0, 0)
    rows = lambda width, col: pl.BlockSpec((nb, CHUNK, width), lambda b, c, col=col: (b, c, col))
    state4 = lambda d1, d2, d3: pl.BlockSpec((nb, d1, d2, d3), lambda b, c: (b, 0, 0, 0))
    return pl.pallas_call(
        _scan_kernel,
        grid=(batch // nb, seq // CHUNK),
        in_specs=[
            rows(M_INNER, 1),
            rows(M_INNER, 2),
            rows(L_INNER, 6),
            rows(L_INNER, 7),
            rows(LANES, 0),
            pl.BlockSpec((CONV_W, M_INNER), const2), pl.BlockSpec((1, M_INNER), const2),
            pl.BlockSpec((CONV_W, M_INNER), const2), pl.BlockSpec((1, M_INNER), const2),
            pl.BlockSpec((CONV_W, L_INNER), const2), pl.BlockSpec((1, L_INNER), const2),
            pl.BlockSpec((1, LANES), const2), pl.BlockSpec((1, LANES), const2),
            pl.BlockSpec((1, M_INNER), const2),
            pl.BlockSpec((L_HEADS, L_HDIM, L_HDIM), const3),
            pl.BlockSpec((L_HEADS, L_HDIM, L_HDIM), const3),
            pl.BlockSpec((1, LANES), const2), pl.BlockSpec((1, LANES), const2),
            pl.BlockSpec((2 * LANES, M_INNER), const2),
            pl.BlockSpec((2 * LANES, L_INNER), const2),
            pl.BlockSpec(((CONV_W - 1) * CHUNK, 2 * CHUNK), const2),
        ],
        out_specs=[
            rows(M_INNER, 0),
            rows(L_INNER, 0),
            state4(M_PAIRS, LANES, M_STATE),
            state4(L_HEADS, L_HDIM, L_HDIM),
            pl.BlockSpec((nb, L_HEADS, L_HDIM), lambda b, c: (b, 0, 0)),
            pl.BlockSpec((nb, 1, LANES), lambda b, c: (b, 0, 0)),
        ],
        out_shape=[
            jax.ShapeDtypeStruct((batch, seq, M_INNER), BF16),
            jax.ShapeDtypeStruct((batch, seq, L_INNER), BF16),
            jax.ShapeDtypeStruct((batch, M_PAIRS, LANES, M_STATE), F32),
            jax.ShapeDtypeStruct((batch, L_HEADS, L_HDIM, L_HDIM), F32),
            jax.ShapeDtypeStruct((batch, L_HEADS, L_HDIM), F32),
            jax.ShapeDtypeStruct((batch, 1, LANES), F32),
        ],
        scratch_shapes=[
            pltpu.VMEM((nb, 2 * CHUNK, M_INNER), BF16),
            pltpu.VMEM((nb, 2 * CHUNK, M_INNER), BF16),
            pltpu.VMEM((nb, 2 * CHUNK, L_INNER), BF16),
            pltpu.VMEM((CHUNK, M_INNER), F32),
            pltpu.VMEM((CHUNK, M_INNER), BF16),
            pltpu.VMEM((nb, CHUNK, L_INNER), BF16),
        ],
        compiler_params=_params("parallel", "arbitrary"),
        name="scan_prompt",
    )(p3, p3, p3, p3, s3, cwx, cbx, cwbc, cbbc, cwl, cbl, dtb, alog, dvec, wq, wk, ib, fb,
      e64, el, shift)


def _ssd_kernel(x_ref, bc_ref, sm_ref, cwx_ref, cbx_ref, cwbc_ref, cbbc_ref,
                dtb_ref, alog_ref, dvec_ref, e64_ref, shift_ref,
                y_ref, h_ref, winx_ref, winbc_ref, xc_scr, bcc_scr):
    n = x_ref.shape[0]

    @pl.when(pl.program_id(1) == 0)
    def _():
        h_ref[...] = jnp.zeros_like(h_ref)
        winx_ref[0:n, :] = jnp.zeros((n, M_INNER), BF16)
        winbc_ref[0:n, :] = jnp.zeros((n, M_INNER), BF16)

    _conv_silu(x_ref, winx_ref, shift_ref, cwx_ref, cbx_ref, xc_scr)
    _conv_silu(bc_ref, winbc_ref, shift_ref, cwbc_ref, cbbc_ref, bcc_scr)
    xc = xc_scr[...]
    bcc = bcc_scr[...]

    lane = _lane_iota((n, LANES))
    dt = _softplus(sm_ref[...] + dtb_ref[...])
    da = jnp.where(lane < M_HEADS, dt * (-LOG2_E * jnp.exp(alog_ref[...])), 0.0)
    causal = _tri(n)
    acum = _cumsum_rows(causal, da)
    acum_t = acum.T
    last = acum[n - 1:n, :]
    exp_last = jnp.exp2(last)
    dt_x = _spread(dt, e64_ref)
    wr_x = _spread(jnp.exp2(last - acum), e64_ref)
    ea_x = _spread(jnp.exp2(acum), e64_ref)

    xdt = xc * dt_x
    xw = (xdt * wr_x).astype(BF16)
    xb = xdt.astype(BF16)
    low_half = jnp.bitwise_and(_lane_iota((n, M_INNER)), LANES - 1) < M_HDIM
    zero = jnp.zeros((), BF16)
    x_lo = jnp.where(low_half, xb, zero)
    x_hi = jnp.where(low_half, zero, xb)
    first_rows = lax.broadcasted_iota(jnp.int32, (LANES, LANES), 0) < M_HDIM

    def weights(cb, h):
        seg = acum[:, h:h + 1] - acum_t[h:h + 1, :]
        return (cb * jnp.exp2(jnp.where(causal, seg, -jnp.inf))).astype(BF16)

    for g in range(M_GROUPS):
        bg = bcc[:, g * M_STATE:(g + 1) * M_STATE]
        cg = bcc[:, (M_GROUPS + g) * M_STATE:(M_GROUPS + g + 1) * M_STATE]
        cb = lax.dot_general(cg, bg, NT_DIMS, preferred_element_type=F32)
        for pp in range(M_PAIRS // M_GROUPS):
            hp = g * (M_PAIRS // M_GROUPS) + pp
            h0, h1 = 2 * hp, 2 * hp + 1
            sl = slice(hp * LANES, (hp + 1) * LANES)
            hs = h_ref[0, hp]
            y = jnp.dot(weights(cb, h0), x_lo[:, sl], preferred_element_type=F32)
            y = y + jnp.dot(weights(cb, h1), x_hi[:, sl], preferred_element_type=F32)
            ys = lax.dot_general(cg, hs.astype(BF16), NT_DIMS, preferred_element_type=F32)
            y_ref[:, sl] = (y + ea_x[:, sl] * ys + dvec_ref[:, sl] * xc[:, sl]).astype(BF16)
            el = jnp.where(first_rows, exp_last[:, h0:h0 + 1], exp_last[:, h1:h1 + 1])
            h_ref[0, hp] = el * hs + lax.dot_general(xw[:, sl], bg, TN_DIMS, preferred_element_type=F32)


def _ssd_prompt(pbig, psmall, cwx, cbx, cwbc, cbbc, dtb, alog, dvec, e64, shift, batch, n_chunks):
    m = pbig.shape[0]
    row = lambda b, c: b * n_chunks + c
    const = lambda b, c: (0, 0)
    return pl.pallas_call(
        _ssd_kernel,
        grid=(batch, n_chunks),
        in_specs=[
            pl.BlockSpec((CHUNK, M_INNER), lambda b, c: (row(b, c), 1)),
            pl.BlockSpec((CHUNK, M_INNER), lambda b, c: (row(b, c), 2)),
            pl.BlockSpec((CHUNK, LANES), lambda b, c: (row(b, c), 0)),
            pl.BlockSpec((CONV_W, M_INNER), const),
            pl.BlockSpec((1, M_INNER), const),
            pl.BlockSpec((CONV_W, M_INNER), const),
            pl.BlockSpec((1, M_INNER), const),
            pl.BlockSpec((1, LANES), const),
            pl.BlockSpec((1, LANES), const),
            pl.BlockSpec((1, M_INNER), const),
            pl.BlockSpec((2 * LANES, M_INNER), const),
            pl.BlockSpec(((CONV_W - 1) * CHUNK, 2 * CHUNK), const),
        ],
        out_specs=[
            pl.BlockSpec((CHUNK, M_INNER), lambda b, c: (row(b, c), 0)),
            pl.BlockSpec((1, M_PAIRS, LANES, M_STATE), lambda b, c: (b, 0, 0, 0)),
        ],
        out_shape=[
            jax.ShapeDtypeStruct((m, M_INNER), BF16),
            jax.ShapeDtypeStruct((batch, M_PAIRS, LANES, M_STATE), F32),
        ],
        scratch_shapes=[
            pltpu.VMEM((2 * CHUNK, M_INNER), BF16),
            pltpu.VMEM((2 * CHUNK, M_INNER), BF16),
            pltpu.VMEM((CHUNK, M_INNER), F32),
            pltpu.VMEM((CHUNK, M_INNER), BF16),
        ],
        compiler_params=_params("parallel", "arbitrary"),
        name="ssd_prompt",
    )(pbig, pbig, psmall, cwx, cbx, cwbc, cbbc, dtb, alog, dvec, e64, shift)


def _mlstm_kernel(u_ref, v_ref, sm_ref, cw_ref, cb_ref, wq_ref, wk_ref,
                  ib_ref, fb_ref, el_ref, shift_ref,
                  hh_ref, c_ref, n_ref, m_ref, win_ref, uc_scr):
    n = u_ref.shape[0]

    @pl.when(pl.program_id(1) == 0)
    def _():
        c_ref[...] = jnp.zeros_like(c_ref)
        n_ref[...] = jnp.zeros_like(n_ref)
        m_ref[...] = jnp.zeros_like(m_ref)
        win_ref[0:n, :] = jnp.zeros((n, L_INNER), BF16)

    _conv_silu(u_ref, win_ref, shift_ref, cw_ref, cb_ref, uc_scr)
    uc = uc_scr[...]

    sm = sm_ref[...]
    lane = _lane_iota((n, LANES))
    gate_lanes = (lane >= F_LANE) & (lane < F_LANE + L_HEADS)
    ig = pltpu.roll(sm + ib_ref[...], F_LANE - I_LANE, axis=1)
    logf = jnp.where(gate_lanes, _log_sigmoid(sm + fb_ref[...]), 0.0)
    causal = _tri(n)
    bcum = _cumsum_rows(causal, logf)
    m_old = m_ref[0]
    bl = bcum[n - 1:n, :]
    src = bl - bcum + ig
    m_new = jnp.maximum(bl + m_old, jnp.max(src, axis=0, keepdims=True))
    keep = jnp.exp(bl + m_old - m_new)
    bcum_t = bcum.T
    ig_t = ig.T
    wr_x = _spread(jnp.exp(src - m_new), el_ref)
    scale = L_HDIM ** -0.5
    c_olds = [c_ref[0, h] for h in range(L_HEADS)]
    n_olds = [n_ref[0, h:h + 1, :] for h in range(L_HEADS)]

    heads = range(L_HEADS)
    lanes_of = [slice(h * L_HDIM, (h + 1) * L_HDIM) for h in heads]
    qs, ks, qbs, kbs = [], [], [], []
    for h in heads:
        ub = uc[:, lanes_of[h]]
        qs.append(jnp.dot(ub, wq_ref[h], preferred_element_type=F32))
        ks.append(jnp.dot(ub, wk_ref[h], preferred_element_type=F32) * scale)
    logds, inters, rowmax = [], [], []
    for h in heads:
        ln = F_LANE + h
        bcol = jnp.broadcast_to(bcum[:, ln:ln + 1], (n, n))
        logd = jnp.where(causal, bcol - bcum_t[ln:ln + 1, :] + ig_t[ln:ln + 1, :], -jnp.inf)
        logds.append(logd)
        inters.append(bcol + m_old[:, ln:ln + 1])
        rowmax.append(jnp.max(logd, axis=1, keepdims=True))
    ss, scs, floors = [], [], []
    for h in heads:
        qbs.append(qs[h].astype(BF16))
        kbs.append(ks[h].astype(BF16))
        m_s = jnp.maximum(inters[h], rowmax[h])
        dm = jnp.exp(logds[h] - m_s)
        scs.append(jnp.exp(inters[h] - m_s))
        floors.append(jnp.exp(-m_s))
        ss.append(lax.dot_general(qbs[h], kbs[h], NT_DIMS, preferred_element_type=F32) * dm)
    nums, dens = [], []
    for h in heads:
        vb = v_ref[:, lanes_of[h]]
        num = jnp.dot(ss[h].astype(BF16), vb, preferred_element_type=F32)
        num = num + scs[h] * jnp.dot(qbs[h], c_olds[h].astype(BF16), preferred_element_type=F32)
        nums.append(num)
        dens.append(jnp.sum(ss[h], axis=1, keepdims=True)
                    + scs[h] * jnp.sum(qs[h] * n_olds[h], axis=1, keepdims=True))
    for h in heads:
        hh = nums[h] / jnp.maximum(jnp.abs(dens[h]), floors[h])
        hh_ref[:, lanes_of[h]] = hh.astype(BF16)
    for h in heads:
        ln = F_LANE + h
        kw = ks[h] * wr_x[:, lanes_of[h]]
        kp = keep[:, ln:ln + 1]
        c_ref[0, h] = kp * c_olds[h] + lax.dot_general(
            kw.astype(BF16), v_ref[:, lanes_of[h]], TN_DIMS, preferred_element_type=F32)
        n_ref[0, h:h + 1, :] = kp * n_olds[h] + jnp.sum(kw, axis=0, keepdims=True)
    m_ref[0] = m_new


def _mlstm_prompt(pbig, psmall, cw, cb, wq, wk, ib, fb, el, shift, batch, n_chunks):
    m = pbig.shape[0]
    row = lambda b, c: b * n_chunks + c
    const2 = lambda b, c: (0, 0)
    return pl.pallas_call(
        _mlstm_kernel,
        grid=(batch, n_chunks),
        in_specs=[
            pl.BlockSpec((CHUNK, L_INNER), lambda b, c: (row(b, c), 6)),
            pl.BlockSpec((CHUNK, L_INNER), lambda b, c: (row(b, c), 7)),
            pl.BlockSpec((CHUNK, LANES), lambda b, c: (row(b, c), 0)),
            pl.BlockSpec((CONV_W, L_INNER), const2),
            pl.BlockSpec((1, L_INNER), const2),
            pl.BlockSpec((L_HEADS, L_HDIM, L_HDIM), lambda b, c: (0, 0, 0)),
            pl.BlockSpec((L_HEADS, L_HDIM, L_HDIM), lambda b, c: (0, 0, 0)),
            pl.BlockSpec((1, LANES), const2),
            pl.BlockSpec((1, LANES), const2),
            pl.BlockSpec((2 * LANES, L_INNER), const2),
            pl.BlockSpec(((CONV_W - 1) * CHUNK, 2 * CHUNK), const2),
        ],
        out_specs=[
            pl.BlockSpec((CHUNK, L_INNER), lambda b, c: (row(b, c), 0)),
            pl.BlockSpec((1, L_HEADS, L_HDIM, L_HDIM), lambda b, c: (b, 0, 0, 0)),
            pl.BlockSpec((1, L_HEADS, L_HDIM), lambda b, c: (b, 0, 0)),
            pl.BlockSpec((1, 1, LANES), lambda b, c: (b, 0, 0)),
        ],
        out_shape=[
            jax.ShapeDtypeStruct((m, L_INNER), BF16),
            jax.ShapeDtypeStruct((batch, L_HEADS, L_HDIM, L_HDIM), F32),
            jax.ShapeDtypeStruct((batch, L_HEADS, L_HDIM), F32),
            jax.ShapeDtypeStruct((batch, 1, LANES), F32),
        ],
        scratch_shapes=[
            pltpu.VMEM((2 * CHUNK, L_INNER), BF16),
            pltpu.VMEM((CHUNK, L_INNER), BF16),
        ],
        compiler_params=_params("parallel", "arbitrary"),
        name="mlstm_prompt",
    )(pbig, pbig, psmall, cw, cb, wq, wk, ib, fb, el, shift)


def _tail_kernel(ys_ref, hh_ref, z_ref, o_ref, ga_ref, gb_ref, x_ref, mnw_ref, lnw_ref,
                 wa_ref, wb_ref, wo_ref, nmw_ref, wup_ref, wdn_ref, fw_ref, y_ref, ya_scr, hb_scr):
    gw = M_INNER // M_GROUPS
    for g in range(M_GROUPS):
        sl = slice(g * gw, (g + 1) * gw)
        yg = ys_ref[:, sl].astype(F32) * _silu(z_ref[:, sl].astype(F32))
        ya_scr[:, sl] = _rms(yg, mnw_ref[:, sl]).astype(BF16)
    for h in range(L_HEADS):
        sl = slice(h * L_HDIM, (h + 1) * L_HDIM)
        gate = _sigmoid(o_ref[:, sl].astype(F32))
        hb_scr[:, sl] = (gate * _rms(hh_ref[:, sl].astype(F32), lnw_ref[:, sl])).astype(BF16)
    a = jnp.dot(ya_scr[...], wa_ref[...], preferred_element_type=F32)
    b = jnp.dot(hb_scr[...], wb_ref[...], preferred_element_type=F32)
    t = _sigmoid(ga_ref[...].astype(F32)) * a + _sigmoid(gb_ref[...].astype(F32)) * b
    x1 = x_ref[...] + jnp.dot(t.astype(BF16), wo_ref[...], preferred_element_type=F32)
    hn = _rms(x1, nmw_ref[...]).astype(BF16)
    acc = x1
    for c in range(D_FF // PROJ_BLOCK):
        sl = slice(c * PROJ_BLOCK, (c + 1) * PROJ_BLOCK)
        up = jnp.dot(hn, wup_ref[:, sl], preferred_element_type=F32)
        act = jnp.square(jnp.maximum(up, 0.0)).astype(BF16)
        acc = acc + jnp.dot(act, wdn_ref[sl, :], preferred_element_type=F32)
    y_ref[...] = _rms(acc, fw_ref[...])


def _tail(ys, hh, pbig, x, mnw, lnw, wa, wb, wo, nmw, wup, wdn, fw, tm):
    m = x.shape[0]
    tm = min(tm, m)
    rows = lambda i: (i, 0)
    const = lambda i: (0, 0)

    def resident(shape):
        return pl.BlockSpec(shape, const, pipeline_mode=pl.Buffered(1))

    return pl.pallas_call(
        _tail_kernel,
        grid=(m // tm,),
        in_specs=[
            pl.BlockSpec((tm, M_INNER), rows),
            pl.BlockSpec((tm, L_INNER), rows),
            pl.BlockSpec((tm, M_INNER), rows),
            pl.BlockSpec((tm, PROJ_BLOCK), lambda i: (i, 8)),
            pl.BlockSpec((tm, PROJ_BLOCK), lambda i: (i, 9)),
            pl.BlockSpec((tm, PROJ_BLOCK), lambda i: (i, 10)),
            pl.BlockSpec((tm, D_MODEL), rows),
            resident((1, M_INNER)),
            resident((1, L_INNER)),
            resident((M_INNER, D_MODEL)),
            resident((L_INNER, D_MODEL)),
            resident((D_MODEL, D_MODEL)),
            resident((1, D_MODEL)),
            resident((D_MODEL, D_FF)),
            resident((D_FF, D_MODEL)),
            resident((1, D_MODEL)),
        ],
        out_specs=pl.BlockSpec((tm, D_MODEL), rows),
        out_shape=jax.ShapeDtypeStruct((m, D_MODEL), F32),
        scratch_shapes=[pltpu.VMEM((tm, M_INNER), BF16), pltpu.VMEM((tm, L_INNER), BF16)],
        compiler_params=_params("parallel"),
        name="tail",
    )(ys, hh, pbig, pbig, pbig, pbig, x, mnw, lnw, wa, wb, wo, nmw, wup, wdn, fw)


def _sample_pre_kernel(xbc_ref, u_ref, sm_ref, cm0_ref, cm1_ref, cm2_ref, cl0_ref, cl1_ref, cl2_ref,
                       m_ref, cwm_ref, cbm_ref, cwl_ref, cbl_ref, wq_ref, wk_ref,
                       dtb_ref, alog_ref, ib_ref, fb_ref,
                       xc_ref, bcc_ref, q_ref, k_ref, dt_ref, da_ref, wr_ref, keep_ref, mnew_ref,
                       cmn_ref, cln_ref):
    xbc = xbc_ref[...].astype(F32)
    conv_m = (cbm_ref[...] + cwm_ref[0:1, :] * cm0_ref[...] + cwm_ref[1:2, :] * cm1_ref[...]
              + cwm_ref[2:3, :] * cm2_ref[...] + cwm_ref[3:4, :] * xbc)
    act = _silu(conv_m)
    xc_ref[...] = act[:, :M_INNER]
    bcc_ref[...] = act[:, M_INNER:]
    cmn_ref[:, 0:M_CONV] = cm1_ref[...]
    cmn_ref[:, M_CONV:2 * M_CONV] = cm2_ref[...]
    cmn_ref[:, 2 * M_CONV:3 * M_CONV] = xbc

    u = u_ref[...].astype(F32)
    conv_l = (cbl_ref[...] + cwl_ref[0:1, :] * cl0_ref[...] + cwl_ref[1:2, :] * cl1_ref[...]
              + cwl_ref[2:3, :] * cl2_ref[...] + cwl_ref[3:4, :] * u)
    uc = _silu(conv_l)
    cln_ref[:, 0:L_INNER] = cl1_ref[...]
    cln_ref[:, L_INNER:2 * L_INNER] = cl2_ref[...]
    cln_ref[:, 2 * L_INNER:3 * L_INNER] = u
    scale = L_HDIM ** -0.5
    for h in range(L_HEADS):
        sl = slice(h * L_HDIM, (h + 1) * L_HDIM)
        ub = uc[:, sl].astype(BF16)
        q_ref[:, sl] = jnp.dot(ub, wq_ref[h], preferred_element_type=F32)
        k_ref[:, sl] = jnp.dot(ub, wk_ref[h], preferred_element_type=F32) * scale

    sm = sm_ref[...]
    dt = _softplus(sm + dtb_ref[...])
    dt_ref[...] = dt
    da_ref[...] = jnp.exp(dt * (-jnp.exp(alog_ref[...])))
    ig = pltpu.roll(sm + ib_ref[...], F_LANE - I_LANE, axis=1)
    logf = _log_sigmoid(sm + fb_ref[...])
    m_old = m_ref[...]
    m_new = jnp.maximum(logf + m_old, ig)
    mnew_ref[...] = m_new
    wr_ref[...] = jnp.exp(ig - m_new)
    keep_ref[...] = jnp.exp(logf + m_old - m_new)


def _sample_pre(pbig, psmall, conv_m, conv_l, m_lanes, cwm, cbm, cwl, cbl, wq, wk, dtb, alog, ib, fb):
    s = pbig.shape[0]
    f = lambda shape: jax.ShapeDtypeStruct(shape, F32)
    full2 = lambda shape: pl.BlockSpec(shape, lambda i: (0, 0))
    full3 = lambda shape: pl.BlockSpec(shape, lambda i: (0, 0, 0))
    state_row = lambda width, j: pl.BlockSpec((s, width), lambda i, j=j: (0, j))
    return pl.pallas_call(
        _sample_pre_kernel,
        grid=(1,),
        in_specs=[
            full2((s, M_CONV)),
            pl.BlockSpec((s, L_INNER), lambda i: (0, 6)),
            full2((s, LANES)),
            state_row(M_CONV, 0), state_row(M_CONV, 1), state_row(M_CONV, 2),
            state_row(L_INNER, 0), state_row(L_INNER, 1), state_row(L_INNER, 2),
            full2((s, LANES)),
            full2((CONV_W, M_CONV)), full2((1, M_CONV)),
            full2((CONV_W, L_INNER)), full2((1, L_INNER)),
            full3((L_HEADS, L_HDIM, L_HDIM)), full3((L_HEADS, L_HDIM, L_HDIM)),
            full2((1, LANES)), full2((1, LANES)), full2((1, LANES)), full2((1, LANES)),
        ],
        out_specs=[
            full2((s, M_INNER)), full2((s, M_INNER)), full2((s, L_INNER)), full2((s, L_INNER)),
            full2((s, LANES)), full2((s, LANES)), full2((s, LANES)), full2((s, LANES)), full2((s, LANES)),
            full2((s, (CONV_W - 1) * M_CONV)), full2((s, (CONV_W - 1) * L_INNER)),
        ],
        out_shape=[
            f((s, M_INNER)), f((s, M_INNER)), f((s, L_INNER)), f((s, L_INNER)),
            f((s, LANES)), f((s, LANES)), f((s, LANES)), f((s, LANES)), f((s, LANES)),
            f((s, (CONV_W - 1) * M_CONV)), f((s, (CONV_W - 1) * L_INNER)),
        ],
        compiler_params=_params("arbitrary"),
        name="sample_pre",
    )(pbig[:, 2 * PROJ_BLOCK:6 * PROJ_BLOCK], pbig, psmall, conv_m, conv_m, conv_m,
      conv_l, conv_l, conv_l, m_lanes, cwm, cbm, cwl, cbl, wq, wk, dtb, alog, ib, fb)


SSD_SB = 8
MLSTM_SB = 16


def _ssd_state_kernel(dt_ref, da_ref, x_ref, b_ref, c_ref, h_ref, hn_ref, y_ref, xs_ref):
    base = pl.program_id(0) * SSD_SB
    rows = lax.broadcasted_iota(jnp.int32, (LANES, LANES), 0)
    lanes = _lane_iota((LANES, LANES))
    diag = rows == lanes
    first_rows = rows < M_HDIM
    ones = jnp.ones((LANES, LANES), BF16)
    pairs_per_group = M_PAIRS // M_GROUPS

    def spread(s):
        for hp in range(M_PAIRS):
            xrow = jnp.broadcast_to(x_ref[s:s + 1, hp * LANES:(hp + 1) * LANES], (LANES, LANES))
            xdiag = jnp.where(diag, xrow, 0.0).astype(BF16)
            xs_ref[s % 2, hp] = jnp.dot(xdiag, ones, preferred_element_type=F32)

    acc = jnp.zeros((LANES, LANES), F32)
    spread(0)
    for s in range(SSD_SB):
        if s + 1 < SSD_SB:
            spread(s + 1)
        for g in range(M_GROUPS):
            brow = b_ref[s, g:g + 1, :]
            crow = c_ref[s, g:g + 1, :]
            for pp in range(pairs_per_group):
                hp = g * pairs_per_group + pp
                h0, h1 = 2 * hp, 2 * hp + 1
                dav = jnp.where(first_rows, da_ref[base + s, h0], da_ref[base + s, h1])
                dtv = jnp.where(first_rows, dt_ref[base + s, h0], dt_ref[base + s, h1])
                hn = dav * h_ref[s, hp] + (dtv * xs_ref[s % 2, hp]) * brow
                hn_ref[s, hp] = hn
                ysum = jnp.dot((hn * crow).astype(BF16), ones, preferred_element_type=F32)
                acc = jnp.where(lanes == s * M_PAIRS + hp, ysum, acc)
    y_ref[...] = acc.T


def _ssd_state(dt, da, xc, b3, c3, h):
    s = h.shape[0]
    smem = pl.BlockSpec(memory_space=pltpu.SMEM)
    return pl.pallas_call(
        _ssd_state_kernel,
        grid=(s // SSD_SB,),
        in_specs=[
            smem, smem,
            pl.BlockSpec((SSD_SB, M_INNER), lambda i: (i, 0)),
            pl.BlockSpec((SSD_SB, M_GROUPS, M_STATE), lambda i: (i, 0, 0)),
            pl.BlockSpec((SSD_SB, M_GROUPS, M_STATE), lambda i: (i, 0, 0)),
            pl.BlockSpec((SSD_SB, M_PAIRS, LANES, M_STATE), lambda i: (i, 0, 0, 0)),
        ],
        out_specs=[
            pl.BlockSpec((SSD_SB, M_PAIRS, LANES, M_STATE), lambda i: (i, 0, 0, 0)),
            pl.BlockSpec((SSD_SB * M_PAIRS, LANES), lambda i: (i, 0)),
        ],
        out_shape=[
            jax.ShapeDtypeStruct(h.shape, F32),
            jax.ShapeDtypeStruct((s * M_PAIRS, LANES), F32),
        ],
        scratch_shapes=[pltpu.VMEM((2, M_PAIRS, LANES, LANES), F32)],
        compiler_params=_params("parallel"),
        name="ssd_state",
    )(dt, da, xc, b3, c3, h)


def _mlstm_state_kernel(wr_ref, keep_ref, q_ref, k_ref, v_ref, c_ref, cn_ref, num_ref):
    base = pl.program_id(0) * MLSTM_SB
    qt = q_ref[...].T
    kt = k_ref[...].T
    for s in range(MLSTM_SB):
        for h in range(L_HEADS):
            r = s * L_HEADS + h
            wr = wr_ref[base + s, h]
            keep = keep_ref[base + s, h]
            c_old = c_ref[s, h]
            num_ref[r:r + 1, :] = jnp.sum(qt[:, r:r + 1] * c_old, axis=0, keepdims=True)
            cn_ref[s, h] = keep * c_old + (kt[:, r:r + 1] * wr) * v_ref[r:r + 1, :]


def _mlstm_state(wr, keep, q_rows, k_rows, v_rows, c):
    s = c.shape[0]
    smem = pl.BlockSpec(memory_space=pltpu.SMEM)
    rows = pl.BlockSpec((LANES, LANES), lambda i: (i, 0))
    return pl.pallas_call(
        _mlstm_state_kernel,
        grid=(s // MLSTM_SB,),
        in_specs=[
            smem, smem, rows, rows, rows,
            pl.BlockSpec((MLSTM_SB, L_HEADS, L_HDIM, L_HDIM), lambda i: (i, 0, 0, 0)),
        ],
        out_specs=[
            pl.BlockSpec((MLSTM_SB, L_HEADS, L_HDIM, L_HDIM), lambda i: (i, 0, 0, 0)),
            rows,
        ],
        out_shape=[
            jax.ShapeDtypeStruct(c.shape, F32),
            jax.ShapeDtypeStruct(q_rows.shape, F32),
        ],
        compiler_params=_params("parallel"),
        name="mlstm_state",
    )(wr, keep, q_rows, k_rows, v_rows, c)


def _sample_post_kernel(y_ref, xc_ref, dvec_ref,
                        q_ref, k_ref, v_ref, n_ref, num_ref, wr_ref, keep_ref, mnew_ref,
                        ys_ref, hh_ref, nn_ref):
    ys_ref[...] = (y_ref[...] + dvec_ref[...] * xc_ref[...]).astype(BF16)

    wr = wr_ref[...]
    keep = keep_ref[...]
    floor = jnp.exp(-mnew_ref[...])
    for h in range(L_HEADS):
        ln = F_LANE + h
        sl = slice(h * L_HDIM, (h + 1) * L_HDIM)
        q = q_ref[:, sl]
        k = k_ref[:, sl]
        n_old = n_ref[:, sl]
        wrc = wr[:, ln:ln + 1]
        kpc = keep[:, ln:ln + 1]
        wgt = jnp.sum(q * k, axis=1, keepdims=True) * wrc
        num = wgt * v_ref[:, sl].astype(F32) + kpc * num_ref[:, sl]
        den = wgt + kpc * jnp.sum(q * n_old, axis=1, keepdims=True)
        hh_ref[:, sl] = (num / jnp.maximum(jnp.abs(den), floor[:, ln:ln + 1])).astype(BF16)
        nn_ref[:, sl] = kpc * n_old + wrc * k


def _sample_post(y, xc, pbig, dvec, q, k, n_rows, num, wr, keep, mnew):
    s = y.shape[0]
    full = lambda shape: pl.BlockSpec(shape, lambda i: (0, 0))
    blk = lambda width, j: pl.BlockSpec((s, width), lambda i, j=j: (0, j))
    return pl.pallas_call(
        _sample_post_kernel,
        grid=(1,),
        in_specs=[
            full((s, M_INNER)), full((s, M_INNER)), full((1, M_INNER)),
            full((s, L_INNER)), full((s, L_INNER)), blk(L_INNER, 7),
            full((s, L_INNER)), full((s, L_INNER)),
            full((s, LANES)), full((s, LANES)), full((s, LANES)),
        ],
        out_specs=[full((s, M_INNER)), full((s, L_INNER)), full((s, L_INNER))],
        out_shape=[
            jax.ShapeDtypeStruct((s, M_INNER), BF16),
            jax.ShapeDtypeStruct((s, L_INNER), BF16),
            jax.ShapeDtypeStruct((s, L_INNER), F32),
        ],
        compiler_params=_params("arbitrary"),
        name="sample_post",
    )(y, xc, dvec, q, k, pbig, n_rows, num, wr, keep, mnew)


def _lanes(vec, first_lane):
    n = vec.shape[0]
    return jnp.pad(vec.astype(F32), (first_lane, LANES - first_lane - n)).reshape(1, LANES)


def _spread_matrix(first_lane, n_heads, width):
    r = lax.broadcasted_iota(jnp.int32, (2 * LANES, n_heads * width), 0) % LANES
    c = lax.broadcasted_iota(jnp.int32, (2 * LANES, n_heads * width), 1)
    return (r - first_lane == c // width).astype(BF16)


def _shift_matrix(n):
    r = lax.broadcasted_iota(jnp.int32, ((CONV_W - 1) * n, 2 * n), 0)
    c = lax.broadcasted_iota(jnp.int32, ((CONV_W - 1) * n, 2 * n), 1)
    return (c == n + r % n - (r // n + 1)).astype(BF16)


def kernel(x_prompt, x_sample, state_mamba_conv, state_mamba_ssm, state_mlstm_conv, state_mlstm_C, state_mlstm_n, state_mlstm_m, w_in, mamba_conv_w, mamba_conv_b, mamba_dt_bias, mamba_A_log, mamba_D, mamba_norm_w, w_branch_a, mlstm_conv_w, mlstm_conv_b, mlstm_wq, mlstm_wk, mlstm_i_bias, mlstm_f_bias, mlstm_norm_w, w_branch_b, w_out, norm_mix_w, norm_mlp_w, w_up, w_down, final_norm_w):
    depth = w_in.shape[0]
    assert depth == 1
    batch, seq, _ = x_prompt.shape
    n_samp, dec_seq, _ = x_sample.shape
    assert dec_seq == 1 and seq % CHUNK == 0 and seq >= SUBLANES
    assert n_samp % MLSTM_SB == 0 and n_samp % SSD_SB == 0 and batch % SCAN_NB == 0
    l = 0

    w = w_in[l]
    c0 = M_INNER + M_CONV
    c1 = c0 + M_HEADS
    c2 = c1 + 3 * L_INNER
    c3 = c2 + 2 * L_HEADS
    w_pieces = (w[:, :c0].astype(BF16), w[:, c1:c2].astype(BF16), w[:, c3:].astype(BF16))
    assert (c0 - DT_LANE) % LANES == 0 and (c2 - I_LANE) % LANES == 0 and F_LANE == I_LANE + L_HEADS
    lane = _lane_iota((1, LANES))
    win_dt = w[:, c0 - DT_LANE:c0 - DT_LANE + LANES]
    win_if = w[:, c2 - I_LANE:c2 - I_LANE + LANES]
    w_small = jnp.where(
        (lane >= DT_LANE) & (lane < DT_LANE + M_HEADS), win_dt,
        jnp.where((lane >= I_LANE) & (lane < I_LANE + 2 * L_HEADS), win_if, 0.0)).astype(BF16)
    g_mix = norm_mix_w[l].reshape(1, D_MODEL)
    cwm = mamba_conv_w[l]
    cbm = mamba_conv_b[l].reshape(1, M_CONV)
    cwl = mlstm_conv_w[l]
    cbl = mlstm_conv_b[l].reshape(1, L_INNER)
    dtb = _lanes(mamba_dt_bias[l], DT_LANE)
    alog = _lanes(mamba_A_log[l], DT_LANE)
    ib = _lanes(mlstm_i_bias[l], I_LANE)
    fb = _lanes(mlstm_f_bias[l], F_LANE)
    dvec = jnp.repeat(mamba_D[l].astype(F32), M_HDIM).reshape(1, M_INNER)
    mnw = mamba_norm_w[l].reshape(1, M_INNER)
    lnw = mlstm_norm_w[l].reshape(1, L_INNER)
    wq = mlstm_wq[l].astype(BF16)
    wk = mlstm_wk[l].astype(BF16)
    wa = w_branch_a[l].astype(BF16)
    wb = w_branch_b[l].astype(BF16)
    wo = w_out[l].astype(BF16)
    wup = w_up[l].astype(BF16)
    wdn = w_down[l].astype(BF16)
    nmw = norm_mlp_w[l].reshape(1, D_MODEL)
    fw = final_norm_w.reshape(1, D_MODEL)
    e64 = _spread_matrix(DT_LANE, M_HEADS, M_HDIM)
    el = _spread_matrix(F_LANE, L_HEADS, L_HDIM)
    shift = _shift_matrix(CHUNK)

    xp = x_prompt.reshape(batch * seq, D_MODEL)
    pbig, psmall = _inproj(xp, g_mix, w_pieces, w_small, tm=2048)
    ya, hb, p_ssm, p_c, p_n, p_m = _scan_prompt(
        pbig, psmall, cwm[:, :M_INNER], cbm[:, :M_INNER], cwm[:, M_INNER:], cbm[:, M_INNER:],
        cwl, cbl, dtb, alog, dvec, wq, wk, ib, fb, e64, el, shift, batch, seq)
    y_prompt = _tail(ya.reshape(batch * seq, M_INNER), hb.reshape(batch * seq, L_INNER), pbig, xp,
                     mnw, lnw, wa, wb, wo, nmw, wup, wdn, fw, tm=512)
    p3 = pbig.reshape(batch, seq, BIG_WIDTH)
    p_conv_m = p3[:, seq - (CONV_W - 1):, 2 * PROJ_BLOCK:6 * PROJ_BLOCK].astype(F32)
    p_conv_l = p3[:, seq - (CONV_W - 1):, 6 * PROJ_BLOCK:7 * PROJ_BLOCK].astype(F32)
    p_ssm = p_ssm.reshape(batch, M_HEADS, M_HDIM, M_STATE)
    p_m = p_m[:, 0, F_LANE:F_LANE + L_HEADS]

    xs = x_sample.reshape(n_samp, D_MODEL)
    sbig, ssmall = _inproj(xs, g_mix, w_pieces, w_small, tm=n_samp)
    m_lanes = jnp.pad(state_mlstm_m[l], ((0, 0), (F_LANE, LANES - F_LANE - L_HEADS)))
    (xc, bcc, q, k, dt, da, wr, keep, mnew, s_conv_m, s_conv_l) = _sample_pre(
        sbig, ssmall, state_mamba_conv[l].reshape(n_samp, (CONV_W - 1) * M_CONV),
        state_mlstm_conv[l].reshape(n_samp, (CONV_W - 1) * L_INNER), m_lanes,
        cwm, cbm, cwl, cbl, wq, wk, dtb, alog, ib, fb)
    s_ssm, y_rows = _ssd_state(
        dt[:, :M_HEADS], da[:, :M_HEADS], xc,
        bcc[:, :M_GROUPS * M_STATE].reshape(n_samp, M_GROUPS, M_STATE),
        bcc[:, M_GROUPS * M_STATE:].reshape(n_samp, M_GROUPS, M_STATE),
        state_mamba_ssm[l].reshape(n_samp, M_PAIRS, LANES, M_STATE))
    s_c, num_rows = _mlstm_state(
        wr[:, F_LANE:F_LANE + L_HEADS], keep[:, F_LANE:F_LANE + L_HEADS],
        q.reshape(n_samp * L_HEADS, L_HDIM), k.reshape(n_samp * L_HEADS, L_HDIM),
        sbig[:, 7 * PROJ_BLOCK:8 * PROJ_BLOCK].astype(F32).reshape(n_samp * L_HEADS, L_HDIM),
        state_mlstm_C[l])
    ya_s, hb_s, s_n = _sample_post(
        y_rows.reshape(n_samp, M_INNER), xc, sbig, dvec, q, k,
        state_mlstm_n[l].reshape(n_samp, L_INNER), num_rows.reshape(n_samp, L_INNER),
        wr, keep, mnew)
    y_sample = _tail(ya_s, hb_s, sbig, xs, mnw, lnw, wa, wb, wo, nmw, wup, wdn, fw, tm=n_samp)

    lead = lambda a: a[None]
    return (
        y_prompt.reshape(batch, seq, D_MODEL),
        y_sample.reshape(n_samp, 1, D_MODEL),
        lead(p_conv_m), lead(p_ssm), lead(p_conv_l), lead(p_c), lead(p_n), lead(p_m),
        lead(s_conv_m.reshape(n_samp, CONV_W - 1, M_CONV)),
        lead(s_ssm.reshape(n_samp, M_HEADS, M_HDIM, M_STATE)),
        lead(s_conv_l.reshape(n_samp, CONV_W - 1, L_INNER)),
        lead(s_c), lead(s_n.reshape(n_samp, L_HEADS, L_HDIM)), lead(mnew[:, F_LANE:F_LANE + L_HEADS]),
    )
```

```python
import jax
import jax.numpy as jnp
from jax import lax
from jax.experimental import pallas as pl
from jax.experimental.pallas import tpu as pltpu

F32 = jnp.float32
BF16 = jnp.bfloat16

D_MODEL = 1024
M_INNER = 2048
M_HEADS = 32
M_HDIM = 64
M_GROUPS = 8
M_PAIRS = M_HEADS // 2
M_STATE = 128
M_CONV = 4096
L_INNER = 1024
L_HEADS = 8
L_HDIM = 128
D_FF = 4096
CONV_W = 4
CHUNK = 128
EPS = 1e-6

LANES = 128
SUBLANES = 8
PROJ_BLOCK = 1024
N_PROJ_BLOCKS = 11
BIG_WIDTH = PROJ_BLOCK * N_PROJ_BLOCKS
DT_LANE = 0
I_LANE = 32
F_LANE = 40
VMEM_LIMIT = 56 * 1024 * 1024

HIGHEST = lax.Precision.HIGHEST
LOG2_E = 1.4426950408889634
NT_DIMS = (((1,), (1,)), ((), ()))
TN_DIMS = (((0,), (0,)), ((), ()))


def _params(*sem):
    return pltpu.CompilerParams(dimension_semantics=sem, vmem_limit_bytes=VMEM_LIMIT)


def _sigmoid(x):
    return 0.5 * jnp.tanh(0.5 * x) + 0.5


def _silu(x):
    h = 0.5 * x
    return h + h * jnp.tanh(h)


def _log1p_exp_neg_abs(x):
    e = jnp.exp(-jnp.abs(x))
    u = 1.0 + e
    return jnp.where(u == 1.0, e, jnp.log(u) * (e / (u - 1.0)))


def _softplus(x):
    return jnp.maximum(x, 0.0) + _log1p_exp_neg_abs(x)


def _log_sigmoid(x):
    return jnp.minimum(x, 0.0) - _log1p_exp_neg_abs(x)


def _rms(x, w):
    return x * lax.rsqrt(jnp.mean(x * x, axis=-1, keepdims=True) + EPS) * w


def _lane_iota(shape):
    return lax.broadcasted_iota(jnp.int32, shape, len(shape) - 1)


def _tri(n):
    r = lax.broadcasted_iota(jnp.int32, (n, n), 0)
    c = lax.broadcasted_iota(jnp.int32, (n, n), 1)
    return r >= c


def _split3(a):
    hi = a.astype(BF16)
    r1 = a - hi.astype(F32)
    mid = r1.astype(BF16)
    lo = (r1 - mid.astype(F32)).astype(BF16)
    return hi, mid, lo


def _cumsum_rows(causal, a):
    tri01 = causal.astype(F32).astype(BF16)
    return jnp.dot(jnp.concatenate([tri01] * 3, axis=1), jnp.concatenate(_split3(a), axis=0),
                   preferred_element_type=F32)


def _spread(a, e2_ref):
    hi, mid, _ = _split3(a)
    return jnp.dot(jnp.concatenate([hi, mid], axis=1), e2_ref[...], preferred_element_type=F32)


CONV_COLS = 512


def _conv_silu(x_ref, win_ref, shift_ref, w_ref, b_ref, out_ref):
    n, width = x_ref.shape
    tile = 16
    win_ref[n:2 * n, :] = x_ref[...]
    for c0 in range(0, width, CONV_COLS):
        cs = slice(c0, c0 + CONV_COLS)
        sh = jnp.dot(shift_ref[...], win_ref[:, cs], preferred_element_type=F32)
        y = b_ref[:, cs] + w_ref[CONV_W - 1:CONV_W, cs] * x_ref[:, cs].astype(F32)
        for back in range(1, CONV_W):
            y = y + w_ref[CONV_W - 1 - back:CONV_W - back, cs] * sh[(back - 1) * n:back * n]
        out_ref[:, cs] = _silu(y).astype(out_ref.dtype)
    win_ref[n - tile:n, :] = x_ref[n - tile:n, :]


W_DT0 = M_INNER + M_CONV
W_U0 = W_DT0 + M_HEADS
W_I0 = W_U0 + 3 * L_INNER
W_G0 = W_I0 + 2 * L_HEADS
W_END = W_G0 + 2 * D_MODEL
assert (W_DT0 - DT_LANE) % LANES == 0 and (W_I0 - I_LANE) % LANES == 0 and F_LANE == I_LANE + L_HEADS
W_PREP_ROWS = 128


def _wprep_kernel(w_ref, big_ref, small_ref):
    big_ref[:, 0:W_DT0] = w_ref[:, 0:W_DT0].astype(BF16)
    big_ref[:, W_DT0:W_DT0 + 3 * L_INNER] = w_ref[:, W_U0:W_I0].astype(BF16)
    big_ref[:, W_DT0 + 3 * L_INNER:BIG_WIDTH] = w_ref[:, W_G0:W_END].astype(BF16)
    lane = _lane_iota((W_PREP_ROWS, LANES))
    win_dt = w_ref[:, W_DT0 - DT_LANE:W_DT0 - DT_LANE + LANES]
    win_if = w_ref[:, W_I0 - I_LANE:W_I0 - I_LANE + LANES]
    small = jnp.where((lane >= DT_LANE) & (lane < DT_LANE + M_HEADS), win_dt,
                      jnp.where((lane >= I_LANE) & (lane < I_LANE + 2 * L_HEADS), win_if, 0.0))
    small_ref[...] = small.astype(BF16)


def _wprep(w):
    rows = w.shape[0]
    return pl.pallas_call(
        _wprep_kernel,
        grid=(rows // W_PREP_ROWS,),
        in_specs=[pl.BlockSpec((W_PREP_ROWS, W_END), lambda i: (i, 0))],
        out_specs=[
            pl.BlockSpec((W_PREP_ROWS, BIG_WIDTH), lambda i: (i, 0)),
            pl.BlockSpec((W_PREP_ROWS, LANES), lambda i: (i, 0)),
        ],
        out_shape=[
            jax.ShapeDtypeStruct((rows, BIG_WIDTH), BF16),
            jax.ShapeDtypeStruct((rows, LANES), BF16),
        ],
        compiler_params=_params("parallel"),
        name="w_prep",
    )(w)


def _inproj_kernel(x_ref, g_ref, w_ref, ws_ref, o_ref, os_ref, xn_ref):
    @pl.when(pl.program_id(1) == 0)
    def _():
        xn_ref[...] = _rms(x_ref[...], g_ref[...]).astype(BF16)
        os_ref[...] = jnp.dot(xn_ref[...], ws_ref[...], preferred_element_type=F32)

    o_ref[...] = jnp.dot(xn_ref[...], w_ref[...], preferred_element_type=F32).astype(BF16)


def _inproj(x, g, w_big, w_small, tm):
    m = x.shape[0]
    tm = min(tm, m)
    return pl.pallas_call(
        _inproj_kernel,
        grid=(m // tm, N_PROJ_BLOCKS),
        in_specs=[
            pl.BlockSpec((tm, D_MODEL), lambda i, j: (i, 0)),
            pl.BlockSpec((1, D_MODEL), lambda i, j: (0, 0)),
            pl.BlockSpec((D_MODEL, PROJ_BLOCK), lambda i, j: (0, j)),
            pl.BlockSpec((D_MODEL, LANES), lambda i, j: (0, 0)),
        ],
        out_specs=[
            pl.BlockSpec((tm, PROJ_BLOCK), lambda i, j: (i, j)),
            pl.BlockSpec((tm, LANES), lambda i, j: (i, 0)),
        ],
        out_shape=[
            jax.ShapeDtypeStruct((m, BIG_WIDTH), BF16),
            jax.ShapeDtypeStruct((m, LANES), F32),
        ],
        scratch_shapes=[pltpu.VMEM((tm, D_MODEL), BF16)],
        compiler_params=_params("parallel", "arbitrary"),
        name="in_proj",
    )(x, g, w_big, w_small)


SCAN_NB = 2


def _ssd_chunk(x_ref, bc_ref, sm, cwx_ref, cbx_ref, cwbc_ref, cbbc_ref, dtb_ref, alog_ref, dvec_ref,
               e64_ref, shift_ref, y_ref, h_ref, winx_ref, winbc_ref, xc_scr, bcc_scr):
    n = x_ref.shape[0]
    _conv_silu(x_ref, winx_ref, shift_ref, cwx_ref, cbx_ref, xc_scr)
    _conv_silu(bc_ref, winbc_ref, shift_ref, cwbc_ref, cbbc_ref, bcc_scr)
    xc = xc_scr[...]
    bcc = bcc_scr[...]

    lane = _lane_iota((n, LANES))
    dt = _softplus(sm + dtb_ref[...])
    da = jnp.where(lane < M_HEADS, dt * (-LOG2_E * jnp.exp(alog_ref[...])), 0.0)
    causal = _tri(n)
    acum = _cumsum_rows(causal, da)
    acum_t = acum.T
    last = acum[n - 1:n, :]
    exp_last = jnp.exp2(last)
    dt_x = _spread(dt, e64_ref)
    wr_x = _spread(jnp.exp2(last - acum), e64_ref)
    ea_x = _spread(jnp.exp2(acum), e64_ref)

    xdt = xc * dt_x
    xw = (xdt * wr_x).astype(BF16)
    xb = xdt.astype(BF16)
    low_half = jnp.bitwise_and(_lane_iota((n, M_INNER)), LANES - 1) < M_HDIM
    zero = jnp.zeros((), BF16)
    x_lo = jnp.where(low_half, xb, zero)
    x_hi = jnp.where(low_half, zero, xb)
    first_rows = lax.broadcasted_iota(jnp.int32, (LANES, LANES), 0) < M_HDIM

    def weights(cb, h):
        seg = acum[:, h:h + 1] - acum_t[h:h + 1, :]
        return (cb * jnp.exp2(jnp.where(causal, seg, -jnp.inf))).astype(BF16)

    pairs_per_group = M_PAIRS // M_GROUPS
    for g in range(M_GROUPS):
        bg = bcc[:, g * M_STATE:(g + 1) * M_STATE]
        cg = bcc[:, (M_GROUPS + g) * M_STATE:(M_GROUPS + g + 1) * M_STATE]
        cb = lax.dot_general(cg, bg, NT_DIMS, preferred_element_type=F32)
        for pp in range(pairs_per_group):
            hp = g * pairs_per_group + pp
            h0, h1 = 2 * hp, 2 * hp + 1
            sl = slice(hp * LANES, (hp + 1) * LANES)
            hs = h_ref[hp]
            y = jnp.dot(weights(cb, h0), x_lo[:, sl], preferred_element_type=F32)
            y = y + jnp.dot(weights(cb, h1), x_hi[:, sl], preferred_element_type=F32)
            ys = lax.dot_general(cg, hs.astype(BF16), NT_DIMS, preferred_element_type=F32)
            y_ref[:, sl] = (y + ea_x[:, sl] * ys + dvec_ref[:, sl] * xc[:, sl]).astype(BF16)
            el = jnp.where(first_rows, exp_last[:, h0:h0 + 1], exp_last[:, h1:h1 + 1])
            h_ref[hp] = el * hs + lax.dot_general(xw[:, sl], bg, TN_DIMS, preferred_element_type=F32)


def _mlstm_chunks(seqs, wq_ref, wk_ref, ib_ref, fb_ref, el_ref):
    n = seqs[0][0].shape[0]
    causal = _tri(n)
    lane = _lane_iota((n, LANES))
    gate_lanes = (lane >= F_LANE) & (lane < F_LANE + L_HEADS)
    scale = L_HDIM ** -0.5
    lanes_of = [slice(h * L_HDIM, (h + 1) * L_HDIM) for h in range(L_HEADS)]

    gates = []
    for uc, v_ref, sm, hh_ref, c_ref, n_ref, m_ref in seqs:
        ig = pltpu.roll(sm + ib_ref[...], F_LANE - I_LANE, axis=1)
        logf = jnp.where(gate_lanes, _log_sigmoid(sm + fb_ref[...]), 0.0)
        bcum = _cumsum_rows(causal, logf)
        m_old = m_ref[...]
        bl = bcum[n - 1:n, :]
        src = bl - bcum + ig
        m_new = jnp.maximum(bl + m_old, jnp.max(src, axis=0, keepdims=True))
        gates.append(dict(
            bcum=bcum, bcum_t=bcum.T, ig_t=ig.T, m_old=m_old, m_new=m_new,
            keep=jnp.exp(bl + m_old - m_new),
            wr_x=_spread(jnp.exp(src - m_new), el_ref)))

    work = [(i, h) for i in range(len(seqs)) for h in range(L_HEADS)]
    c_olds = {(i, h): seqs[i][4][h] for i, h in work}
    n_olds = {(i, h): seqs[i][5][h:h + 1, :] for i, h in work}
    qs, ks, qbs, kbs = {}, {}, {}, {}
    for i, h in work:
        ub = seqs[i][0][:, lanes_of[h]]
        qs[i, h] = jnp.dot(ub, wq_ref[h], preferred_element_type=F32)
        ks[i, h] = jnp.dot(ub, wk_ref[h], preferred_element_type=F32) * scale
    logds, inters, rowmax = {}, {}, {}
    for i, h in work:
        g = gates[i]
        ln = F_LANE + h
        bcol = jnp.broadcast_to(g["bcum"][:, ln:ln + 1], (n, n))
        logd = jnp.where(causal, bcol - g["bcum_t"][ln:ln + 1, :] + g["ig_t"][ln:ln + 1, :], -jnp.inf)
        logds[i, h] = logd
        inters[i, h] = bcol + g["m_old"][:, ln:ln + 1]
        rowmax[i, h] = jnp.max(logd, axis=1, keepdims=True)
    ss, scs, floors = {}, {}, {}
    for i, h in work:
        qbs[i, h] = qs[i, h].astype(BF16)
        kbs[i, h] = ks[i, h].astype(BF16)
        m_s = jnp.maximum(inters[i, h], rowmax[i, h])
        dm = jnp.exp(logds[i, h] - m_s)
        scs[i, h] = jnp.exp(inters[i, h] - m_s)
        floors[i, h] = jnp.exp(-m_s)
        ss[i, h] = lax.dot_general(qbs[i, h], kbs[i, h], NT_DIMS, preferred_element_type=F32) * dm
    nums, dens = {}, {}
    for i, h in work:
        vb = seqs[i][1][:, lanes_of[h]]
        num = jnp.dot(ss[i, h].astype(BF16), vb, preferred_element_type=F32)
        nums[i, h] = num + scs[i, h] * jnp.dot(
            qbs[i, h], c_olds[i, h].astype(BF16), preferred_element_type=F32)
        dens[i, h] = (jnp.sum(ss[i, h], axis=1, keepdims=True)
                      + scs[i, h] * jnp.sum(qs[i, h] * n_olds[i, h], axis=1, keepdims=True))
    for i, h in work:
        hh = nums[i, h] / jnp.maximum(jnp.abs(dens[i, h]), floors[i, h])
        seqs[i][3][:, lanes_of[h]] = hh.astype(BF16)
    for i, h in work:
        g = gates[i]
        ln = F_LANE + h
        kw = ks[i, h] * g["wr_x"][:, lanes_of[h]]
        kp = g["keep"][:, ln:ln + 1]
        seqs[i][4][h] = kp * c_olds[i, h] + lax.dot_general(
            kw.astype(BF16), seqs[i][1][:, lanes_of[h]], TN_DIMS, preferred_element_type=F32)
        seqs[i][5][h:h + 1, :] = kp * n_olds[i, h] + jnp.sum(kw, axis=0, keepdims=True)
    for i in range(len(seqs)):
        seqs[i][6][...] = gates[i]["m_new"]


def _scan_kernel(x_ref, bc_ref, u_ref, v_ref, sm_ref,
                 cwx_ref, cbx_ref, cwbc_ref, cbbc_ref, cwl_ref, cbl_ref, dtb_ref, alog_ref, dvec_ref,
                 wq_ref, wk_ref, ib_ref, fb_ref, e64_ref, el_ref, shift_ref,
                 y_ref, hh_ref, h_ref, c_ref, n_ref, m_ref,
                 winx_ref, winbc_ref, winu_ref, xc_scr, bcc_scr, uc_scr):
    n = x_ref.shape[1]

    @pl.when(pl.program_id(1) == 0)
    def _():
        h_ref[...] = jnp.zeros_like(h_ref)
        c_ref[...] = jnp.zeros_like(c_ref)
        n_ref[...] = jnp.zeros_like(n_ref)
        m_ref[...] = jnp.zeros_like(m_ref)
        winx_ref[:, 0:n, :] = jnp.zeros((SCAN_NB, n, M_INNER), BF16)
        winbc_ref[:, 0:n, :] = jnp.zeros((SCAN_NB, n, M_INNER), BF16)
        winu_ref[:, 0:n, :] = jnp.zeros((SCAN_NB, n, L_INNER), BF16)

    seqs = []
    for i in range(SCAN_NB):
        _ssd_chunk(x_ref.at[i], bc_ref.at[i], sm_ref[i], cwx_ref, cbx_ref, cwbc_ref, cbbc_ref,
                   dtb_ref, alog_ref, dvec_ref, e64_ref, shift_ref, y_ref.at[i], h_ref.at[i],
                   winx_ref.at[i], winbc_ref.at[i], xc_scr, bcc_scr)
        _conv_silu(u_ref.at[i], winu_ref.at[i], shift_ref, cwl_ref, cbl_ref, uc_scr.at[i])
        seqs.append((uc_scr.at[i], v_ref.at[i], sm_ref[i], hh_ref.at[i], c_ref.at[i], n_ref.at[i],
                     m_ref.at[i]))
    _mlstm_chunks(seqs, wq_ref, wk_ref, ib_ref, fb_ref, el_ref)


def _scan_prompt(pbig, psmall, cwx, cbx, cwbc, cbbc, cwl, cbl, dtb, alog, dvec, wq, wk, ib, fb,
                 e64, el, shift, batch, seq):
    p3 = pbig.reshape(batch, seq, BIG_WIDTH)
    s3 = psmall.reshape(batch, seq, LANES)
    nb = SCAN_NB
    const2 = lambda b, c: (0, 0)
    const3 = lambda b, c: (0, 0, 0)
    rows = lambda width, col: pl.BlockSpec((nb, CHUNK, width), lambda b, c, col=col: (b, c, col))
    state4 = lambda d1, d2, d3: pl.BlockSpec((nb, d1, d2, d3), lambda b, c: (b, 0, 0, 0))
    return pl.pallas_call(
        _scan_kernel,
        grid=(batch // nb, seq // CHUNK),
        in_specs=[
            rows(M_INNER, 1),
            rows(M_INNER, 2),
            rows(L_INNER, 6),
            rows(L_INNER, 7),
            rows(LANES, 0),
            pl.BlockSpec((CONV_W, M_INNER), const2), pl.BlockSpec((1, M_INNER), const2),
            pl.BlockSpec((CONV_W, M_INNER), const2), pl.BlockSpec((1, M_INNER), const2),
            pl.BlockSpec((CONV_W, L_INNER), const2), pl.BlockSpec((1, L_INNER), const2),
            pl.BlockSpec((1, LANES), const2), pl.BlockSpec((1, LANES), const2),
            pl.BlockSpec((1, M_INNER), const2),
            pl.BlockSpec((L_HEADS, L_HDIM, L_HDIM), const3),
            pl.BlockSpec((L_HEADS, L_HDIM, L_HDIM), const3),
            pl.BlockSpec((1, LANES), const2), pl.BlockSpec((1, LANES), const2),
            pl.BlockSpec((2 * LANES, M_INNER), const2),
            pl.BlockSpec((2 * LANES, L_INNER), const2),
            pl.BlockSpec(((CONV_W - 1) * CHUNK, 2 * CHUNK), const2),
        ],
        out_specs=[
            rows(M_INNER, 0),
            rows(L_INNER, 0),
            state4(M_PAIRS, LANES, M_STATE),
            state4(L_HEADS, L_HDIM, L_HDIM),
            pl.BlockSpec((nb, L_HEADS, L_HDIM), lambda b, c: (b, 0, 0)),
            pl.BlockSpec((nb, 1, LANES), lambda b, c: (b, 0, 0)),
        ],
        out_shape=[
            jax.ShapeDtypeStruct((batch, seq, M_INNER), BF16),
            jax.ShapeDtypeStruct((batch, seq, L_INNER), BF16),
            jax.ShapeDtypeStruct((batch, M_PAIRS, LANES, M_STATE), F32),
            jax.ShapeDtypeStruct((batch, L_HEADS, L_HDIM, L_HDIM), F32),
            jax.ShapeDtypeStruct((batch, L_HEADS, L_HDIM), F32),
            jax.ShapeDtypeStruct((batch, 1, LANES), F32),
        ],
        scratch_shapes=[
            pltpu.VMEM((nb, 2 * CHUNK, M_INNER), BF16),
            pltpu.VMEM((nb, 2 * CHUNK, M_INNER), BF16),
            pltpu.VMEM((nb, 2 * CHUNK, L_INNER), BF16),
            pltpu.VMEM((CHUNK, M_INNER), F32),
            pltpu.VMEM((CHUNK, M_INNER), BF16),
            pltpu.VMEM((nb, CHUNK, L_INNER), BF16),
        ],
        compiler_params=_params("parallel", "arbitrary"),
        name="scan_prompt",
    )(p3, p3, p3, p3, s3, cwx, cbx, cwbc, cbbc, cwl, cbl, dtb, alog, dvec, wq, wk, ib, fb,
      e64, el, shift)


def _ssd_kernel(x_ref, bc_ref, sm_ref, cwx_ref, cbx_ref, cwbc_ref, cbbc_ref,
                dtb_ref, alog_ref, dvec_ref, e64_ref, shift_ref,
                y_ref, h_ref, winx_ref, winbc_ref, xc_scr, bcc_scr):
    n = x_ref.shape[0]

    @pl.when(pl.program_id(1) == 0)
    def _():
        h_ref[...] = jnp.zeros_like(h_ref)
        winx_ref[0:n, :] = jnp.zeros((n, M_INNER), BF16)
        winbc_ref[0:n, :] = jnp.zeros((n, M_INNER), BF16)

    _conv_silu(x_ref, winx_ref, shift_ref, cwx_ref, cbx_ref, xc_scr)
    _conv_silu(bc_ref, winbc_ref, shift_ref, cwbc_ref, cbbc_ref, bcc_scr)
    xc = xc_scr[...]
    bcc = bcc_scr[...]

    lane = _lane_iota((n, LANES))
    dt = _softplus(sm_ref[...] + dtb_ref[...])
    da = jnp.where(lane < M_HEADS, dt * (-LOG2_E * jnp.exp(alog_ref[...])), 0.0)
    causal = _tri(n)
    acum = _cumsum_rows(causal, da)
    acum_t = acum.T
    last = acum[n - 1:n, :]
    exp_last = jnp.exp2(last)
    dt_x = _spread(dt, e64_ref)
    wr_x = _spread(jnp.exp2(last - acum), e64_ref)
    ea_x = _spread(jnp.exp2(acum), e64_ref)

    xdt = xc * dt_x
    xw = (xdt * wr_x).astype(BF16)
    xb = xdt.astype(BF16)
    low_half = jnp.bitwise_and(_lane_iota((n, M_INNER)), LANES - 1) < M_HDIM
    zero = jnp.zeros((), BF16)
    x_lo = jnp.where(low_half, xb, zero)
    x_hi = jnp.where(low_half, zero, xb)
    first_rows = lax.broadcasted_iota(jnp.int32, (LANES, LANES), 0) < M_HDIM

    def weights(cb, h):
        seg = acum[:, h:h + 1] - acum_t[h:h + 1, :]
        return (cb * jnp.exp2(jnp.where(causal, seg, -jnp.inf))).astype(BF16)

    for g in range(M_GROUPS):
        bg = bcc[:, g * M_STATE:(g + 1) * M_STATE]
        cg = bcc[:, (M_GROUPS + g) * M_STATE:(M_GROUPS + g + 1) * M_STATE]
        cb = lax.dot_general(cg, bg, NT_DIMS, preferred_element_type=F32)
        for pp in range(M_PAIRS // M_GROUPS):
            hp = g * (M_PAIRS // M_GROUPS) + pp
            h0, h1 = 2 * hp, 2 * hp + 1
            sl = slice(hp * LANES, (hp + 1) * LANES)
            hs = h_ref[0, hp]
            y = jnp.dot(weights(cb, h0), x_lo[:, sl], preferred_element_type=F32)
            y = y + jnp.dot(weights(cb, h1), x_hi[:, sl], preferred_element_type=F32)
            ys = lax.dot_general(cg, hs.astype(BF16), NT_DIMS, preferred_element_type=F32)
            y_ref[:, sl] = (y + ea_x[:, sl] * ys + dvec_ref[:, sl] * xc[:, sl]).astype(BF16)
            el = jnp.where(first_rows, exp_last[:, h0:h0 + 1], exp_last[:, h1:h1 + 1])
            h_ref[0, hp] = el * hs + lax.dot_general(xw[:, sl], bg, TN_DIMS, preferred_element_type=F32)


def _ssd_prompt(pbig, psmall, cwx, cbx, cwbc, cbbc, dtb, alog, dvec, e64, shift, batch, n_chunks):
    m = pbig.shape[0]
    row = lambda b, c: b * n_chunks + c
    const = lambda b, c: (0, 0)
    return pl.pallas_call(
        _ssd_kernel,
        grid=(batch, n_chunks),
        in_specs=[
            pl.BlockSpec((CHUNK, M_INNER), lambda b, c: (row(b, c), 1)),
            pl.BlockSpec((CHUNK, M_INNER), lambda b, c: (row(b, c), 2)),
            pl.BlockSpec((CHUNK, LANES), lambda b, c: (row(b, c), 0)),
            pl.BlockSpec((CONV_W, M_INNER), const),
            pl.BlockSpec((1, M_INNER), const),
            pl.BlockSpec((CONV_W, M_INNER), const),
            pl.BlockSpec((1, M_INNER), const),
            pl.BlockSpec((1, LANES), const),
            pl.BlockSpec((1, LANES), const),
            pl.BlockSpec((1, M_INNER), const),
            pl.BlockSpec((2 * LANES, M_INNER), const),
            pl.BlockSpec(((CONV_W - 1) * CHUNK, 2 * CHUNK), const),
        ],
        out_specs=[
            pl.BlockSpec((CHUNK, M_INNER), lambda b, c: (row(b, c), 0)),
            pl.BlockSpec((1, M_PAIRS, LANES, M_STATE), lambda b, c: (b, 0, 0, 0)),
        ],
        out_shape=[
            jax.ShapeDtypeStruct((m, M_INNER), BF16),
            jax.ShapeDtypeStruct((batch, M_PAIRS, LANES, M_STATE), F32),
        ],
        scratch_shapes=[
            pltpu.VMEM((2 * CHUNK, M_INNER), BF16),
            pltpu.VMEM((2 * CHUNK, M_INNER), BF16),
            pltpu.VMEM((CHUNK, M_INNER), F32),
            pltpu.VMEM((CHUNK, M_INNER), BF16),
        ],
        compiler_params=_params("parallel", "arbitrary"),
        name="ssd_prompt",
    )(pbig, pbig, psmall, cwx, cbx, cwbc, cbbc, dtb, alog, dvec, e64, shift)


def _mlstm_kernel(u_ref, v_ref, sm_ref, cw_ref, cb_ref, wq_ref, wk_ref,
                  ib_ref, fb_ref, el_ref, shift_ref,
                  hh_ref, c_ref, n_ref, m_ref, win_ref, uc_scr):
    n = u_ref.shape[0]

    @pl.when(pl.program_id(1) == 0)
    def _():
        c_ref[...] = jnp.zeros_like(c_ref)
        n_ref[...] = jnp.zeros_like(n_ref)
        m_ref[...] = jnp.zeros_like(m_ref)
        win_ref[0:n, :] = jnp.zeros((n, L_INNER), BF16)

    _conv_silu(u_ref, win_ref, shift_ref, cw_ref, cb_ref, uc_scr)
    uc = uc_scr[...]

    sm = sm_ref[...]
    lane = _lane_iota((n, LANES))
    gate_lanes = (lane >= F_LANE) & (lane < F_LANE + L_HEADS)
    ig = pltpu.roll(sm + ib_ref[...], F_LANE - I_LANE, axis=1)
    logf = jnp.where(gate_lanes, _log_sigmoid(sm + fb_ref[...]), 0.0)
    causal = _tri(n)
    bcum = _cumsum_rows(causal, logf)
    m_old = m_ref[0]
    bl = bcum[n - 1:n, :]
    src = bl - bcum + ig
    m_new = jnp.maximum(bl + m_old, jnp.max(src, axis=0, keepdims=True))
    keep = jnp.exp(bl + m_old - m_new)
    bcum_t = bcum.T
    ig_t = ig.T
    wr_x = _spread(jnp.exp(src - m_new), el_ref)
    scale = L_HDIM ** -0.5
    c_olds = [c_ref[0, h] for h in range(L_HEADS)]
    n_olds = [n_ref[0, h:h + 1, :] for h in range(L_HEADS)]

    heads = range(L_HEADS)
    lanes_of = [slice(h * L_HDIM, (h + 1) * L_HDIM) for h in heads]
    qs, ks, qbs, kbs = [], [], [], []
    for h in heads:
        ub = uc[:, lanes_of[h]]
        qs.append(jnp.dot(ub, wq_ref[h], preferred_element_type=F32))
        ks.append(jnp.dot(ub, wk_ref[h], preferred_element_type=F32) * scale)
    logds, inters, rowmax = [], [], []
    for h in heads:
        ln = F_LANE + h
        bcol = jnp.broadcast_to(bcum[:, ln:ln + 1], (n, n))
        logd = jnp.where(causal, bcol - bcum_t[ln:ln + 1, :] + ig_t[ln:ln + 1, :], -jnp.inf)
        logds.append(logd)
        inters.append(bcol + m_old[:, ln:ln + 1])
        rowmax.append(jnp.max(logd, axis=1, keepdims=True))
    ss, scs, floors = [], [], []
    for h in heads:
        qbs.append(qs[h].astype(BF16))
        kbs.append(ks[h].astype(BF16))
        m_s = jnp.maximum(inters[h], rowmax[h])
        dm = jnp.exp(logds[h] - m_s)
        scs.append(jnp.exp(inters[h] - m_s))
        floors.append(jnp.exp(-m_s))
        ss.append(lax.dot_general(qbs[h], kbs[h], NT_DIMS, preferred_element_type=F32) * dm)
    nums, dens = [], []
    for h in heads:
        vb = v_ref[:, lanes_of[h]]
        num = jnp.dot(ss[h].astype(BF16), vb, preferred_element_type=F32)
        num = num + scs[h] * jnp.dot(qbs[h], c_olds[h].astype(BF16), preferred_element_type=F32)
        nums.append(num)
        dens.append(jnp.sum(ss[h], axis=1, keepdims=True)
                    + scs[h] * jnp.sum(qs[h] * n_olds[h], axis=1, keepdims=True))
    for h in heads:
        hh = nums[h] / jnp.maximum(jnp.abs(dens[h]), floors[h])
        hh_ref[:, lanes_of[h]] = hh.astype(BF16)
    for h in heads:
        ln = F_LANE + h
        kw = ks[h] * wr_x[:, lanes_of[h]]
        kp = keep[:, ln:ln + 1]
        c_ref[0, h] = kp * c_olds[h] + lax.dot_general(
            kw.astype(BF16), v_ref[:, lanes_of[h]], TN_DIMS, preferred_element_type=F32)
        n_ref[0, h:h + 1, :] = kp * n_olds[h] + jnp.sum(kw, axis=0, keepdims=True)
    m_ref[0] = m_new


def _mlstm_prompt(pbig, psmall, cw, cb, wq, wk, ib, fb, el, shift, batch, n_chunks):
    m = pbig.shape[0]
    row = lambda b, c: b * n_chunks + c
    const2 = lambda b, c: (0, 0)
    return pl.pallas_call(
        _mlstm_kernel,
        grid=(batch, n_chunks),
        in_specs=[
            pl.BlockSpec((CHUNK, L_INNER), lambda b, c: (row(b, c), 6)),
            pl.BlockSpec((CHUNK, L_INNER), lambda b, c: (row(b, c), 7)),
            pl.BlockSpec((CHUNK, LANES), lambda b, c: (row(b, c), 0)),
            pl.BlockSpec((CONV_W, L_INNER), const2),
            pl.BlockSpec((1, L_INNER), const2),
            pl.BlockSpec((L_HEADS, L_HDIM, L_HDIM), lambda b, c: (0, 0, 0)),
            pl.BlockSpec((L_HEADS, L_HDIM, L_HDIM), lambda b, c: (0, 0, 0)),
            pl.BlockSpec((1, LANES), const2),
            pl.BlockSpec((1, LANES), const2),
            pl.BlockSpec((2 * LANES, L_INNER), const2),
            pl.BlockSpec(((CONV_W - 1) * CHUNK, 2 * CHUNK), const2),
        ],
        out_specs=[
            pl.BlockSpec((CHUNK, L_INNER), lambda b, c: (row(b, c), 0)),
            pl.BlockSpec((1, L_HEADS, L_HDIM, L_HDIM), lambda b, c: (b, 0, 0, 0)),
            pl.BlockSpec((1, L_HEADS, L_HDIM), lambda b, c: (b, 0, 0)),
            pl.BlockSpec((1, 1, LANES), lambda b, c: (b, 0, 0)),
        ],
        out_shape=[
            jax.ShapeDtypeStruct((m, L_INNER), BF16),
            jax.ShapeDtypeStruct((batch, L_HEADS, L_HDIM, L_HDIM), F32),
            jax.ShapeDtypeStruct((batch, L_HEADS, L_HDIM), F32),
            jax.ShapeDtypeStruct((batch, 1, LANES), F32),
        ],
        scratch_shapes=[
            pltpu.VMEM((2 * CHUNK, L_INNER), BF16),
            pltpu.VMEM((CHUNK, L_INNER), BF16),
        ],
        compiler_params=_params("parallel", "arbitrary"),
        name="mlstm_prompt",
    )(pbig, pbig, psmall, cw, cb, wq, wk, ib, fb, el, shift)


def _tail_kernel(ys_ref, hh_ref, z_ref, o_ref, ga_ref, gb_ref, x_ref, mnw_ref, lnw_ref,
                 wa_ref, wb_ref, wo_ref, nmw_ref, wup_ref, wdn_ref, fw_ref, y_ref, ya_scr, hb_scr):
    gw = M_INNER // M_GROUPS
    for g in range(M_GROUPS):
        sl = slice(g * gw, (g + 1) * gw)
        yg = ys_ref[:, sl].astype(F32) * _silu(z_ref[:, sl].astype(F32))
        ya_scr[:, sl] = _rms(yg, mnw_ref[:, sl]).astype(BF16)
    for h in range(L_HEADS):
        sl = slice(h * L_HDIM, (h + 1) * L_HDIM)
        gate = _sigmoid(o_ref[:, sl].astype(F32))
        hb_scr[:, sl] = (gate * _rms(hh_ref[:, sl].astype(F32), lnw_ref[:, sl])).astype(BF16)
    a = jnp.dot(ya_scr[...], wa_ref[...], preferred_element_type=F32)
    b = jnp.dot(hb_scr[...], wb_ref[...], preferred_element_type=F32)
    t = _sigmoid(ga_ref[...].astype(F32)) * a + _sigmoid(gb_ref[...].astype(F32)) * b
    x1 = x_ref[...] + jnp.dot(t.astype(BF16), wo_ref[...], preferred_element_type=F32)
    hn = _rms(x1, nmw_ref[...]).astype(BF16)
    acc = x1
    for c in range(D_FF // PROJ_BLOCK):
        sl = slice(c * PROJ_BLOCK, (c + 1) * PROJ_BLOCK)
        up = jnp.dot(hn, wup_ref[:, sl], preferred_element_type=F32)
        act = jnp.square(jnp.maximum(up, 0.0)).astype(BF16)
        acc = acc + jnp.dot(act, wdn_ref[sl, :], preferred_element_type=F32)
    y_ref[...] = _rms(acc, fw_ref[...])


def _tail(ys, hh, pbig, x, mnw, lnw, wa, wb, wo, nmw, wup, wdn, fw, tm):
    m = x.shape[0]
    tm = min(tm, m)
    rows = lambda i: (i, 0)
    const = lambda i: (0, 0)

    def resident(shape):
        return pl.BlockSpec(shape, const, pipeline_mode=pl.Buffered(1))

    return pl.pallas_call(
        _tail_kernel,
        grid=(m // tm,),
        in_specs=[
            pl.BlockSpec((tm, M_INNER), rows),
            pl.BlockSpec((tm, L_INNER), rows),
            pl.BlockSpec((tm, M_INNER), rows),
            pl.BlockSpec((tm, PROJ_BLOCK), lambda i: (i, 8)),
            pl.BlockSpec((tm, PROJ_BLOCK), lambda i: (i, 9)),
            pl.BlockSpec((tm, PROJ_BLOCK), lambda i: (i, 10)),
            pl.BlockSpec((tm, D_MODEL), rows),
            resident((1, M_INNER)),
            resident((1, L_INNER)),
            resident((M_INNER, D_MODEL)),
            resident((L_INNER, D_MODEL)),
            resident((D_MODEL, D_MODEL)),
            resident((1, D_MODEL)),
            resident((D_MODEL, D_FF)),
            resident((D_FF, D_MODEL)),
            resident((1, D_MODEL)),
        ],
        out_specs=pl.BlockSpec((tm, D_MODEL), rows),
        out_shape=jax.ShapeDtypeStruct((m, D_MODEL), F32),
        scratch_shapes=[pltpu.VMEM((tm, M_INNER), BF16), pltpu.VMEM((tm, L_INNER), BF16)],
        compiler_params=_params("parallel"),
        name="tail",
    )(ys, hh, pbig, pbig, pbig, pbig, x, mnw, lnw, wa, wb, wo, nmw, wup, wdn, fw)


def _sample_pre_kernel(xbc_ref, u_ref, sm_ref, cm0_ref, cm1_ref, cm2_ref, cl0_ref, cl1_ref, cl2_ref,
                       m_ref, cwm_ref, cbm_ref, cwl_ref, cbl_ref, wq_ref, wk_ref,
                       dtb_ref, alog_ref, ib_ref, fb_ref,
                       xc_ref, bcc_ref, q_ref, k_ref, dt_ref, da_ref, wr_ref, keep_ref, mnew_ref,
                       cmn_ref, cln_ref):
    xbc = xbc_ref[...].astype(F32)
    conv_m = (cbm_ref[...] + cwm_ref[0:1, :] * cm0_ref[...] + cwm_ref[1:2, :] * cm1_ref[...]
              + cwm_ref[2:3, :] * cm2_ref[...] + cwm_ref[3:4, :] * xbc)
    act = _silu(conv_m)
    xc_ref[...] = act[:, :M_INNER]
    bcc_ref[...] = act[:, M_INNER:]
    cmn_ref[:, 0:M_CONV] = cm1_ref[...]
    cmn_ref[:, M_CONV:2 * M_CONV] = cm2_ref[...]
    cmn_ref[:, 2 * M_CONV:3 * M_CONV] = xbc

    u = u_ref[...].astype(F32)
    conv_l = (cbl_ref[...] + cwl_ref[0:1, :] * cl0_ref[...] + cwl_ref[1:2, :] * cl1_ref[...]
              + cwl_ref[2:3, :] * cl2_ref[...] + cwl_ref[3:4, :] * u)
    uc = _silu(conv_l)
    cln_ref[:, 0:L_INNER] = cl1_ref[...]
    cln_ref[:, L_INNER:2 * L_INNER] = cl2_ref[...]
    cln_ref[:, 2 * L_INNER:3 * L_INNER] = u
    scale = L_HDIM ** -0.5
    for h in range(L_HEADS):
        sl = slice(h * L_HDIM, (h + 1) * L_HDIM)
        ub = uc[:, sl].astype(BF16)
        q_ref[:, sl] = jnp.dot(ub, wq_ref[h], preferred_element_type=F32)
        k_ref[:, sl] = jnp.dot(ub, wk_ref[h], preferred_element_type=F32) * scale

    sm = sm_ref[...]
    dt = _softplus(sm + dtb_ref[...])
    dt_ref[...] = dt
    da_ref[...] = jnp.exp(dt * (-jnp.exp(alog_ref[...])))
    ig = pltpu.roll(sm + ib_ref[...], F_LANE - I_LANE, axis=1)
    logf = _log_sigmoid(sm + fb_ref[...])
    m_old = m_ref[...]
    m_new = jnp.maximum(logf + m_old, ig)
    mnew_ref[...] = m_new
    wr_ref[...] = jnp.exp(ig - m_new)
    keep_ref[...] = jnp.exp(logf + m_old - m_new)


def _sample_pre(pbig, psmall, conv_m, conv_l, m_lanes, cwm, cbm, cwl, cbl, wq, wk, dtb, alog, ib, fb):
    s = pbig.shape[0]
    f = lambda shape: jax.ShapeDtypeStruct(shape, F32)
    full2 = lambda shape: pl.BlockSpec(shape, lambda i: (0, 0))
    full3 = lambda shape: pl.BlockSpec(shape, lambda i: (0, 0, 0))
    state_row = lambda width, j: pl.BlockSpec((s, width), lambda i, j=j: (0, j))
    return pl.pallas_call(
        _sample_pre_kernel,
        grid=(1,),
        in_specs=[
            full2((s, M_CONV)),
            pl.BlockSpec((s, L_INNER), lambda i: (0, 6)),
            full2((s, LANES)),
            state_row(M_CONV, 0), state_row(M_CONV, 1), state_row(M_CONV, 2),
            state_row(L_INNER, 0), state_row(L_INNER, 1), state_row(L_INNER, 2),
            full2((s, LANES)),
            full2((CONV_W, M_CONV)), full2((1, M_CONV)),
            full2((CONV_W, L_INNER)), full2((1, L_INNER)),
            full3((L_HEADS, L_HDIM, L_HDIM)), full3((L_HEADS, L_HDIM, L_HDIM)),
            full2((1, LANES)), full2((1, LANES)), full2((1, LANES)), full2((1, LANES)),
        ],
        out_specs=[
            full2((s, M_INNER)), full2((s, M_INNER)), full2((s, L_INNER)), full2((s, L_INNER)),
            full2((s, LANES)), full2((s, LANES)), full2((s, LANES)), full2((s, LANES)), full2((s, LANES)),
            full2((s, (CONV_W - 1) * M_CONV)), full2((s, (CONV_W - 1) * L_INNER)),
        ],
        out_shape=[
            f((s, M_INNER)), f((s, M_INNER)), f((s, L_INNER)), f((s, L_INNER)),
            f((s, LANES)), f((s, LANES)), f((s, LANES)), f((s, LANES)), f((s, LANES)),
            f((s, (CONV_W - 1) * M_CONV)), f((s, (CONV_W - 1) * L_INNER)),
        ],
        compiler_params=_params("arbitrary"),
        name="sample_pre",
    )(pbig[:, 2 * PROJ_BLOCK:6 * PROJ_BLOCK], pbig, psmall, conv_m, conv_m, conv_m,
      conv_l, conv_l, conv_l, m_lanes, cwm, cbm, cwl, cbl, wq, wk, dtb, alog, ib, fb)


SSD_SB = 8
MLSTM_SB = 16


def _ssd_state_kernel(dt_ref, da_ref, x_ref, b_ref, c_ref, h_ref, hn_ref, y_ref, xs_ref):
    base = pl.program_id(0) * SSD_SB
    rows = lax.broadcasted_iota(jnp.int32, (LANES, LANES), 0)
    lanes = _lane_iota((LANES, LANES))
    diag = rows == lanes
    first_rows = rows < M_HDIM
    ones = jnp.ones((LANES, LANES), BF16)
    pairs_per_group = M_PAIRS // M_GROUPS

    def spread(s):
        for hp in range(M_PAIRS):
            xrow = jnp.broadcast_to(x_ref[s:s + 1, hp * LANES:(hp + 1) * LANES], (LANES, LANES))
            xdiag = jnp.where(diag, xrow, 0.0).astype(BF16)
            xs_ref[s % 2, hp] = jnp.dot(xdiag, ones, preferred_element_type=F32)

    acc = jnp.zeros((LANES, LANES), F32)
    spread(0)
    for s in range(SSD_SB):
        if s + 1 < SSD_SB:
            spread(s + 1)
        for g in range(M_GROUPS):
            brow = b_ref[s, g:g + 1, :]
            crow = c_ref[s, g:g + 1, :]
            for pp in range(pairs_per_group):
                hp = g * pairs_per_group + pp
                h0, h1 = 2 * hp, 2 * hp + 1
                dav = jnp.where(first_rows, da_ref[base + s, h0], da_ref[base + s, h1])
                dtv = jnp.where(first_rows, dt_ref[base + s, h0], dt_ref[base + s, h1])
                hn = dav * h_ref[s, hp] + (dtv * xs_ref[s % 2, hp]) * brow
                hn_ref[s, hp] = hn
                ysum = jnp.dot((hn * crow).astype(BF16), ones, preferred_element_type=F32)
                acc = jnp.where(lanes == s * M_PAIRS + hp, ysum, acc)
    y_ref[...] = acc.T


def _ssd_state(dt, da, xc, b3, c3, h):
    s = h.shape[0]
    smem = pl.BlockSpec(memory_space=pltpu.SMEM)
    return pl.pallas_call(
        _ssd_state_kernel,
        grid=(s // SSD_SB,),
        in_specs=[
            smem, smem,
            pl.BlockSpec((SSD_SB, M_INNER), lambda i: (i, 0)),
            pl.BlockSpec((SSD_SB, M_GROUPS, M_STATE), lambda i: (i, 0, 0)),
            pl.BlockSpec((SSD_SB, M_GROUPS, M_STATE), lambda i: (i, 0, 0)),
            pl.BlockSpec((SSD_SB, M_PAIRS, LANES, M_STATE), lambda i: (i, 0, 0, 0)),
        ],
        out_specs=[
            pl.BlockSpec((SSD_SB, M_PAIRS, LANES, M_STATE), lambda i: (i, 0, 0, 0)),
            pl.BlockSpec((SSD_SB * M_PAIRS, LANES), lambda i: (i, 0)),
        ],
        out_shape=[
            jax.ShapeDtypeStruct(h.shape, F32),
            jax.ShapeDtypeStruct((s * M_PAIRS, LANES), F32),
        ],
        scratch_shapes=[pltpu.VMEM((2, M_PAIRS, LANES, LANES), F32)],
        compiler_params=_params("parallel"),
        name="ssd_state",
    )(dt, da, xc, b3, c3, h)


def _mlstm_state_kernel(wr_ref, keep_ref, q_ref, k_ref, v_ref, c_ref, cn_ref, num_ref):
    base = pl.program_id(0) * MLSTM_SB
    qt = q_ref[...].T
    kt = k_ref[...].T
    for s in range(MLSTM_SB):
        for h in range(L_HEADS):
            r = s * L_HEADS + h
            wr = wr_ref[base + s, h]
            keep = keep_ref[base + s, h]
            c_old = c_ref[s, h]
            num_ref[r:r + 1, :] = jnp.sum(qt[:, r:r + 1] * c_old, axis=0, keepdims=True)
            cn_ref[s, h] = keep * c_old + (kt[:, r:r + 1] * wr) * v_ref[r:r + 1, :]


def _mlstm_state(wr, keep, q_rows, k_rows, v_rows, c):
    s = c.shape[0]
    smem = pl.BlockSpec(memory_space=pltpu.SMEM)
    rows = pl.BlockSpec((LANES, LANES), lambda i: (i, 0))
    return pl.pallas_call(
        _mlstm_state_kernel,
        grid=(s // MLSTM_SB,),
        in_specs=[
            smem, smem, rows, rows, rows,
            pl.BlockSpec((MLSTM_SB, L_HEADS, L_HDIM, L_HDIM), lambda i: (i, 0, 0, 0)),
        ],
        out_specs=[
            pl.BlockSpec((MLSTM_SB, L_HEADS, L_HDIM, L_HDIM), lambda i: (i, 0, 0, 0)),
            rows,
        ],
        out_shape=[
            jax.ShapeDtypeStruct(c.shape, F32),
            jax.ShapeDtypeStruct(q_rows.shape, F32),
        ],
        compiler_params=_params("parallel"),
        name="mlstm_state",
    )(wr, keep, q_rows, k_rows, v_rows, c)


def _sample_post_kernel(y_ref, xc_ref, dvec_ref,
                        q_ref, k_ref, v_ref, n_ref, num_ref, wr_ref, keep_ref, mnew_ref,
                        ys_ref, hh_ref, nn_ref):
    ys_ref[...] = (y_ref[...] + dvec_ref[...] * xc_ref[...]).astype(BF16)

    wr = wr_ref[...]
    keep = keep_ref[...]
    floor = jnp.exp(-mnew_ref[...])
    for h in range(L_HEADS):
        ln = F_LANE + h
        sl = slice(h * L_HDIM, (h + 1) * L_HDIM)
        q = q_ref[:, sl]
        k = k_ref[:, sl]
        n_old = n_ref[:, sl]
        wrc = wr[:, ln:ln + 1]
        kpc = keep[:, ln:ln + 1]
        wgt = jnp.sum(q * k, axis=1, keepdims=True) * wrc
        num = wgt * v_ref[:, sl].astype(F32) + kpc * num_ref[:, sl]
        den = wgt + kpc * jnp.sum(q * n_old, axis=1, keepdims=True)
        hh_ref[:, sl] = (num / jnp.maximum(jnp.abs(den), floor[:, ln:ln + 1])).astype(BF16)
        nn_ref[:, sl] = kpc * n_old + wrc * k


def _sample_post(y, xc, pbig, dvec, q, k, n_rows, num, wr, keep, mnew):
    s = y.shape[0]
    full = lambda shape: pl.BlockSpec(shape, lambda i: (0, 0))
    blk = lambda width, j: pl.BlockSpec((s, width), lambda i, j=j: (0, j))
    return pl.pallas_call(
        _sample_post_kernel,
        grid=(1,),
        in_specs=[
            full((s, M_INNER)), full((s, M_INNER)), full((1, M_INNER)),
            full((s, L_INNER)), full((s, L_INNER)), blk(L_INNER, 7),
            full((s, L_INNER)), full((s, L_INNER)),
            full((s, LANES)), full((s, LANES)), full((s, LANES)),
        ],
        out_specs=[full((s, M_INNER)), full((s, L_INNER)), full((s, L_INNER))],
        out_shape=[
            jax.ShapeDtypeStruct((s, M_INNER), BF16),
            jax.ShapeDtypeStruct((s, L_INNER), BF16),
            jax.ShapeDtypeStruct((s, L_INNER), F32),
        ],
        compiler_params=_params("arbitrary"),
        name="sample_post",
    )(y, xc, dvec, q, k, pbig, n_rows, num, wr, keep, mnew)


def _lanes(vec, first_lane):
    n = vec.shape[0]
    return jnp.pad(vec.astype(F32), (first_lane, LANES - first_lane - n)).reshape(1, LANES)


def _spread_matrix(first_lane, n_heads, width):
    r = lax.broadcasted_iota(jnp.int32, (2 * LANES, n_heads * width), 0) % LANES
    c = lax.broadcasted_iota(jnp.int32, (2 * LANES, n_heads * width), 1)
    return (r - first_lane == c // width).astype(BF16)


def _shift_matrix(n):
    r = lax.broadcasted_iota(jnp.int32, ((CONV_W - 1) * n, 2 * n), 0)
    c = lax.broadcasted_iota(jnp.int32, ((CONV_W - 1) * n, 2 * n), 1)
    return (c == n + r % n - (r // n + 1)).astype(BF16)


def kernel(x_prompt, x_sample, state_mamba_conv, state_mamba_ssm, state_mlstm_conv, state_mlstm_C, state_mlstm_n, state_mlstm_m, w_in, mamba_conv_w, mamba_conv_b, mamba_dt_bias, mamba_A_log, mamba_D, mamba_norm_w, w_branch_a, mlstm_conv_w, mlstm_conv_b, mlstm_wq, mlstm_wk, mlstm_i_bias, mlstm_f_bias, mlstm_norm_w, w_branch_b, w_out, norm_mix_w, norm_mlp_w, w_up, w_down, final_norm_w):
    depth = w_in.shape[0]
    assert depth == 1
    batch, seq, _ = x_prompt.shape
    n_samp, dec_seq, _ = x_sample.shape
    assert dec_seq == 1 and seq % CHUNK == 0 and seq >= SUBLANES
    assert n_samp % MLSTM_SB == 0 and n_samp % SSD_SB == 0 and batch % SCAN_NB == 0
    l = 0

    assert w_in.shape[2] == W_END
    w_big, w_small = _wprep(w_in[l])
    g_mix = norm_mix_w[l].reshape(1, D_MODEL)
    cwm = mamba_conv_w[l]
    cbm = mamba_conv_b[l].reshape(1, M_CONV)
    cwl = mlstm_conv_w[l]
    cbl = mlstm_conv_b[l].reshape(1, L_INNER)
    dtb = _lanes(mamba_dt_bias[l], DT_LANE)
    alog = _lanes(mamba_A_log[l], DT_LANE)
    ib = _lanes(mlstm_i_bias[l], I_LANE)
    fb = _lanes(mlstm_f_bias[l], F_LANE)
    dvec = jnp.repeat(mamba_D[l].astype(F32), M_HDIM).reshape(1, M_INNER)
    mnw = mamba_norm_w[l].reshape(1, M_INNER)
    lnw = mlstm_norm_w[l].reshape(1, L_INNER)
    wq = mlstm_wq[l].astype(BF16)
    wk = mlstm_wk[l].astype(BF16)
    wa = w_branch_a[l].astype(BF16)
    wb = w_branch_b[l].astype(BF16)
    wo = w_out[l].astype(BF16)
    wup = w_up[l].astype(BF16)
    wdn = w_down[l].astype(BF16)
    nmw = norm_mlp_w[l].reshape(1, D_MODEL)
    fw = final_norm_w.reshape(1, D_MODEL)
    e64 = _spread_matrix(DT_LANE, M_HEADS, M_HDIM)
    el = _spread_matrix(F_LANE, L_HEADS, L_HDIM)
    shift = _shift_matrix(CHUNK)

    xp = x_prompt.reshape(batch * seq, D_MODEL)
    pbig, psmall = _inproj(xp, g_mix, w_big, w_small, tm=2048)
    ya, hb, p_ssm, p_c, p_n, p_m = _scan_prompt(
        pbig, psmall, cwm[:, :M_INNER], cbm[:, :M_INNER], cwm[:, M_INNER:], cbm[:, M_INNER:],
        cwl, cbl, dtb, alog, dvec, wq, wk, ib, fb, e64, el, shift, batch, seq)
    y_prompt = _tail(ya.reshape(batch * seq, M_INNER), hb.reshape(batch * seq, L_INNER), pbig, xp,
                     mnw, lnw, wa, wb, wo, nmw, wup, wdn, fw, tm=512)
    p3 = pbig.reshape(batch, seq, BIG_WIDTH)
    p_conv_m = p3[:, seq - (CONV_W - 1):, 2 * PROJ_BLOCK:6 * PROJ_BLOCK].astype(F32)
    p_conv_l = p3[:, seq - (CONV_W - 1):, 6 * PROJ_BLOCK:7 * PROJ_BLOCK].astype(F32)
    p_ssm = p_ssm.reshape(batch, M_HEADS, M_HDIM, M_STATE)
    p_m = p_m[:, 0, F_LANE:F_LANE + L_HEADS]

    xs = x_sample.reshape(n_samp, D_MODEL)
    sbig, ssmall = _inproj(xs, g_mix, w_big, w_small, tm=n_samp)
    m_lanes = jnp.pad(state_mlstm_m[l], ((0, 0), (F_LANE, LANES - F_LANE - L_HEADS)))
    (xc, bcc, q, k, dt, da, wr, keep, mnew, s_conv_m, s_conv_l) = _sample_pre(
        sbig, ssmall, state_mamba_conv[l].reshape(n_samp, (CONV_W - 1) * M_CONV),
        state_mlstm_conv[l].reshape(n_samp, (CONV_W - 1) * L_INNER), m_lanes,
        cwm, cbm, cwl, cbl, wq, wk, dtb, alog, ib, fb)
    s_ssm, y_rows = _ssd_state(
        dt[:, :M_HEADS], da[:, :M_HEADS], xc,
        bcc[:, :M_GROUPS * M_STATE].reshape(n_samp, M_GROUPS, M_STATE),
        bcc[:, M_GROUPS * M_STATE:].reshape(n_samp, M_GROUPS, M_STATE),
        state_mamba_ssm[l].reshape(n_samp, M_PAIRS, LANES, M_STATE))
    s_c, num_rows = _mlstm_state(
        wr[:, F_LANE:F_LANE + L_HEADS], keep[:, F_LANE:F_LANE + L_HEADS],
        q.reshape(n_samp * L_HEADS, L_HDIM), k.reshape(n_samp * L_HEADS, L_HDIM),
        sbig[:, 7 * PROJ_BLOCK:8 * PROJ_BLOCK].astype(F32).reshape(n_samp * L_HEADS, L_HDIM),
        state_mlstm_C[l])
    ya_s, hb_s, s_n = _sample_post(
        y_rows.reshape(n_samp, M_INNER), xc, sbig, dvec, q, k,
        state_mlstm_n[l].reshape(n_samp, L_INNER), num_rows.reshape(n_samp, L_INNER),
        wr, keep, mnew)
    y_sample = _tail(ya_s, hb_s, sbig, xs, mnw, lnw, wa, wb, wo, nmw, wup, wdn, fw, tm=n_samp)

    lead = lambda a: a[None]
    return (
        y_prompt.reshape(batch, seq, D_MODEL),
        y_sample.reshape(n_samp, 1, D_MODEL),
        lead(p_conv_m), lead(p_ssm), lead(p_conv_l), lead(p_c), lead(p_n), lead(p_m),
        lead(s_conv_m.reshape(n_samp, CONV_W - 1, M_CONV)),
        lead(s_ssm.reshape(n_samp, M_HEADS, M_HDIM, M_STATE)),
        lead(s_conv_l.reshape(n_samp, CONV_W - 1, L_INNER)),
        lead(s_c), lead(s_n.reshape(n_samp, L_HEADS, L_HDIM)), lead(mnew[:, F_LANE:F_LANE + L_HEADS]),
    )
```

```python
import jax
import jax.numpy as jnp
from jax import lax
from jax.experimental import pallas as pl
from jax.experimental.pallas import tpu as pltpu

F32 = jnp.float32
BF16 = jnp.bfloat16

D_MODEL = 1024
M_INNER = 2048
M_HEADS = 32
M_HDIM = 64
M_GROUPS = 8
M_PAIRS = M_HEADS // 2
M_STATE = 128
M_CONV = 4096
L_INNER = 1024
L_HEADS = 8
L_HDIM = 128
D_FF = 4096
CONV_W = 4
CHUNK = 128
EPS = 1e-6

LANES = 128
SUBLANES = 8
PROJ_BLOCK = 1024
N_PROJ_BLOCKS = 11
BIG_WIDTH = PROJ_BLOCK * N_PROJ_BLOCKS
DT_LANE = 0
I_LANE = 32
F_LANE = 40
VMEM_LIMIT = 56 * 1024 * 1024

HIGHEST = lax.Precision.HIGHEST
LOG2_E = 1.4426950408889634
NT_DIMS = (((1,), (1,)), ((), ()))
TN_DIMS = (((0,), (0,)), ((), ()))


def _params(*sem):
    return pltpu.CompilerParams(dimension_semantics=sem, vmem_limit_bytes=VMEM_LIMIT)


def _sigmoid(x):
    return 0.5 * jnp.tanh(0.5 * x) + 0.5


def _silu(x):
    h = 0.5 * x
    return h + h * jnp.tanh(h)


def _log1p_exp_neg_abs(x):
    e = jnp.exp(-jnp.abs(x))
    u = 1.0 + e
    return jnp.where(u == 1.0, e, jnp.log(u) * (e / (u - 1.0)))


def _softplus(x):
    return jnp.maximum(x, 0.0) + _log1p_exp_neg_abs(x)


def _log_sigmoid(x):
    return jnp.minimum(x, 0.0) - _log1p_exp_neg_abs(x)


def _rms(x, w):
    return x * lax.rsqrt(jnp.mean(x * x, axis=-1, keepdims=True) + EPS) * w


def _lane_iota(shape):
    return lax.broadcasted_iota(jnp.int32, shape, len(shape) - 1)


def _tri(n):
    r = lax.broadcasted_iota(jnp.int32, (n, n), 0)
    c = lax.broadcasted_iota(jnp.int32, (n, n), 1)
    return r >= c


def _split3(a):
    hi = a.astype(BF16)
    r1 = a - hi.astype(F32)
    mid = r1.astype(BF16)
    lo = (r1 - mid.astype(F32)).astype(BF16)
    return hi, mid, lo


def _cumsum_rows(causal, a):
    tri01 = causal.astype(F32).astype(BF16)
    return jnp.dot(jnp.concatenate([tri01] * 3, axis=1), jnp.concatenate(_split3(a), axis=0),
                   preferred_element_type=F32)


def _spread(a, e2_ref):
    hi, mid, _ = _split3(a)
    return jnp.dot(jnp.concatenate([hi, mid], axis=1), e2_ref[...], preferred_element_type=F32)


CONV_COLS = 512


def _conv_silu(x_ref, win_ref, shift_ref, w_ref, b_ref, out_ref):
    n, width = x_ref.shape
    tile = 16
    win_ref[n:2 * n, :] = x_ref[...]
    for c0 in range(0, width, CONV_COLS):
        cs = slice(c0, c0 + CONV_COLS)
        sh = jnp.dot(shift_ref[...], win_ref[:, cs], preferred_element_type=F32)
        y = b_ref[:, cs] + w_ref[CONV_W - 1:CONV_W, cs] * x_ref[:, cs].astype(F32)
        for back in range(1, CONV_W):
            y = y + w_ref[CONV_W - 1 - back:CONV_W - back, cs] * sh[(back - 1) * n:back * n]
        out_ref[:, cs] = _silu(y).astype(out_ref.dtype)
    win_ref[n - tile:n, :] = x_ref[n - tile:n, :]


W_DT0 = M_INNER + M_CONV
W_U0 = W_DT0 + M_HEADS
W_I0 = W_U0 + 3 * L_INNER
W_G0 = W_I0 + 2 * L_HEADS
W_END = W_G0 + 2 * D_MODEL
assert (W_DT0 - DT_LANE) % LANES == 0 and (W_I0 - I_LANE) % LANES == 0 and F_LANE == I_LANE + L_HEADS
W_PREP_ROWS = 128


W_FIRST_UVO = W_DT0 // PROJ_BLOCK
W_FIRST_GATE = W_FIRST_UVO + 3 * L_INNER // PROJ_BLOCK
W_SHIFT_UVO = W_U0 - W_DT0
W_SHIFT_GATE = W_G0 - (W_DT0 + 3 * L_INNER)
assert W_DT0 % PROJ_BLOCK == 0 and (W_I0 - I_LANE) % PROJ_BLOCK == 0
BF16_ROWS = 16
W_NEXT_ROWS = 64
assert max(W_SHIFT_UVO, W_SHIFT_GATE) <= W_NEXT_ROWS and PROJ_BLOCK % W_NEXT_ROWS == 0


def _wprep_kernel(a_ref, b_ref, big_ref, small_ref):
    j = pl.program_id(0)

    def emit(shift):
        if shift == 0:
            big_ref[...] = a_ref[...].astype(BF16)
        else:
            big_ref[0:PROJ_BLOCK - shift, :] = a_ref[shift:PROJ_BLOCK, :].astype(BF16)
            big_ref[PROJ_BLOCK - shift:PROJ_BLOCK, :] = b_ref[0:shift, :].astype(BF16)

    @pl.when(j == 0)
    def _():
        small_ref[...] = jnp.zeros_like(small_ref)

    @pl.when(j < W_FIRST_UVO)
    def _():
        emit(0)

    @pl.when((j >= W_FIRST_UVO) & (j < W_FIRST_GATE))
    def _():
        emit(W_SHIFT_UVO)

    @pl.when(j >= W_FIRST_GATE)
    def _():
        emit(W_SHIFT_GATE)

    @pl.when(j == W_FIRST_UVO)
    def _():
        small_ref[DT_LANE:DT_LANE + M_HEADS, :] = a_ref[DT_LANE:DT_LANE + M_HEADS, :].astype(BF16)

    @pl.when(j == W_FIRST_GATE)
    def _():
        small_ref[I_LANE:I_LANE + 2 * L_HEADS, :] = a_ref[I_LANE:I_LANE + 2 * L_HEADS, :].astype(BF16)


def _wprep(w_t):
    assert W_SHIFT_UVO % BF16_ROWS == 0 and W_SHIFT_GATE % BF16_ROWS == 0
    assert DT_LANE % BF16_ROWS == 0 and M_HEADS % BF16_ROWS == 0 and I_LANE % BF16_ROWS == 0
    return pl.pallas_call(
        _wprep_kernel,
        grid=(N_PROJ_BLOCKS,),
        in_specs=[
            pl.BlockSpec((PROJ_BLOCK, D_MODEL), lambda j: (j, 0)),
            pl.BlockSpec((W_NEXT_ROWS, D_MODEL), lambda j: ((j + 1) * (PROJ_BLOCK // W_NEXT_ROWS), 0)),
        ],
        out_specs=[
            pl.BlockSpec((PROJ_BLOCK, D_MODEL), lambda j: (j, 0)),
            pl.BlockSpec((LANES, D_MODEL), lambda j: (0, 0)),
        ],
        out_shape=[
            jax.ShapeDtypeStruct((BIG_WIDTH, D_MODEL), BF16),
            jax.ShapeDtypeStruct((LANES, D_MODEL), BF16),
        ],
        compiler_params=_params("arbitrary"),
        name="w_prep",
    )(w_t, w_t)


def _inproj_kernel(x_ref, g_ref, w_ref, ws_ref, o_ref, os_ref, xn_ref):
    @pl.when(pl.program_id(1) == 0)
    def _():
        xn_ref[...] = _rms(x_ref[...], g_ref[...]).astype(BF16)
        os_ref[...] = lax.dot_general(xn_ref[...], ws_ref[...], NT_DIMS, preferred_element_type=F32)

    o_ref[...] = lax.dot_general(
        xn_ref[...], w_ref[...], NT_DIMS, preferred_element_type=F32).astype(BF16)


def _inproj(x, g, w_big, w_small, tm):
    m = x.shape[0]
    tm = min(tm, m)
    return pl.pallas_call(
        _inproj_kernel,
        grid=(m // tm, N_PROJ_BLOCKS),
        in_specs=[
            pl.BlockSpec((tm, D_MODEL), lambda i, j: (i, 0)),
            pl.BlockSpec((1, D_MODEL), lambda i, j: (0, 0)),
            pl.BlockSpec((PROJ_BLOCK, D_MODEL), lambda i, j: (j, 0)),
            pl.BlockSpec((LANES, D_MODEL), lambda i, j: (0, 0)),
        ],
        out_specs=[
            pl.BlockSpec((tm, PROJ_BLOCK), lambda i, j: (i, j)),
            pl.BlockSpec((tm, LANES), lambda i, j: (i, 0)),
        ],
        out_shape=[
            jax.ShapeDtypeStruct((m, BIG_WIDTH), BF16),
            jax.ShapeDtypeStruct((m, LANES), F32),
        ],
        scratch_shapes=[pltpu.VMEM((tm, D_MODEL), BF16)],
        compiler_params=_params("parallel", "arbitrary"),
        name="in_proj",
    )(x, g, w_big, w_small)


SCAN_NB = 2


def _ssd_chunk(x_ref, bc_ref, sm, cwx_ref, cbx_ref, cwbc_ref, cbbc_ref, dtb_ref, alog_ref, dvec_ref,
               e64_ref, shift_ref, y_ref, h_ref, winx_ref, winbc_ref, xc_scr, bcc_scr):
    n = x_ref.shape[0]
    _conv_silu(x_ref, winx_ref, shift_ref, cwx_ref, cbx_ref, xc_scr)
    _conv_silu(bc_ref, winbc_ref, shift_ref, cwbc_ref, cbbc_ref, bcc_scr)
    xc = xc_scr[...]
    bcc = bcc_scr[...]

    lane = _lane_iota((n, LANES))
    dt = _softplus(sm + dtb_ref[...])
    da = jnp.where(lane < M_HEADS, dt * (-LOG2_E * jnp.exp(alog_ref[...])), 0.0)
    causal = _tri(n)
    acum = _cumsum_rows(causal, da)
    acum_t = acum.T
    last = acum[n - 1:n, :]
    exp_last = jnp.exp2(last)
    dt_x = _spread(dt, e64_ref)
    wr_x = _spread(jnp.exp2(last - acum), e64_ref)
    ea_x = _spread(jnp.exp2(acum), e64_ref)

    xdt = xc * dt_x
    xw = (xdt * wr_x).astype(BF16)
    xb = xdt.astype(BF16)
    low_half = jnp.bitwise_and(_lane_iota((n, M_INNER)), LANES - 1) < M_HDIM
    zero = jnp.zeros((), BF16)
    x_lo = jnp.where(low_half, xb, zero)
    x_hi = jnp.where(low_half, zero, xb)
    first_rows = lax.broadcasted_iota(jnp.int32, (LANES, LANES), 0) < M_HDIM

    def weights(cb, h):
        seg = acum[:, h:h + 1] - acum_t[h:h + 1, :]
        return (cb * jnp.exp2(jnp.where(causal, seg, -jnp.inf))).astype(BF16)

    pairs_per_group = M_PAIRS // M_GROUPS
    for g in range(M_GROUPS):
        bg = bcc[:, g * M_STATE:(g + 1) * M_STATE]
        cg = bcc[:, (M_GROUPS + g) * M_STATE:(M_GROUPS + g + 1) * M_STATE]
        cb = lax.dot_general(cg, bg, NT_DIMS, preferred_element_type=F32)
        for pp in range(pairs_per_group):
            hp = g * pairs_per_group + pp
            h0, h1 = 2 * hp, 2 * hp + 1
            sl = slice(hp * LANES, (hp + 1) * LANES)
            hs = h_ref[hp]
            y = jnp.dot(weights(cb, h0), x_lo[:, sl], preferred_element_type=F32)
            y = y + jnp.dot(weights(cb, h1), x_hi[:, sl], preferred_element_type=F32)
            ys = lax.dot_general(cg, hs.astype(BF16), NT_DIMS, preferred_element_type=F32)
            y_ref[:, sl] = (y + ea_x[:, sl] * ys + dvec_ref[:, sl] * xc[:, sl]).astype(BF16)
            el = jnp.where(first_rows, exp_last[:, h0:h0 + 1], exp_last[:, h1:h1 + 1])
            h_ref[hp] = el * hs + lax.dot_general(xw[:, sl], bg, TN_DIMS, preferred_element_type=F32)


def _mlstm_chunks(seqs, wq_ref, wk_ref, ib_ref, fb_ref, el_ref):
    n = seqs[0][0].shape[0]
    causal = _tri(n)
    lane = _lane_iota((n, LANES))
    gate_lanes = (lane >= F_LANE) & (lane < F_LANE + L_HEADS)
    scale = L_HDIM ** -0.5
    lanes_of = [slice(h * L_HDIM, (h + 1) * L_HDIM) for h in range(L_HEADS)]

    gates = []
    for uc, v_ref, sm, hh_ref, c_ref, n_ref, m_ref in seqs:
        ig = pltpu.roll(sm + ib_ref[...], F_LANE - I_LANE, axis=1)
        logf = jnp.where(gate_lanes, _log_sigmoid(sm + fb_ref[...]), 0.0)
        bcum = _cumsum_rows(causal, logf)
        m_old = m_ref[...]
        bl = bcum[n - 1:n, :]
        src = bl - bcum + ig
        m_new = jnp.maximum(bl + m_old, jnp.max(src, axis=0, keepdims=True))
        gates.append(dict(
            bcum=bcum, bcum_t=bcum.T, ig_t=ig.T, m_old=m_old, m_new=m_new,
            keep=jnp.exp(bl + m_old - m_new),
            wr_x=_spread(jnp.exp(src - m_new), el_ref)))

    work = [(i, h) for i in range(len(seqs)) for h in range(L_HEADS)]
    c_olds = {(i, h): seqs[i][4][h] for i, h in work}
    n_olds = {(i, h): seqs[i][5][h:h + 1, :] for i, h in work}
    qs, ks, qbs, kbs = {}, {}, {}, {}
    for i, h in work:
        ub = seqs[i][0][:, lanes_of[h]]
        qs[i, h] = jnp.dot(ub, wq_ref[h], preferred_element_type=F32)
        ks[i, h] = jnp.dot(ub, wk_ref[h], preferred_element_type=F32) * scale
    logds, inters, rowmax = {}, {}, {}
    for i, h in work:
        g = gates[i]
        ln = F_LANE + h
        bcol = jnp.broadcast_to(g["bcum"][:, ln:ln + 1], (n, n))
        logd = jnp.where(causal, bcol - g["bcum_t"][ln:ln + 1, :] + g["ig_t"][ln:ln + 1, :], -jnp.inf)
        logds[i, h] = logd
        inters[i, h] = bcol + g["m_old"][:, ln:ln + 1]
        rowmax[i, h] = jnp.max(logd, axis=1, keepdims=True)
    ss, scs, floors = {}, {}, {}
    for i, h in work:
        qbs[i, h] = qs[i, h].astype(BF16)
        kbs[i, h] = ks[i, h].astype(BF16)
        m_s = jnp.maximum(inters[i, h], rowmax[i, h])
        dm = jnp.exp(logds[i, h] - m_s)
        scs[i, h] = jnp.exp(inters[i, h] - m_s)
        floors[i, h] = jnp.exp(-m_s)
        ss[i, h] = lax.dot_general(qbs[i, h], kbs[i, h], NT_DIMS, preferred_element_type=F32) * dm
    nums, dens = {}, {}
    for i, h in work:
        vb = seqs[i][1][:, lanes_of[h]]
        num = jnp.dot(ss[i, h].astype(BF16), vb, preferred_element_type=F32)
        nums[i, h] = num + scs[i, h] * jnp.dot(
            qbs[i, h], c_olds[i, h].astype(BF16), preferred_element_type=F32)
        dens[i, h] = (jnp.sum(ss[i, h], axis=1, keepdims=True)
                      + scs[i, h] * jnp.sum(qs[i, h] * n_olds[i, h], axis=1, keepdims=True))
    for i, h in work:
        hh = nums[i, h] / jnp.maximum(jnp.abs(dens[i, h]), floors[i, h])
        seqs[i][3][:, lanes_of[h]] = hh.astype(BF16)
    for i, h in work:
        g = gates[i]
        ln = F_LANE + h
        kw = ks[i, h] * g["wr_x"][:, lanes_of[h]]
        kp = g["keep"][:, ln:ln + 1]
        seqs[i][4][h] = kp * c_olds[i, h] + lax.dot_general(
            kw.astype(BF16), seqs[i][1][:, lanes_of[h]], TN_DIMS, preferred_element_type=F32)
        seqs[i][5][h:h + 1, :] = kp * n_olds[i, h] + jnp.sum(kw, axis=0, keepdims=True)
    for i in range(len(seqs)):
        seqs[i][6][...] = gates[i]["m_new"]


def _diag_rows(row, diag):
    return jnp.where(diag, jnp.broadcast_to(row, diag.shape), 0.0).astype(BF16)


def _state_spreads(x_ref, q_ref, k_ref, xs_scr, qs_scr, ks_scr):
    ns = x_ref.shape[0]
    rows = lax.broadcasted_iota(jnp.int32, (LANES, LANES), 0)
    diag = rows == _lane_iota((LANES, LANES))
    ones = jnp.ones((LANES, LANES), BF16)
    ones2 = jnp.ones((2 * LANES, LANES), BF16)
    for s in range(ns):
        for hp in range(M_PAIRS):
            xrow = x_ref[s:s + 1, hp * LANES:(hp + 1) * LANES]
            xs_scr[s * M_PAIRS + hp] = jnp.dot(_diag_rows(xrow, diag), ones, preferred_element_type=F32)
    for r in range(ns * L_HEADS):
        for src_ref, dst_scr in ((q_ref, qs_scr), (k_ref, ks_scr)):
            row = src_ref[r:r + 1, :]
            hi = row.astype(BF16).astype(F32)
            lhs = jnp.concatenate([_diag_rows(hi, diag), _diag_rows(row - hi, diag)], axis=1)
            dst_scr[r] = jnp.dot(lhs, ones2, preferred_element_type=F32)


def _state_updates(base, dt_ref, da_ref, wr_ref, keep_ref, b_ref, c_ref, v_ref, h_ref, cst_ref,
                   hn_ref, cn_ref, y_ref, num_ref, xs_scr, qs_scr, ks_scr):
    ns = b_ref.shape[0]
    rows = lax.broadcasted_iota(jnp.int32, (LANES, LANES), 0)
    lanes = _lane_iota((LANES, LANES))
    first_rows = rows < M_HDIM
    ones = jnp.ones((LANES, LANES), BF16)
    pairs_per_group = M_PAIRS // M_GROUPS
    acc = jnp.zeros((LANES, LANES), F32)
    for s in range(ns):
        for hp in range(M_PAIRS):
            g = hp // pairs_per_group
            h0, h1 = 2 * hp, 2 * hp + 1
            brow = b_ref[s, g:g + 1, :]
            crow = c_ref[s, g:g + 1, :]
            dav = jnp.where(first_rows, da_ref[base + s, h0], da_ref[base + s, h1])
            dtv = jnp.where(first_rows, dt_ref[base + s, h0], dt_ref[base + s, h1])
            hn = dav * h_ref[s, hp] + (dtv * xs_scr[s * M_PAIRS + hp]) * brow
            hn_ref[s, hp] = hn
            ysum = jnp.dot((hn * crow).astype(BF16), ones, preferred_element_type=F32)
            acc = jnp.where(lanes == s * M_PAIRS + hp, ysum, acc)
    y_ref[...] = acc.T[0:ns * M_PAIRS, :]
    for s in range(ns):
        for h in range(L_HEADS):
            r = s * L_HEADS + h
            c_old = cst_ref[s, h]
            num_ref[r:r + 1, :] = jnp.sum(qs_scr[r] * c_old, axis=0, keepdims=True)
            cn_ref[s, h] = (keep_ref[base + s, h] * c_old
                            + (ks_scr[r] * wr_ref[base + s, h]) * v_ref[r:r + 1, :])


def _scan_kernel(dt_ref, da_ref, wr_ref, keep_ref, sx_ref, sb_ref, sc_ref, sq_ref, sk_ref, sv_ref,
                 sh_ref, scst_ref, x_ref, bc_ref, u_ref, v_ref, sm_ref,
                 cwx_ref, cbx_ref, cwbc_ref, cbbc_ref, cwl_ref, cbl_ref, dtb_ref, alog_ref, dvec_ref,
                 wq_ref, wk_ref, ib_ref, fb_ref, e64_ref, el_ref, shift_ref,
                 y_ref, hh_ref, h_ref, c_ref, n_ref, m_ref, shn_ref, scn_ref, sy_ref, snum_ref,
                 winx_ref, winbc_ref, winu_ref, xc_scr, bcc_scr, uc_scr, xs_scr, qs_scr, ks_scr):
    n = x_ref.shape[1]
    step = pl.program_id(0) * pl.num_programs(1) + pl.program_id(1)
    _state_spreads(sx_ref, sq_ref, sk_ref, xs_scr, qs_scr, ks_scr)

    @pl.when(pl.program_id(1) == 0)
    def _():
        h_ref[...] = jnp.zeros_like(h_ref)
        c_ref[...] = jnp.zeros_like(c_ref)
        n_ref[...] = jnp.zeros_like(n_ref)
        m_ref[...] = jnp.zeros_like(m_ref)
        winx_ref[:, 0:n, :] = jnp.zeros((SCAN_NB, n, M_INNER), BF16)
        winbc_ref[:, 0:n, :] = jnp.zeros((SCAN_NB, n, M_INNER), BF16)
        winu_ref[:, 0:n, :] = jnp.zeros((SCAN_NB, n, L_INNER), BF16)

    seqs = []
    for i in range(SCAN_NB):
        _ssd_chunk(x_ref.at[i], bc_ref.at[i], sm_ref[i], cwx_ref, cbx_ref, cwbc_ref, cbbc_ref,
                   dtb_ref, alog_ref, dvec_ref, e64_ref, shift_ref, y_ref.at[i], h_ref.at[i],
                   winx_ref.at[i], winbc_ref.at[i], xc_scr, bcc_scr)
        _conv_silu(u_ref.at[i], winu_ref.at[i], shift_ref, cwl_ref, cbl_ref, uc_scr.at[i])
        seqs.append((uc_scr.at[i], v_ref.at[i], sm_ref[i], hh_ref.at[i], c_ref.at[i], n_ref.at[i],
                     m_ref.at[i]))
    _mlstm_chunks(seqs, wq_ref, wk_ref, ib_ref, fb_ref, el_ref)
    _state_updates(step * sb_ref.shape[0], dt_ref, da_ref, wr_ref, keep_ref, sb_ref, sc_ref, sv_ref,
                   sh_ref, scst_ref, shn_ref, scn_ref, sy_ref, snum_ref, xs_scr, qs_scr, ks_scr)


def _scan_prompt(pbig, psmall, cwx, cbx, cwbc, cbbc, cwl, cbl, dtb, alog, dvec, wq, wk, ib, fb,
                 e64, el, shift, batch, seq, samples):
    p3 = pbig.reshape(batch, seq, BIG_WIDTH)
    s3 = psmall.reshape(batch, seq, LANES)
    nb = SCAN_NB
    n_chunks = seq // CHUNK
    n_steps = (batch // nb) * n_chunks
    s_dt, s_da, s_wr, s_keep, s_x, s_b, s_c, s_q, s_k, s_v, s_h, s_cst = samples
    n_samp = s_h.shape[0]
    ns = n_samp // n_steps
    assert ns * n_steps == n_samp and ns * M_PAIRS <= LANES and (ns * L_HEADS) % SUBLANES == 0
    smem = pl.BlockSpec(memory_space=pltpu.SMEM)
    step = lambda b, c: b * n_chunks + c
    srows = lambda n_rows: pl.BlockSpec((n_rows, LANES), lambda b, c: (step(b, c), 0))
    sgrp = pl.BlockSpec((ns, M_GROUPS, M_STATE), lambda b, c: (step(b, c), 0, 0))
    sstate = lambda d1: pl.BlockSpec((ns, d1, LANES, LANES), lambda b, c: (step(b, c), 0, 0, 0))
    const2 = lambda b, c: (0, 0)
    const3 = lambda b, c: (0, 0, 0)
    rows = lambda width, col: pl.BlockSpec((nb, CHUNK, width), lambda b, c, col=col: (b, c, col))
    state4 = lambda d1, d2, d3: pl.BlockSpec((nb, d1, d2, d3), lambda b, c: (b, 0, 0, 0))
    return pl.pallas_call(
        _scan_kernel,
        grid=(batch // nb, seq // CHUNK),
        in_specs=[
            smem, smem, smem, smem,
            pl.BlockSpec((None, ns, M_INNER), lambda b, c: (step(b, c), 0, 0)),
            sgrp, sgrp, srows(ns * L_HEADS), srows(ns * L_HEADS), srows(ns * L_HEADS),
            sstate(M_PAIRS), sstate(L_HEADS),
            rows(M_INNER, 1),
            rows(M_INNER, 2),
            rows(L_INNER, 6),
            rows(L_INNER, 7),
            rows(LANES, 0),
            pl.BlockSpec((CONV_W, M_INNER), const2), pl.BlockSpec((1, M_INNER), const2),
            pl.BlockSpec((CONV_W, M_INNER), const2), pl.BlockSpec((1, M_INNER), const2),
            pl.BlockSpec((CONV_W, L_INNER), const2), pl.BlockSpec((1, L_INNER), const2),
            pl.BlockSpec((1, LANES), const2), pl.BlockSpec((1, LANES), const2),
            pl.BlockSpec((1, M_INNER), const2),
            pl.BlockSpec((L_HEADS, L_HDIM, L_HDIM), const3),
            pl.BlockSpec((L_HEADS, L_HDIM, L_HDIM), const3),
            pl.BlockSpec((1, LANES), const2), pl.BlockSpec((1, LANES), const2),
            pl.BlockSpec((2 * LANES, M_INNER), const2),
            pl.BlockSpec((2 * LANES, L_INNER), const2),
            pl.BlockSpec(((CONV_W - 1) * CHUNK, 2 * CHUNK), const2),
        ],
        out_specs=[
            rows(M_INNER, 0),
            rows(L_INNER, 0),
            state4(M_PAIRS, LANES, M_STATE),
            state4(L_HEADS, L_HDIM, L_HDIM),
            pl.BlockSpec((nb, L_HEADS, L_HDIM), lambda b, c: (b, 0, 0)),
            pl.BlockSpec((nb, 1, LANES), lambda b, c: (b, 0, 0)),
            sstate(M_PAIRS), sstate(L_HEADS), srows(ns * M_PAIRS), srows(ns * L_HEADS),
        ],
        out_shape=[
            jax.ShapeDtypeStruct((batch, seq, M_INNER), BF16),
            jax.ShapeDtypeStruct((batch, seq, L_INNER), BF16),
            jax.ShapeDtypeStruct((batch, M_PAIRS, LANES, M_STATE), F32),
            jax.ShapeDtypeStruct((batch, L_HEADS, L_HDIM, L_HDIM), F32),
            jax.ShapeDtypeStruct((batch, L_HEADS, L_HDIM), F32),
            jax.ShapeDtypeStruct((batch, 1, LANES), F32),
            jax.ShapeDtypeStruct(s_h.shape, F32),
            jax.ShapeDtypeStruct(s_cst.shape, F32),
            jax.ShapeDtypeStruct((n_samp * M_PAIRS, LANES), F32),
            jax.ShapeDtypeStruct((n_samp * L_HEADS, LANES), F32),
        ],
        scratch_shapes=[
            pltpu.VMEM((nb, 2 * CHUNK, M_INNER), BF16),
            pltpu.VMEM((nb, 2 * CHUNK, M_INNER), BF16),
            pltpu.VMEM((nb, 2 * CHUNK, L_INNER), BF16),
            pltpu.VMEM((CHUNK, M_INNER), F32),
            pltpu.VMEM((CHUNK, M_INNER), BF16),
            pltpu.VMEM((nb, CHUNK, L_INNER), BF16),
            pltpu.VMEM((ns * M_PAIRS, LANES, LANES), F32),
            pltpu.VMEM((ns * L_HEADS, LANES, LANES), F32),
            pltpu.VMEM((ns * L_HEADS, LANES, LANES), F32),
        ],
        compiler_params=_params("parallel", "arbitrary"),
        name="scan_prompt",
    )(s_dt, s_da, s_wr, s_keep, s_x.reshape(n_steps, ns, M_INNER), s_b, s_c, s_q, s_k, s_v, s_h, s_cst,
      p3, p3, p3, p3, s3, cwx, cbx, cwbc, cbbc, cwl, cbl, dtb, alog, dvec, wq, wk, ib, fb,
      e64, el, shift)


def _ssd_kernel(x_ref, bc_ref, sm_ref, cwx_ref, cbx_ref, cwbc_ref, cbbc_ref,
                dtb_ref, alog_ref, dvec_ref, e64_ref, shift_ref,
                y_ref, h_ref, winx_ref, winbc_ref, xc_scr, bcc_scr):
    n = x_ref.shape[0]

    @pl.when(pl.program_id(1) == 0)
    def _():
        h_ref[...] = jnp.zeros_like(h_ref)
        winx_ref[0:n, :] = jnp.zeros((n, M_INNER), BF16)
        winbc_ref[0:n, :] = jnp.zeros((n, M_INNER), BF16)

    _conv_silu(x_ref, winx_ref, shift_ref, cwx_ref, cbx_ref, xc_scr)
    _conv_silu(bc_ref, winbc_ref, shift_ref, cwbc_ref, cbbc_ref, bcc_scr)
    xc = xc_scr[...]
    bcc = bcc_scr[...]

    lane = _lane_iota((n, LANES))
    dt = _softplus(sm_ref[...] + dtb_ref[...])
    da = jnp.where(lane < M_HEADS, dt * (-LOG2_E * jnp.exp(alog_ref[...])), 0.0)
    causal = _tri(n)
    acum = _cumsum_rows(causal, da)
    acum_t = acum.T
    last = acum[n - 1:n, :]
    exp_last = jnp.exp2(last)
    dt_x = _spread(dt, e64_ref)
    wr_x = _spread(jnp.exp2(last - acum), e64_ref)
    ea_x = _spread(jnp.exp2(acum), e64_ref)

    xdt = xc * dt_x
    xw = (xdt * wr_x).astype(BF16)
    xb = xdt.astype(BF16)
    low_half = jnp.bitwise_and(_lane_iota((n, M_INNER)), LANES - 1) < M_HDIM
    zero = jnp.zeros((), BF16)
    x_lo = jnp.where(low_half, xb, zero)
    x_hi = jnp.where(low_half, zero, xb)
    first_rows = lax.broadcasted_iota(jnp.int32, (LANES, LANES), 0) < M_HDIM

    def weights(cb, h):
        seg = acum[:, h:h + 1] - acum_t[h:h + 1, :]
        return (cb * jnp.exp2(jnp.where(causal, seg, -jnp.inf))).astype(BF16)

    for g in range(M_GROUPS):
        bg = bcc[:, g * M_STATE:(g + 1) * M_STATE]
        cg = bcc[:, (M_GROUPS + g) * M_STATE:(M_GROUPS + g + 1) * M_STATE]
        cb = lax.dot_general(cg, bg, NT_DIMS, preferred_element_type=F32)
        for pp in range(M_PAIRS // M_GROUPS):
            hp = g * (M_PAIRS // M_GROUPS) + pp
            h0, h1 = 2 * hp, 2 * hp + 1
            sl = slice(hp * LANES, (hp + 1) * LANES)
            hs = h_ref[0, hp]
            y = jnp.dot(weights(cb, h0), x_lo[:, sl], preferred_element_type=F32)
            y = y + jnp.dot(weights(cb, h1), x_hi[:, sl], preferred_element_type=F32)
            ys = lax.dot_general(cg, hs.astype(BF16), NT_DIMS, preferred_element_type=F32)
            y_ref[:, sl] = (y + ea_x[:, sl] * ys + dvec_ref[:, sl] * xc[:, sl]).astype(BF16)
            el = jnp.where(first_rows, exp_last[:, h0:h0 + 1], exp_last[:, h1:h1 + 1])
            h_ref[0, hp] = el * hs + lax.dot_general(xw[:, sl], bg, TN_DIMS, preferred_element_type=F32)


def _ssd_prompt(pbig, psmall, cwx, cbx, cwbc, cbbc, dtb, alog, dvec, e64, shift, batch, n_chunks):
    m = pbig.shape[0]
    row = lambda b, c: b * n_chunks + c
    const = lambda b, c: (0, 0)
    return pl.pallas_call(
        _ssd_kernel,
        grid=(batch, n_chunks),
        in_specs=[
            pl.BlockSpec((CHUNK, M_INNER), lambda b, c: (row(b, c), 1)),
            pl.BlockSpec((CHUNK, M_INNER), lambda b, c: (row(b, c), 2)),
            pl.BlockSpec((CHUNK, LANES), lambda b, c: (row(b, c), 0)),
            pl.BlockSpec((CONV_W, M_INNER), const),
            pl.BlockSpec((1, M_INNER), const),
            pl.BlockSpec((CONV_W, M_INNER), const),
            pl.BlockSpec((1, M_INNER), const),
            pl.BlockSpec((1, LANES), const),
            pl.BlockSpec((1, LANES), const),
            pl.BlockSpec((1, M_INNER), const),
            pl.BlockSpec((2 * LANES, M_INNER), const),
            pl.BlockSpec(((CONV_W - 1) * CHUNK, 2 * CHUNK), const),
        ],
        out_specs=[
            pl.BlockSpec((CHUNK, M_INNER), lambda b, c: (row(b, c), 0)),
            pl.BlockSpec((1, M_PAIRS, LANES, M_STATE), lambda b, c: (b, 0, 0, 0)),
        ],
        out_shape=[
            jax.ShapeDtypeStruct((m, M_INNER), BF16),
            jax.ShapeDtypeStruct((batch, M_PAIRS, LANES, M_STATE), F32),
        ],
        scratch_shapes=[
            pltpu.VMEM((2 * CHUNK, M_INNER), BF16),
            pltpu.VMEM((2 * CHUNK, M_INNER), BF16),
            pltpu.VMEM((CHUNK, M_INNER), F32),
            pltpu.VMEM((CHUNK, M_INNER), BF16),
        ],
        compiler_params=_params("parallel", "arbitrary"),
        name="ssd_prompt",
    )(pbig, pbig, psmall, cwx, cbx, cwbc, cbbc, dtb, alog, dvec, e64, shift)


def _mlstm_kernel(u_ref, v_ref, sm_ref, cw_ref, cb_ref, wq_ref, wk_ref,
                  ib_ref, fb_ref, el_ref, shift_ref,
                  hh_ref, c_ref, n_ref, m_ref, win_ref, uc_scr):
    n = u_ref.shape[0]

    @pl.when(pl.program_id(1) == 0)
    def _():
        c_ref[...] = jnp.zeros_like(c_ref)
        n_ref[...] = jnp.zeros_like(n_ref)
        m_ref[...] = jnp.zeros_like(m_ref)
        win_ref[0:n, :] = jnp.zeros((n, L_INNER), BF16)

    _conv_silu(u_ref, win_ref, shift_ref, cw_ref, cb_ref, uc_scr)
    uc = uc_scr[...]

    sm = sm_ref[...]
    lane = _lane_iota((n, LANES))
    gate_lanes = (lane >= F_LANE) & (lane < F_LANE + L_HEADS)
    ig = pltpu.roll(sm + ib_ref[...], F_LANE - I_LANE, axis=1)
    logf = jnp.where(gate_lanes, _log_sigmoid(sm + fb_ref[...]), 0.0)
    causal = _tri(n)
    bcum = _cumsum_rows(causal, logf)
    m_old = m_ref[0]
    bl = bcum[n - 1:n, :]
    src = bl - bcum + ig
    m_new = jnp.maximum(bl + m_old, jnp.max(src, axis=0, keepdims=True))
    keep = jnp.exp(bl + m_old - m_new)
    bcum_t = bcum.T
    ig_t = ig.T
    wr_x = _spread(jnp.exp(src - m_new), el_ref)
    scale = L_HDIM ** -0.5
    c_olds = [c_ref[0, h] for h in range(L_HEADS)]
    n_olds = [n_ref[0, h:h + 1, :] for h in range(L_HEADS)]

    heads = range(L_HEADS)
    lanes_of = [slice(h * L_HDIM, (h + 1) * L_HDIM) for h in heads]
    qs, ks, qbs, kbs = [], [], [], []
    for h in heads:
        ub = uc[:, lanes_of[h]]
        qs.append(jnp.dot(ub, wq_ref[h], preferred_element_type=F32))
        ks.append(jnp.dot(ub, wk_ref[h], preferred_element_type=F32) * scale)
    logds, inters, rowmax = [], [], []
    for h in heads:
        ln = F_LANE + h
        bcol = jnp.broadcast_to(bcum[:, ln:ln + 1], (n, n))
        logd = jnp.where(causal, bcol - bcum_t[ln:ln + 1, :] + ig_t[ln:ln + 1, :], -jnp.inf)
        logds.append(logd)
        inters.append(bcol + m_old[:, ln:ln + 1])
        rowmax.append(jnp.max(logd, axis=1, keepdims=True))
    ss, scs, floors = [], [], []
    for h in heads:
        qbs.append(qs[h].astype(BF16))
        kbs.append(ks[h].astype(BF16))
        m_s = jnp.maximum(inters[h], rowmax[h])
        dm = jnp.exp(logds[h] - m_s)
        scs.append(jnp.exp(inters[h] - m_s))
        floors.append(jnp.exp(-m_s))
        ss.append(lax.dot_general(qbs[h], kbs[h], NT_DIMS, preferred_element_type=F32) * dm)
    nums, dens = [], []
    for h in heads:
        vb = v_ref[:, lanes_of[h]]
        num = jnp.dot(ss[h].astype(BF16), vb, preferred_element_type=F32)
        num = num + scs[h] * jnp.dot(qbs[h], c_olds[h].astype(BF16), preferred_element_type=F32)
        nums.append(num)
        dens.append(jnp.sum(ss[h], axis=1, keepdims=True)
                    + scs[h] * jnp.sum(qs[h] * n_olds[h], axis=1, keepdims=True))
    for h in heads:
        hh = nums[h] / jnp.maximum(jnp.abs(dens[h]), floors[h])
        hh_ref[:, lanes_of[h]] = hh.astype(BF16)
    for h in heads:
        ln = F_LANE + h
        kw = ks[h] * wr_x[:, lanes_of[h]]
        kp = keep[:, ln:ln + 1]
        c_ref[0, h] = kp * c_olds[h] + lax.dot_general(
            kw.astype(BF16), v_ref[:, lanes_of[h]], TN_DIMS, preferred_element_type=F32)
        n_ref[0, h:h + 1, :] = kp * n_olds[h] + jnp.sum(kw, axis=0, keepdims=True)
    m_ref[0] = m_new


def _mlstm_prompt(pbig, psmall, cw, cb, wq, wk, ib, fb, el, shift, batch, n_chunks):
    m = pbig.shape[0]
    row = lambda b, c: b * n_chunks + c
    const2 = lambda b, c: (0, 0)
    return pl.pallas_call(
        _mlstm_kernel,
        grid=(batch, n_chunks),
        in_specs=[
            pl.BlockSpec((CHUNK, L_INNER), lambda b, c: (row(b, c), 6)),
            pl.BlockSpec((CHUNK, L_INNER), lambda b, c: (row(b, c), 7)),
            pl.BlockSpec((CHUNK, LANES), lambda b, c: (row(b, c), 0)),
            pl.BlockSpec((CONV_W, L_INNER), const2),
            pl.BlockSpec((1, L_INNER), const2),
            pl.BlockSpec((L_HEADS, L_HDIM, L_HDIM), lambda b, c: (0, 0, 0)),
            pl.BlockSpec((L_HEADS, L_HDIM, L_HDIM), lambda b, c: (0, 0, 0)),
            pl.BlockSpec((1, LANES), const2),
            pl.BlockSpec((1, LANES), const2),
            pl.BlockSpec((2 * LANES, L_INNER), const2),
            pl.BlockSpec(((CONV_W - 1) * CHUNK, 2 * CHUNK), const2),
        ],
        out_specs=[
            pl.BlockSpec((CHUNK, L_INNER), lambda b, c: (row(b, c), 0)),
            pl.BlockSpec((1, L_HEADS, L_HDIM, L_HDIM), lambda b, c: (b, 0, 0, 0)),
            pl.BlockSpec((1, L_HEADS, L_HDIM), lambda b, c: (b, 0, 0)),
            pl.BlockSpec((1, 1, LANES), lambda b, c: (b, 0, 0)),
        ],
        out_shape=[
            jax.ShapeDtypeStruct((m, L_INNER), BF16),
            jax.ShapeDtypeStruct((batch, L_HEADS, L_HDIM, L_HDIM), F32),
            jax.ShapeDtypeStruct((batch, L_HEADS, L_HDIM), F32),
            jax.ShapeDtypeStruct((batch, 1, LANES), F32),
        ],
        scratch_shapes=[
            pltpu.VMEM((2 * CHUNK, L_INNER), BF16),
            pltpu.VMEM((CHUNK, L_INNER), BF16),
        ],
        compiler_params=_params("parallel", "arbitrary"),
        name="mlstm_prompt",
    )(pbig, pbig, psmall, cw, cb, wq, wk, ib, fb, el, shift)


def _tail_kernel(ys_ref, hh_ref, z_ref, o_ref, ga_ref, gb_ref, x_ref, mnw_ref, lnw_ref,
                 wa_ref, wb_ref, wo_ref, nmw_ref, wup_ref, wdn_ref, fw_ref, y_ref, ya_scr, hb_scr):
    gw = M_INNER // M_GROUPS
    for g in range(M_GROUPS):
        sl = slice(g * gw, (g + 1) * gw)
        yg = ys_ref[:, sl].astype(F32) * _silu(z_ref[:, sl].astype(F32))
        ya_scr[:, sl] = _rms(yg, mnw_ref[:, sl]).astype(BF16)
    for h in range(L_HEADS):
        sl = slice(h * L_HDIM, (h + 1) * L_HDIM)
        gate = _sigmoid(o_ref[:, sl].astype(F32))
        hb_scr[:, sl] = (gate * _rms(hh_ref[:, sl].astype(F32), lnw_ref[:, sl])).astype(BF16)
    a = jnp.dot(ya_scr[...], wa_ref[...], preferred_element_type=F32)
    b = jnp.dot(hb_scr[...], wb_ref[...], preferred_element_type=F32)
    t = _sigmoid(ga_ref[...].astype(F32)) * a + _sigmoid(gb_ref[...].astype(F32)) * b
    x1 = x_ref[...] + jnp.dot(t.astype(BF16), wo_ref[...], preferred_element_type=F32)
    hn = _rms(x1, nmw_ref[...]).astype(BF16)
    acc = x1
    for c in range(D_FF // PROJ_BLOCK):
        sl = slice(c * PROJ_BLOCK, (c + 1) * PROJ_BLOCK)
        up = jnp.dot(hn, wup_ref[:, sl], preferred_element_type=F32)
        act = jnp.square(jnp.maximum(up, 0.0)).astype(BF16)
        acc = acc + jnp.dot(act, wdn_ref[sl, :], preferred_element_type=F32)
    y_ref[...] = _rms(acc, fw_ref[...])


def _tail(ys, hh, pbig, x, mnw, lnw, wa, wb, wo, nmw, wup, wdn, fw, tm):
    m = x.shape[0]
    tm = min(tm, m)
    rows = lambda i: (i, 0)
    const = lambda i: (0, 0)

    def resident(shape):
        return pl.BlockSpec(shape, const, pipeline_mode=pl.Buffered(1))

    return pl.pallas_call(
        _tail_kernel,
        grid=(m // tm,),
        in_specs=[
            pl.BlockSpec((tm, M_INNER), rows),
            pl.BlockSpec((tm, L_INNER), rows),
            pl.BlockSpec((tm, M_INNER), rows),
            pl.BlockSpec((tm, PROJ_BLOCK), lambda i: (i, 8)),
            pl.BlockSpec((tm, PROJ_BLOCK), lambda i: (i, 9)),
            pl.BlockSpec((tm, PROJ_BLOCK), lambda i: (i, 10)),
            pl.BlockSpec((tm, D_MODEL), rows),
            resident((1, M_INNER)),
            resident((1, L_INNER)),
            resident((M_INNER, D_MODEL)),
            resident((L_INNER, D_MODEL)),
            resident((D_MODEL, D_MODEL)),
            resident((1, D_MODEL)),
            resident((D_MODEL, D_FF)),
            resident((D_FF, D_MODEL)),
            resident((1, D_MODEL)),
        ],
        out_specs=pl.BlockSpec((tm, D_MODEL), rows),
        out_shape=jax.ShapeDtypeStruct((m, D_MODEL), F32),
        scratch_shapes=[pltpu.VMEM((tm, M_INNER), BF16), pltpu.VMEM((tm, L_INNER), BF16)],
        compiler_params=_params("parallel"),
        name="tail",
    )(ys, hh, pbig, pbig, pbig, pbig, x, mnw, lnw, wa, wb, wo, nmw, wup, wdn, fw)


def _sample_pre_kernel(xbc_ref, u_ref, sm_ref, cm0_ref, cm1_ref, cm2_ref, cl0_ref, cl1_ref, cl2_ref,
                       m_ref, cwm_ref, cbm_ref, cwl_ref, cbl_ref, wq_ref, wk_ref,
                       dtb_ref, alog_ref, ib_ref, fb_ref,
                       xc_ref, bcc_ref, q_ref, k_ref, dt_ref, da_ref, wr_ref, keep_ref, mnew_ref,
                       cmn_ref, cln_ref):
    xbc = xbc_ref[...].astype(F32)
    conv_m = (cbm_ref[...] + cwm_ref[0:1, :] * cm0_ref[...] + cwm_ref[1:2, :] * cm1_ref[...]
              + cwm_ref[2:3, :] * cm2_ref[...] + cwm_ref[3:4, :] * xbc)
    act = _silu(conv_m)
    xc_ref[...] = act[:, :M_INNER]
    bcc_ref[...] = act[:, M_INNER:]
    cmn_ref[0] = cm1_ref[...]
    cmn_ref[1] = cm2_ref[...]
    cmn_ref[2] = xbc

    u = u_ref[...].astype(F32)
    conv_l = (cbl_ref[...] + cwl_ref[0:1, :] * cl0_ref[...] + cwl_ref[1:2, :] * cl1_ref[...]
              + cwl_ref[2:3, :] * cl2_ref[...] + cwl_ref[3:4, :] * u)
    uc = _silu(conv_l)
    cln_ref[0] = cl1_ref[...]
    cln_ref[1] = cl2_ref[...]
    cln_ref[2] = u
    scale = L_HDIM ** -0.5
    for h in range(L_HEADS):
        sl = slice(h * L_HDIM, (h + 1) * L_HDIM)
        ub = uc[:, sl].astype(BF16)
        q_ref[:, sl] = jnp.dot(ub, wq_ref[h], preferred_element_type=F32)
        k_ref[:, sl] = jnp.dot(ub, wk_ref[h], preferred_element_type=F32) * scale

    sm = sm_ref[...]
    dt = _softplus(sm + dtb_ref[...])
    dt_ref[...] = dt
    da_ref[...] = jnp.exp(dt * (-jnp.exp(alog_ref[...])))
    ig = pltpu.roll(sm + ib_ref[...], F_LANE - I_LANE, axis=1)
    logf = _log_sigmoid(sm + fb_ref[...])
    m_old = m_ref[...]
    m_new = jnp.maximum(logf + m_old, ig)
    mnew_ref[...] = m_new
    wr_ref[...] = jnp.exp(ig - m_new)
    keep_ref[...] = jnp.exp(logf + m_old - m_new)


def _sample_pre(pbig, psmall, conv_m, conv_l, m_lanes, cwm, cbm, cwl, cbl, wq, wk, dtb, alog, ib, fb):
    s = pbig.shape[0]
    f = lambda shape: jax.ShapeDtypeStruct(shape, F32)
    full2 = lambda shape: pl.BlockSpec(shape, lambda i: (0, 0))
    full3 = lambda shape: pl.BlockSpec(shape, lambda i: (0, 0, 0))
    state_row = lambda width, j: pl.BlockSpec((None, s, width), lambda i, j=j: (j, 0, 0))
    return pl.pallas_call(
        _sample_pre_kernel,
        grid=(1,),
        in_specs=[
            full2((s, M_CONV)),
            pl.BlockSpec((s, L_INNER), lambda i: (0, 6)),
            full2((s, LANES)),
            state_row(M_CONV, 0), state_row(M_CONV, 1), state_row(M_CONV, 2),
            state_row(L_INNER, 0), state_row(L_INNER, 1), state_row(L_INNER, 2),
            full2((s, LANES)),
            full2((CONV_W, M_CONV)), full2((1, M_CONV)),
            full2((CONV_W, L_INNER)), full2((1, L_INNER)),
            full3((L_HEADS, L_HDIM, L_HDIM)), full3((L_HEADS, L_HDIM, L_HDIM)),
            full2((1, LANES)), full2((1, LANES)), full2((1, LANES)), full2((1, LANES)),
        ],
        out_specs=[
            full2((s, M_INNER)), full2((s, M_INNER)), full2((s, L_INNER)), full2((s, L_INNER)),
            full2((s, LANES)), full2((s, LANES)), full2((s, LANES)), full2((s, LANES)), full2((s, LANES)),
            full3((CONV_W - 1, s, M_CONV)), full3((CONV_W - 1, s, L_INNER)),
        ],
        out_shape=[
            f((s, M_INNER)), f((s, M_INNER)), f((s, L_INNER)), f((s, L_INNER)),
            f((s, LANES)), f((s, LANES)), f((s, LANES)), f((s, LANES)), f((s, LANES)),
            f((CONV_W - 1, s, M_CONV)), f((CONV_W - 1, s, L_INNER)),
        ],
        compiler_params=_params("arbitrary"),
        name="sample_pre",
    )(pbig[:, 2 * PROJ_BLOCK:6 * PROJ_BLOCK], pbig, psmall, conv_m, conv_m, conv_m,
      conv_l, conv_l, conv_l, m_lanes, cwm, cbm, cwl, cbl, wq, wk, dtb, alog, ib, fb)


SSD_SB = 8
MLSTM_SB = 16


def _ssd_state_kernel(dt_ref, da_ref, x_ref, b_ref, c_ref, h_ref, hn_ref, y_ref, xs_ref):
    base = pl.program_id(0) * SSD_SB
    rows = lax.broadcasted_iota(jnp.int32, (LANES, LANES), 0)
    lanes = _lane_iota((LANES, LANES))
    diag = rows == lanes
    first_rows = rows < M_HDIM
    ones = jnp.ones((LANES, LANES), BF16)
    pairs_per_group = M_PAIRS // M_GROUPS

    def spread(s):
        for hp in range(M_PAIRS):
            xrow = jnp.broadcast_to(x_ref[s:s + 1, hp * LANES:(hp + 1) * LANES], (LANES, LANES))
            xdiag = jnp.where(diag, xrow, 0.0).astype(BF16)
            xs_ref[s % 2, hp] = jnp.dot(xdiag, ones, preferred_element_type=F32)

    acc = jnp.zeros((LANES, LANES), F32)
    spread(0)
    for s in range(SSD_SB):
        if s + 1 < SSD_SB:
            spread(s + 1)
        for g in range(M_GROUPS):
            brow = b_ref[s, g:g + 1, :]
            crow = c_ref[s, g:g + 1, :]
            for pp in range(pairs_per_group):
                hp = g * pairs_per_group + pp
                h0, h1 = 2 * hp, 2 * hp + 1
                dav = jnp.where(first_rows, da_ref[base + s, h0], da_ref[base + s, h1])
                dtv = jnp.where(first_rows, dt_ref[base + s, h0], dt_ref[base + s, h1])
                hn = dav * h_ref[s, hp] + (dtv * xs_ref[s % 2, hp]) * brow
                hn_ref[s, hp] = hn
                ysum = jnp.dot((hn * crow).astype(BF16), ones, preferred_element_type=F32)
                acc = jnp.where(lanes == s * M_PAIRS + hp, ysum, acc)
    y_ref[...] = acc.T


def _ssd_state(dt, da, xc, b3, c3, h):
    s = h.shape[0]
    smem = pl.BlockSpec(memory_space=pltpu.SMEM)
    return pl.pallas_call(
        _ssd_state_kernel,
        grid=(s // SSD_SB,),
        in_specs=[
            smem, smem,
            pl.BlockSpec((SSD_SB, M_INNER), lambda i: (i, 0)),
            pl.BlockSpec((SSD_SB, M_GROUPS, M_STATE), lambda i: (i, 0, 0)),
            pl.BlockSpec((SSD_SB, M_GROUPS, M_STATE), lambda i: (i, 0, 0)),
            pl.BlockSpec((SSD_SB, M_PAIRS, LANES, M_STATE), lambda i: (i, 0, 0, 0)),
        ],
        out_specs=[
            pl.BlockSpec((SSD_SB, M_PAIRS, LANES, M_STATE), lambda i: (i, 0, 0, 0)),
            pl.BlockSpec((SSD_SB * M_PAIRS, LANES), lambda i: (i, 0)),
        ],
        out_shape=[
            jax.ShapeDtypeStruct(h.shape, F32),
            jax.ShapeDtypeStruct((s * M_PAIRS, LANES), F32),
        ],
        scratch_shapes=[pltpu.VMEM((2, M_PAIRS, LANES, LANES), F32)],
        compiler_params=_params("parallel"),
        name="ssd_state",
    )(dt, da, xc, b3, c3, h)


def _mlstm_state_kernel(wr_ref, keep_ref, q_ref, k_ref, v_ref, c_ref, cn_ref, num_ref):
    base = pl.program_id(0) * MLSTM_SB
    qt = q_ref[...].T
    kt = k_ref[...].T
    for s in range(MLSTM_SB):
        for h in range(L_HEADS):
            r = s * L_HEADS + h
            wr = wr_ref[base + s, h]
            keep = keep_ref[base + s, h]
            c_old = c_ref[s, h]
            num_ref[r:r + 1, :] = jnp.sum(qt[:, r:r + 1] * c_old, axis=0, keepdims=True)
            cn_ref[s, h] = keep * c_old + (kt[:, r:r + 1] * wr) * v_ref[r:r + 1, :]


def _mlstm_state(wr, keep, q_rows, k_rows, v_rows, c):
    s = c.shape[0]
    smem = pl.BlockSpec(memory_space=pltpu.SMEM)
    rows = pl.BlockSpec((LANES, LANES), lambda i: (i, 0))
    return pl.pallas_call(
        _mlstm_state_kernel,
        grid=(s // MLSTM_SB,),
        in_specs=[
            smem, smem, rows, rows, rows,
            pl.BlockSpec((MLSTM_SB, L_HEADS, L_HDIM, L_HDIM), lambda i: (i, 0, 0, 0)),
        ],
        out_specs=[
            pl.BlockSpec((MLSTM_SB, L_HEADS, L_HDIM, L_HDIM), lambda i: (i, 0, 0, 0)),
            rows,
        ],
        out_shape=[
            jax.ShapeDtypeStruct(c.shape, F32),
            jax.ShapeDtypeStruct(q_rows.shape, F32),
        ],
        compiler_params=_params("parallel"),
        name="mlstm_state",
    )(wr, keep, q_rows, k_rows, v_rows, c)


def _sample_post_kernel(y_ref, xc_ref, dvec_ref,
                        q_ref, k_ref, v_ref, n_ref, num_ref, wr_ref, keep_ref, mnew_ref,
                        ys_ref, hh_ref, nn_ref):
    ys_ref[...] = (y_ref[...] + dvec_ref[...] * xc_ref[...]).astype(BF16)

    wr = wr_ref[...]
    keep = keep_ref[...]
    floor = jnp.exp(-mnew_ref[...])
    for h in range(L_HEADS):
        ln = F_LANE + h
        sl = slice(h * L_HDIM, (h + 1) * L_HDIM)
        q = q_ref[:, sl]
        k = k_ref[:, sl]
        n_old = n_ref[:, sl]
        wrc = wr[:, ln:ln + 1]
        kpc = keep[:, ln:ln + 1]
        wgt = jnp.sum(q * k, axis=1, keepdims=True) * wrc
        num = wgt * v_ref[:, sl].astype(F32) + kpc * num_ref[:, sl]
        den = wgt + kpc * jnp.sum(q * n_old, axis=1, keepdims=True)
        hh_ref[:, sl] = (num / jnp.maximum(jnp.abs(den), floor[:, ln:ln + 1])).astype(BF16)
        nn_ref[:, sl] = kpc * n_old + wrc * k


def _sample_post(y, xc, pbig, dvec, q, k, n_rows, num, wr, keep, mnew):
    s = y.shape[0]
    full = lambda shape: pl.BlockSpec(shape, lambda i: (0, 0))
    blk = lambda width, j: pl.BlockSpec((s, width), lambda i, j=j: (0, j))
    return pl.pallas_call(
        _sample_post_kernel,
        grid=(1,),
        in_specs=[
            full((s, M_INNER)), full((s, M_INNER)), full((1, M_INNER)),
            full((s, L_INNER)), full((s, L_INNER)), blk(L_INNER, 7),
            full((s, L_INNER)), full((s, L_INNER)),
            full((s, LANES)), full((s, LANES)), full((s, LANES)),
        ],
        out_specs=[full((s, M_INNER)), full((s, L_INNER)), full((s, L_INNER))],
        out_shape=[
            jax.ShapeDtypeStruct((s, M_INNER), BF16),
            jax.ShapeDtypeStruct((s, L_INNER), BF16),
            jax.ShapeDtypeStruct((s, L_INNER), F32),
        ],
        compiler_params=_params("arbitrary"),
        name="sample_post",
    )(y, xc, dvec, q, k, pbig, n_rows, num, wr, keep, mnew)


def _lanes(vec, first_lane):
    n = vec.shape[0]
    return jnp.pad(vec.astype(F32), (first_lane, LANES - first_lane - n)).reshape(1, LANES)


def _spread_matrix(first_lane, n_heads, width):
    r = lax.broadcasted_iota(jnp.int32, (2 * LANES, n_heads * width), 0) % LANES
    c = lax.broadcasted_iota(jnp.int32, (2 * LANES, n_heads * width), 1)
    return (r - first_lane == c // width).astype(BF16)


def _shift_matrix(n):
    r = lax.broadcasted_iota(jnp.int32, ((CONV_W - 1) * n, 2 * n), 0)
    c = lax.broadcasted_iota(jnp.int32, ((CONV_W - 1) * n, 2 * n), 1)
    return (c == n + r % n - (r // n + 1)).astype(BF16)


def kernel(x_prompt, x_sample, state_mamba_conv, state_mamba_ssm, state_mlstm_conv, state_mlstm_C, state_mlstm_n, state_mlstm_m, w_in, mamba_conv_w, mamba_conv_b, mamba_dt_bias, mamba_A_log, mamba_D, mamba_norm_w, w_branch_a, mlstm_conv_w, mlstm_conv_b, mlstm_wq, mlstm_wk, mlstm_i_bias, mlstm_f_bias, mlstm_norm_w, w_branch_b, w_out, norm_mix_w, norm_mlp_w, w_up, w_down, final_norm_w):
    depth = w_in.shape[0]
    assert depth == 1
    batch, seq, _ = x_prompt.shape
    n_samp, dec_seq, _ = x_sample.shape
    assert dec_seq == 1 and seq % CHUNK == 0 and seq >= SUBLANES
    assert batch % SCAN_NB == 0
    l = 0

    assert w_in.shape[2] == W_END
    w_big, w_small = _wprep(jnp.transpose(w_in[l]))
    g_mix = norm_mix_w[l].reshape(1, D_MODEL)
    cwm = mamba_conv_w[l]
    cbm = mamba_conv_b[l].reshape(1, M_CONV)
    cwl = mlstm_conv_w[l]
    cbl = mlstm_conv_b[l].reshape(1, L_INNER)
    dtb = _lanes(mamba_dt_bias[l], DT_LANE)
    alog = _lanes(mamba_A_log[l], DT_LANE)
    ib = _lanes(mlstm_i_bias[l], I_LANE)
    fb = _lanes(mlstm_f_bias[l], F_LANE)
    dvec = jnp.repeat(mamba_D[l].astype(F32), M_HDIM).reshape(1, M_INNER)
    mnw = mamba_norm_w[l].reshape(1, M_INNER)
    lnw = mlstm_norm_w[l].reshape(1, L_INNER)
    wq = mlstm_wq[l].astype(BF16)
    wk = mlstm_wk[l].astype(BF16)
    wa = w_branch_a[l].astype(BF16)
    wb = w_branch_b[l].astype(BF16)
    wo = w_out[l].astype(BF16)
    wup = w_up[l].astype(BF16)
    wdn = w_down[l].astype(BF16)
    nmw = norm_mlp_w[l].reshape(1, D_MODEL)
    fw = final_norm_w.reshape(1, D_MODEL)
    e64 = _spread_matrix(DT_LANE, M_HEADS, M_HDIM)
    el = _spread_matrix(F_LANE, L_HEADS, L_HDIM)
    shift = _shift_matrix(CHUNK)

    xs = x_sample.reshape(n_samp, D_MODEL)
    sbig, ssmall = _inproj(xs, g_mix, w_big, w_small, tm=n_samp)
    m_lanes = jnp.pad(state_mlstm_m[l], ((0, 0), (F_LANE, LANES - F_LANE - L_HEADS)))
    (xc, bcc, q, k, dt, da, wr, keep, mnew, s_conv_m, s_conv_l) = _sample_pre(
        sbig, ssmall, jnp.transpose(state_mamba_conv[l], (1, 0, 2)),
        jnp.transpose(state_mlstm_conv[l], (1, 0, 2)), m_lanes,
        cwm, cbm, cwl, cbl, wq, wk, dtb, alog, ib, fb)
    samples = (
        dt[:, :M_HEADS], da[:, :M_HEADS],
        wr[:, F_LANE:F_LANE + L_HEADS], keep[:, F_LANE:F_LANE + L_HEADS], xc,
        bcc[:, :M_GROUPS * M_STATE].reshape(n_samp, M_GROUPS, M_STATE),
        bcc[:, M_GROUPS * M_STATE:].reshape(n_samp, M_GROUPS, M_STATE),
        q.reshape(n_samp * L_HEADS, L_HDIM), k.reshape(n_samp * L_HEADS, L_HDIM),
        sbig[:, 7 * PROJ_BLOCK:8 * PROJ_BLOCK].astype(F32).reshape(n_samp * L_HEADS, L_HDIM),
        state_mamba_ssm[l].reshape(n_samp, M_PAIRS, LANES, M_STATE), state_mlstm_C[l])

    xp = x_prompt.reshape(batch * seq, D_MODEL)
    pbig, psmall = _inproj(xp, g_mix, w_big, w_small, tm=2048)
    ya, hb, p_ssm, p_c, p_n, p_m, s_ssm, s_c, y_rows, num_rows = _scan_prompt(
        pbig, psmall, cwm[:, :M_INNER], cbm[:, :M_INNER], cwm[:, M_INNER:], cbm[:, M_INNER:],
        cwl, cbl, dtb, alog, dvec, wq, wk, ib, fb, e64, el, shift, batch, seq, samples)
    y_prompt = _tail(ya.reshape(batch * seq, M_INNER), hb.reshape(batch * seq, L_INNER), pbig, xp,
                     mnw, lnw, wa, wb, wo, nmw, wup, wdn, fw, tm=512)
    p3 = pbig.reshape(batch, seq, BIG_WIDTH)
    p_conv_m = p3[:, seq - (CONV_W - 1):, 2 * PROJ_BLOCK:6 * PROJ_BLOCK].astype(F32)
    p_conv_l = p3[:, seq - (CONV_W - 1):, 6 * PROJ_BLOCK:7 * PROJ_BLOCK].astype(F32)
    p_ssm = p_ssm.reshape(batch, M_HEADS, M_HDIM, M_STATE)
    p_m = p_m[:, 0, F_LANE:F_LANE + L_HEADS]

    ya_s, hb_s, s_n = _sample_post(
        y_rows.reshape(n_samp, M_INNER), xc, sbig, dvec, q, k,
        state_mlstm_n[l].reshape(n_samp, L_INNER), num_rows.reshape(n_samp, L_INNER),
        wr, keep, mnew)
    y_sample = _tail(ya_s, hb_s, sbig, xs, mnw, lnw, wa, wb, wo, nmw, wup, wdn, fw, tm=n_samp)

    lead = lambda a: a[None]
    return (
        y_prompt.reshape(batch, seq, D_MODEL),
        y_sample.reshape(n_samp, 1, D_MODEL),
        lead(p_conv_m), lead(p_ssm), lead(p_conv_l), lead(p_c), lead(p_n), lead(p_m),
        lead(jnp.transpose(s_conv_m, (1, 0, 2))),
        lead(s_ssm.reshape(n_samp, M_HEADS, M_HDIM, M_STATE)),
        lead(jnp.transpose(s_conv_l, (1, 0, 2))),
        lead(s_c), lead(s_n.reshape(n_samp, L_HEADS, L_HDIM)), lead(mnew[:, F_LANE:F_LANE + L_HEADS]),
    )
```

```python
import jax
import jax.numpy as jnp
from jax import lax
from jax.experimental import pallas as pl
from jax.experimental.pallas import tpu as pltpu

F32 = jnp.float32
BF16 = jnp.bfloat16

D_MODEL = 1024
M_INNER = 2048
M_HEADS = 32
M_HDIM = 64
M_GROUPS = 8
M_PAIRS = M_HEADS // 2
M_STATE = 128
M_CONV = 4096
L_INNER = 1024
L_HEADS = 8
L_HDIM = 128
D_FF = 4096
CONV_W = 4
CHUNK = 128
EPS = 1e-6

LANES = 128
SUBLANES = 8
PROJ_BLOCK = 1024
N_PROJ_BLOCKS = 11
BIG_WIDTH = PROJ_BLOCK * N_PROJ_BLOCKS
DT_LANE = 0
I_LANE = 32
F_LANE = 40
VMEM_LIMIT = 56 * 1024 * 1024

LOG2_E = 1.4426950408889634
NT_DIMS = (((1,), (1,)), ((), ()))
TN_DIMS = (((0,), (0,)), ((), ()))


def _params(*sem):
    return pltpu.CompilerParams(dimension_semantics=sem, vmem_limit_bytes=VMEM_LIMIT)


def _sigmoid(x):
    return 0.5 * jnp.tanh(0.5 * x) + 0.5


def _silu(x):
    h = 0.5 * x
    return h + h * jnp.tanh(h)


def _log1p_exp_neg_abs(x):
    e = jnp.exp(-jnp.abs(x))
    u = 1.0 + e
    return jnp.where(u == 1.0, e, jnp.log(u) * (e / (u - 1.0)))


def _softplus(x):
    return jnp.maximum(x, 0.0) + _log1p_exp_neg_abs(x)


def _log_sigmoid(x):
    return jnp.minimum(x, 0.0) - _log1p_exp_neg_abs(x)


def _rms(x, w):
    return x * lax.rsqrt(jnp.mean(x * x, axis=-1, keepdims=True) + EPS) * w


def _lane_iota(shape):
    return lax.broadcasted_iota(jnp.int32, shape, len(shape) - 1)


def _tri(n):
    r = lax.broadcasted_iota(jnp.int32, (n, n), 0)
    c = lax.broadcasted_iota(jnp.int32, (n, n), 1)
    return r >= c


def _split3(a):
    hi = a.astype(BF16)
    r1 = a - hi.astype(F32)
    mid = r1.astype(BF16)
    lo = (r1 - mid.astype(F32)).astype(BF16)
    return hi, mid, lo


def _cumsum_rows(causal, a):
    tri01 = causal.astype(F32).astype(BF16)
    return jnp.dot(jnp.concatenate([tri01] * 3, axis=1), jnp.concatenate(_split3(a), axis=0),
                   preferred_element_type=F32)


def _spread_many(arrs, e2_ref):
    n = arrs[0].shape[0]
    pieces = [jnp.concatenate(_split3(a)[:2], axis=1) for a in arrs]
    out = jnp.dot(jnp.concatenate(pieces, axis=0), e2_ref[...], preferred_element_type=F32)
    return [out[i * n:(i + 1) * n] for i in range(len(arrs))]


def _spread(a, e2_ref):
    return _spread_many((a,), e2_ref)[0]


CONV_COLS = 512
BF16_ROWS = 16


def _conv_silu(x_ref, win_ref, shift_ref, w_ref, b_ref, out_ref):
    n, width = x_ref.shape
    win_ref[n:2 * n, :] = x_ref[...]
    for c0 in range(0, width, CONV_COLS):
        cs = slice(c0, c0 + CONV_COLS)
        sh = jnp.dot(shift_ref[...], win_ref[:, cs], preferred_element_type=F32)
        y = b_ref[:, cs] + w_ref[CONV_W - 1:CONV_W, cs] * x_ref[:, cs].astype(F32)
        for back in range(1, CONV_W):
            y = y + w_ref[CONV_W - 1 - back:CONV_W - back, cs] * sh[(back - 1) * n:back * n]
        out_ref[:, cs] = _silu(y).astype(out_ref.dtype)
    win_ref[n - BF16_ROWS:n, :] = x_ref[n - BF16_ROWS:n, :]


W_DT0 = M_INNER + M_CONV
W_U0 = W_DT0 + M_HEADS
W_I0 = W_U0 + 3 * L_INNER
W_G0 = W_I0 + 2 * L_HEADS
W_END = W_G0 + 2 * D_MODEL
W_FIRST_UVO = W_DT0 // PROJ_BLOCK
W_FIRST_GATE = W_FIRST_UVO + 3 * L_INNER // PROJ_BLOCK
W_SHIFT_UVO = W_U0 - W_DT0
W_SHIFT_GATE = W_G0 - (W_DT0 + 3 * L_INNER)
W_NEXT_ROWS = 64
assert W_DT0 % PROJ_BLOCK == 0 and DT_LANE == 0
assert (W_I0 - I_LANE) == W_FIRST_GATE * PROJ_BLOCK and F_LANE == I_LANE + L_HEADS
assert max(W_SHIFT_UVO, W_SHIFT_GATE) <= W_NEXT_ROWS and PROJ_BLOCK % W_NEXT_ROWS == 0
assert W_SHIFT_UVO % BF16_ROWS == 0 and W_SHIFT_GATE % BF16_ROWS == 0
assert M_HEADS % BF16_ROWS == 0 and I_LANE % BF16_ROWS == 0 and (2 * L_HEADS) % BF16_ROWS == 0


def _wprep_kernel(a_ref, b_ref, big_ref, small_ref):
    j = pl.program_id(0)

    def emit(shift):
        if shift == 0:
            big_ref[...] = a_ref[...].astype(BF16)
        else:
            big_ref[0:PROJ_BLOCK - shift, :] = a_ref[shift:PROJ_BLOCK, :].astype(BF16)
            big_ref[PROJ_BLOCK - shift:PROJ_BLOCK, :] = b_ref[0:shift, :].astype(BF16)

    @pl.when(j == 0)
    def _():
        small_ref[...] = jnp.zeros_like(small_ref)

    @pl.when(j < W_FIRST_UVO)
    def _():
        emit(0)

    @pl.when((j >= W_FIRST_UVO) & (j < W_FIRST_GATE))
    def _():
        emit(W_SHIFT_UVO)

    @pl.when(j >= W_FIRST_GATE)
    def _():
        emit(W_SHIFT_GATE)

    @pl.when(j == W_FIRST_UVO)
    def _():
        small_ref[DT_LANE:DT_LANE + M_HEADS, :] = a_ref[DT_LANE:DT_LANE + M_HEADS, :].astype(BF16)

    @pl.when(j == W_FIRST_GATE)
    def _():
        small_ref[I_LANE:I_LANE + 2 * L_HEADS, :] = a_ref[I_LANE:I_LANE + 2 * L_HEADS, :].astype(BF16)


def _wprep(w_t):
    return pl.pallas_call(
        _wprep_kernel,
        grid=(N_PROJ_BLOCKS,),
        in_specs=[
            pl.BlockSpec((PROJ_BLOCK, D_MODEL), lambda j: (j, 0)),
            pl.BlockSpec((W_NEXT_ROWS, D_MODEL), lambda j: ((j + 1) * (PROJ_BLOCK // W_NEXT_ROWS), 0)),
        ],
        out_specs=[
            pl.BlockSpec((PROJ_BLOCK, D_MODEL), lambda j: (j, 0)),
            pl.BlockSpec((LANES, D_MODEL), lambda j: (0, 0)),
        ],
        out_shape=[
            jax.ShapeDtypeStruct((BIG_WIDTH, D_MODEL), BF16),
            jax.ShapeDtypeStruct((LANES, D_MODEL), BF16),
        ],
        compiler_params=_params("arbitrary"),
        name="w_prep",
    )(w_t, w_t)


def _inproj_kernel(x_ref, g_ref, w_ref, ws_ref, o_ref, os_ref, xn_ref):
    @pl.when(pl.program_id(1) == 0)
    def _():
        xn_ref[...] = _rms(x_ref[...], g_ref[...]).astype(BF16)
        os_ref[...] = lax.dot_general(xn_ref[...], ws_ref[...], NT_DIMS, preferred_element_type=F32)

    o_ref[...] = lax.dot_general(
        xn_ref[...], w_ref[...], NT_DIMS, preferred_element_type=F32).astype(BF16)


def _inproj(x, g, w_big, w_small, tm):
    m = x.shape[0]
    tm = min(tm, m)
    return pl.pallas_call(
        _inproj_kernel,
        grid=(m // tm, N_PROJ_BLOCKS),
        in_specs=[
            pl.BlockSpec((tm, D_MODEL), lambda i, j: (i, 0)),
            pl.BlockSpec((1, D_MODEL), lambda i, j: (0, 0)),
            pl.BlockSpec((PROJ_BLOCK, D_MODEL), lambda i, j: (j, 0)),
            pl.BlockSpec((LANES, D_MODEL), lambda i, j: (0, 0)),
        ],
        out_specs=[
            pl.BlockSpec((tm, PROJ_BLOCK), lambda i, j: (i, j)),
            pl.BlockSpec((tm, LANES), lambda i, j: (i, 0)),
        ],
        out_shape=[
            jax.ShapeDtypeStruct((m, BIG_WIDTH), BF16),
            jax.ShapeDtypeStruct((m, LANES), F32),
        ],
        scratch_shapes=[pltpu.VMEM((tm, D_MODEL), BF16)],
        compiler_params=_params("parallel", "arbitrary"),
        name="in_proj",
    )(x, g, w_big, w_small)


SCAN_NB = 2


def _ssd_chunk(x_ref, bc_ref, sm, cwx_ref, cbx_ref, cwbc_ref, cbbc_ref, dtb_ref, alog_ref, dvec_ref,
               e64_ref, shift_ref, y_ref, h_ref, winx_ref, winbc_ref, xc_scr, bcc_scr):
    n = x_ref.shape[0]
    _conv_silu(x_ref, winx_ref, shift_ref, cwx_ref, cbx_ref, xc_scr)
    _conv_silu(bc_ref, winbc_ref, shift_ref, cwbc_ref, cbbc_ref, bcc_scr)
    xc = xc_scr[...]
    bcc = bcc_scr[...]

    lane = _lane_iota((n, LANES))
    dt = _softplus(sm + dtb_ref[...])
    da = jnp.where(lane < M_HEADS, dt * (-LOG2_E * jnp.exp(alog_ref[...])), 0.0)
    causal = _tri(n)
    acum = _cumsum_rows(causal, da)
    acum_t = acum.T
    last = acum[n - 1:n, :]
    exp_last = jnp.exp2(last)
    dt_x, wr_x, ea_x = _spread_many((dt, jnp.exp2(last - acum), jnp.exp2(acum)), e64_ref)

    xdt = xc * dt_x
    xw = (xdt * wr_x).astype(BF16)
    xb = xdt.astype(BF16)
    low_half = jnp.bitwise_and(_lane_iota((n, M_INNER)), LANES - 1) < M_HDIM
    zero = jnp.zeros((), BF16)
    x_lo = jnp.where(low_half, xb, zero)
    x_hi = jnp.where(low_half, zero, xb)
    first_rows = lax.broadcasted_iota(jnp.int32, (LANES, LANES), 0) < M_HDIM

    def weights(cb, h):
        seg = acum[:, h:h + 1] - acum_t[h:h + 1, :]
        return (cb * jnp.exp2(jnp.where(causal, seg, -jnp.inf))).astype(BF16)

    pairs_per_group = M_PAIRS // M_GROUPS
    for g in range(M_GROUPS):
        bg = bcc[:, g * M_STATE:(g + 1) * M_STATE]
        cg = bcc[:, (M_GROUPS + g) * M_STATE:(M_GROUPS + g + 1) * M_STATE]
        cb = lax.dot_general(cg, bg, NT_DIMS, preferred_element_type=F32)
        for pp in range(pairs_per_group):
            hp = g * pairs_per_group + pp
            h0, h1 = 2 * hp, 2 * hp + 1
            sl = slice(hp * LANES, (hp + 1) * LANES)
            hs = h_ref[hp]
            y = jnp.dot(weights(cb, h0), x_lo[:, sl], preferred_element_type=F32)
            y = y + jnp.dot(weights(cb, h1), x_hi[:, sl], preferred_element_type=F32)
            ys = lax.dot_general(cg, hs.astype(BF16), NT_DIMS, preferred_element_type=F32)
            y_ref[:, sl] = (y + ea_x[:, sl] * ys + dvec_ref[:, sl] * xc[:, sl]).astype(BF16)
            el = jnp.where(first_rows, exp_last[:, h0:h0 + 1], exp_last[:, h1:h1 + 1])
            h_ref[hp] = el * hs + lax.dot_general(xw[:, sl], bg, TN_DIMS, preferred_element_type=F32)


def _mlstm_chunks(seqs, wq_ref, wk_ref, ib_ref, fb_ref, el_ref):
    n = seqs[0][0].shape[0]
    causal = _tri(n)
    lane = _lane_iota((n, LANES))
    gate_lanes = (lane >= F_LANE) & (lane < F_LANE + L_HEADS)
    scale = L_HDIM ** -0.5
    lanes_of = [slice(h * L_HDIM, (h + 1) * L_HDIM) for h in range(L_HEADS)]

    gates = []
    for uc, v_ref, sm, hh_ref, c_ref, n_ref, m_ref in seqs:
        ig = pltpu.roll(sm + ib_ref[...], F_LANE - I_LANE, axis=1)
        logf = jnp.where(gate_lanes, _log_sigmoid(sm + fb_ref[...]), 0.0)
        bcum = _cumsum_rows(causal, logf)
        m_old = m_ref[...]
        bl = bcum[n - 1:n, :]
        src = bl - bcum + ig
        m_new = jnp.maximum(bl + m_old, jnp.max(src, axis=0, keepdims=True))
        gates.append(dict(
            bcum=bcum, bcum_t=bcum.T, ig_t=ig.T, m_old=m_old, m_new=m_new,
            keep=jnp.exp(bl + m_old - m_new),
            wr_x=_spread(jnp.exp(src - m_new), el_ref)))

    work = [(i, h) for i in range(len(seqs)) for h in range(L_HEADS)]
    c_olds = {(i, h): seqs[i][4][h] for i, h in work}
    n_olds = {(i, h): seqs[i][5][h:h + 1, :] for i, h in work}
    qs, ks, qbs, kbs = {}, {}, {}, {}
    for i, h in work:
        ub = seqs[i][0][:, lanes_of[h]]
        qs[i, h] = jnp.dot(ub, wq_ref[h], preferred_element_type=F32)
        ks[i, h] = jnp.dot(ub, wk_ref[h], preferred_element_type=F32) * scale
    logds, inters, rowmax = {}, {}, {}
    for i, h in work:
        g = gates[i]
        ln = F_LANE + h
        bcol = jnp.broadcast_to(g["bcum"][:, ln:ln + 1], (n, n))
        logd = jnp.where(causal, bcol - g["bcum_t"][ln:ln + 1, :] + g["ig_t"][ln:ln + 1, :], -jnp.inf)
        logds[i, h] = logd
        inters[i, h] = bcol + g["m_old"][:, ln:ln + 1]
        rowmax[i, h] = jnp.max(logd, axis=1, keepdims=True)
    ss, scs, floors = {}, {}, {}
    for i, h in work:
        qbs[i, h] = qs[i, h].astype(BF16)
        kbs[i, h] = ks[i, h].astype(BF16)
        m_s = jnp.maximum(inters[i, h], rowmax[i, h])
        dm = jnp.exp(logds[i, h] - m_s)
        scs[i, h] = jnp.exp(inters[i, h] - m_s)
        floors[i, h] = jnp.exp(-m_s)
        ss[i, h] = lax.dot_general(qbs[i, h], kbs[i, h], NT_DIMS, preferred_element_type=F32) * dm
    nums, dens = {}, {}
    for i, h in work:
        vb = seqs[i][1][:, lanes_of[h]]
        num = jnp.dot(ss[i, h].astype(BF16), vb, preferred_element_type=F32)
        nums[i, h] = num + scs[i, h] * jnp.dot(
            qbs[i, h], c_olds[i, h].astype(BF16), preferred_element_type=F32)
        dens[i, h] = (jnp.sum(ss[i, h], axis=1, keepdims=True)
                      + scs[i, h] * jnp.sum(qs[i, h] * n_olds[i, h], axis=1, keepdims=True))
    for i, h in work:
        hh = nums[i, h] / jnp.maximum(jnp.abs(dens[i, h]), floors[i, h])
        seqs[i][3][:, lanes_of[h]] = hh.astype(BF16)
    for i, h in work:
        g = gates[i]
        ln = F_LANE + h
        kw = ks[i, h] * g["wr_x"][:, lanes_of[h]]
        kp = g["keep"][:, ln:ln + 1]
        seqs[i][4][h] = kp * c_olds[i, h] + lax.dot_general(
            kw.astype(BF16), seqs[i][1][:, lanes_of[h]], TN_DIMS, preferred_element_type=F32)
        seqs[i][5][h:h + 1, :] = kp * n_olds[i, h] + jnp.sum(kw, axis=0, keepdims=True)
    for i in range(len(seqs)):
        seqs[i][6][...] = gates[i]["m_new"]


def _diag_rows(row, diag):
    return jnp.where(diag, jnp.broadcast_to(row, diag.shape), 0.0).astype(BF16)


def _state_spreads(x_ref, q_ref, k_ref, xs_scr, qs_scr, ks_scr):
    ns = x_ref.shape[0]
    rows = lax.broadcasted_iota(jnp.int32, (LANES, LANES), 0)
    diag = rows == _lane_iota((LANES, LANES))
    ones = jnp.ones((LANES, LANES), BF16)
    ones2 = jnp.ones((2 * LANES, LANES), BF16)
    for s in range(ns):
        for hp in range(M_PAIRS):
            xrow = x_ref[s:s + 1, hp * LANES:(hp + 1) * LANES]
            xs_scr[s * M_PAIRS + hp] = jnp.dot(_diag_rows(xrow, diag), ones, preferred_element_type=F32)
    for r in range(ns * L_HEADS):
        for src_ref, dst_scr in ((q_ref, qs_scr), (k_ref, ks_scr)):
            row = src_ref[r:r + 1, :]
            hi = row.astype(BF16).astype(F32)
            lhs = jnp.concatenate([_diag_rows(hi, diag), _diag_rows(row - hi, diag)], axis=1)
            dst_scr[r] = jnp.dot(lhs, ones2, preferred_element_type=F32)


def _state_updates(base, dt_ref, da_ref, wr_ref, keep_ref, b_ref, c_ref, v_ref, h_ref, cst_ref,
                   hn_ref, cn_ref, y_ref, num_ref, xs_scr, qs_scr, ks_scr):
    ns = b_ref.shape[0]
    rows = lax.broadcasted_iota(jnp.int32, (LANES, LANES), 0)
    lanes = _lane_iota((LANES, LANES))
    first_rows = rows < M_HDIM
    ones = jnp.ones((LANES, LANES), BF16)
    pairs_per_group = M_PAIRS // M_GROUPS
    acc = jnp.zeros((LANES, LANES), F32)
    for s in range(ns):
        for hp in range(M_PAIRS):
            g = hp // pairs_per_group
            h0, h1 = 2 * hp, 2 * hp + 1
            brow = b_ref[s, g:g + 1, :]
            crow = c_ref[s, g:g + 1, :]
            dav = jnp.where(first_rows, da_ref[base + s, h0], da_ref[base + s, h1])
            dtv = jnp.where(first_rows, dt_ref[base + s, h0], dt_ref[base + s, h1])
            hn = dav * h_ref[s, hp] + (dtv * xs_scr[s * M_PAIRS + hp]) * brow
            hn_ref[s, hp] = hn
            ysum = jnp.dot((hn * crow).astype(BF16), ones, preferred_element_type=F32)
            acc = jnp.where(lanes == s * M_PAIRS + hp, ysum, acc)
    y_ref[...] = acc.T[0:ns * M_PAIRS, :]
    for s in range(ns):
        for h in range(L_HEADS):
            r = s * L_HEADS + h
            c_old = cst_ref[s, h]
            num_ref[r:r + 1, :] = jnp.sum(qs_scr[r] * c_old, axis=0, keepdims=True)
            cn_ref[s, h] = (keep_ref[base + s, h] * c_old
                            + (ks_scr[r] * wr_ref[base + s, h]) * v_ref[r:r + 1, :])


def _scan_kernel(dt_ref, da_ref, wr_ref, keep_ref, sx_ref, sb_ref, sc_ref, sq_ref, sk_ref, sv_ref,
                 sh_ref, scst_ref, x_ref, bc_ref, u_ref, v_ref, sm_ref,
                 cwx_ref, cbx_ref, cwbc_ref, cbbc_ref, cwl_ref, cbl_ref, dtb_ref, alog_ref, dvec_ref,
                 wq_ref, wk_ref, ib_ref, fb_ref, e64_ref, el_ref, shift_ref,
                 y_ref, hh_ref, h_ref, c_ref, n_ref, m_ref, shn_ref, scn_ref, sy_ref, snum_ref,
                 winx_ref, winbc_ref, winu_ref, xc_scr, bcc_scr, uc_scr, xs_scr, qs_scr, ks_scr):
    n = x_ref.shape[1]
    step = pl.program_id(0) * pl.num_programs(1) + pl.program_id(1)
    _state_spreads(sx_ref, sq_ref, sk_ref, xs_scr, qs_scr, ks_scr)

    @pl.when(pl.program_id(1) == 0)
    def _():
        h_ref[...] = jnp.zeros_like(h_ref)
        c_ref[...] = jnp.zeros_like(c_ref)
        n_ref[...] = jnp.zeros_like(n_ref)
        m_ref[...] = jnp.zeros_like(m_ref)
        winx_ref[:, 0:n, :] = jnp.zeros((SCAN_NB, n, M_INNER), BF16)
        winbc_ref[:, 0:n, :] = jnp.zeros((SCAN_NB, n, M_INNER), BF16)
        winu_ref[:, 0:n, :] = jnp.zeros((SCAN_NB, n, L_INNER), BF16)

    seqs = []
    for i in range(SCAN_NB):
        _ssd_chunk(x_ref.at[i], bc_ref.at[i], sm_ref[i], cwx_ref, cbx_ref, cwbc_ref, cbbc_ref,
                   dtb_ref, alog_ref, dvec_ref, e64_ref, shift_ref, y_ref.at[i], h_ref.at[i],
                   winx_ref.at[i], winbc_ref.at[i], xc_scr, bcc_scr)
        _conv_silu(u_ref.at[i], winu_ref.at[i], shift_ref, cwl_ref, cbl_ref, uc_scr.at[i])
        seqs.append((uc_scr.at[i], v_ref.at[i], sm_ref[i], hh_ref.at[i], c_ref.at[i], n_ref.at[i],
                     m_ref.at[i]))
    _mlstm_chunks(seqs, wq_ref, wk_ref, ib_ref, fb_ref, el_ref)
    _state_updates(step * sb_ref.shape[0], dt_ref, da_ref, wr_ref, keep_ref, sb_ref, sc_ref, sv_ref,
                   sh_ref, scst_ref, shn_ref, scn_ref, sy_ref, snum_ref, xs_scr, qs_scr, ks_scr)


def _scan_prompt(pbig, psmall, cwx, cbx, cwbc, cbbc, cwl, cbl, dtb, alog, dvec, wq, wk, ib, fb,
                 e64, el, shift, batch, seq, samples):
    p3 = pbig.reshape(batch, seq, BIG_WIDTH)
    s3 = psmall.reshape(batch, seq, LANES)
    nb = SCAN_NB
    n_chunks = seq // CHUNK
    n_steps = (batch // nb) * n_chunks
    s_dt, s_da, s_wr, s_keep, s_x, s_b, s_c, s_q, s_k, s_v, s_h, s_cst = samples
    n_samp = s_h.shape[0]
    ns = n_samp // n_steps
    assert ns * n_steps == n_samp and ns * M_PAIRS <= LANES and (ns * L_HEADS) % SUBLANES == 0
    smem = pl.BlockSpec(memory_space=pltpu.SMEM)
    step = lambda b, c: b * n_chunks + c
    srows = lambda n_rows: pl.BlockSpec((n_rows, LANES), lambda b, c: (step(b, c), 0))
    sgrp = pl.BlockSpec((ns, M_GROUPS, M_STATE), lambda b, c: (step(b, c), 0, 0))
    sstate = lambda d1: pl.BlockSpec((ns, d1, LANES, LANES), lambda b, c: (step(b, c), 0, 0, 0))
    const2 = lambda b, c: (0, 0)
    const3 = lambda b, c: (0, 0, 0)
    rows = lambda width, col: pl.BlockSpec((nb, CHUNK, width), lambda b, c, col=col: (b, c, col))
    state4 = lambda d1, d2, d3: pl.BlockSpec((nb, d1, d2, d3), lambda b, c: (b, 0, 0, 0))
    return pl.pallas_call(
        _scan_kernel,
        grid=(batch // nb, seq // CHUNK),
        in_specs=[
            smem, smem, smem, smem,
            pl.BlockSpec((None, ns, M_INNER), lambda b, c: (step(b, c), 0, 0)),
            sgrp, sgrp, srows(ns * L_HEADS), srows(ns * L_HEADS), srows(ns * L_HEADS),
            sstate(M_PAIRS), sstate(L_HEADS),
            rows(M_INNER, 1),
            rows(M_INNER, 2),
            rows(L_INNER, 6),
            rows(L_INNER, 7),
            rows(LANES, 0),
            pl.BlockSpec((CONV_W, M_INNER), const2), pl.BlockSpec((1, M_INNER), const2),
            pl.BlockSpec((CONV_W, M_INNER), const2), pl.BlockSpec((1, M_INNER), const2),
            pl.BlockSpec((CONV_W, L_INNER), const2), pl.BlockSpec((1, L_INNER), const2),
            pl.BlockSpec((1, LANES), const2), pl.BlockSpec((1, LANES), const2),
            pl.BlockSpec((1, M_INNER), const2),
            pl.BlockSpec((L_HEADS, L_HDIM, L_HDIM), const3),
            pl.BlockSpec((L_HEADS, L_HDIM, L_HDIM), const3),
            pl.BlockSpec((1, LANES), const2), pl.BlockSpec((1, LANES), const2),
            pl.BlockSpec((2 * LANES, M_INNER), const2),
            pl.BlockSpec((2 * LANES, L_INNER), const2),
            pl.BlockSpec(((CONV_W - 1) * CHUNK, 2 * CHUNK), const2),
        ],
        out_specs=[
            rows(M_INNER, 0),
            rows(L_INNER, 0),
            state4(M_PAIRS, LANES, M_STATE),
            state4(L_HEADS, L_HDIM, L_HDIM),
            pl.BlockSpec((nb, L_HEADS, L_HDIM), lambda b, c: (b, 0, 0)),
            pl.BlockSpec((nb, 1, LANES), lambda b, c: (b, 0, 0)),
            sstate(M_PAIRS), sstate(L_HEADS), srows(ns * M_PAIRS), srows(ns * L_HEADS),
        ],
        out_shape=[
            jax.ShapeDtypeStruct((batch, seq, M_INNER), BF16),
            jax.ShapeDtypeStruct((batch, seq, L_INNER), BF16),
            jax.ShapeDtypeStruct((batch, M_PAIRS, LANES, M_STATE), F32),
            jax.ShapeDtypeStruct((batch, L_HEADS, L_HDIM, L_HDIM), F32),
            jax.ShapeDtypeStruct((batch, L_HEADS, L_HDIM), F32),
            jax.ShapeDtypeStruct((batch, 1, LANES), F32),
            jax.ShapeDtypeStruct(s_h.shape, F32),
            jax.ShapeDtypeStruct(s_cst.shape, F32),
            jax.ShapeDtypeStruct((n_samp * M_PAIRS, LANES), F32),
            jax.ShapeDtypeStruct((n_samp * L_HEADS, LANES), F32),
        ],
        scratch_shapes=[
            pltpu.VMEM((nb, 2 * CHUNK, M_INNER), BF16),
            pltpu.VMEM((nb, 2 * CHUNK, M_INNER), BF16),
            pltpu.VMEM((nb, 2 * CHUNK, L_INNER), BF16),
            pltpu.VMEM((CHUNK, M_INNER), F32),
            pltpu.VMEM((CHUNK, M_INNER), BF16),
            pltpu.VMEM((nb, CHUNK, L_INNER), BF16),
            pltpu.VMEM((ns * M_PAIRS, LANES, LANES), F32),
            pltpu.VMEM((ns * L_HEADS, LANES, LANES), F32),
            pltpu.VMEM((ns * L_HEADS, LANES, LANES), F32),
        ],
        compiler_params=_params("parallel", "arbitrary"),
        name="scan_prompt",
    )(s_dt, s_da, s_wr, s_keep, s_x.reshape(n_steps, ns, M_INNER), s_b, s_c, s_q, s_k, s_v, s_h, s_cst,
      p3, p3, p3, p3, s3, cwx, cbx, cwbc, cbbc, cwl, cbl, dtb, alog, dvec, wq, wk, ib, fb,
      e64, el, shift)


def _tail_kernel(ys_ref, hh_ref, z_ref, o_ref, ga_ref, gb_ref, x_ref, mnw_ref, lnw_ref,
                 wa_ref, wb_ref, wo_ref, nmw_ref, wup_ref, wdn_ref, fw_ref, y_ref):
    gw = M_INNER // M_GROUPS
    a = None
    for g in range(M_GROUPS):
        sl = slice(g * gw, (g + 1) * gw)
        yg = ys_ref[:, sl].astype(F32) * _silu(z_ref[:, sl].astype(F32))
        part = jnp.dot(_rms(yg, mnw_ref[:, sl]).astype(BF16), wa_ref[sl, :], preferred_element_type=F32)
        a = part if a is None else a + part
    b = None
    for hp in range(L_HEADS // 2):
        pieces = []
        for h in (2 * hp, 2 * hp + 1):
            sl = slice(h * L_HDIM, (h + 1) * L_HDIM)
            gate = _sigmoid(o_ref[:, sl].astype(F32))
            pieces.append((gate * _rms(hh_ref[:, sl].astype(F32), lnw_ref[:, sl])).astype(BF16))
        rows = slice(2 * hp * L_HDIM, (2 * hp + 2) * L_HDIM)
        part = jnp.dot(jnp.concatenate(pieces, axis=1), wb_ref[rows, :], preferred_element_type=F32)
        b = part if b is None else b + part
    t = _sigmoid(ga_ref[...].astype(F32)) * a + _sigmoid(gb_ref[...].astype(F32)) * b
    x1 = x_ref[...] + jnp.dot(t.astype(BF16), wo_ref[...], preferred_element_type=F32)
    hn = _rms(x1, nmw_ref[...]).astype(BF16)
    acc = x1
    for c in range(D_FF // PROJ_BLOCK):
        sl = slice(c * PROJ_BLOCK, (c + 1) * PROJ_BLOCK)
        up = jnp.dot(hn, wup_ref[:, sl], preferred_element_type=F32)
        act = jnp.square(jnp.maximum(up, 0.0)).astype(BF16)
        acc = acc + jnp.dot(act, wdn_ref[sl, :], preferred_element_type=F32)
    y_ref[...] = _rms(acc, fw_ref[...])


def _tail(ys, hh, pbig, x, mnw, lnw, wa, wb, wo, nmw, wup, wdn, fw, tm):
    m = x.shape[0]
    tm = min(tm, m)
    rows = lambda i: (i, 0)
    const = lambda i: (0, 0)

    def resident(shape):
        return pl.BlockSpec(shape, const, pipeline_mode=pl.Buffered(1))

    return pl.pallas_call(
        _tail_kernel,
        grid=(m // tm,),
        in_specs=[
            pl.BlockSpec((tm, M_INNER), rows),
            pl.BlockSpec((tm, L_INNER), rows),
            pl.BlockSpec((tm, M_INNER), rows),
            pl.BlockSpec((tm, PROJ_BLOCK), lambda i: (i, 8)),
            pl.BlockSpec((tm, PROJ_BLOCK), lambda i: (i, 9)),
            pl.BlockSpec((tm, PROJ_BLOCK), lambda i: (i, 10)),
            pl.BlockSpec((tm, D_MODEL), rows),
            resident((1, M_INNER)),
            resident((1, L_INNER)),
            resident((M_INNER, D_MODEL)),
            resident((L_INNER, D_MODEL)),
            resident((D_MODEL, D_MODEL)),
            resident((1, D_MODEL)),
            resident((D_MODEL, D_FF)),
            resident((D_FF, D_MODEL)),
            resident((1, D_MODEL)),
        ],
        out_specs=pl.BlockSpec((tm, D_MODEL), rows),
        out_shape=jax.ShapeDtypeStruct((m, D_MODEL), F32),
        compiler_params=_params("parallel"),
        name="tail",
    )(ys, hh, pbig, pbig, pbig, pbig, x, mnw, lnw, wa, wb, wo, nmw, wup, wdn, fw)


def _sample_pre_kernel(xbc_ref, u_ref, sm_ref, cm0_ref, cm1_ref, cm2_ref, cl0_ref, cl1_ref, cl2_ref,
                       m_ref, cwm_ref, cbm_ref, cwl_ref, cbl_ref, wq_ref, wk_ref,
                       dtb_ref, alog_ref, ib_ref, fb_ref,
                       xc_ref, bcc_ref, q_ref, k_ref, dt_ref, da_ref, wr_ref, keep_ref, mnew_ref,
                       cmn_ref, cln_ref):
    xbc = xbc_ref[...].astype(F32)
    conv_m = (cbm_ref[...] + cwm_ref[0:1, :] * cm0_ref[...] + cwm_ref[1:2, :] * cm1_ref[...]
              + cwm_ref[2:3, :] * cm2_ref[...] + cwm_ref[3:4, :] * xbc)
    act = _silu(conv_m)
    xc_ref[...] = act[:, :M_INNER]
    bcc_ref[...] = act[:, M_INNER:]
    cmn_ref[0] = cm1_ref[...]
    cmn_ref[1] = cm2_ref[...]
    cmn_ref[2] = xbc

    u = u_ref[...].astype(F32)
    conv_l = (cbl_ref[...] + cwl_ref[0:1, :] * cl0_ref[...] + cwl_ref[1:2, :] * cl1_ref[...]
              + cwl_ref[2:3, :] * cl2_ref[...] + cwl_ref[3:4, :] * u)
    uc = _silu(conv_l)
    cln_ref[0] = cl1_ref[...]
    cln_ref[1] = cl2_ref[...]
    cln_ref[2] = u
    scale = L_HDIM ** -0.5
    for h in range(L_HEADS):
        sl = slice(h * L_HDIM, (h + 1) * L_HDIM)
        ub = uc[:, sl].astype(BF16)
        q_ref[:, sl] = jnp.dot(ub, wq_ref[h], preferred_element_type=F32)
        k_ref[:, sl] = jnp.dot(ub, wk_ref[h], preferred_element_type=F32) * scale

    sm = sm_ref[...]
    dt = _softplus(sm + dtb_ref[...])
    dt_ref[...] = dt
    da_ref[...] = jnp.exp(dt * (-jnp.exp(alog_ref[...])))
    ig = pltpu.roll(sm + ib_ref[...], F_LANE - I_LANE, axis=1)
    logf = _log_sigmoid(sm + fb_ref[...])
    m_old = m_ref[...]
    m_new = jnp.maximum(logf + m_old, ig)
    mnew_ref[...] = m_new
    wr_ref[...] = jnp.exp(ig - m_new)
    keep_ref[...] = jnp.exp(logf + m_old - m_new)


def _sample_pre(pbig, psmall, conv_m, conv_l, m_lanes, cwm, cbm, cwl, cbl, wq, wk, dtb, alog, ib, fb):
    s = pbig.shape[0]
    f = lambda shape: jax.ShapeDtypeStruct(shape, F32)
    full2 = lambda shape: pl.BlockSpec(shape, lambda i: (0, 0))
    full3 = lambda shape: pl.BlockSpec(shape, lambda i: (0, 0, 0))
    state_row = lambda width, j: pl.BlockSpec((None, s, width), lambda i, j=j: (j, 0, 0))
    return pl.pallas_call(
        _sample_pre_kernel,
        grid=(1,),
        in_specs=[
            full2((s, M_CONV)),
            pl.BlockSpec((s, L_INNER), lambda i: (0, 6)),
            full2((s, LANES)),
            state_row(M_CONV, 0), state_row(M_CONV, 1), state_row(M_CONV, 2),
            state_row(L_INNER, 0), state_row(L_INNER, 1), state_row(L_INNER, 2),
            full2((s, LANES)),
            full2((CONV_W, M_CONV)), full2((1, M_CONV)),
            full2((CONV_W, L_INNER)), full2((1, L_INNER)),
            full3((L_HEADS, L_HDIM, L_HDIM)), full3((L_HEADS, L_HDIM, L_HDIM)),
            full2((1, LANES)), full2((1, LANES)), full2((1, LANES)), full2((1, LANES)),
        ],
        out_specs=[
            full2((s, M_INNER)), full2((s, M_INNER)), full2((s, L_INNER)), full2((s, L_INNER)),
            full2((s, LANES)), full2((s, LANES)), full2((s, LANES)), full2((s, LANES)), full2((s, LANES)),
            full3((CONV_W - 1, s, M_CONV)), full3((CONV_W - 1, s, L_INNER)),
        ],
        out_shape=[
            f((s, M_INNER)), f((s, M_INNER)), f((s, L_INNER)), f((s, L_INNER)),
            f((s, LANES)), f((s, LANES)), f((s, LANES)), f((s, LANES)), f((s, LANES)),
            f((CONV_W - 1, s, M_CONV)), f((CONV_W - 1, s, L_INNER)),
        ],
        compiler_params=_params("arbitrary"),
        name="sample_pre",
    )(pbig[:, 2 * PROJ_BLOCK:6 * PROJ_BLOCK], pbig, psmall, conv_m, conv_m, conv_m,
      conv_l, conv_l, conv_l, m_lanes, cwm, cbm, cwl, cbl, wq, wk, dtb, alog, ib, fb)


def _sample_post_kernel(y_ref, xc_ref, dvec_ref,
                        q_ref, k_ref, v_ref, n_ref, num_ref, wr_ref, keep_ref, mnew_ref,
                        ys_ref, hh_ref, nn_ref):
    ys_ref[...] = (y_ref[...] + dvec_ref[...] * xc_ref[...]).astype(BF16)

    wr = wr_ref[...]
    keep = keep_ref[...]
    floor = jnp.exp(-mnew_ref[...])
    for h in range(L_HEADS):
        ln = F_LANE + h
        sl = slice(h * L_HDIM, (h + 1) * L_HDIM)
        q = q_ref[:, sl]
        k = k_ref[:, sl]
        n_old = n_ref[:, sl]
        wrc = wr[:, ln:ln + 1]
        kpc = keep[:, ln:ln + 1]
        wgt = jnp.sum(q * k, axis=1, keepdims=True) * wrc
        num = wgt * v_ref[:, sl].astype(F32) + kpc * num_ref[:, sl]
        den = wgt + kpc * jnp.sum(q * n_old, axis=1, keepdims=True)
        hh_ref[:, sl] = (num / jnp.maximum(jnp.abs(den), floor[:, ln:ln + 1])).astype(BF16)
        nn_ref[:, sl] = kpc * n_old + wrc * k


def _sample_post(y, xc, pbig, dvec, q, k, n_rows, num, wr, keep, mnew):
    s = y.shape[0]
    full = lambda shape: pl.BlockSpec(shape, lambda i: (0, 0))
    blk = lambda width, j: pl.BlockSpec((s, width), lambda i, j=j: (0, j))
    return pl.pallas_call(
        _sample_post_kernel,
        grid=(1,),
        in_specs=[
            full((s, M_INNER)), full((s, M_INNER)), full((1, M_INNER)),
            full((s, L_INNER)), full((s, L_INNER)), blk(L_INNER, 7),
            full((s, L_INNER)), full((s, L_INNER)),
            full((s, LANES)), full((s, LANES)), full((s, LANES)),
        ],
        out_specs=[full((s, M_INNER)), full((s, L_INNER)), full((s, L_INNER))],
        out_shape=[
            jax.ShapeDtypeStruct((s, M_INNER), BF16),
            jax.ShapeDtypeStruct((s, L_INNER), BF16),
            jax.ShapeDtypeStruct((s, L_INNER), F32),
        ],
        compiler_params=_params("arbitrary"),
        name="sample_post",
    )(y, xc, dvec, q, k, pbig, n_rows, num, wr, keep, mnew)


def _lanes(vec, first_lane):
    n = vec.shape[0]
    return jnp.pad(vec.astype(F32), (first_lane, LANES - first_lane - n)).reshape(1, LANES)


def _spread_matrix(first_lane, n_heads, width):
    r = lax.broadcasted_iota(jnp.int32, (2 * LANES, n_heads * width), 0) % LANES
    c = lax.broadcasted_iota(jnp.int32, (2 * LANES, n_heads * width), 1)
    return (r - first_lane == c // width).astype(BF16)


def _shift_matrix(n):
    r = lax.broadcasted_iota(jnp.int32, ((CONV_W - 1) * n, 2 * n), 0)
    c = lax.broadcasted_iota(jnp.int32, ((CONV_W - 1) * n, 2 * n), 1)
    return (c == n + r % n - (r // n + 1)).astype(BF16)


def kernel(x_prompt, x_sample, state_mamba_conv, state_mamba_ssm, state_mlstm_conv, state_mlstm_C, state_mlstm_n, state_mlstm_m, w_in, mamba_conv_w, mamba_conv_b, mamba_dt_bias, mamba_A_log, mamba_D, mamba_norm_w, w_branch_a, mlstm_conv_w, mlstm_conv_b, mlstm_wq, mlstm_wk, mlstm_i_bias, mlstm_f_bias, mlstm_norm_w, w_branch_b, w_out, norm_mix_w, norm_mlp_w, w_up, w_down, final_norm_w):
    depth = w_in.shape[0]
    assert depth == 1
    batch, seq, _ = x_prompt.shape
    n_samp, dec_seq, _ = x_sample.shape
    assert dec_seq == 1 and seq % CHUNK == 0 and seq >= SUBLANES
    assert batch % SCAN_NB == 0
    l = 0

    assert w_in.shape[2] == W_END
    w_big, w_small = _wprep(jnp.transpose(w_in[l]))
    g_mix = norm_mix_w[l].reshape(1, D_MODEL)
    cwm = mamba_conv_w[l]
    cbm = mamba_conv_b[l].reshape(1, M_CONV)
    cwl = mlstm_conv_w[l]
    cbl = mlstm_conv_b[l].reshape(1, L_INNER)
    dtb = _lanes(mamba_dt_bias[l], DT_LANE)
    alog = _lanes(mamba_A_log[l], DT_LANE)
    ib = _lanes(mlstm_i_bias[l], I_LANE)
    fb = _lanes(mlstm_f_bias[l], F_LANE)
    dvec = jnp.repeat(mamba_D[l].astype(F32), M_HDIM).reshape(1, M_INNER)
    mnw = mamba_norm_w[l].reshape(1, M_INNER)
    lnw = mlstm_norm_w[l].reshape(1, L_INNER)
    wq = mlstm_wq[l].astype(BF16)
    wk = mlstm_wk[l].astype(BF16)
    wa = w_branch_a[l].astype(BF16)
    wb = w_branch_b[l].astype(BF16)
    wo = w_out[l].astype(BF16)
    wup = w_up[l].astype(BF16)
    wdn = w_down[l].astype(BF16)
    nmw = norm_mlp_w[l].reshape(1, D_MODEL)
    fw = final_norm_w.reshape(1, D_MODEL)
    e64 = _spread_matrix(DT_LANE, M_HEADS, M_HDIM)
    el = _spread_matrix(F_LANE, L_HEADS, L_HDIM)
    shift = _shift_matrix(CHUNK)

    xs = x_sample.reshape(n_samp, D_MODEL)
    sbig, ssmall = _inproj(xs, g_mix, w_big, w_small, tm=n_samp)
    m_lanes = jnp.pad(state_mlstm_m[l], ((0, 0), (F_LANE, LANES - F_LANE - L_HEADS)))
    (xc, bcc, q, k, dt, da, wr, keep, mnew, s_conv_m, s_conv_l) = _sample_pre(
        sbig, ssmall, jnp.transpose(state_mamba_conv[l], (1, 0, 2)),
        jnp.transpose(state_mlstm_conv[l], (1, 0, 2)), m_lanes,
        cwm, cbm, cwl, cbl, wq, wk, dtb, alog, ib, fb)
    samples = (
        dt[:, :M_HEADS], da[:, :M_HEADS],
        wr[:, F_LANE:F_LANE + L_HEADS], keep[:, F_LANE:F_LANE + L_HEADS], xc,
        bcc[:, :M_GROUPS * M_STATE].reshape(n_samp, M_GROUPS, M_STATE),
        bcc[:, M_GROUPS * M_STATE:].reshape(n_samp, M_GROUPS, M_STATE),
        q.reshape(n_samp * L_HEADS, L_HDIM), k.reshape(n_samp * L_HEADS, L_HDIM),
        sbig[:, 7 * PROJ_BLOCK:8 * PROJ_BLOCK].astype(F32).reshape(n_samp * L_HEADS, L_HDIM),
        state_mamba_ssm[l].reshape(n_samp, M_PAIRS, LANES, M_STATE), state_mlstm_C[l])

    xp = x_prompt.reshape(batch * seq, D_MODEL)
    pbig, psmall = _inproj(xp, g_mix, w_big, w_small, tm=2048)
    ya, hb, p_ssm, p_c, p_n, p_m, s_ssm, s_c, y_rows, num_rows = _scan_prompt(
        pbig, psmall, cwm[:, :M_INNER], cbm[:, :M_INNER], cwm[:, M_INNER:], cbm[:, M_INNER:],
        cwl, cbl, dtb, alog, dvec, wq, wk, ib, fb, e64, el, shift, batch, seq, samples)
    y_prompt = _tail(ya.reshape(batch * seq, M_INNER), hb.reshape(batch * seq, L_INNER), pbig, xp,
                     mnw, lnw, wa, wb, wo, nmw, wup, wdn, fw, tm=512)
    p3 = pbig.reshape(batch, seq, BIG_WIDTH)
    p_conv_m = p3[:, seq - (CONV_W - 1):, 2 * PROJ_BLOCK:6 * PROJ_BLOCK].astype(F32)
    p_conv_l = p3[:, seq - (CONV_W - 1):, 6 * PROJ_BLOCK:7 * PROJ_BLOCK].astype(F32)
    p_ssm = p_ssm.reshape(batch, M_HEADS, M_HDIM, M_STATE)
    p_m = p_m[:, 0, F_LANE:F_LANE + L_HEADS]

    ya_s, hb_s, s_n = _sample_post(
        y_rows.reshape(n_samp, M_INNER), xc, sbig, dvec, q, k,
        state_mlstm_n[l].reshape(n_samp, L_INNER), num_rows.reshape(n_samp, L_INNER),
        wr, keep, mnew)
    y_sample = _tail(ya_s, hb_s, sbig, xs, mnw, lnw, wa, wb, wo, nmw, wup, wdn, fw, tm=n_samp)

    lead = lambda a: a[None]
    return (
        y_prompt.reshape(batch, seq, D_MODEL),
        y_sample.reshape(n_samp, 1, D_MODEL),
        lead(p_conv_m), lead(p_ssm), lead(p_conv_l), lead(p_c), lead(p_n), lead(p_m),
        lead(jnp.transpose(s_conv_m, (1, 0, 2))),
        lead(s_ssm.reshape(n_samp, M_HEADS, M_HDIM, M_STATE)),
        lead(jnp.transpose(s_conv_l, (1, 0, 2))),
        lead(s_c), lead(s_n.reshape(n_samp, L_HEADS, L_HDIM)), lead(mnew[:, F_LANE:F_LANE + L_HEADS]),
    )
```

```python
import jax
import jax.numpy as jnp
from jax import lax
from jax.experimental import pallas as pl
from jax.experimental.pallas import tpu as pltpu

F32 = jnp.float32
BF16 = jnp.bfloat16

D_MODEL = 1024
M_INNER = 2048
M_HEADS = 32
M_HDIM = 64
M_GROUPS = 8
M_PAIRS = M_HEADS // 2
M_STATE = 128
M_CONV = 4096
L_INNER = 1024
L_HEADS = 8
L_HDIM = 128
D_FF = 4096
CONV_W = 4
CHUNK = 128
EPS = 1e-6

LANES = 128
SUBLANES = 8
PROJ_BLOCK = 1024
N_PROJ_BLOCKS = 11
BIG_WIDTH = PROJ_BLOCK * N_PROJ_BLOCKS
DT_LANE = 0
I_LANE = 32
F_LANE = 40
VMEM_LIMIT = 56 * 1024 * 1024

LOG2_E = 1.4426950408889634
NT_DIMS = (((1,), (1,)), ((), ()))
TN_DIMS = (((0,), (0,)), ((), ()))


def _params(*sem):
    return pltpu.CompilerParams(dimension_semantics=sem, vmem_limit_bytes=VMEM_LIMIT)


def _sigmoid(x):
    return 0.5 * jnp.tanh(0.5 * x) + 0.5


def _silu(x):
    h = 0.5 * x
    return h + h * jnp.tanh(h)


def _log1p_exp_neg_abs(x):
    e = jnp.exp(-jnp.abs(x))
    u = 1.0 + e
    return jnp.where(u == 1.0, e, jnp.log(u) * (e / (u - 1.0)))


def _softplus(x):
    return jnp.maximum(x, 0.0) + _log1p_exp_neg_abs(x)


def _log_sigmoid(x):
    return jnp.minimum(x, 0.0) - _log1p_exp_neg_abs(x)


def _rms(x, w):
    return x * lax.rsqrt(jnp.mean(x * x, axis=-1, keepdims=True) + EPS) * w


def _lane_iota(shape):
    return lax.broadcasted_iota(jnp.int32, shape, len(shape) - 1)


def _tri(n):
    r = lax.broadcasted_iota(jnp.int32, (n, n), 0)
    c = lax.broadcasted_iota(jnp.int32, (n, n), 1)
    return r >= c


def _split3(a):
    hi = a.astype(BF16)
    r1 = a - hi.astype(F32)
    mid = r1.astype(BF16)
    lo = (r1 - mid.astype(F32)).astype(BF16)
    return hi, mid, lo


def _cumsum_rows(causal, a):
    tri01 = causal.astype(F32).astype(BF16)
    return jnp.dot(jnp.concatenate([tri01] * 3, axis=1), jnp.concatenate(_split3(a), axis=0),
                   preferred_element_type=F32)


def _spread_many(arrs, e2_ref):
    n = arrs[0].shape[0]
    pieces = [jnp.concatenate(_split3(a)[:2], axis=1) for a in arrs]
    out = jnp.dot(jnp.concatenate(pieces, axis=0), e2_ref[...], preferred_element_type=F32)
    return [out[i * n:(i + 1) * n] for i in range(len(arrs))]


def _spread(a, e2_ref):
    return _spread_many((a,), e2_ref)[0]


CONV_COLS = 512
BF16_ROWS = 16


def _conv_silu(x_ref, win_ref, shift_ref, w_ref, b_ref, out_ref):
    n, width = x_ref.shape
    win_ref[n:2 * n, :] = x_ref[...]
    for c0 in range(0, width, CONV_COLS):
        cs = slice(c0, c0 + CONV_COLS)
        sh = jnp.dot(shift_ref[...], win_ref[:, cs], preferred_element_type=F32)
        y = b_ref[:, cs] + w_ref[CONV_W - 1:CONV_W, cs] * x_ref[:, cs].astype(F32)
        for back in range(1, CONV_W):
            y = y + w_ref[CONV_W - 1 - back:CONV_W - back, cs] * sh[(back - 1) * n:back * n]
        out_ref[:, cs] = _silu(y).astype(out_ref.dtype)
    win_ref[n - BF16_ROWS:n, :] = x_ref[n - BF16_ROWS:n, :]


W_DT0 = M_INNER + M_CONV
W_U0 = W_DT0 + M_HEADS
W_I0 = W_U0 + 3 * L_INNER
W_G0 = W_I0 + 2 * L_HEADS
W_END = W_G0 + 2 * D_MODEL
W_FIRST_UVO = W_DT0 // PROJ_BLOCK
W_FIRST_GATE = W_FIRST_UVO + 3 * L_INNER // PROJ_BLOCK
W_SHIFT_UVO = W_U0 - W_DT0
W_SHIFT_GATE = W_G0 - (W_DT0 + 3 * L_INNER)
W_NEXT_ROWS = 64
assert W_DT0 % PROJ_BLOCK == 0 and DT_LANE == 0
assert (W_I0 - I_LANE) == W_FIRST_GATE * PROJ_BLOCK and F_LANE == I_LANE + L_HEADS
assert max(W_SHIFT_UVO, W_SHIFT_GATE) <= W_NEXT_ROWS and PROJ_BLOCK % W_NEXT_ROWS == 0
assert W_SHIFT_UVO % BF16_ROWS == 0 and W_SHIFT_GATE % BF16_ROWS == 0
assert M_HEADS % BF16_ROWS == 0 and I_LANE % BF16_ROWS == 0 and (2 * L_HEADS) % BF16_ROWS == 0


def _wprep_kernel(a_ref, b_ref, big_ref, small_ref):
    j = pl.program_id(0)

    def emit(shift):
        if shift == 0:
            big_ref[...] = a_ref[...].astype(BF16)
        else:
            big_ref[0:PROJ_BLOCK - shift, :] = a_ref[shift:PROJ_BLOCK, :].astype(BF16)
            big_ref[PROJ_BLOCK - shift:PROJ_BLOCK, :] = b_ref[0:shift, :].astype(BF16)

    @pl.when(j == 0)
    def _():
        small_ref[...] = jnp.zeros_like(small_ref)

    @pl.when(j < W_FIRST_UVO)
    def _():
        emit(0)

    @pl.when((j >= W_FIRST_UVO) & (j < W_FIRST_GATE))
    def _():
        emit(W_SHIFT_UVO)

    @pl.when(j >= W_FIRST_GATE)
    def _():
        emit(W_SHIFT_GATE)

    @pl.when(j == W_FIRST_UVO)
    def _():
        small_ref[DT_LANE:DT_LANE + M_HEADS, :] = a_ref[DT_LANE:DT_LANE + M_HEADS, :].astype(BF16)

    @pl.when(j == W_FIRST_GATE)
    def _():
        small_ref[I_LANE:I_LANE + 2 * L_HEADS, :] = a_ref[I_LANE:I_LANE + 2 * L_HEADS, :].astype(BF16)


def _wprep(w_t):
    return pl.pallas_call(
        _wprep_kernel,
        grid=(N_PROJ_BLOCKS,),
        in_specs=[
            pl.BlockSpec((PROJ_BLOCK, D_MODEL), lambda j: (j, 0)),
            pl.BlockSpec((W_NEXT_ROWS, D_MODEL), lambda j: ((j + 1) * (PROJ_BLOCK // W_NEXT_ROWS), 0)),
        ],
        out_specs=[
            pl.BlockSpec((PROJ_BLOCK, D_MODEL), lambda j: (j, 0)),
            pl.BlockSpec((LANES, D_MODEL), lambda j: (0, 0)),
        ],
        out_shape=[
            jax.ShapeDtypeStruct((BIG_WIDTH, D_MODEL), BF16),
            jax.ShapeDtypeStruct((LANES, D_MODEL), BF16),
        ],
        compiler_params=_params("arbitrary"),
        name="w_prep",
    )(w_t, w_t)


def _inproj_kernel(x_ref, xs_ref, g_ref, w_ref, ws_ref, o_ref, os_ref, so_ref, sos_ref, xn_ref, xsn_ref):
    i = pl.program_id(0)
    j = pl.program_id(1)

    @pl.when(j == 0)
    def _():
        xn_ref[...] = _rms(x_ref[...], g_ref[...]).astype(BF16)
        os_ref[...] = lax.dot_general(xn_ref[...], ws_ref[...], NT_DIMS, preferred_element_type=F32)

    @pl.when((i == 0) & (j == 0))
    def _():
        xsn_ref[...] = _rms(xs_ref[...], g_ref[...]).astype(BF16)
        sos_ref[...] = lax.dot_general(xsn_ref[...], ws_ref[...], NT_DIMS, preferred_element_type=F32)

    @pl.when(i == 0)
    def _():
        so_ref[...] = lax.dot_general(
            xsn_ref[...], w_ref[...], NT_DIMS, preferred_element_type=F32).astype(BF16)

    o_ref[...] = lax.dot_general(
        xn_ref[...], w_ref[...], NT_DIMS, preferred_element_type=F32).astype(BF16)


def _inproj(x, xs, g, w_big, w_small, tm):
    m = x.shape[0]
    s = xs.shape[0]
    tm = min(tm, m)
    last = N_PROJ_BLOCKS - 1
    sample_block = lambda i, j: (0, jnp.where(i == 0, j, last))
    return pl.pallas_call(
        _inproj_kernel,
        grid=(m // tm, N_PROJ_BLOCKS),
        in_specs=[
            pl.BlockSpec((tm, D_MODEL), lambda i, j: (i, 0)),
            pl.BlockSpec((s, D_MODEL), lambda i, j: (0, 0)),
            pl.BlockSpec((1, D_MODEL), lambda i, j: (0, 0)),
            pl.BlockSpec((PROJ_BLOCK, D_MODEL), lambda i, j: (j, 0)),
            pl.BlockSpec((LANES, D_MODEL), lambda i, j: (0, 0)),
        ],
        out_specs=[
            pl.BlockSpec((tm, PROJ_BLOCK), lambda i, j: (i, j)),
            pl.BlockSpec((tm, LANES), lambda i, j: (i, 0)),
            pl.BlockSpec((s, PROJ_BLOCK), sample_block),
            pl.BlockSpec((s, LANES), lambda i, j: (0, 0)),
        ],
        out_shape=[
            jax.ShapeDtypeStruct((m, BIG_WIDTH), BF16),
            jax.ShapeDtypeStruct((m, LANES), F32),
            jax.ShapeDtypeStruct((s, BIG_WIDTH), BF16),
            jax.ShapeDtypeStruct((s, LANES), F32),
        ],
        scratch_shapes=[pltpu.VMEM((tm, D_MODEL), BF16), pltpu.VMEM((s, D_MODEL), BF16)],
        compiler_params=_params("arbitrary", "arbitrary"),
        name="in_proj",
    )(x, xs, g, w_big, w_small)


SCAN_NB = 2


def _ssd_chunk(x_ref, bc_ref, sm, cwx_ref, cbx_ref, cwbc_ref, cbbc_ref, dtb_ref, alog_ref, dvec_ref,
               e64_ref, shift_ref, y_ref, h_ref, winx_ref, winbc_ref, xc_scr, bcc_scr):
    n = x_ref.shape[0]
    _conv_silu(x_ref, winx_ref, shift_ref, cwx_ref, cbx_ref, xc_scr)
    _conv_silu(bc_ref, winbc_ref, shift_ref, cwbc_ref, cbbc_ref, bcc_scr)
    xc = xc_scr[...]
    bcc = bcc_scr[...]

    lane = _lane_iota((n, LANES))
    dt = _softplus(sm + dtb_ref[...])
    da = jnp.where(lane < M_HEADS, dt * (-LOG2_E * jnp.exp(alog_ref[...])), 0.0)
    causal = _tri(n)
    acum = _cumsum_rows(causal, da)
    acum_t = acum.T
    last = acum[n - 1:n, :]
    exp_last = jnp.exp2(last)
    dt_x, wr_x, ea_x = _spread_many((dt, jnp.exp2(last - acum), jnp.exp2(acum)), e64_ref)

    xdt = xc * dt_x
    xw = (xdt * wr_x).astype(BF16)
    xb = xdt.astype(BF16)
    low_half = jnp.bitwise_and(_lane_iota((n, M_INNER)), LANES - 1) < M_HDIM
    zero = jnp.zeros((), BF16)
    x_lo = jnp.where(low_half, xb, zero)
    x_hi = jnp.where(low_half, zero, xb)
    first_rows = lax.broadcasted_iota(jnp.int32, (LANES, LANES), 0) < M_HDIM

    def weights(cb, h):
        seg = acum[:, h:h + 1] - acum_t[h:h + 1, :]
        return (cb * jnp.exp2(jnp.where(causal, seg, -jnp.inf))).astype(BF16)

    pairs_per_group = M_PAIRS // M_GROUPS
    for g in range(M_GROUPS):
        bg = bcc[:, g * M_STATE:(g + 1) * M_STATE]
        cg = bcc[:, (M_GROUPS + g) * M_STATE:(M_GROUPS + g + 1) * M_STATE]
        cb = lax.dot_general(cg, bg, NT_DIMS, preferred_element_type=F32)
        for pp in range(pairs_per_group):
            hp = g * pairs_per_group + pp
            h0, h1 = 2 * hp, 2 * hp + 1
            sl = slice(hp * LANES, (hp + 1) * LANES)
            hs = h_ref[hp]
            y = jnp.dot(weights(cb, h0), x_lo[:, sl], preferred_element_type=F32)
            y = y + jnp.dot(weights(cb, h1), x_hi[:, sl], preferred_element_type=F32)
            ys = lax.dot_general(cg, hs.astype(BF16), NT_DIMS, preferred_element_type=F32)
            y_ref[:, sl] = (y + ea_x[:, sl] * ys + dvec_ref[:, sl] * xc[:, sl]).astype(BF16)
            el = jnp.where(first_rows, exp_last[:, h0:h0 + 1], exp_last[:, h1:h1 + 1])
            h_ref[hp] = el * hs + lax.dot_general(xw[:, sl], bg, TN_DIMS, preferred_element_type=F32)


def _mlstm_chunks(seqs, wq_ref, wk_ref, ib_ref, fb_ref, el_ref):
    n = seqs[0][0].shape[0]
    causal = _tri(n)
    lane = _lane_iota((n, LANES))
    gate_lanes = (lane >= F_LANE) & (lane < F_LANE + L_HEADS)
    scale = L_HDIM ** -0.5
    lanes_of = [slice(h * L_HDIM, (h + 1) * L_HDIM) for h in range(L_HEADS)]

    gates = []
    for uc, v_ref, sm, hh_ref, c_ref, n_ref, m_ref in seqs:
        ig = pltpu.roll(sm + ib_ref[...], F_LANE - I_LANE, axis=1)
        logf = jnp.where(gate_lanes, _log_sigmoid(sm + fb_ref[...]), 0.0)
        bcum = _cumsum_rows(causal, logf)
        m_old = m_ref[...]
        bl = bcum[n - 1:n, :]
        src = bl - bcum + ig
        m_new = jnp.maximum(bl + m_old, jnp.max(src, axis=0, keepdims=True))
        gates.append(dict(
            bcum=bcum, bcum_t=bcum.T, ig_t=ig.T, m_old=m_old, m_new=m_new,
            keep=jnp.exp(bl + m_old - m_new),
            wr_x=_spread(jnp.exp(src - m_new), el_ref)))

    work = [(i, h) for i in range(len(seqs)) for h in range(L_HEADS)]
    c_olds = {(i, h): seqs[i][4][h] for i, h in work}
    n_olds = {(i, h): seqs[i][5][h:h + 1, :] for i, h in work}
    qs, ks, qbs, kbs = {}, {}, {}, {}
    for i, h in work:
        ub = seqs[i][0][:, lanes_of[h]]
        qs[i, h] = jnp.dot(ub, wq_ref[h], preferred_element_type=F32)
        ks[i, h] = jnp.dot(ub, wk_ref[h], preferred_element_type=F32) * scale
    logds, inters, rowmax = {}, {}, {}
    for i, h in work:
        g = gates[i]
        ln = F_LANE + h
        bcol = jnp.broadcast_to(g["bcum"][:, ln:ln + 1], (n, n))
        logd = jnp.where(causal, bcol - g["bcum_t"][ln:ln + 1, :] + g["ig_t"][ln:ln + 1, :], -jnp.inf)
        logds[i, h] = logd
        inters[i, h] = bcol + g["m_old"][:, ln:ln + 1]
        rowmax[i, h] = jnp.max(logd, axis=1, keepdims=True)
    ss, scs, floors = {}, {}, {}
    for i, h in work:
        qbs[i, h] = qs[i, h].astype(BF16)
        kbs[i, h] = ks[i, h].astype(BF16)
        m_s = jnp.maximum(inters[i, h], rowmax[i, h])
        dm = jnp.exp(logds[i, h] - m_s)
        scs[i, h] = jnp.exp(inters[i, h] - m_s)
        floors[i, h] = jnp.exp(-m_s)
        ss[i, h] = lax.dot_general(qbs[i, h], kbs[i, h], NT_DIMS, preferred_element_type=F32) * dm
    nums, dens = {}, {}
    for i, h in work:
        vb = seqs[i][1][:, lanes_of[h]]
        num = jnp.dot(ss[i, h].astype(BF16), vb, preferred_element_type=F32)
        nums[i, h] = num + scs[i, h] * jnp.dot(
            qbs[i, h], c_olds[i, h].astype(BF16), preferred_element_type=F32)
        dens[i, h] = (jnp.sum(ss[i, h], axis=1, keepdims=True)
                      + scs[i, h] * jnp.sum(qs[i, h] * n_olds[i, h], axis=1, keepdims=True))
    for i, h in work:
        hh = nums[i, h] / jnp.maximum(jnp.abs(dens[i, h]), floors[i, h])
        seqs[i][3][:, lanes_of[h]] = hh.astype(BF16)
    for i, h in work:
        g = gates[i]
        ln = F_LANE + h
        kw = ks[i, h] * g["wr_x"][:, lanes_of[h]]
        kp = g["keep"][:, ln:ln + 1]
        seqs[i][4][h] = kp * c_olds[i, h] + lax.dot_general(
            kw.astype(BF16), seqs[i][1][:, lanes_of[h]], TN_DIMS, preferred_element_type=F32)
        seqs[i][5][h:h + 1, :] = kp * n_olds[i, h] + jnp.sum(kw, axis=0, keepdims=True)
    for i in range(len(seqs)):
        seqs[i][6][...] = gates[i]["m_new"]


def _diag_rows(row, diag):
    return jnp.where(diag, jnp.broadcast_to(row, diag.shape), 0.0).astype(BF16)


def _state_spreads(x_ref, q_ref, k_ref, xs_scr, qs_scr, ks_scr):
    ns = x_ref.shape[0]
    rows = lax.broadcasted_iota(jnp.int32, (LANES, LANES), 0)
    diag = rows == _lane_iota((LANES, LANES))
    ones = jnp.ones((LANES, LANES), BF16)
    ones2 = jnp.ones((2 * LANES, LANES), BF16)
    for s in range(ns):
        for hp in range(M_PAIRS):
            xrow = x_ref[s:s + 1, hp * LANES:(hp + 1) * LANES]
            xs_scr[s * M_PAIRS + hp] = jnp.dot(_diag_rows(xrow, diag), ones, preferred_element_type=F32)
    for r in range(ns * L_HEADS):
        for src_ref, dst_scr in ((q_ref, qs_scr), (k_ref, ks_scr)):
            row = src_ref[r:r + 1, :]
            hi = row.astype(BF16).astype(F32)
            lhs = jnp.concatenate([_diag_rows(hi, diag), _diag_rows(row - hi, diag)], axis=1)
            dst_scr[r] = jnp.dot(lhs, ones2, preferred_element_type=F32)


def _state_updates(base, dt_ref, da_ref, wr_ref, keep_ref, b_ref, c_ref, v_ref, h_ref, cst_ref,
                   hn_ref, cn_ref, y_ref, num_ref, xs_scr, qs_scr, ks_scr):
    ns = b_ref.shape[0]
    rows = lax.broadcasted_iota(jnp.int32, (LANES, LANES), 0)
    lanes = _lane_iota((LANES, LANES))
    first_rows = rows < M_HDIM
    ones = jnp.ones((LANES, LANES), BF16)
    pairs_per_group = M_PAIRS // M_GROUPS
    acc = jnp.zeros((LANES, LANES), F32)
    for s in range(ns):
        for hp in range(M_PAIRS):
            g = hp // pairs_per_group
            h0, h1 = 2 * hp, 2 * hp + 1
            brow = b_ref[s, g:g + 1, :]
            crow = c_ref[s, g:g + 1, :]
            dav = jnp.where(first_rows, da_ref[base + s, h0], da_ref[base + s, h1])
            dtv = jnp.where(first_rows, dt_ref[base + s, h0], dt_ref[base + s, h1])
            hn = dav * h_ref[s, hp] + (dtv * xs_scr[s * M_PAIRS + hp]) * brow
            hn_ref[s, hp] = hn
            ysum = jnp.dot((hn * crow).astype(BF16), ones, preferred_element_type=F32)
            acc = jnp.where(lanes == s * M_PAIRS + hp, ysum, acc)
    y_ref[...] = acc.T[0:ns * M_PAIRS, :]
    for s in range(ns):
        for h in range(L_HEADS):
            r = s * L_HEADS + h
            c_old = cst_ref[s, h]
            num_ref[r:r + 1, :] = jnp.sum(qs_scr[r] * c_old, axis=0, keepdims=True)
            cn_ref[s, h] = (keep_ref[base + s, h] * c_old
                            + (ks_scr[r] * wr_ref[base + s, h]) * v_ref[r:r + 1, :])


def _scan_kernel(dt_ref, da_ref, wr_ref, keep_ref, sx_ref, sb_ref, sc_ref, sq_ref, sk_ref, sv_ref,
                 sh_ref, scst_ref, x_ref, bc_ref, u_ref, v_ref, sm_ref,
                 cwx_ref, cbx_ref, cwbc_ref, cbbc_ref, cwl_ref, cbl_ref, dtb_ref, alog_ref, dvec_ref,
                 wq_ref, wk_ref, ib_ref, fb_ref, e64_ref, el_ref, shift_ref,
                 y_ref, hh_ref, h_ref, c_ref, n_ref, m_ref, shn_ref, scn_ref, sy_ref, snum_ref,
                 winx_ref, winbc_ref, winu_ref, xc_scr, bcc_scr, uc_scr, xs_scr, qs_scr, ks_scr):
    n = x_ref.shape[1]
    step = pl.program_id(0) * pl.num_programs(1) + pl.program_id(1)
    _state_spreads(sx_ref, sq_ref, sk_ref, xs_scr, qs_scr, ks_scr)

    @pl.when(pl.program_id(1) == 0)
    def _():
        h_ref[...] = jnp.zeros_like(h_ref)
        c_ref[...] = jnp.zeros_like(c_ref)
        n_ref[...] = jnp.zeros_like(n_ref)
        m_ref[...] = jnp.zeros_like(m_ref)
        winx_ref[:, 0:n, :] = jnp.zeros((SCAN_NB, n, M_INNER), BF16)
        winbc_ref[:, 0:n, :] = jnp.zeros((SCAN_NB, n, M_INNER), BF16)
        winu_ref[:, 0:n, :] = jnp.zeros((SCAN_NB, n, L_INNER), BF16)

    seqs = []
    for i in range(SCAN_NB):
        _ssd_chunk(x_ref.at[i], bc_ref.at[i], sm_ref[i], cwx_ref, cbx_ref, cwbc_ref, cbbc_ref,
                   dtb_ref, alog_ref, dvec_ref, e64_ref, shift_ref, y_ref.at[i], h_ref.at[i],
                   winx_ref.at[i], winbc_ref.at[i], xc_scr, bcc_scr)
        _conv_silu(u_ref.at[i], winu_ref.at[i], shift_ref, cwl_ref, cbl_ref, uc_scr.at[i])
        seqs.append((uc_scr.at[i], v_ref.at[i], sm_ref[i], hh_ref.at[i], c_ref.at[i], n_ref.at[i],
                     m_ref.at[i]))
    _mlstm_chunks(seqs, wq_ref, wk_ref, ib_ref, fb_ref, el_ref)
    _state_updates(step * sb_ref.shape[0], dt_ref, da_ref, wr_ref, keep_ref, sb_ref, sc_ref, sv_ref,
                   sh_ref, scst_ref, shn_ref, scn_ref, sy_ref, snum_ref, xs_scr, qs_scr, ks_scr)


def _scan_prompt(pbig, psmall, cwx, cbx, cwbc, cbbc, cwl, cbl, dtb, alog, dvec, wq, wk, ib, fb,
                 e64, el, shift, batch, seq, samples):
    p3 = pbig.reshape(batch, seq, BIG_WIDTH)
    s3 = psmall.reshape(batch, seq, LANES)
    nb = SCAN_NB
    n_chunks = seq // CHUNK
    n_steps = (batch // nb) * n_chunks
    s_dt, s_da, s_wr, s_keep, s_x, s_b, s_c, s_q, s_k, s_v, s_h, s_cst = samples
    n_samp = s_h.shape[0]
    ns = n_samp // n_steps
    assert ns * n_steps == n_samp and ns * M_PAIRS <= LANES and (ns * L_HEADS) % SUBLANES == 0
    smem = pl.BlockSpec(memory_space=pltpu.SMEM)
    step = lambda b, c: b * n_chunks + c
    srows = lambda n_rows: pl.BlockSpec((n_rows, LANES), lambda b, c: (step(b, c), 0))
    sgrp = pl.BlockSpec((ns, M_GROUPS, M_STATE), lambda b, c: (step(b, c), 0, 0))
    sstate = lambda d1: pl.BlockSpec((ns, d1, LANES, LANES), lambda b, c: (step(b, c), 0, 0, 0))
    const2 = lambda b, c: (0, 0)
    const3 = lambda b, c: (0, 0, 0)
    rows = lambda width, col: pl.BlockSpec((nb, CHUNK, width), lambda b, c, col=col: (b, c, col))
    state4 = lambda d1, d2, d3: pl.BlockSpec((nb, d1, d2, d3), lambda b, c: (b, 0, 0, 0))
    return pl.pallas_call(
        _scan_kernel,
        grid=(batch // nb, seq // CHUNK),
        in_specs=[
            smem, smem, smem, smem,
            pl.BlockSpec((None, ns, M_INNER), lambda b, c: (step(b, c), 0, 0)),
            sgrp, sgrp, srows(ns * L_HEADS), srows(ns * L_HEADS), srows(ns * L_HEADS),
            sstate(M_PAIRS), sstate(L_HEADS),
            rows(M_INNER, 1),
            rows(M_INNER, 2),
            rows(L_INNER, 6),
            rows(L_INNER, 7),
            rows(LANES, 0),
            pl.BlockSpec((CONV_W, M_INNER), const2), pl.BlockSpec((1, M_INNER), const2),
            pl.BlockSpec((CONV_W, M_INNER), const2), pl.BlockSpec((1, M_INNER), const2),
            pl.BlockSpec((CONV_W, L_INNER), const2), pl.BlockSpec((1, L_INNER), const2),
            pl.BlockSpec((1, LANES), const2), pl.BlockSpec((1, LANES), const2),
            pl.BlockSpec((1, M_INNER), const2),
            pl.BlockSpec((L_HEADS, L_HDIM, L_HDIM), const3),
            pl.BlockSpec((L_HEADS, L_HDIM, L_HDIM), const3),
            pl.BlockSpec((1, LANES), const2), pl.BlockSpec((1, LANES), const2),
            pl.BlockSpec((2 * LANES, M_INNER), const2),
            pl.BlockSpec((2 * LANES, L_INNER), const2),
            pl.BlockSpec(((CONV_W - 1) * CHUNK, 2 * CHUNK), const2),
        ],
        out_specs=[
            rows(M_INNER, 0),
            rows(L_INNER, 0),
            state4(M_PAIRS, LANES, M_STATE),
            state4(L_HEADS, L_HDIM, L_HDIM),
            pl.BlockSpec((nb, L_HEADS, L_HDIM), lambda b, c: (b, 0, 0)),
            pl.BlockSpec((nb, 1, LANES), lambda b, c: (b, 0, 0)),
            sstate(M_PAIRS), sstate(L_HEADS), srows(ns * M_PAIRS), srows(ns * L_HEADS),
        ],
        out_shape=[
            jax.ShapeDtypeStruct((batch, seq, M_INNER), BF16),
            jax.ShapeDtypeStruct((batch, seq, L_INNER), BF16),
            jax.ShapeDtypeStruct((batch, M_PAIRS, LANES, M_STATE), F32),
            jax.ShapeDtypeStruct((batch, L_HEADS, L_HDIM, L_HDIM), F32),
            jax.ShapeDtypeStruct((batch, L_HEADS, L_HDIM), F32),
            jax.ShapeDtypeStruct((batch, 1, LANES), F32),
            jax.ShapeDtypeStruct(s_h.shape, F32),
            jax.ShapeDtypeStruct(s_cst.shape, F32),
            jax.ShapeDtypeStruct((n_samp * M_PAIRS, LANES), F32),
            jax.ShapeDtypeStruct((n_samp * L_HEADS, LANES), F32),
        ],
        scratch_shapes=[
            pltpu.VMEM((nb, 2 * CHUNK, M_INNER), BF16),
            pltpu.VMEM((nb, 2 * CHUNK, M_INNER), BF16),
            pltpu.VMEM((nb, 2 * CHUNK, L_INNER), BF16),
            pltpu.VMEM((CHUNK, M_INNER), F32),
            pltpu.VMEM((CHUNK, M_INNER), BF16),
            pltpu.VMEM((nb, CHUNK, L_INNER), BF16),
            pltpu.VMEM((ns * M_PAIRS, LANES, LANES), F32),
            pltpu.VMEM((ns * L_HEADS, LANES, LANES), F32),
            pltpu.VMEM((ns * L_HEADS, LANES, LANES), F32),
        ],
        compiler_params=_params("parallel", "arbitrary"),
        name="scan_prompt",
    )(s_dt, s_da, s_wr, s_keep, s_x.reshape(n_steps, ns, M_INNER), s_b, s_c, s_q, s_k, s_v, s_h, s_cst,
      p3, p3, p3, p3, s3, cwx, cbx, cwbc, cbbc, cwl, cbl, dtb, alog, dvec, wq, wk, ib, fb,
      e64, el, shift)


def _tail_kernel(ys_ref, hh_ref, z_ref, o_ref, ga_ref, gb_ref, x_ref, mnw_ref, lnw_ref,
                 wa_ref, wb_ref, wo_ref, nmw_ref, wup_ref, wdn_ref, fw_ref, y_ref):
    gw = M_INNER // M_GROUPS
    a = None
    for g in range(M_GROUPS):
        sl = slice(g * gw, (g + 1) * gw)
        yg = ys_ref[:, sl].astype(F32) * _silu(z_ref[:, sl].astype(F32))
        part = jnp.dot(_rms(yg, mnw_ref[:, sl]).astype(BF16), wa_ref[sl, :], preferred_element_type=F32)
        a = part if a is None else a + part
    b = None
    for hp in range(L_HEADS // 2):
        pieces = []
        for h in (2 * hp, 2 * hp + 1):
            sl = slice(h * L_HDIM, (h + 1) * L_HDIM)
            gate = _sigmoid(o_ref[:, sl].astype(F32))
            pieces.append((gate * _rms(hh_ref[:, sl].astype(F32), lnw_ref[:, sl])).astype(BF16))
        rows = slice(2 * hp * L_HDIM, (2 * hp + 2) * L_HDIM)
        part = jnp.dot(jnp.concatenate(pieces, axis=1), wb_ref[rows, :], preferred_element_type=F32)
        b = part if b is None else b + part
    t = _sigmoid(ga_ref[...].astype(F32)) * a + _sigmoid(gb_ref[...].astype(F32)) * b
    x1 = x_ref[...] + jnp.dot(t.astype(BF16), wo_ref[...], preferred_element_type=F32)
    hn = _rms(x1, nmw_ref[...]).astype(BF16)
    acc = x1
    for c in range(D_FF // PROJ_BLOCK):
        sl = slice(c * PROJ_BLOCK, (c + 1) * PROJ_BLOCK)
        up = jnp.dot(hn, wup_ref[:, sl], preferred_element_type=F32)
        act = jnp.square(jnp.maximum(up, 0.0)).astype(BF16)
        acc = acc + jnp.dot(act, wdn_ref[sl, :], preferred_element_type=F32)
    y_ref[...] = _rms(acc, fw_ref[...])


def _tail(ys, hh, pbig, x, mnw, lnw, wa, wb, wo, nmw, wup, wdn, fw, tm):
    m = x.shape[0]
    tm = min(tm, m)
    rows = lambda i: (i, 0)
    const = lambda i: (0, 0)

    def resident(shape):
        return pl.BlockSpec(shape, const, pipeline_mode=pl.Buffered(1))

    return pl.pallas_call(
        _tail_kernel,
        grid=(m // tm,),
        in_specs=[
            pl.BlockSpec((tm, M_INNER), rows),
            pl.BlockSpec((tm, L_INNER), rows),
            pl.BlockSpec((tm, M_INNER), rows),
            pl.BlockSpec((tm, PROJ_BLOCK), lambda i: (i, 8)),
            pl.BlockSpec((tm, PROJ_BLOCK), lambda i: (i, 9)),
            pl.BlockSpec((tm, PROJ_BLOCK), lambda i: (i, 10)),
            pl.BlockSpec((tm, D_MODEL), rows),
            resident((1, M_INNER)),
            resident((1, L_INNER)),
            resident((M_INNER, D_MODEL)),
            resident((L_INNER, D_MODEL)),
            resident((D_MODEL, D_MODEL)),
            resident((1, D_MODEL)),
            resident((D_MODEL, D_FF)),
            resident((D_FF, D_MODEL)),
            resident((1, D_MODEL)),
        ],
        out_specs=pl.BlockSpec((tm, D_MODEL), rows),
        out_shape=jax.ShapeDtypeStruct((m, D_MODEL), F32),
        compiler_params=_params("parallel"),
        name="tail",
    )(ys, hh, pbig, pbig, pbig, pbig, x, mnw, lnw, wa, wb, wo, nmw, wup, wdn, fw)


def _sample_pre_kernel(xbc_ref, u_ref, sm_ref, cm0_ref, cm1_ref, cm2_ref, cl0_ref, cl1_ref, cl2_ref,
                       m_ref, cwm_ref, cbm_ref, cwl_ref, cbl_ref, wq_ref, wk_ref,
                       dtb_ref, alog_ref, ib_ref, fb_ref,
                       xc_ref, bcc_ref, q_ref, k_ref, dt_ref, da_ref, wr_ref, keep_ref, mnew_ref,
                       cmn_ref, cln_ref):
    xbc = xbc_ref[...].astype(F32)
    conv_m = (cbm_ref[...] + cwm_ref[0:1, :] * cm0_ref[...] + cwm_ref[1:2, :] * cm1_ref[...]
              + cwm_ref[2:3, :] * cm2_ref[...] + cwm_ref[3:4, :] * xbc)
    act = _silu(conv_m)
    xc_ref[...] = act[:, :M_INNER]
    bcc_ref[...] = act[:, M_INNER:]
    cmn_ref[0] = cm1_ref[...]
    cmn_ref[1] = cm2_ref[...]
    cmn_ref[2] = xbc

    u = u_ref[...].astype(F32)
    conv_l = (cbl_ref[...] + cwl_ref[0:1, :] * cl0_ref[...] + cwl_ref[1:2, :] * cl1_ref[...]
              + cwl_ref[2:3, :] * cl2_ref[...] + cwl_ref[3:4, :] * u)
    uc = _silu(conv_l)
    cln_ref[0] = cl1_ref[...]
    cln_ref[1] = cl2_ref[...]
    cln_ref[2] = u
    scale = L_HDIM ** -0.5
    for h in range(L_HEADS):
        sl = slice(h * L_HDIM, (h + 1) * L_HDIM)
        ub = uc[:, sl].astype(BF16)
        q_ref[:, sl] = jnp.dot(ub, wq_ref[h], preferred_element_type=F32)
        k_ref[:, sl] = jnp.dot(ub, wk_ref[h], preferred_element_type=F32) * scale

    sm = sm_ref[...]
    dt = _softplus(sm + dtb_ref[...])
    dt_ref[...] = dt
    da_ref[...] = jnp.exp(dt * (-jnp.exp(alog_ref[...])))
    ig = pltpu.roll(sm + ib_ref[...], F_LANE - I_LANE, axis=1)
    logf = _log_sigmoid(sm + fb_ref[...])
    m_old = m_ref[...]
    m_new = jnp.maximum(logf + m_old, ig)
    mnew_ref[...] = m_new
    wr_ref[...] = jnp.exp(ig - m_new)
    keep_ref[...] = jnp.exp(logf + m_old - m_new)


def _sample_pre(pbig, psmall, conv_m, conv_l, m_lanes, cwm, cbm, cwl, cbl, wq, wk, dtb, alog, ib, fb):
    s = pbig.shape[0]
    f = lambda shape: jax.ShapeDtypeStruct(shape, F32)
    full2 = lambda shape: pl.BlockSpec(shape, lambda i: (0, 0))
    full3 = lambda shape: pl.BlockSpec(shape, lambda i: (0, 0, 0))
    state_row = lambda width, j: pl.BlockSpec((None, s, width), lambda i, j=j: (j, 0, 0))
    return pl.pallas_call(
        _sample_pre_kernel,
        grid=(1,),
        in_specs=[
            full2((s, M_CONV)),
            pl.BlockSpec((s, L_INNER), lambda i: (0, 6)),
            full2((s, LANES)),
            state_row(M_CONV, 0), state_row(M_CONV, 1), state_row(M_CONV, 2),
            state_row(L_INNER, 0), state_row(L_INNER, 1), state_row(L_INNER, 2),
            full2((s, LANES)),
            full2((CONV_W, M_CONV)), full2((1, M_CONV)),
            full2((CONV_W, L_INNER)), full2((1, L_INNER)),
            full3((L_HEADS, L_HDIM, L_HDIM)), full3((L_HEADS, L_HDIM, L_HDIM)),
            full2((1, LANES)), full2((1, LANES)), full2((1, LANES)), full2((1, LANES)),
        ],
        out_specs=[
            full2((s, M_INNER)), full2((s, M_INNER)), full2((s, L_INNER)), full2((s, L_INNER)),
            full2((s, LANES)), full2((s, LANES)), full2((s, LANES)), full2((s, LANES)), full2((s, LANES)),
            full3((CONV_W - 1, s, M_CONV)), full3((CONV_W - 1, s, L_INNER)),
        ],
        out_shape=[
            f((s, M_INNER)), f((s, M_INNER)), f((s, L_INNER)), f((s, L_INNER)),
            f((s, LANES)), f((s, LANES)), f((s, LANES)), f((s, LANES)), f((s, LANES)),
            f((CONV_W - 1, s, M_CONV)), f((CONV_W - 1, s, L_INNER)),
        ],
        compiler_params=_params("arbitrary"),
        name="sample_pre",
    )(pbig[:, 2 * PROJ_BLOCK:6 * PROJ_BLOCK], pbig, psmall, conv_m, conv_m, conv_m,
      conv_l, conv_l, conv_l, m_lanes, cwm, cbm, cwl, cbl, wq, wk, dtb, alog, ib, fb)


def _sample_post_kernel(y_ref, xc_ref, dvec_ref,
                        q_ref, k_ref, v_ref, n_ref, num_ref, wr_ref, keep_ref, mnew_ref,
                        ys_ref, hh_ref, nn_ref):
    ys_ref[...] = (y_ref[...] + dvec_ref[...] * xc_ref[...]).astype(BF16)

    wr = wr_ref[...]
    keep = keep_ref[...]
    floor = jnp.exp(-mnew_ref[...])
    for h in range(L_HEADS):
        ln = F_LANE + h
        sl = slice(h * L_HDIM, (h + 1) * L_HDIM)
        q = q_ref[:, sl]
        k = k_ref[:, sl]
        n_old = n_ref[:, sl]
        wrc = wr[:, ln:ln + 1]
        kpc = keep[:, ln:ln + 1]
        wgt = jnp.sum(q * k, axis=1, keepdims=True) * wrc
        num = wgt * v_ref[:, sl].astype(F32) + kpc * num_ref[:, sl]
        den = wgt + kpc * jnp.sum(q * n_old, axis=1, keepdims=True)
        hh_ref[:, sl] = (num / jnp.maximum(jnp.abs(den), floor[:, ln:ln + 1])).astype(BF16)
        nn_ref[:, sl] = kpc * n_old + wrc * k


def _sample_post(y, xc, pbig, dvec, q, k, n_rows, num, wr, keep, mnew):
    s = y.shape[0]
    full = lambda shape: pl.BlockSpec(shape, lambda i: (0, 0))
    blk = lambda width, j: pl.BlockSpec((s, width), lambda i, j=j: (0, j))
    return pl.pallas_call(
        _sample_post_kernel,
        grid=(1,),
        in_specs=[
            full((s, M_INNER)), full((s, M_INNER)), full((1, M_INNER)),
            full((s, L_INNER)), full((s, L_INNER)), blk(L_INNER, 7),
            full((s, L_INNER)), full((s, L_INNER)),
            full((s, LANES)), full((s, LANES)), full((s, LANES)),
        ],
        out_specs=[full((s, M_INNER)), full((s, L_INNER)), full((s, L_INNER))],
        out_shape=[
            jax.ShapeDtypeStruct((s, M_INNER), BF16),
            jax.ShapeDtypeStruct((s, L_INNER), BF16),
            jax.ShapeDtypeStruct((s, L_INNER), F32),
        ],
        compiler_params=_params("arbitrary"),
        name="sample_post",
    )(y, xc, dvec, q, k, pbig, n_rows, num, wr, keep, mnew)


def _lanes(vec, first_lane):
    n = vec.shape[0]
    return jnp.pad(vec.astype(F32), (first_lane, LANES - first_lane - n)).reshape(1, LANES)


def _spread_matrix(first_lane, n_heads, width):
    r = lax.broadcasted_iota(jnp.int32, (2 * LANES, n_heads * width), 0) % LANES
    c = lax.broadcasted_iota(jnp.int32, (2 * LANES, n_heads * width), 1)
    return (r - first_lane == c // width).astype(BF16)


def _shift_matrix(n):
    r = lax.broadcasted_iota(jnp.int32, ((CONV_W - 1) * n, 2 * n), 0)
    c = lax.broadcasted_iota(jnp.int32, ((CONV_W - 1) * n, 2 * n), 1)
    return (c == n + r % n - (r // n + 1)).astype(BF16)


def kernel(x_prompt, x_sample, state_mamba_conv, state_mamba_ssm, state_mlstm_conv, state_mlstm_C, state_mlstm_n, state_mlstm_m, w_in, mamba_conv_w, mamba_conv_b, mamba_dt_bias, mamba_A_log, mamba_D, mamba_norm_w, w_branch_a, mlstm_conv_w, mlstm_conv_b, mlstm_wq, mlstm_wk, mlstm_i_bias, mlstm_f_bias, mlstm_norm_w, w_branch_b, w_out, norm_mix_w, norm_mlp_w, w_up, w_down, final_norm_w):
    depth = w_in.shape[0]
    assert depth == 1
    batch, seq, _ = x_prompt.shape
    n_samp, dec_seq, _ = x_sample.shape
    assert dec_seq == 1 and seq % CHUNK == 0 and seq >= SUBLANES
    assert batch % SCAN_NB == 0
    l = 0

    assert w_in.shape[2] == W_END
    w_big, w_small = _wprep(jnp.transpose(w_in[l]))
    g_mix = norm_mix_w[l].reshape(1, D_MODEL)
    cwm = mamba_conv_w[l]
    cbm = mamba_conv_b[l].reshape(1, M_CONV)
    cwl = mlstm_conv_w[l]
    cbl = mlstm_conv_b[l].reshape(1, L_INNER)
    dtb = _lanes(mamba_dt_bias[l], DT_LANE)
    alog = _lanes(mamba_A_log[l], DT_LANE)
    ib = _lanes(mlstm_i_bias[l], I_LANE)
    fb = _lanes(mlstm_f_bias[l], F_LANE)
    dvec = jnp.repeat(mamba_D[l].astype(F32), M_HDIM).reshape(1, M_INNER)
    mnw = mamba_norm_w[l].reshape(1, M_INNER)
    lnw = mlstm_norm_w[l].reshape(1, L_INNER)
    wq = mlstm_wq[l].astype(BF16)
    wk = mlstm_wk[l].astype(BF16)
    wa = w_branch_a[l].astype(BF16)
    wb = w_branch_b[l].astype(BF16)
    wo = w_out[l].astype(BF16)
    wup = w_up[l].astype(BF16)
    wdn = w_down[l].astype(BF16)
    nmw = norm_mlp_w[l].reshape(1, D_MODEL)
    fw = final_norm_w.reshape(1, D_MODEL)
    e64 = _spread_matrix(DT_LANE, M_HEADS, M_HDIM)
    el = _spread_matrix(F_LANE, L_HEADS, L_HDIM)
    shift = _shift_matrix(CHUNK)

    xp = x_prompt.reshape(batch * seq, D_MODEL)
    xs = x_sample.reshape(n_samp, D_MODEL)
    pbig, psmall, sbig, ssmall = _inproj(xp, xs, g_mix, w_big, w_small, tm=2048)

    m_lanes = jnp.pad(state_mlstm_m[l], ((0, 0), (F_LANE, LANES - F_LANE - L_HEADS)))
    (xc, bcc, q, k, dt, da, wr, keep, mnew, s_conv_m, s_conv_l) = _sample_pre(
        sbig, ssmall, jnp.transpose(state_mamba_conv[l], (1, 0, 2)),
        jnp.transpose(state_mlstm_conv[l], (1, 0, 2)), m_lanes,
        cwm, cbm, cwl, cbl, wq, wk, dtb, alog, ib, fb)
    samples = (
        dt[:, :M_HEADS], da[:, :M_HEADS],
        wr[:, F_LANE:F_LANE + L_HEADS], keep[:, F_LANE:F_LANE + L_HEADS], xc,
        bcc[:, :M_GROUPS * M_STATE].reshape(n_samp, M_GROUPS, M_STATE),
        bcc[:, M_GROUPS * M_STATE:].reshape(n_samp, M_GROUPS, M_STATE),
        q.reshape(n_samp * L_HEADS, L_HDIM), k.reshape(n_samp * L_HEADS, L_HDIM),
        sbig[:, 7 * PROJ_BLOCK:8 * PROJ_BLOCK].astype(F32).reshape(n_samp * L_HEADS, L_HDIM),
        state_mamba_ssm[l].reshape(n_samp, M_PAIRS, LANES, M_STATE), state_mlstm_C[l])

    ya, hb, p_ssm, p_c, p_n, p_m, s_ssm, s_c, y_rows, num_rows = _scan_prompt(
        pbig, psmall, cwm[:, :M_INNER], cbm[:, :M_INNER], cwm[:, M_INNER:], cbm[:, M_INNER:],
        cwl, cbl, dtb, alog, dvec, wq, wk, ib, fb, e64, el, shift, batch, seq, samples)
    y_prompt = _tail(ya.reshape(batch * seq, M_INNER), hb.reshape(batch * seq, L_INNER), pbig, xp,
                     mnw, lnw, wa, wb, wo, nmw, wup, wdn, fw, tm=512)
    p3 = pbig.reshape(batch, seq, BIG_WIDTH)
    p_conv_m = p3[:, seq - (CONV_W - 1):, 2 * PROJ_BLOCK:6 * PROJ_BLOCK].astype(F32)
    p_conv_l = p3[:, seq - (CONV_W - 1):, 6 * PROJ_BLOCK:7 * PROJ_BLOCK].astype(F32)
    p_ssm = p_ssm.reshape(batch, M_HEADS, M_HDIM, M_STATE)
    p_m = p_m[:, 0, F_LANE:F_LANE + L_HEADS]

    ya_s, hb_s, s_n = _sample_post(
        y_rows.reshape(n_samp, M_INNER), xc, sbig, dvec, q, k,
        state_mlstm_n[l].reshape(n_samp, L_INNER), num_rows.reshape(n_samp, L_INNER),
        wr, keep, mnew)
    y_sample = _tail(ya_s, hb_s, sbig, xs, mnw, lnw, wa, wb, wo, nmw, wup, wdn, fw, tm=n_samp)

    lead = lambda a: a[None]
    return (
        y_prompt.reshape(batch, seq, D_MODEL),
        y_sample.reshape(n_samp, 1, D_MODEL),
        lead(p_conv_m), lead(p_ssm), lead(p_conv_l), lead(p_c), lead(p_n), lead(p_m),
        lead(jnp.transpose(s_conv_m, (1, 0, 2))),
        lead(s_ssm.reshape(n_samp, M_HEADS, M_HDIM, M_STATE)),
        lead(jnp.transpose(s_conv_l, (1, 0, 2))),
        lead(s_c), lead(s_n.reshape(n_samp, L_HEADS, L_HDIM)), lead(mnew[:, F_LANE:F_LANE + L_HEADS]),
    )
```

```python
import jax
import jax.numpy as jnp
from jax import lax
from jax.experimental import pallas as pl
from jax.experimental.pallas import tpu as pltpu

F32 = jnp.float32
BF16 = jnp.bfloat16

D_MODEL = 1024
M_INNER = 2048
M_HEADS = 32
M_HDIM = 64
M_GROUPS = 8
M_PAIRS = M_HEADS // 2
M_STATE = 128
M_CONV = 4096
L_INNER = 1024
L_HEADS = 8
L_HDIM = 128
D_FF = 4096
CONV_W = 4
CHUNK = 128
EPS = 1e-6

LANES = 128
SUBLANES = 8
PROJ_BLOCK = 1024
N_PROJ_BLOCKS = 11
BIG_WIDTH = PROJ_BLOCK * N_PROJ_BLOCKS
DT_LANE = 0
I_LANE = 32
F_LANE = 40
VMEM_LIMIT = 56 * 1024 * 1024

LOG2_E = 1.4426950408889634
NT_DIMS = (((1,), (1,)), ((), ()))
TN_DIMS = (((0,), (0,)), ((), ()))


def _params(*sem):
    return pltpu.CompilerParams(dimension_semantics=sem, vmem_limit_bytes=VMEM_LIMIT)


def _sigmoid(x):
    return 0.5 * jnp.tanh(0.5 * x) + 0.5


def _silu(x):
    h = 0.5 * x
    return h + h * jnp.tanh(h)


def _log1p_exp_neg_abs(x):
    e = jnp.exp(-jnp.abs(x))
    u = 1.0 + e
    return jnp.where(u == 1.0, e, jnp.log(u) * (e / (u - 1.0)))


def _softplus(x):
    return jnp.maximum(x, 0.0) + _log1p_exp_neg_abs(x)


def _log_sigmoid(x):
    return jnp.minimum(x, 0.0) - _log1p_exp_neg_abs(x)


def _rms(x, w):
    return x * lax.rsqrt(jnp.mean(x * x, axis=-1, keepdims=True) + EPS) * w


def _lane_iota(shape):
    return lax.broadcasted_iota(jnp.int32, shape, len(shape) - 1)


def _tri(n):
    r = lax.broadcasted_iota(jnp.int32, (n, n), 0)
    c = lax.broadcasted_iota(jnp.int32, (n, n), 1)
    return r >= c


def _split3(a):
    hi = a.astype(BF16)
    r1 = a - hi.astype(F32)
    mid = r1.astype(BF16)
    lo = (r1 - mid.astype(F32)).astype(BF16)
    return hi, mid, lo


def _cumsum_rows(causal, a):
    tri01 = causal.astype(F32).astype(BF16)
    return jnp.dot(jnp.concatenate([tri01] * 3, axis=1), jnp.concatenate(_split3(a), axis=0),
                   preferred_element_type=F32)


def _spread_many(arrs, e2_ref):
    n = arrs[0].shape[0]
    pieces = [jnp.concatenate(_split3(a)[:2], axis=1) for a in arrs]
    out = jnp.dot(jnp.concatenate(pieces, axis=0), e2_ref[...], preferred_element_type=F32)
    return [out[i * n:(i + 1) * n] for i in range(len(arrs))]


def _spread(a, e2_ref):
    return _spread_many((a,), e2_ref)[0]


CONV_COLS = 512
BF16_ROWS = 16


def _conv_silu(x_ref, win_ref, shift_ref, w_ref, b_ref, out_ref):
    n, width = x_ref.shape
    win_ref[n:2 * n, :] = x_ref[...]
    for c0 in range(0, width, CONV_COLS):
        cs = slice(c0, c0 + CONV_COLS)
        sh = jnp.dot(shift_ref[...], win_ref[:, cs], preferred_element_type=F32)
        y = b_ref[:, cs] + w_ref[CONV_W - 1:CONV_W, cs] * x_ref[:, cs].astype(F32)
        for back in range(1, CONV_W):
            y = y + w_ref[CONV_W - 1 - back:CONV_W - back, cs] * sh[(back - 1) * n:back * n]
        out_ref[:, cs] = _silu(y).astype(out_ref.dtype)
    win_ref[n - BF16_ROWS:n, :] = x_ref[n - BF16_ROWS:n, :]


W_DT0 = M_INNER + M_CONV
W_U0 = W_DT0 + M_HEADS
W_I0 = W_U0 + 3 * L_INNER
W_G0 = W_I0 + 2 * L_HEADS
W_END = W_G0 + 2 * D_MODEL
W_FIRST_UVO = W_DT0 // PROJ_BLOCK
W_FIRST_GATE = W_FIRST_UVO + 3 * L_INNER // PROJ_BLOCK
W_SHIFT_UVO = W_U0 - W_DT0
W_SHIFT_GATE = W_G0 - (W_DT0 + 3 * L_INNER)
W_NEXT_ROWS = 64
assert W_DT0 % PROJ_BLOCK == 0 and DT_LANE == 0
assert (W_I0 - I_LANE) == W_FIRST_GATE * PROJ_BLOCK and F_LANE == I_LANE + L_HEADS
assert max(W_SHIFT_UVO, W_SHIFT_GATE) <= W_NEXT_ROWS and PROJ_BLOCK % W_NEXT_ROWS == 0
assert W_SHIFT_UVO % BF16_ROWS == 0 and W_SHIFT_GATE % BF16_ROWS == 0
assert M_HEADS % BF16_ROWS == 0 and I_LANE % BF16_ROWS == 0 and (2 * L_HEADS) % BF16_ROWS == 0


def _wprep_kernel(a_ref, b_ref, big_ref, small_ref):
    j = pl.program_id(0)

    def emit(shift):
        if shift == 0:
            big_ref[...] = a_ref[...].astype(BF16)
        else:
            big_ref[0:PROJ_BLOCK - shift, :] = a_ref[shift:PROJ_BLOCK, :].astype(BF16)
            big_ref[PROJ_BLOCK - shift:PROJ_BLOCK, :] = b_ref[0:shift, :].astype(BF16)

    @pl.when(j == 0)
    def _():
        small_ref[...] = jnp.zeros_like(small_ref)

    @pl.when(j < W_FIRST_UVO)
    def _():
        emit(0)

    @pl.when((j >= W_FIRST_UVO) & (j < W_FIRST_GATE))
    def _():
        emit(W_SHIFT_UVO)

    @pl.when(j >= W_FIRST_GATE)
    def _():
        emit(W_SHIFT_GATE)

    @pl.when(j == W_FIRST_UVO)
    def _():
        small_ref[DT_LANE:DT_LANE + M_HEADS, :] = a_ref[DT_LANE:DT_LANE + M_HEADS, :].astype(BF16)

    @pl.when(j == W_FIRST_GATE)
    def _():
        small_ref[I_LANE:I_LANE + 2 * L_HEADS, :] = a_ref[I_LANE:I_LANE + 2 * L_HEADS, :].astype(BF16)


def _wprep(w_t):
    return pl.pallas_call(
        _wprep_kernel,
        grid=(N_PROJ_BLOCKS,),
        in_specs=[
            pl.BlockSpec((PROJ_BLOCK, D_MODEL), lambda j: (j, 0)),
            pl.BlockSpec((W_NEXT_ROWS, D_MODEL), lambda j: ((j + 1) * (PROJ_BLOCK // W_NEXT_ROWS), 0)),
        ],
        out_specs=[
            pl.BlockSpec((PROJ_BLOCK, D_MODEL), lambda j: (j, 0)),
            pl.BlockSpec((LANES, D_MODEL), lambda j: (0, 0)),
        ],
        out_shape=[
            jax.ShapeDtypeStruct((BIG_WIDTH, D_MODEL), BF16),
            jax.ShapeDtypeStruct((LANES, D_MODEL), BF16),
        ],
        compiler_params=_params("arbitrary"),
        name="w_prep",
    )(w_t, w_t)


def _inproj_kernel(x_ref, xs_ref, g_ref, w_ref, ws_ref, o_ref, os_ref, so_ref, sos_ref, xn_ref, xsn_ref):
    i = pl.program_id(0)
    j = pl.program_id(1)

    @pl.when(j == 0)
    def _():
        xn_ref[...] = _rms(x_ref[...], g_ref[...]).astype(BF16)
        os_ref[...] = lax.dot_general(xn_ref[...], ws_ref[...], NT_DIMS, preferred_element_type=F32)

    @pl.when((i == 0) & (j == 0))
    def _():
        xsn_ref[...] = _rms(xs_ref[...], g_ref[...]).astype(BF16)
        sos_ref[...] = lax.dot_general(xsn_ref[...], ws_ref[...], NT_DIMS, preferred_element_type=F32)

    @pl.when(i == 0)
    def _():
        so_ref[...] = lax.dot_general(
            xsn_ref[...], w_ref[...], NT_DIMS, preferred_element_type=F32).astype(BF16)

    o_ref[...] = lax.dot_general(
        xn_ref[...], w_ref[...], NT_DIMS, preferred_element_type=F32).astype(BF16)


def _inproj(x, xs, g, w_big, w_small, tm):
    m = x.shape[0]
    s = xs.shape[0]
    tm = min(tm, m)
    last = N_PROJ_BLOCKS - 1
    sample_block = lambda i, j: (0, jnp.where(i == 0, j, last))
    return pl.pallas_call(
        _inproj_kernel,
        grid=(m // tm, N_PROJ_BLOCKS),
        in_specs=[
            pl.BlockSpec((tm, D_MODEL), lambda i, j: (i, 0)),
            pl.BlockSpec((s, D_MODEL), lambda i, j: (0, 0)),
            pl.BlockSpec((1, D_MODEL), lambda i, j: (0, 0)),
            pl.BlockSpec((PROJ_BLOCK, D_MODEL), lambda i, j: (j, 0)),
            pl.BlockSpec((LANES, D_MODEL), lambda i, j: (0, 0)),
        ],
        out_specs=[
            pl.BlockSpec((tm, PROJ_BLOCK), lambda i, j: (i, j)),
            pl.BlockSpec((tm, LANES), lambda i, j: (i, 0)),
            pl.BlockSpec((s, PROJ_BLOCK), sample_block),
            pl.BlockSpec((s, LANES), lambda i, j: (0, 0)),
        ],
        out_shape=[
            jax.ShapeDtypeStruct((m, BIG_WIDTH), BF16),
            jax.ShapeDtypeStruct((m, LANES), F32),
            jax.ShapeDtypeStruct((s, BIG_WIDTH), BF16),
            jax.ShapeDtypeStruct((s, LANES), F32),
        ],
        scratch_shapes=[pltpu.VMEM((tm, D_MODEL), BF16), pltpu.VMEM((s, D_MODEL), BF16)],
        compiler_params=_params("arbitrary", "arbitrary"),
        name="in_proj",
    )(x, xs, g, w_big, w_small)


SCAN_NB = 2


def _ssd_chunk(x_ref, bc_ref, sm, cwx_ref, cbx_ref, cwbc_ref, cbbc_ref, dtb_ref, alog_ref, dvec_ref,
               e64_ref, shift_ref, y_ref, h_ref, winx_ref, winbc_ref, xc_scr, bcc_scr):
    n = x_ref.shape[0]
    _conv_silu(x_ref, winx_ref, shift_ref, cwx_ref, cbx_ref, xc_scr)
    _conv_silu(bc_ref, winbc_ref, shift_ref, cwbc_ref, cbbc_ref, bcc_scr)
    xc = xc_scr[...]
    bcc = bcc_scr[...]

    lane = _lane_iota((n, LANES))
    dt = _softplus(sm + dtb_ref[...])
    da = jnp.where(lane < M_HEADS, dt * (-LOG2_E * jnp.exp(alog_ref[...])), 0.0)
    causal = _tri(n)
    acum = _cumsum_rows(causal, da)
    acum_t = acum.T
    last = acum[n - 1:n, :]
    exp_last = jnp.exp2(last)
    dt_x, wr_x, ea_x = _spread_many((dt, jnp.exp2(last - acum), jnp.exp2(acum)), e64_ref)

    xdt = xc * dt_x
    xw = (xdt * wr_x).astype(BF16)
    xb = xdt.astype(BF16)
    low_half = jnp.bitwise_and(_lane_iota((n, M_INNER)), LANES - 1) < M_HDIM
    zero = jnp.zeros((), BF16)
    x_lo = jnp.where(low_half, xb, zero)
    x_hi = jnp.where(low_half, zero, xb)
    first_rows = lax.broadcasted_iota(jnp.int32, (LANES, LANES), 0) < M_HDIM

    def weights(cb, h):
        seg = acum[:, h:h + 1] - acum_t[h:h + 1, :]
        return (cb * jnp.exp2(jnp.where(causal, seg, -jnp.inf))).astype(BF16)

    pairs_per_group = M_PAIRS // M_GROUPS
    for g in range(M_GROUPS):
        bg = bcc[:, g * M_STATE:(g + 1) * M_STATE]
        cg = bcc[:, (M_GROUPS + g) * M_STATE:(M_GROUPS + g + 1) * M_STATE]
        cb = lax.dot_general(cg, bg, NT_DIMS, preferred_element_type=F32)
        for pp in range(pairs_per_group):
            hp = g * pairs_per_group + pp
            h0, h1 = 2 * hp, 2 * hp + 1
            sl = slice(hp * LANES, (hp + 1) * LANES)
            hs = h_ref[hp]
            y = jnp.dot(weights(cb, h0), x_lo[:, sl], preferred_element_type=F32)
            y = y + jnp.dot(weights(cb, h1), x_hi[:, sl], preferred_element_type=F32)
            ys = lax.dot_general(cg, hs.astype(BF16), NT_DIMS, preferred_element_type=F32)
            y_ref[:, sl] = (y + ea_x[:, sl] * ys + dvec_ref[:, sl] * xc[:, sl]).astype(BF16)
            el = jnp.where(first_rows, exp_last[:, h0:h0 + 1], exp_last[:, h1:h1 + 1])
            h_ref[hp] = el * hs + lax.dot_general(xw[:, sl], bg, TN_DIMS, preferred_element_type=F32)


def _mlstm_chunks(seqs, wq_ref, wk_ref, ib_ref, fb_ref, el_ref):
    n = seqs[0][0].shape[0]
    causal = _tri(n)
    lane = _lane_iota((n, LANES))
    gate_lanes = (lane >= F_LANE) & (lane < F_LANE + L_HEADS)
    scale = L_HDIM ** -0.5
    lanes_of = [slice(h * L_HDIM, (h + 1) * L_HDIM) for h in range(L_HEADS)]

    gates = []
    for uc, v_ref, sm, hh_ref, c_ref, n_ref, m_ref in seqs:
        ig = pltpu.roll(sm + ib_ref[...], F_LANE - I_LANE, axis=1)
        logf = jnp.where(gate_lanes, _log_sigmoid(sm + fb_ref[...]), 0.0)
        bcum = _cumsum_rows(causal, logf)
        m_old = m_ref[...]
        bl = bcum[n - 1:n, :]
        src = bl - bcum + ig
        m_new = jnp.maximum(bl + m_old, jnp.max(src, axis=0, keepdims=True))
        gates.append(dict(
            bcum=bcum, bcum_t=bcum.T, ig_t=ig.T, m_old=m_old, m_new=m_new,
            keep=jnp.exp(bl + m_old - m_new),
            wr_x=_spread(jnp.exp(src - m_new), el_ref)))

    work = [(i, h) for i in range(len(seqs)) for h in range(L_HEADS)]
    c_olds = {(i, h): seqs[i][4][h] for i, h in work}
    n_olds = {(i, h): seqs[i][5][h:h + 1, :] for i, h in work}
    qs, ks, qbs, kbs = {}, {}, {}, {}
    for i, h in work:
        ub = seqs[i][0][:, lanes_of[h]]
        qs[i, h] = jnp.dot(ub, wq_ref[h], preferred_element_type=F32)
        ks[i, h] = jnp.dot(ub, wk_ref[h], preferred_element_type=F32) * scale
    logds, inters, rowmax = {}, {}, {}
    for i, h in work:
        g = gates[i]
        ln = F_LANE + h
        bcol = jnp.broadcast_to(g["bcum"][:, ln:ln + 1], (n, n))
        logd = jnp.where(causal, bcol - g["bcum_t"][ln:ln + 1, :] + g["ig_t"][ln:ln + 1, :], -jnp.inf)
        logds[i, h] = logd
        inters[i, h] = bcol + g["m_old"][:, ln:ln + 1]
        rowmax[i, h] = jnp.max(logd, axis=1, keepdims=True)
    ss, scs, floors = {}, {}, {}
    for i, h in work:
        qbs[i, h] = qs[i, h].astype(BF16)
        kbs[i, h] = ks[i, h].astype(BF16)
        m_s = jnp.maximum(inters[i, h], rowmax[i, h])
        dm = jnp.exp(logds[i, h] - m_s)
        scs[i, h] = jnp.exp(inters[i, h] - m_s)
        floors[i, h] = jnp.exp(-m_s)
        ss[i, h] = lax.dot_general(qbs[i, h], kbs[i, h], NT_DIMS, preferred_element_type=F32) * dm
    nums, dens = {}, {}
    for i, h in work:
        vb = seqs[i][1][:, lanes_of[h]]
        num = jnp.dot(ss[i, h].astype(BF16), vb, preferred_element_type=F32)
        nums[i, h] = num + scs[i, h] * jnp.dot(
            qbs[i, h], c_olds[i, h].astype(BF16), preferred_element_type=F32)
        dens[i, h] = (jnp.sum(ss[i, h], axis=1, keepdims=True)
                      + scs[i, h] * jnp.sum(qs[i, h] * n_olds[i, h], axis=1, keepdims=True))
    for i, h in work:
        hh = nums[i, h] / jnp.maximum(jnp.abs(dens[i, h]), floors[i, h])
        seqs[i][3][:, lanes_of[h]] = hh.astype(BF16)
    for i, h in work:
        g = gates[i]
        ln = F_LANE + h
        kw = ks[i, h] * g["wr_x"][:, lanes_of[h]]
        kp = g["keep"][:, ln:ln + 1]
        seqs[i][4][h] = kp * c_olds[i, h] + lax.dot_general(
            kw.astype(BF16), seqs[i][1][:, lanes_of[h]], TN_DIMS, preferred_element_type=F32)
        seqs[i][5][h:h + 1, :] = kp * n_olds[i, h] + jnp.sum(kw, axis=0, keepdims=True)
    for i in range(len(seqs)):
        seqs[i][6][...] = gates[i]["m_new"]


def _diag_rows(row, diag):
    return jnp.where(diag, jnp.broadcast_to(row, diag.shape), 0.0).astype(BF16)


def _state_spreads(x_ref, q_ref, k_ref, xs_scr, qs_scr, ks_scr):
    ns = x_ref.shape[0]
    rows = lax.broadcasted_iota(jnp.int32, (LANES, LANES), 0)
    diag = rows == _lane_iota((LANES, LANES))
    ones = jnp.ones((LANES, LANES), BF16)
    ones2 = jnp.ones((2 * LANES, LANES), BF16)
    for s in range(ns):
        for hp in range(M_PAIRS):
            xrow = x_ref[s:s + 1, hp * LANES:(hp + 1) * LANES]
            xs_scr[s * M_PAIRS + hp] = jnp.dot(_diag_rows(xrow, diag), ones, preferred_element_type=F32)
    for r in range(ns * L_HEADS):
        for src_ref, dst_scr in ((q_ref, qs_scr), (k_ref, ks_scr)):
            row = src_ref[r:r + 1, :]
            hi = row.astype(BF16).astype(F32)
            lhs = jnp.concatenate([_diag_rows(hi, diag), _diag_rows(row - hi, diag)], axis=1)
            dst_scr[r] = jnp.dot(lhs, ones2, preferred_element_type=F32)


def _state_updates(base, dt_ref, da_ref, wr_ref, keep_ref, b_ref, c_ref, v_ref, h_ref, cst_ref,
                   hn_ref, cn_ref, y_ref, num_ref, xs_scr, qs_scr, ks_scr):
    ns = b_ref.shape[0]
    rows = lax.broadcasted_iota(jnp.int32, (LANES, LANES), 0)
    lanes = _lane_iota((LANES, LANES))
    first_rows = rows < M_HDIM
    ones = jnp.ones((LANES, LANES), BF16)
    pairs_per_group = M_PAIRS // M_GROUPS
    acc = jnp.zeros((LANES, LANES), F32)
    for s in range(ns):
        for hp in range(M_PAIRS):
            g = hp // pairs_per_group
            h0, h1 = 2 * hp, 2 * hp + 1
            brow = b_ref[s, g:g + 1, :]
            crow = c_ref[s, g:g + 1, :]
            dav = jnp.where(first_rows, da_ref[base + s, h0], da_ref[base + s, h1])
            dtv = jnp.where(first_rows, dt_ref[base + s, h0], dt_ref[base + s, h1])
            hn = dav * h_ref[s, hp] + (dtv * xs_scr[s * M_PAIRS + hp]) * brow
            hn_ref[s, hp] = hn
            ysum = jnp.dot((hn * crow).astype(BF16), ones, preferred_element_type=F32)
            acc = jnp.where(lanes == s * M_PAIRS + hp, ysum, acc)
    y_ref[...] = acc.T[0:ns * M_PAIRS, :]
    for s in range(ns):
        for h in range(L_HEADS):
            r = s * L_HEADS + h
            c_old = cst_ref[s, h]
            num_ref[r:r + 1, :] = jnp.sum(qs_scr[r] * c_old, axis=0, keepdims=True)
            cn_ref[s, h] = (keep_ref[base + s, h] * c_old
                            + (ks_scr[r] * wr_ref[base + s, h]) * v_ref[r:r + 1, :])


def _scan_kernel(dt_ref, da_ref, wr_ref, keep_ref, sx_ref, sb_ref, sc_ref, sq_ref, sk_ref, sv_ref,
                 sh_ref, scst_ref, x_ref, bc_ref, u_ref, v_ref, sm_ref,
                 cwx_ref, cbx_ref, cwbc_ref, cbbc_ref, cwl_ref, cbl_ref, dtb_ref, alog_ref, dvec_ref,
                 wq_ref, wk_ref, ib_ref, fb_ref, e64_ref, el_ref, shift_ref,
                 y_ref, hh_ref, h_ref, c_ref, n_ref, m_ref, shn_ref, scn_ref, sy_ref, snum_ref,
                 winx_ref, winbc_ref, winu_ref, xc_scr, bcc_scr, uc_scr, xs_scr, qs_scr, ks_scr):
    n = x_ref.shape[1]
    step = pl.program_id(0) * pl.num_programs(1) + pl.program_id(1)

    @pl.when(pl.program_id(1) == 0)
    def _():
        h_ref[...] = jnp.zeros_like(h_ref)
        c_ref[...] = jnp.zeros_like(c_ref)
        n_ref[...] = jnp.zeros_like(n_ref)
        m_ref[...] = jnp.zeros_like(m_ref)
        winx_ref[:, 0:n, :] = jnp.zeros((SCAN_NB, n, M_INNER), BF16)
        winbc_ref[:, 0:n, :] = jnp.zeros((SCAN_NB, n, M_INNER), BF16)
        winu_ref[:, 0:n, :] = jnp.zeros((SCAN_NB, n, L_INNER), BF16)

    seqs = []
    for i in range(SCAN_NB):
        _ssd_chunk(x_ref.at[i], bc_ref.at[i], sm_ref[i], cwx_ref, cbx_ref, cwbc_ref, cbbc_ref,
                   dtb_ref, alog_ref, dvec_ref, e64_ref, shift_ref, y_ref.at[i], h_ref.at[i],
                   winx_ref.at[i], winbc_ref.at[i], xc_scr, bcc_scr)
        _conv_silu(u_ref.at[i], winu_ref.at[i], shift_ref, cwl_ref, cbl_ref, uc_scr.at[i])
        seqs.append((uc_scr.at[i], v_ref.at[i], sm_ref[i], hh_ref.at[i], c_ref.at[i], n_ref.at[i],
                     m_ref.at[i]))
        if i == 0:
            _state_spreads(sx_ref, sq_ref, sk_ref, xs_scr, qs_scr, ks_scr)
    _mlstm_chunks(seqs, wq_ref, wk_ref, ib_ref, fb_ref, el_ref)
    _state_updates(step * sb_ref.shape[0], dt_ref, da_ref, wr_ref, keep_ref, sb_ref, sc_ref, sv_ref,
                   sh_ref, scst_ref, shn_ref, scn_ref, sy_ref, snum_ref, xs_scr, qs_scr, ks_scr)


def _scan_prompt(pbig, psmall, cwx, cbx, cwbc, cbbc, cwl, cbl, dtb, alog, dvec, wq, wk, ib, fb,
                 e64, el, shift, batch, seq, samples):
    p3 = pbig.reshape(batch, seq, BIG_WIDTH)
    s3 = psmall.reshape(batch, seq, LANES)
    nb = SCAN_NB
    n_chunks = seq // CHUNK
    n_steps = (batch // nb) * n_chunks
    s_dt, s_da, s_wr, s_keep, s_x, s_b, s_c, s_q, s_k, s_v, s_h, s_cst = samples
    n_samp = s_h.shape[0]
    ns = n_samp // n_steps
    assert ns * n_steps == n_samp and ns * M_PAIRS <= LANES and (ns * L_HEADS) % SUBLANES == 0
    smem = pl.BlockSpec(memory_space=pltpu.SMEM)
    step = lambda b, c: b * n_chunks + c
    srows = lambda n_rows: pl.BlockSpec((n_rows, LANES), lambda b, c: (step(b, c), 0))
    sgrp = pl.BlockSpec((ns, M_GROUPS, M_STATE), lambda b, c: (step(b, c), 0, 0))
    sstate = lambda d1: pl.BlockSpec((ns, d1, LANES, LANES), lambda b, c: (step(b, c), 0, 0, 0))
    const2 = lambda b, c: (0, 0)
    const3 = lambda b, c: (0, 0, 0)
    rows = lambda width, col: pl.BlockSpec((nb, CHUNK, width), lambda b, c, col=col: (b, c, col))
    state4 = lambda d1, d2, d3: pl.BlockSpec((nb, d1, d2, d3), lambda b, c: (b, 0, 0, 0))
    return pl.pallas_call(
        _scan_kernel,
        grid=(batch // nb, seq // CHUNK),
        in_specs=[
            smem, smem, smem, smem,
            pl.BlockSpec((None, ns, M_INNER), lambda b, c: (step(b, c), 0, 0)),
            sgrp, sgrp, srows(ns * L_HEADS), srows(ns * L_HEADS), srows(ns * L_HEADS),
            sstate(M_PAIRS), sstate(L_HEADS),
            rows(M_INNER, 1),
            rows(M_INNER, 2),
            rows(L_INNER, 6),
            rows(L_INNER, 7),
            rows(LANES, 0),
            pl.BlockSpec((CONV_W, M_INNER), const2), pl.BlockSpec((1, M_INNER), const2),
            pl.BlockSpec((CONV_W, M_INNER), const2), pl.BlockSpec((1, M_INNER), const2),
            pl.BlockSpec((CONV_W, L_INNER), const2), pl.BlockSpec((1, L_INNER), const2),
            pl.BlockSpec((1, LANES), const2), pl.BlockSpec((1, LANES), const2),
            pl.BlockSpec((1, M_INNER), const2),
            pl.BlockSpec((L_HEADS, L_HDIM, L_HDIM), const3),
            pl.BlockSpec((L_HEADS, L_HDIM, L_HDIM), const3),
            pl.BlockSpec((1, LANES), const2), pl.BlockSpec((1, LANES), const2),
            pl.BlockSpec((2 * LANES, M_INNER), const2),
            pl.BlockSpec((2 * LANES, L_INNER), const2),
            pl.BlockSpec(((CONV_W - 1) * CHUNK, 2 * CHUNK), const2),
        ],
        out_specs=[
            rows(M_INNER, 0),
            rows(L_INNER, 0),
            state4(M_PAIRS, LANES, M_STATE),
            state4(L_HEADS, L_HDIM, L_HDIM),
            pl.BlockSpec((nb, L_HEADS, L_HDIM), lambda b, c: (b, 0, 0)),
            pl.BlockSpec((nb, 1, LANES), lambda b, c: (b, 0, 0)),
            sstate(M_PAIRS), sstate(L_HEADS), srows(ns * M_PAIRS), srows(ns * L_HEADS),
        ],
        out_shape=[
            jax.ShapeDtypeStruct((batch, seq, M_INNER), BF16),
            jax.ShapeDtypeStruct((batch, seq, L_INNER), BF16),
            jax.ShapeDtypeStruct((batch, M_PAIRS, LANES, M_STATE), F32),
            jax.ShapeDtypeStruct((batch, L_HEADS, L_HDIM, L_HDIM), F32),
            jax.ShapeDtypeStruct((batch, L_HEADS, L_HDIM), F32),
            jax.ShapeDtypeStruct((batch, 1, LANES), F32),
            jax.ShapeDtypeStruct(s_h.shape, F32),
            jax.ShapeDtypeStruct(s_cst.shape, F32),
            jax.ShapeDtypeStruct((n_samp * M_PAIRS, LANES), F32),
            jax.ShapeDtypeStruct((n_samp * L_HEADS, LANES), F32),
        ],
        scratch_shapes=[
            pltpu.VMEM((nb, 2 * CHUNK, M_INNER), BF16),
            pltpu.VMEM((nb, 2 * CHUNK, M_INNER), BF16),
            pltpu.VMEM((nb, 2 * CHUNK, L_INNER), BF16),
            pltpu.VMEM((CHUNK, M_INNER), F32),
            pltpu.VMEM((CHUNK, M_INNER), BF16),
            pltpu.VMEM((nb, CHUNK, L_INNER), BF16),
            pltpu.VMEM((ns * M_PAIRS, LANES, LANES), F32),
            pltpu.VMEM((ns * L_HEADS, LANES, LANES), F32),
            pltpu.VMEM((ns * L_HEADS, LANES, LANES), F32),
        ],
        compiler_params=_params("parallel", "arbitrary"),
        name="scan_prompt",
    )(s_dt, s_da, s_wr, s_keep, s_x.reshape(n_steps, ns, M_INNER), s_b, s_c, s_q, s_k, s_v, s_h, s_cst,
      p3, p3, p3, p3, s3, cwx, cbx, cwbc, cbbc, cwl, cbl, dtb, alog, dvec, wq, wk, ib, fb,
      e64, el, shift)


def _tail_kernel(ys_ref, hh_ref, z_ref, o_ref, ga_ref, gb_ref, x_ref, mnw_ref, lnw_ref,
                 wa_ref, wb_ref, wo_ref, nmw_ref, wup_ref, wdn_ref, fw_ref, y_ref):
    gw = M_INNER // M_GROUPS
    a = None
    for g in range(M_GROUPS):
        sl = slice(g * gw, (g + 1) * gw)
        yg = ys_ref[:, sl].astype(F32) * _silu(z_ref[:, sl].astype(F32))
        part = jnp.dot(_rms(yg, mnw_ref[:, sl]).astype(BF16), wa_ref[sl, :], preferred_element_type=F32)
        a = part if a is None else a + part
    b = None
    for hp in range(L_HEADS // 2):
        pieces = []
        for h in (2 * hp, 2 * hp + 1):
            sl = slice(h * L_HDIM, (h + 1) * L_HDIM)
            gate = _sigmoid(o_ref[:, sl].astype(F32))
            pieces.append((gate * _rms(hh_ref[:, sl].astype(F32), lnw_ref[:, sl])).astype(BF16))
        rows = slice(2 * hp * L_HDIM, (2 * hp + 2) * L_HDIM)
        part = jnp.dot(jnp.concatenate(pieces, axis=1), wb_ref[rows, :], preferred_element_type=F32)
        b = part if b is None else b + part
    t = _sigmoid(ga_ref[...].astype(F32)) * a + _sigmoid(gb_ref[...].astype(F32)) * b
    x1 = x_ref[...] + jnp.dot(t.astype(BF16), wo_ref[...], preferred_element_type=F32)
    hn = _rms(x1, nmw_ref[...]).astype(BF16)
    acc = x1
    for c in range(D_FF // PROJ_BLOCK):
        sl = slice(c * PROJ_BLOCK, (c + 1) * PROJ_BLOCK)
        up = jnp.dot(hn, wup_ref[:, sl], preferred_element_type=F32)
        act = jnp.square(jnp.maximum(up, 0.0)).astype(BF16)
        acc = acc + jnp.dot(act, wdn_ref[sl, :], preferred_element_type=F32)
    y_ref[...] = _rms(acc, fw_ref[...])


def _tail(ys, hh, pbig, x, mnw, lnw, wa, wb, wo, nmw, wup, wdn, fw, tm):
    m = x.shape[0]
    tm = min(tm, m)
    rows = lambda i: (i, 0)
    const = lambda i: (0, 0)

    def resident(shape):
        return pl.BlockSpec(shape, const, pipeline_mode=pl.Buffered(1))

    return pl.pallas_call(
        _tail_kernel,
        grid=(m // tm,),
        in_specs=[
            pl.BlockSpec((tm, M_INNER), rows),
            pl.BlockSpec((tm, L_INNER), rows),
            pl.BlockSpec((tm, M_INNER), rows),
            pl.BlockSpec((tm, PROJ_BLOCK), lambda i: (i, 8)),
            pl.BlockSpec((tm, PROJ_BLOCK), lambda i: (i, 9)),
            pl.BlockSpec((tm, PROJ_BLOCK), lambda i: (i, 10)),
            pl.BlockSpec((tm, D_MODEL), rows),
            resident((1, M_INNER)),
            resident((1, L_INNER)),
            resident((M_INNER, D_MODEL)),
            resident((L_INNER, D_MODEL)),
            resident((D_MODEL, D_MODEL)),
            resident((1, D_MODEL)),
            resident((D_MODEL, D_FF)),
            resident((D_FF, D_MODEL)),
            resident((1, D_MODEL)),
        ],
        out_specs=pl.BlockSpec((tm, D_MODEL), rows),
        out_shape=jax.ShapeDtypeStruct((m, D_MODEL), F32),
        compiler_params=_params("parallel"),
        name="tail",
    )(ys, hh, pbig, pbig, pbig, pbig, x, mnw, lnw, wa, wb, wo, nmw, wup, wdn, fw)


def _sample_pre_kernel(xbc_ref, u_ref, sm_ref, cm0_ref, cm1_ref, cm2_ref, cl0_ref, cl1_ref, cl2_ref,
                       m_ref, cwm_ref, cbm_ref, cwl_ref, cbl_ref, wq_ref, wk_ref,
                       dtb_ref, alog_ref, ib_ref, fb_ref,
                       xc_ref, bcc_ref, q_ref, k_ref, dt_ref, da_ref, wr_ref, keep_ref, mnew_ref,
                       cmn_ref, cln_ref):
    xbc = xbc_ref[...].astype(F32)
    conv_m = (cbm_ref[...] + cwm_ref[0:1, :] * cm0_ref[...] + cwm_ref[1:2, :] * cm1_ref[...]
              + cwm_ref[2:3, :] * cm2_ref[...] + cwm_ref[3:4, :] * xbc)
    act = _silu(conv_m)
    xc_ref[...] = act[:, :M_INNER]
    bcc_ref[...] = act[:, M_INNER:]
    cmn_ref[0] = cm1_ref[...]
    cmn_ref[1] = cm2_ref[...]
    cmn_ref[2] = xbc

    u = u_ref[...].astype(F32)
    conv_l = (cbl_ref[...] + cwl_ref[0:1, :] * cl0_ref[...] + cwl_ref[1:2, :] * cl1_ref[...]
              + cwl_ref[2:3, :] * cl2_ref[...] + cwl_ref[3:4, :] * u)
    uc = _silu(conv_l)
    cln_ref[0] = cl1_ref[...]
    cln_ref[1] = cl2_ref[...]
    cln_ref[2] = u
    scale = L_HDIM ** -0.5
    for h in range(L_HEADS):
        sl = slice(h * L_HDIM, (h + 1) * L_HDIM)
        ub = uc[:, sl].astype(BF16)
        q_ref[:, sl] = jnp.dot(ub, wq_ref[h], preferred_element_type=F32)
        k_ref[:, sl] = jnp.dot(ub, wk_ref[h], preferred_element_type=F32) * scale

    sm = sm_ref[...]
    dt = _softplus(sm + dtb_ref[...])
    dt_ref[...] = dt
    da_ref[...] = jnp.exp(dt * (-jnp.exp(alog_ref[...])))
    ig = pltpu.roll(sm + ib_ref[...], F_LANE - I_LANE, axis=1)
    logf = _log_sigmoid(sm + fb_ref[...])
    m_old = m_ref[...]
    m_new = jnp.maximum(logf + m_old, ig)
    mnew_ref[...] = m_new
    wr_ref[...] = jnp.exp(ig - m_new)
    keep_ref[...] = jnp.exp(logf + m_old - m_new)


def _sample_pre(pbig, psmall, conv_m, conv_l, m_lanes, cwm, cbm, cwl, cbl, wq, wk, dtb, alog, ib, fb):
    s = pbig.shape[0]
    f = lambda shape: jax.ShapeDtypeStruct(shape, F32)
    full2 = lambda shape: pl.BlockSpec(shape, lambda i: (0, 0))
    full3 = lambda shape: pl.BlockSpec(shape, lambda i: (0, 0, 0))
    state_row = lambda width, j: pl.BlockSpec((None, s, width), lambda i, j=j: (j, 0, 0))
    return pl.pallas_call(
        _sample_pre_kernel,
        grid=(1,),
        in_specs=[
            full2((s, M_CONV)),
            pl.BlockSpec((s, L_INNER), lambda i: (0, 6)),
            full2((s, LANES)),
            state_row(M_CONV, 0), state_row(M_CONV, 1), state_row(M_CONV, 2),
            state_row(L_INNER, 0), state_row(L_INNER, 1), state_row(L_INNER, 2),
            full2((s, LANES)),
            full2((CONV_W, M_CONV)), full2((1, M_CONV)),
            full2((CONV_W, L_INNER)), full2((1, L_INNER)),
            full3((L_HEADS, L_HDIM, L_HDIM)), full3((L_HEADS, L_HDIM, L_HDIM)),
            full2((1, LANES)), full2((1, LANES)), full2((1, LANES)), full2((1, LANES)),
        ],
        out_specs=[
            full2((s, M_INNER)), full2((s, M_INNER)), full2((s, L_INNER)), full2((s, L_INNER)),
            full2((s, LANES)), full2((s, LANES)), full2((s, LANES)), full2((s, LANES)), full2((s, LANES)),
            full3((CONV_W - 1, s, M_CONV)), full3((CONV_W - 1, s, L_INNER)),
        ],
        out_shape=[
            f((s, M_INNER)), f((s, M_INNER)), f((s, L_INNER)), f((s, L_INNER)),
            f((s, LANES)), f((s, LANES)), f((s, LANES)), f((s, LANES)), f((s, LANES)),
            f((CONV_W - 1, s, M_CONV)), f((CONV_W - 1, s, L_INNER)),
        ],
        compiler_params=_params("arbitrary"),
        name="sample_pre",
    )(pbig[:, 2 * PROJ_BLOCK:6 * PROJ_BLOCK], pbig, psmall, conv_m, conv_m, conv_m,
      conv_l, conv_l, conv_l, m_lanes, cwm, cbm, cwl, cbl, wq, wk, dtb, alog, ib, fb)


def _sample_post_kernel(y_ref, xc_ref, dvec_ref,
                        q_ref, k_ref, v_ref, n_ref, num_ref, wr_ref, keep_ref, mnew_ref,
                        ys_ref, hh_ref, nn_ref):
    ys_ref[...] = (y_ref[...] + dvec_ref[...] * xc_ref[...]).astype(BF16)

    wr = wr_ref[...]
    keep = keep_ref[...]
    floor = jnp.exp(-mnew_ref[...])
    for h in range(L_HEADS):
        ln = F_LANE + h
        sl = slice(h * L_HDIM, (h + 1) * L_HDIM)
        q = q_ref[:, sl]
        k = k_ref[:, sl]
        n_old = n_ref[:, sl]
        wrc = wr[:, ln:ln + 1]
        kpc = keep[:, ln:ln + 1]
        wgt = jnp.sum(q * k, axis=1, keepdims=True) * wrc
        num = wgt * v_ref[:, sl].astype(F32) + kpc * num_ref[:, sl]
        den = wgt + kpc * jnp.sum(q * n_old, axis=1, keepdims=True)
        hh_ref[:, sl] = (num / jnp.maximum(jnp.abs(den), floor[:, ln:ln + 1])).astype(BF16)
        nn_ref[:, sl] = kpc * n_old + wrc * k


def _sample_post(y, xc, pbig, dvec, q, k, n_rows, num, wr, keep, mnew):
    s = y.shape[0]
    full = lambda shape: pl.BlockSpec(shape, lambda i: (0, 0))
    blk = lambda width, j: pl.BlockSpec((s, width), lambda i, j=j: (0, j))
    return pl.pallas_call(
        _sample_post_kernel,
        grid=(1,),
        in_specs=[
            full((s, M_INNER)), full((s, M_INNER)), full((1, M_INNER)),
            full((s, L_INNER)), full((s, L_INNER)), blk(L_INNER, 7),
            full((s, L_INNER)), full((s, L_INNER)),
            full((s, LANES)), full((s, LANES)), full((s, LANES)),
        ],
        out_specs=[full((s, M_INNER)), full((s, L_INNER)), full((s, L_INNER))],
        out_shape=[
            jax.ShapeDtypeStruct((s, M_INNER), BF16),
            jax.ShapeDtypeStruct((s, L_INNER), BF16),
            jax.ShapeDtypeStruct((s, L_INNER), F32),
        ],
        compiler_params=_params("arbitrary"),
        name="sample_post",
    )(y, xc, dvec, q, k, pbig, n_rows, num, wr, keep, mnew)


def _lanes(vec, first_lane):
    n = vec.shape[0]
    return jnp.pad(vec.astype(F32), (first_lane, LANES - first_lane - n)).reshape(1, LANES)


def _spread_matrix(first_lane, n_heads, width):
    r = lax.broadcasted_iota(jnp.int32, (2 * LANES, n_heads * width), 0) % LANES
    c = lax.broadcasted_iota(jnp.int32, (2 * LANES, n_heads * width), 1)
    return (r - first_lane == c // width).astype(BF16)


def _shift_matrix(n):
    r = lax.broadcasted_iota(jnp.int32, ((CONV_W - 1) * n, 2 * n), 0)
    c = lax.broadcasted_iota(jnp.int32, ((CONV_W - 1) * n, 2 * n), 1)
    return (c == n + r % n - (r // n + 1)).astype(BF16)


def kernel(x_prompt, x_sample, state_mamba_conv, state_mamba_ssm, state_mlstm_conv, state_mlstm_C, state_mlstm_n, state_mlstm_m, w_in, mamba_conv_w, mamba_conv_b, mamba_dt_bias, mamba_A_log, mamba_D, mamba_norm_w, w_branch_a, mlstm_conv_w, mlstm_conv_b, mlstm_wq, mlstm_wk, mlstm_i_bias, mlstm_f_bias, mlstm_norm_w, w_branch_b, w_out, norm_mix_w, norm_mlp_w, w_up, w_down, final_norm_w):
    depth = w_in.shape[0]
    assert depth == 1
    batch, seq, _ = x_prompt.shape
    n_samp, dec_seq, _ = x_sample.shape
    assert dec_seq == 1 and seq % CHUNK == 0 and seq >= SUBLANES
    assert batch % SCAN_NB == 0
    l = 0

    assert w_in.shape[2] == W_END
    w_big, w_small = _wprep(jnp.transpose(w_in[l]))
    g_mix = norm_mix_w[l].reshape(1, D_MODEL)
    cwm = mamba_conv_w[l]
    cbm = mamba_conv_b[l].reshape(1, M_CONV)
    cwl = mlstm_conv_w[l]
    cbl = mlstm_conv_b[l].reshape(1, L_INNER)
    dtb = _lanes(mamba_dt_bias[l], DT_LANE)
    alog = _lanes(mamba_A_log[l], DT_LANE)
    ib = _lanes(mlstm_i_bias[l], I_LANE)
    fb = _lanes(mlstm_f_bias[l], F_LANE)
    dvec = jnp.repeat(mamba_D[l].astype(F32), M_HDIM).reshape(1, M_INNER)
    mnw = mamba_norm_w[l].reshape(1, M_INNER)
    lnw = mlstm_norm_w[l].reshape(1, L_INNER)
    wq = mlstm_wq[l].astype(BF16)
    wk = mlstm_wk[l].astype(BF16)
    wa = w_branch_a[l].astype(BF16)
    wb = w_branch_b[l].astype(BF16)
    wo = w_out[l].astype(BF16)
    wup = w_up[l].astype(BF16)
    wdn = w_down[l].astype(BF16)
    nmw = norm_mlp_w[l].reshape(1, D_MODEL)
    fw = final_norm_w.reshape(1, D_MODEL)
    e64 = _spread_matrix(DT_LANE, M_HEADS, M_HDIM)
    el = _spread_matrix(F_LANE, L_HEADS, L_HDIM)
    shift = _shift_matrix(CHUNK)

    xp = x_prompt.reshape(batch * seq, D_MODEL)
    xs = x_sample.reshape(n_samp, D_MODEL)
    pbig, psmall, sbig, ssmall = _inproj(xp, xs, g_mix, w_big, w_small, tm=2048)

    m_lanes = jnp.pad(state_mlstm_m[l], ((0, 0), (F_LANE, LANES - F_LANE - L_HEADS)))
    (xc, bcc, q, k, dt, da, wr, keep, mnew, s_conv_m, s_conv_l) = _sample_pre(
        sbig, ssmall, jnp.transpose(state_mamba_conv[l], (1, 0, 2)),
        jnp.transpose(state_mlstm_conv[l], (1, 0, 2)), m_lanes,
        cwm, cbm, cwl, cbl, wq, wk, dtb, alog, ib, fb)
    samples = (
        dt[:, :M_HEADS], da[:, :M_HEADS],
        wr[:, F_LANE:F_LANE + L_HEADS], keep[:, F_LANE:F_LANE + L_HEADS], xc,
        bcc[:, :M_GROUPS * M_STATE].reshape(n_samp, M_GROUPS, M_STATE),
        bcc[:, M_GROUPS * M_STATE:].reshape(n_samp, M_GROUPS, M_STATE),
        q.reshape(n_samp * L_HEADS, L_HDIM), k.reshape(n_samp * L_HEADS, L_HDIM),
        sbig[:, 7 * PROJ_BLOCK:8 * PROJ_BLOCK].astype(F32).reshape(n_samp * L_HEADS, L_HDIM),
        state_mamba_ssm[l].reshape(n_samp, M_PAIRS, LANES, M_STATE), state_mlstm_C[l])

    ya, hb, p_ssm, p_c, p_n, p_m, s_ssm, s_c, y_rows, num_rows = _scan_prompt(
        pbig, psmall, cwm[:, :M_INNER], cbm[:, :M_INNER], cwm[:, M_INNER:], cbm[:, M_INNER:],
        cwl, cbl, dtb, alog, dvec, wq, wk, ib, fb, e64, el, shift, batch, seq, samples)
    y_prompt = _tail(ya.reshape(batch * seq, M_INNER), hb.reshape(batch * seq, L_INNER), pbig, xp,
                     mnw, lnw, wa, wb, wo, nmw, wup, wdn, fw, tm=512)
    p3 = pbig.reshape(batch, seq, BIG_WIDTH)
    p_conv_m = p3[:, seq - (CONV_W - 1):, 2 * PROJ_BLOCK:6 * PROJ_BLOCK].astype(F32)
    p_conv_l = p3[:, seq - (CONV_W - 1):, 6 * PROJ_BLOCK:7 * PROJ_BLOCK].astype(F32)
    p_ssm = p_ssm.reshape(batch, M_HEADS, M_HDIM, M_STATE)
    p_m = p_m[:, 0, F_LANE:F_LANE + L_HEADS]

    ya_s, hb_s, s_n = _sample_post(
        y_rows.reshape(n_samp, M_INNER), xc, sbig, dvec, q, k,
        state_mlstm_n[l].reshape(n_samp, L_INNER), num_rows.reshape(n_samp, L_INNER),
        wr, keep, mnew)
    y_sample = _tail(ya_s, hb_s, sbig, xs, mnw, lnw, wa, wb, wo, nmw, wup, wdn, fw, tm=n_samp)

    lead = lambda a: a[None]
    return (
        y_prompt.reshape(batch, seq, D_MODEL),
        y_sample.reshape(n_samp, 1, D_MODEL),
        lead(p_conv_m), lead(p_ssm), lead(p_conv_l), lead(p_c), lead(p_n), lead(p_m),
        lead(jnp.transpose(s_conv_m, (1, 0, 2))),
        lead(s_ssm.reshape(n_samp, M_HEADS, M_HDIM, M_STATE)),
        lead(jnp.transpose(s_conv_l, (1, 0, 2))),
        lead(s_c), lead(s_n.reshape(n_samp, L_HEADS, L_HDIM)), lead(mnew[:, F_LANE:F_LANE + L_HEADS]),
    )
```

```python
import jax
import jax.numpy as jnp
from jax import lax
from jax.experimental import pallas as pl
from jax.experimental.pallas import tpu as pltpu

F32 = jnp.float32
BF16 = jnp.bfloat16

D_MODEL = 1024
M_INNER = 2048
M_HEADS = 32
M_HDIM = 64
M_GROUPS = 8
M_PAIRS = M_HEADS // 2
M_STATE = 128
M_CONV = 4096
L_INNER = 1024
L_HEADS = 8
L_HDIM = 128
D_FF = 4096
CONV_W = 4
CHUNK = 128
EPS = 1e-6

LANES = 128
SUBLANES = 8
PROJ_BLOCK = 1024
N_PROJ_BLOCKS = 11
BIG_WIDTH = PROJ_BLOCK * N_PROJ_BLOCKS
DT_LANE = 0
I_LANE = 32
F_LANE = 40
VMEM_LIMIT = 56 * 1024 * 1024

LOG2_E = 1.4426950408889634
NT_DIMS = (((1,), (1,)), ((), ()))
TN_DIMS = (((0,), (0,)), ((), ()))


def _params(*sem):
    return pltpu.CompilerParams(dimension_semantics=sem, vmem_limit_bytes=VMEM_LIMIT)


def _sigmoid(x):
    return 0.5 * jnp.tanh(0.5 * x) + 0.5


def _silu(x):
    h = 0.5 * x
    return h + h * jnp.tanh(h)


def _log1p_exp_neg_abs(x):
    e = jnp.exp(-jnp.abs(x))
    u = 1.0 + e
    return jnp.where(u == 1.0, e, jnp.log(u) * (e / (u - 1.0)))


def _softplus(x):
    return jnp.maximum(x, 0.0) + _log1p_exp_neg_abs(x)


def _log_sigmoid(x):
    return jnp.minimum(x, 0.0) - _log1p_exp_neg_abs(x)


def _rms(x, w):
    return x * lax.rsqrt(jnp.mean(x * x, axis=-1, keepdims=True) + EPS) * w


def _lane_iota(shape):
    return lax.broadcasted_iota(jnp.int32, shape, len(shape) - 1)


def _tri(n):
    r = lax.broadcasted_iota(jnp.int32, (n, n), 0)
    c = lax.broadcasted_iota(jnp.int32, (n, n), 1)
    return r >= c


def _split3(a):
    hi = a.astype(BF16)
    r1 = a - hi.astype(F32)
    mid = r1.astype(BF16)
    lo = (r1 - mid.astype(F32)).astype(BF16)
    return hi, mid, lo


def _cumsum_rows(causal, a):
    tri01 = causal.astype(F32).astype(BF16)
    return jnp.dot(jnp.concatenate([tri01] * 3, axis=1), jnp.concatenate(_split3(a), axis=0),
                   preferred_element_type=F32)


def _spread_many(arrs, e2_ref):
    n = arrs[0].shape[0]
    pieces = [jnp.concatenate(_split3(a)[:2], axis=1) for a in arrs]
    out = jnp.dot(jnp.concatenate(pieces, axis=0), e2_ref[...], preferred_element_type=F32)
    return [out[i * n:(i + 1) * n] for i in range(len(arrs))]


def _spread(a, e2_ref):
    return _spread_many((a,), e2_ref)[0]


CONV_COLS = 512
BF16_ROWS = 16


def _conv_silu(x_ref, win_ref, shift_ref, w_ref, b_ref, out_ref):
    n, width = x_ref.shape
    win_ref[n:2 * n, :] = x_ref[...]
    for c0 in range(0, width, CONV_COLS):
        cs = slice(c0, c0 + CONV_COLS)
        sh = jnp.dot(shift_ref[...], win_ref[:, cs], preferred_element_type=F32)
        y = b_ref[:, cs] + w_ref[CONV_W - 1:CONV_W, cs] * x_ref[:, cs].astype(F32)
        for back in range(1, CONV_W):
            y = y + w_ref[CONV_W - 1 - back:CONV_W - back, cs] * sh[(back - 1) * n:back * n]
        out_ref[:, cs] = _silu(y).astype(out_ref.dtype)
    win_ref[n - BF16_ROWS:n, :] = x_ref[n - BF16_ROWS:n, :]


W_DT0 = M_INNER + M_CONV
W_U0 = W_DT0 + M_HEADS
W_I0 = W_U0 + 3 * L_INNER
W_G0 = W_I0 + 2 * L_HEADS
W_END = W_G0 + 2 * D_MODEL
W_FIRST_UVO = W_DT0 // PROJ_BLOCK
W_FIRST_GATE = W_FIRST_UVO + 3 * L_INNER // PROJ_BLOCK
W_SHIFT_UVO = W_U0 - W_DT0
W_SHIFT_GATE = W_G0 - (W_DT0 + 3 * L_INNER)
W_NEXT_ROWS = 64
assert W_DT0 % PROJ_BLOCK == 0 and DT_LANE == 0
assert (W_I0 - I_LANE) == W_FIRST_GATE * PROJ_BLOCK and F_LANE == I_LANE + L_HEADS
assert max(W_SHIFT_UVO, W_SHIFT_GATE) <= W_NEXT_ROWS and PROJ_BLOCK % W_NEXT_ROWS == 0
assert W_SHIFT_UVO % BF16_ROWS == 0 and W_SHIFT_GATE % BF16_ROWS == 0
assert M_HEADS % BF16_ROWS == 0 and I_LANE % BF16_ROWS == 0 and (2 * L_HEADS) % BF16_ROWS == 0


def _wprep_kernel(a_ref, b_ref, big_ref, small_ref):
    j = pl.program_id(0)

    def emit(shift):
        if shift == 0:
            big_ref[...] = a_ref[...].astype(BF16)
        else:
            big_ref[0:PROJ_BLOCK - shift, :] = a_ref[shift:PROJ_BLOCK, :].astype(BF16)
            big_ref[PROJ_BLOCK - shift:PROJ_BLOCK, :] = b_ref[0:shift, :].astype(BF16)

    @pl.when(j == 0)
    def _():
        small_ref[...] = jnp.zeros_like(small_ref)

    @pl.when(j < W_FIRST_UVO)
    def _():
        emit(0)

    @pl.when((j >= W_FIRST_UVO) & (j < W_FIRST_GATE))
    def _():
        emit(W_SHIFT_UVO)

    @pl.when(j >= W_FIRST_GATE)
    def _():
        emit(W_SHIFT_GATE)

    @pl.when(j == W_FIRST_UVO)
    def _():
        small_ref[DT_LANE:DT_LANE + M_HEADS, :] = a_ref[DT_LANE:DT_LANE + M_HEADS, :].astype(BF16)

    @pl.when(j == W_FIRST_GATE)
    def _():
        small_ref[I_LANE:I_LANE + 2 * L_HEADS, :] = a_ref[I_LANE:I_LANE + 2 * L_HEADS, :].astype(BF16)


def _wprep(w_t):
    return pl.pallas_call(
        _wprep_kernel,
        grid=(N_PROJ_BLOCKS,),
        in_specs=[
            pl.BlockSpec((PROJ_BLOCK, D_MODEL), lambda j: (j, 0)),
            pl.BlockSpec((W_NEXT_ROWS, D_MODEL), lambda j: ((j + 1) * (PROJ_BLOCK // W_NEXT_ROWS), 0)),
        ],
        out_specs=[
            pl.BlockSpec((PROJ_BLOCK, D_MODEL), lambda j: (j, 0)),
            pl.BlockSpec((LANES, D_MODEL), lambda j: (0, 0)),
        ],
        out_shape=[
            jax.ShapeDtypeStruct((BIG_WIDTH, D_MODEL), BF16),
            jax.ShapeDtypeStruct((LANES, D_MODEL), BF16),
        ],
        compiler_params=_params("arbitrary"),
        name="w_prep",
    )(w_t, w_t)


def _inproj_kernel(x_ref, xs_ref, g_ref, w_ref, ws_ref, o_ref, os_ref, so_ref, sos_ref, xn_ref, xsn_ref):
    i = pl.program_id(0)
    j = pl.program_id(1)

    @pl.when(j == 0)
    def _():
        xn_ref[...] = _rms(x_ref[...], g_ref[...]).astype(BF16)
        os_ref[...] = lax.dot_general(xn_ref[...], ws_ref[...], NT_DIMS, preferred_element_type=F32)

    @pl.when((i == 0) & (j == 0))
    def _():
        xsn_ref[...] = _rms(xs_ref[...], g_ref[...]).astype(BF16)
        sos_ref[...] = lax.dot_general(xsn_ref[...], ws_ref[...], NT_DIMS, preferred_element_type=F32)

    @pl.when(i == 0)
    def _():
        so_ref[...] = lax.dot_general(
            xsn_ref[...], w_ref[...], NT_DIMS, preferred_element_type=F32).astype(BF16)

    o_ref[...] = lax.dot_general(
        xn_ref[...], w_ref[...], NT_DIMS, preferred_element_type=F32).astype(BF16)


def _inproj(x, xs, g, w_big, w_small, tm):
    m = x.shape[0]
    s = xs.shape[0]
    tm = min(tm, m)
    last = N_PROJ_BLOCKS - 1
    sample_block = lambda i, j: (0, jnp.where(i == 0, j, last))
    return pl.pallas_call(
        _inproj_kernel,
        grid=(m // tm, N_PROJ_BLOCKS),
        in_specs=[
            pl.BlockSpec((tm, D_MODEL), lambda i, j: (i, 0)),
            pl.BlockSpec((s, D_MODEL), lambda i, j: (0, 0)),
            pl.BlockSpec((1, D_MODEL), lambda i, j: (0, 0)),
            pl.BlockSpec((PROJ_BLOCK, D_MODEL), lambda i, j: (j, 0)),
            pl.BlockSpec((LANES, D_MODEL), lambda i, j: (0, 0)),
        ],
        out_specs=[
            pl.BlockSpec((tm, PROJ_BLOCK), lambda i, j: (i, j)),
            pl.BlockSpec((tm, LANES), lambda i, j: (i, 0)),
            pl.BlockSpec((s, PROJ_BLOCK), sample_block),
            pl.BlockSpec((s, LANES), lambda i, j: (0, 0)),
        ],
        out_shape=[
            jax.ShapeDtypeStruct((m, BIG_WIDTH), BF16),
            jax.ShapeDtypeStruct((m, LANES), F32),
            jax.ShapeDtypeStruct((s, BIG_WIDTH), BF16),
            jax.ShapeDtypeStruct((s, LANES), F32),
        ],
        scratch_shapes=[pltpu.VMEM((tm, D_MODEL), BF16), pltpu.VMEM((s, D_MODEL), BF16)],
        compiler_params=_params("arbitrary", "arbitrary"),
        name="in_proj",
    )(x, xs, g, w_big, w_small)


SCAN_NB = 2


def _ssd_chunk(x_ref, bc_ref, sm, cwx_ref, cbx_ref, cwbc_ref, cbbc_ref, dtb_ref, alog_ref, dvec_ref,
               e64_ref, shift_ref, y_ref, h_ref, winx_ref, winbc_ref, xc_scr, bcc_scr):
    n = x_ref.shape[0]
    _conv_silu(x_ref, winx_ref, shift_ref, cwx_ref, cbx_ref, xc_scr)
    _conv_silu(bc_ref, winbc_ref, shift_ref, cwbc_ref, cbbc_ref, bcc_scr)
    xc = xc_scr[...]
    bcc = bcc_scr[...]

    lane = _lane_iota((n, LANES))
    dt = _softplus(sm + dtb_ref[...])
    da = jnp.where(lane < M_HEADS, dt * (-LOG2_E * jnp.exp(alog_ref[...])), 0.0)
    causal = _tri(n)
    acum = _cumsum_rows(causal, da)
    acum_t = acum.T
    last = acum[n - 1:n, :]
    exp_last = jnp.exp2(last)
    dt_x, wr_x, ea_x = _spread_many((dt, jnp.exp2(last - acum), jnp.exp2(acum)), e64_ref)

    xdt = xc * dt_x
    xw = (xdt * wr_x).astype(BF16)
    xb = xdt.astype(BF16)
    low_half = jnp.bitwise_and(_lane_iota((n, M_INNER)), LANES - 1) < M_HDIM
    zero = jnp.zeros((), BF16)
    x_lo = jnp.where(low_half, xb, zero)
    x_hi = jnp.where(low_half, zero, xb)
    first_rows = lax.broadcasted_iota(jnp.int32, (LANES, LANES), 0) < M_HDIM

    def weights(cb, h):
        seg = acum[:, h:h + 1] - acum_t[h:h + 1, :]
        return (cb * jnp.exp2(jnp.where(causal, seg, -jnp.inf))).astype(BF16)

    pairs_per_group = M_PAIRS // M_GROUPS
    for g in range(M_GROUPS):
        bg = bcc[:, g * M_STATE:(g + 1) * M_STATE]
        cg = bcc[:, (M_GROUPS + g) * M_STATE:(M_GROUPS + g + 1) * M_STATE]
        cb = lax.dot_general(cg, bg, NT_DIMS, preferred_element_type=F32)
        for pp in range(pairs_per_group):
            hp = g * pairs_per_group + pp
            h0, h1 = 2 * hp, 2 * hp + 1
            sl = slice(hp * LANES, (hp + 1) * LANES)
            hs = h_ref[hp]
            y = jnp.dot(weights(cb, h0), x_lo[:, sl], preferred_element_type=F32)
            y = y + jnp.dot(weights(cb, h1), x_hi[:, sl], preferred_element_type=F32)
            ys = lax.dot_general(cg, hs.astype(BF16), NT_DIMS, preferred_element_type=F32)
            y_ref[:, sl] = (y + ea_x[:, sl] * ys + dvec_ref[:, sl] * xc[:, sl]).astype(BF16)
            el = jnp.where(first_rows, exp_last[:, h0:h0 + 1], exp_last[:, h1:h1 + 1])
            h_ref[hp] = el * hs + lax.dot_general(xw[:, sl], bg, TN_DIMS, preferred_element_type=F32)


def _mlstm_chunks(seqs, wq_ref, wk_ref, ib_ref, fb_ref, el_ref):
    n = seqs[0][0].shape[0]
    causal = _tri(n)
    lane = _lane_iota((n, LANES))
    gate_lanes = (lane >= F_LANE) & (lane < F_LANE + L_HEADS)
    scale = L_HDIM ** -0.5
    lanes_of = [slice(h * L_HDIM, (h + 1) * L_HDIM) for h in range(L_HEADS)]

    gates = []
    for uc, v_ref, sm, hh_ref, c_ref, n_ref, m_ref in seqs:
        ig = pltpu.roll(sm + ib_ref[...], F_LANE - I_LANE, axis=1)
        logf = jnp.where(gate_lanes, _log_sigmoid(sm + fb_ref[...]), 0.0)
        bcum = _cumsum_rows(causal, logf)
        m_old = m_ref[...]
        bl = bcum[n - 1:n, :]
        src = bl - bcum + ig
        m_new = jnp.maximum(bl + m_old, jnp.max(src, axis=0, keepdims=True))
        gates.append(dict(
            bcum=bcum, bcum_t=bcum.T, ig_t=ig.T, m_old=m_old, m_new=m_new,
            keep=jnp.exp(bl + m_old - m_new),
            wr_x=_spread(jnp.exp(src - m_new), el_ref)))

    work = [(i, h) for i in range(len(seqs)) for h in range(L_HEADS)]
    c_olds = {(i, h): seqs[i][4][h] for i, h in work}
    n_olds = {(i, h): seqs[i][5][h:h + 1, :] for i, h in work}
    qs, ks, qbs, kbs = {}, {}, {}, {}
    for i, h in work:
        ub = seqs[i][0][:, lanes_of[h]]
        qs[i, h] = jnp.dot(ub, wq_ref[h], preferred_element_type=F32)
        ks[i, h] = jnp.dot(ub, wk_ref[h], preferred_element_type=F32) * scale
    logds, inters, rowmax = {}, {}, {}
    for i, h in work:
        g = gates[i]
        ln = F_LANE + h
        bcol = jnp.broadcast_to(g["bcum"][:, ln:ln + 1], (n, n))
        logd = jnp.where(causal, bcol - g["bcum_t"][ln:ln + 1, :] + g["ig_t"][ln:ln + 1, :], -jnp.inf)
        logds[i, h] = logd
        inters[i, h] = bcol + g["m_old"][:, ln:ln + 1]
        rowmax[i, h] = jnp.max(logd, axis=1, keepdims=True)
    ss, scs, floors = {}, {}, {}
    for i, h in work:
        qbs[i, h] = qs[i, h].astype(BF16)
        kbs[i, h] = ks[i, h].astype(BF16)
        m_s = jnp.maximum(inters[i, h], rowmax[i, h])
        dm = jnp.exp(logds[i, h] - m_s)
        scs[i, h] = jnp.exp(inters[i, h] - m_s)
        floors[i, h] = jnp.exp(-m_s)
        ss[i, h] = lax.dot_general(qbs[i, h], kbs[i, h], NT_DIMS, preferred_element_type=F32) * dm
    nums, dens = {}, {}
    for i, h in work:
        vb = seqs[i][1][:, lanes_of[h]]
        num = jnp.dot(ss[i, h].astype(BF16), vb, preferred_element_type=F32)
        nums[i, h] = num + scs[i, h] * jnp.dot(
            qbs[i, h], c_olds[i, h].astype(BF16), preferred_element_type=F32)
        dens[i, h] = (jnp.sum(ss[i, h], axis=1, keepdims=True)
                      + scs[i, h] * jnp.sum(qs[i, h] * n_olds[i, h], axis=1, keepdims=True))
    for i, h in work:
        hh = nums[i, h] / jnp.maximum(jnp.abs(dens[i, h]), floors[i, h])
        seqs[i][3][:, lanes_of[h]] = hh.astype(BF16)
    for i, h in work:
        g = gates[i]
        ln = F_LANE + h
        kw = ks[i, h] * g["wr_x"][:, lanes_of[h]]
        kp = g["keep"][:, ln:ln + 1]
        seqs[i][4][h] = kp * c_olds[i, h] + lax.dot_general(
            kw.astype(BF16), seqs[i][1][:, lanes_of[h]], TN_DIMS, preferred_element_type=F32)
        seqs[i][5][h:h + 1, :] = kp * n_olds[i, h] + jnp.sum(kw, axis=0, keepdims=True)
    for i in range(len(seqs)):
        seqs[i][6][...] = gates[i]["m_new"]


def _diag_rows(row, diag):
    return jnp.where(diag, jnp.broadcast_to(row, diag.shape), 0.0).astype(BF16)


def _state_spreads(x_ref, q_ref, k_ref, xs_scr, qs_scr, ks_scr):
    ns = x_ref.shape[0]
    rows = lax.broadcasted_iota(jnp.int32, (LANES, LANES), 0)
    diag = rows == _lane_iota((LANES, LANES))
    ones = jnp.ones((LANES, LANES), BF16)
    ones2 = jnp.ones((2 * LANES, LANES), BF16)
    for s in range(ns):
        for hp in range(M_PAIRS):
            xrow = x_ref[s:s + 1, hp * LANES:(hp + 1) * LANES]
            xs_scr[s * M_PAIRS + hp] = jnp.dot(_diag_rows(xrow, diag), ones, preferred_element_type=F32)
    for r in range(ns * L_HEADS):
        for src_ref, dst_scr in ((q_ref, qs_scr), (k_ref, ks_scr)):
            row = src_ref[r:r + 1, :]
            hi = row.astype(BF16).astype(F32)
            lhs = jnp.concatenate([_diag_rows(hi, diag), _diag_rows(row - hi, diag)], axis=1)
            dst_scr[r] = jnp.dot(lhs, ones2, preferred_element_type=F32)


def _state_updates(base, dt_ref, da_ref, wr_ref, keep_ref, b_ref, c_ref, v_ref, h_ref, cst_ref,
                   hn_ref, cn_ref, y_ref, num_ref, xs_scr, qs_scr, ks_scr):
    ns = b_ref.shape[0]
    rows = lax.broadcasted_iota(jnp.int32, (LANES, LANES), 0)
    lanes = _lane_iota((LANES, LANES))
    first_rows = rows < M_HDIM
    ones = jnp.ones((LANES, LANES), BF16)
    pairs_per_group = M_PAIRS // M_GROUPS
    acc = jnp.zeros((LANES, LANES), F32)
    for s in range(ns):
        for hp in range(M_PAIRS):
            g = hp // pairs_per_group
            h0, h1 = 2 * hp, 2 * hp + 1
            brow = b_ref[s, g:g + 1, :]
            crow = c_ref[s, g:g + 1, :]
            dav = jnp.where(first_rows, da_ref[base + s, h0], da_ref[base + s, h1])
            dtv = jnp.where(first_rows, dt_ref[base + s, h0], dt_ref[base + s, h1])
            hn = dav * h_ref[s, hp] + (dtv * xs_scr[s * M_PAIRS + hp]) * brow
            hn_ref[s, hp] = hn
            ysum = jnp.dot((hn * crow).astype(BF16), ones, preferred_element_type=F32)
            acc = jnp.where(lanes == s * M_PAIRS + hp, ysum, acc)
    y_ref[...] = acc.T[0:ns * M_PAIRS, :]
    for s in range(ns):
        for h in range(L_HEADS):
            r = s * L_HEADS + h
            c_old = cst_ref[s, h]
            num_ref[r:r + 1, :] = jnp.sum(qs_scr[r] * c_old, axis=0, keepdims=True)
            cn_ref[s, h] = (keep_ref[base + s, h] * c_old
                            + (ks_scr[r] * wr_ref[base + s, h]) * v_ref[r:r + 1, :])


def _scan_kernel(dt_ref, da_ref, wr_ref, keep_ref, sx_ref, sb_ref, sc_ref, sq_ref, sk_ref, sv_ref,
                 sh_ref, scst_ref, x_ref, bc_ref, u_ref, v_ref, sm_ref,
                 cwx_ref, cbx_ref, cwbc_ref, cbbc_ref, cwl_ref, cbl_ref, dtb_ref, alog_ref, dvec_ref,
                 wq_ref, wk_ref, ib_ref, fb_ref, e64_ref, el_ref, shift_ref,
                 y_ref, hh_ref, h_ref, c_ref, n_ref, m_ref, shn_ref, scn_ref, sy_ref, snum_ref,
                 winx_ref, winbc_ref, winu_ref, xc_scr, bcc_scr, uc_scr, xs_scr, qs_scr, ks_scr):
    n = x_ref.shape[1]
    step = pl.program_id(0) * pl.num_programs(1) + pl.program_id(1)

    @pl.when(pl.program_id(1) == 0)
    def _():
        h_ref[...] = jnp.zeros_like(h_ref)
        c_ref[...] = jnp.zeros_like(c_ref)
        n_ref[...] = jnp.zeros_like(n_ref)
        m_ref[...] = jnp.zeros_like(m_ref)
        winx_ref[:, 0:n, :] = jnp.zeros((SCAN_NB, n, M_INNER), BF16)
        winbc_ref[:, 0:n, :] = jnp.zeros((SCAN_NB, n, M_INNER), BF16)
        winu_ref[:, 0:n, :] = jnp.zeros((SCAN_NB, n, L_INNER), BF16)

    seqs = []
    for i in range(SCAN_NB):
        _ssd_chunk(x_ref.at[i], bc_ref.at[i], sm_ref[i], cwx_ref, cbx_ref, cwbc_ref, cbbc_ref,
                   dtb_ref, alog_ref, dvec_ref, e64_ref, shift_ref, y_ref.at[i], h_ref.at[i],
                   winx_ref.at[i], winbc_ref.at[i], xc_scr, bcc_scr)
        _conv_silu(u_ref.at[i], winu_ref.at[i], shift_ref, cwl_ref, cbl_ref, uc_scr.at[i])
        seqs.append((uc_scr.at[i], v_ref.at[i], sm_ref[i], hh_ref.at[i], c_ref.at[i], n_ref.at[i],
                     m_ref.at[i]))
        if i == 0:
            _state_spreads(sx_ref, sq_ref, sk_ref, xs_scr, qs_scr, ks_scr)
    _state_updates(step * sb_ref.shape[0], dt_ref, da_ref, wr_ref, keep_ref, sb_ref, sc_ref, sv_ref,
                   sh_ref, scst_ref, shn_ref, scn_ref, sy_ref, snum_ref, xs_scr, qs_scr, ks_scr)
    _mlstm_chunks(seqs, wq_ref, wk_ref, ib_ref, fb_ref, el_ref)


def _scan_prompt(pbig, psmall, cwx, cbx, cwbc, cbbc, cwl, cbl, dtb, alog, dvec, wq, wk, ib, fb,
                 e64, el, shift, batch, seq, samples):
    p3 = pbig.reshape(batch, seq, BIG_WIDTH)
    s3 = psmall.reshape(batch, seq, LANES)
    nb = SCAN_NB
    n_chunks = seq // CHUNK
    n_steps = (batch // nb) * n_chunks
    s_dt, s_da, s_wr, s_keep, s_x, s_b, s_c, s_q, s_k, s_v, s_h, s_cst = samples
    n_samp = s_h.shape[0]
    ns = n_samp // n_steps
    assert ns * n_steps == n_samp and ns * M_PAIRS <= LANES and (ns * L_HEADS) % SUBLANES == 0
    smem = pl.BlockSpec(memory_space=pltpu.SMEM)
    step = lambda b, c: b * n_chunks + c
    srows = lambda n_rows: pl.BlockSpec((n_rows, LANES), lambda b, c: (step(b, c), 0))
    sgrp = pl.BlockSpec((ns, M_GROUPS, M_STATE), lambda b, c: (step(b, c), 0, 0))
    sstate = lambda d1: pl.BlockSpec((ns, d1, LANES, LANES), lambda b, c: (step(b, c), 0, 0, 0))
    const2 = lambda b, c: (0, 0)
    const3 = lambda b, c: (0, 0, 0)
    rows = lambda width, col: pl.BlockSpec((nb, CHUNK, width), lambda b, c, col=col: (b, c, col))
    state4 = lambda d1, d2, d3: pl.BlockSpec((nb, d1, d2, d3), lambda b, c: (b, 0, 0, 0))
    return pl.pallas_call(
        _scan_kernel,
        grid=(batch // nb, seq // CHUNK),
        in_specs=[
            smem, smem, smem, smem,
            pl.BlockSpec((None, ns, M_INNER), lambda b, c: (step(b, c), 0, 0)),
            sgrp, sgrp, srows(ns * L_HEADS), srows(ns * L_HEADS), srows(ns * L_HEADS),
            sstate(M_PAIRS), sstate(L_HEADS),
            rows(M_INNER, 1),
            rows(M_INNER, 2),
            rows(L_INNER, 6),
            rows(L_INNER, 7),
            rows(LANES, 0),
            pl.BlockSpec((CONV_W, M_INNER), const2), pl.BlockSpec((1, M_INNER), const2),
            pl.BlockSpec((CONV_W, M_INNER), const2), pl.BlockSpec((1, M_INNER), const2),
            pl.BlockSpec((CONV_W, L_INNER), const2), pl.BlockSpec((1, L_INNER), const2),
            pl.BlockSpec((1, LANES), const2), pl.BlockSpec((1, LANES), const2),
            pl.BlockSpec((1, M_INNER), const2),
            pl.BlockSpec((L_HEADS, L_HDIM, L_HDIM), const3),
            pl.BlockSpec((L_HEADS, L_HDIM, L_HDIM), const3),
            pl.BlockSpec((1, LANES), const2), pl.BlockSpec((1, LANES), const2),
            pl.BlockSpec((2 * LANES, M_INNER), const2),
            pl.BlockSpec((2 * LANES, L_INNER), const2),
            pl.BlockSpec(((CONV_W - 1) * CHUNK, 2 * CHUNK), const2),
        ],
        out_specs=[
            rows(M_INNER, 0),
            rows(L_INNER, 0),
            state4(M_PAIRS, LANES, M_STATE),
            state4(L_HEADS, L_HDIM, L_HDIM),
            pl.BlockSpec((nb, L_HEADS, L_HDIM), lambda b, c: (b, 0, 0)),
            pl.BlockSpec((nb, 1, LANES), lambda b, c: (b, 0, 0)),
            sstate(M_PAIRS), sstate(L_HEADS), srows(ns * M_PAIRS), srows(ns * L_HEADS),
        ],
        out_shape=[
            jax.ShapeDtypeStruct((batch, seq, M_INNER), BF16),
            jax.ShapeDtypeStruct((batch, seq, L_INNER), BF16),
            jax.ShapeDtypeStruct((batch, M_PAIRS, LANES, M_STATE), F32),
            jax.ShapeDtypeStruct((batch, L_HEADS, L_HDIM, L_HDIM), F32),
            jax.ShapeDtypeStruct((batch, L_HEADS, L_HDIM), F32),
            jax.ShapeDtypeStruct((batch, 1, LANES), F32),
            jax.ShapeDtypeStruct(s_h.shape, F32),
            jax.ShapeDtypeStruct(s_cst.shape, F32),
            jax.ShapeDtypeStruct((n_samp * M_PAIRS, LANES), F32),
            jax.ShapeDtypeStruct((n_samp * L_HEADS, LANES), F32),
        ],
        scratch_shapes=[
            pltpu.VMEM((nb, 2 * CHUNK, M_INNER), BF16),
            pltpu.VMEM((nb, 2 * CHUNK, M_INNER), BF16),
            pltpu.VMEM((nb, 2 * CHUNK, L_INNER), BF16),
            pltpu.VMEM((CHUNK, M_INNER), F32),
            pltpu.VMEM((CHUNK, M_INNER), BF16),
            pltpu.VMEM((nb, CHUNK, L_INNER), BF16),
            pltpu.VMEM((ns * M_PAIRS, LANES, LANES), F32),
            pltpu.VMEM((ns * L_HEADS, LANES, LANES), F32),
            pltpu.VMEM((ns * L_HEADS, LANES, LANES), F32),
        ],
        compiler_params=_params("parallel", "arbitrary"),
        name="scan_prompt",
    )(s_dt, s_da, s_wr, s_keep, s_x.reshape(n_steps, ns, M_INNER), s_b, s_c, s_q, s_k, s_v, s_h, s_cst,
      p3, p3, p3, p3, s3, cwx, cbx, cwbc, cbbc, cwl, cbl, dtb, alog, dvec, wq, wk, ib, fb,
      e64, el, shift)


def _tail_kernel(ys_ref, hh_ref, z_ref, o_ref, ga_ref, gb_ref, x_ref, mnw_ref, lnw_ref,
                 wa_ref, wb_ref, wo_ref, nmw_ref, wup_ref, wdn_ref, fw_ref, y_ref):
    gw = M_INNER // M_GROUPS
    a = None
    for g in range(M_GROUPS):
        sl = slice(g * gw, (g + 1) * gw)
        yg = ys_ref[:, sl].astype(F32) * _silu(z_ref[:, sl].astype(F32))
        part = jnp.dot(_rms(yg, mnw_ref[:, sl]).astype(BF16), wa_ref[sl, :], preferred_element_type=F32)
        a = part if a is None else a + part
    b = None
    for hp in range(L_HEADS // 2):
        pieces = []
        for h in (2 * hp, 2 * hp + 1):
            sl = slice(h * L_HDIM, (h + 1) * L_HDIM)
            gate = _sigmoid(o_ref[:, sl].astype(F32))
            pieces.append((gate * _rms(hh_ref[:, sl].astype(F32), lnw_ref[:, sl])).astype(BF16))
        rows = slice(2 * hp * L_HDIM, (2 * hp + 2) * L_HDIM)
        part = jnp.dot(jnp.concatenate(pieces, axis=1), wb_ref[rows, :], preferred_element_type=F32)
        b = part if b is None else b + part
    t = _sigmoid(ga_ref[...].astype(F32)) * a + _sigmoid(gb_ref[...].astype(F32)) * b
    x1 = x_ref[...] + jnp.dot(t.astype(BF16), wo_ref[...], preferred_element_type=F32)
    hn = _rms(x1, nmw_ref[...]).astype(BF16)
    acc = x1
    for c in range(D_FF // PROJ_BLOCK):
        sl = slice(c * PROJ_BLOCK, (c + 1) * PROJ_BLOCK)
        up = jnp.dot(hn, wup_ref[:, sl], preferred_element_type=F32)
        act = jnp.square(jnp.maximum(up, 0.0)).astype(BF16)
        acc = acc + jnp.dot(act, wdn_ref[sl, :], preferred_element_type=F32)
    y_ref[...] = _rms(acc, fw_ref[...])


def _tail(ys, hh, pbig, x, mnw, lnw, wa, wb, wo, nmw, wup, wdn, fw, tm):
    m = x.shape[0]
    tm = min(tm, m)
    rows = lambda i: (i, 0)
    const = lambda i: (0, 0)

    def resident(shape):
        return pl.BlockSpec(shape, const, pipeline_mode=pl.Buffered(1))

    return pl.pallas_call(
        _tail_kernel,
        grid=(m // tm,),
        in_specs=[
            pl.BlockSpec((tm, M_INNER), rows),
            pl.BlockSpec((tm, L_INNER), rows),
            pl.BlockSpec((tm, M_INNER), rows),
            pl.BlockSpec((tm, PROJ_BLOCK), lambda i: (i, 8)),
            pl.BlockSpec((tm, PROJ_BLOCK), lambda i: (i, 9)),
            pl.BlockSpec((tm, PROJ_BLOCK), lambda i: (i, 10)),
            pl.BlockSpec((tm, D_MODEL), rows),
            resident((1, M_INNER)),
            resident((1, L_INNER)),
            resident((M_INNER, D_MODEL)),
            resident((L_INNER, D_MODEL)),
            resident((D_MODEL, D_MODEL)),
            resident((1, D_MODEL)),
            resident((D_MODEL, D_FF)),
            resident((D_FF, D_MODEL)),
            resident((1, D_MODEL)),
        ],
        out_specs=pl.BlockSpec((tm, D_MODEL), rows),
        out_shape=jax.ShapeDtypeStruct((m, D_MODEL), F32),
        compiler_params=_params("parallel"),
        name="tail",
    )(ys, hh, pbig, pbig, pbig, pbig, x, mnw, lnw, wa, wb, wo, nmw, wup, wdn, fw)


def _sample_pre_kernel(xbc_ref, u_ref, sm_ref, cm0_ref, cm1_ref, cm2_ref, cl0_ref, cl1_ref, cl2_ref,
                       m_ref, cwm_ref, cbm_ref, cwl_ref, cbl_ref, wq_ref, wk_ref,
                       dtb_ref, alog_ref, ib_ref, fb_ref,
                       xc_ref, bcc_ref, q_ref, k_ref, dt_ref, da_ref, wr_ref, keep_ref, mnew_ref,
                       cmn_ref, cln_ref):
    xbc = xbc_ref[...].astype(F32)
    conv_m = (cbm_ref[...] + cwm_ref[0:1, :] * cm0_ref[...] + cwm_ref[1:2, :] * cm1_ref[...]
              + cwm_ref[2:3, :] * cm2_ref[...] + cwm_ref[3:4, :] * xbc)
    act = _silu(conv_m)
    xc_ref[...] = act[:, :M_INNER]
    bcc_ref[...] = act[:, M_INNER:]
    cmn_ref[0] = cm1_ref[...]
    cmn_ref[1] = cm2_ref[...]
    cmn_ref[2] = xbc

    u = u_ref[...].astype(F32)
    conv_l = (cbl_ref[...] + cwl_ref[0:1, :] * cl0_ref[...] + cwl_ref[1:2, :] * cl1_ref[...]
              + cwl_ref[2:3, :] * cl2_ref[...] + cwl_ref[3:4, :] * u)
    uc = _silu(conv_l)
    cln_ref[0] = cl1_ref[...]
    cln_ref[1] = cl2_ref[...]
    cln_ref[2] = u
    scale = L_HDIM ** -0.5
    for h in range(L_HEADS):
        sl = slice(h * L_HDIM, (h + 1) * L_HDIM)
        ub = uc[:, sl].astype(BF16)
        q_ref[:, sl] = jnp.dot(ub, wq_ref[h], preferred_element_type=F32)
        k_ref[:, sl] = jnp.dot(ub, wk_ref[h], preferred_element_type=F32) * scale

    sm = sm_ref[...]
    dt = _softplus(sm + dtb_ref[...])
    dt_ref[...] = dt
    da_ref[...] = jnp.exp(dt * (-jnp.exp(alog_ref[...])))
    ig = pltpu.roll(sm + ib_ref[...], F_LANE - I_LANE, axis=1)
    logf = _log_sigmoid(sm + fb_ref[...])
    m_old = m_ref[...]
    m_new = jnp.maximum(logf + m_old, ig)
    mnew_ref[...] = m_new
    wr_ref[...] = jnp.exp(ig - m_new)
    keep_ref[...] = jnp.exp(logf + m_old - m_new)


def _sample_pre(pbig, psmall, conv_m, conv_l, m_lanes, cwm, cbm, cwl, cbl, wq, wk, dtb, alog, ib, fb):
    s = pbig.shape[0]
    f = lambda shape: jax.ShapeDtypeStruct(shape, F32)
    full2 = lambda shape: pl.BlockSpec(shape, lambda i: (0, 0))
    full3 = lambda shape: pl.BlockSpec(shape, lambda i: (0, 0, 0))
    state_row = lambda width, j: pl.BlockSpec((None, s, width), lambda i, j=j: (j, 0, 0))
    return pl.pallas_call(
        _sample_pre_kernel,
        grid=(1,),
        in_specs=[
            full2((s, M_CONV)),
            pl.BlockSpec((s, L_INNER), lambda i: (0, 6)),
            full2((s, LANES)),
            state_row(M_CONV, 0), state_row(M_CONV, 1), state_row(M_CONV, 2),
            state_row(L_INNER, 0), state_row(L_INNER, 1), state_row(L_INNER, 2),
            full2((s, LANES)),
            full2((CONV_W, M_CONV)), full2((1, M_CONV)),
            full2((CONV_W, L_INNER)), full2((1, L_INNER)),
            full3((L_HEADS, L_HDIM, L_HDIM)), full3((L_HEADS, L_HDIM, L_HDIM)),
            full2((1, LANES)), full2((1, LANES)), full2((1, LANES)), full2((1, LANES)),
        ],
        out_specs=[
            full2((s, M_INNER)), full2((s, M_INNER)), full2((s, L_INNER)), full2((s, L_INNER)),
            full2((s, LANES)), full2((s, LANES)), full2((s, LANES)), full2((s, LANES)), full2((s, LANES)),
            full3((CONV_W - 1, s, M_CONV)), full3((CONV_W - 1, s, L_INNER)),
        ],
        out_shape=[
            f((s, M_INNER)), f((s, M_INNER)), f((s, L_INNER)), f((s, L_INNER)),
            f((s, LANES)), f((s, LANES)), f((s, LANES)), f((s, LANES)), f((s, LANES)),
            f((CONV_W - 1, s, M_CONV)), f((CONV_W - 1, s, L_INNER)),
        ],
        compiler_params=_params("arbitrary"),
        name="sample_pre",
    )(pbig[:, 2 * PROJ_BLOCK:6 * PROJ_BLOCK], pbig, psmall, conv_m, conv_m, conv_m,
      conv_l, conv_l, conv_l, m_lanes, cwm, cbm, cwl, cbl, wq, wk, dtb, alog, ib, fb)


def _sample_post_kernel(y_ref, xc_ref, dvec_ref,
                        q_ref, k_ref, v_ref, n_ref, num_ref, wr_ref, keep_ref, mnew_ref,
                        ys_ref, hh_ref, nn_ref):
    ys_ref[...] = (y_ref[...] + dvec_ref[...] * xc_ref[...]).astype(BF16)

    wr = wr_ref[...]
    keep = keep_ref[...]
    floor = jnp.exp(-mnew_ref[...])
    for h in range(L_HEADS):
        ln = F_LANE + h
        sl = slice(h * L_HDIM, (h + 1) * L_HDIM)
        q = q_ref[:, sl]
        k = k_ref[:, sl]
        n_old = n_ref[:, sl]
        wrc = wr[:, ln:ln + 1]
        kpc = keep[:, ln:ln + 1]
        wgt = jnp.sum(q * k, axis=1, keepdims=True) * wrc
        num = wgt * v_ref[:, sl].astype(F32) + kpc * num_ref[:, sl]
        den = wgt + kpc * jnp.sum(q * n_old, axis=1, keepdims=True)
        hh_ref[:, sl] = (num / jnp.maximum(jnp.abs(den), floor[:, ln:ln + 1])).astype(BF16)
        nn_ref[:, sl] = kpc * n_old + wrc * k


def _sample_post(y, xc, pbig, dvec, q, k, n_rows, num, wr, keep, mnew):
    s = y.shape[0]
    full = lambda shape: pl.BlockSpec(shape, lambda i: (0, 0))
    blk = lambda width, j: pl.BlockSpec((s, width), lambda i, j=j: (0, j))
    return pl.pallas_call(
        _sample_post_kernel,
        grid=(1,),
        in_specs=[
            full((s, M_INNER)), full((s, M_INNER)), full((1, M_INNER)),
            full((s, L_INNER)), full((s, L_INNER)), blk(L_INNER, 7),
            full((s, L_INNER)), full((s, L_INNER)),
            full((s, LANES)), full((s, LANES)), full((s, LANES)),
        ],
        out_specs=[full((s, M_INNER)), full((s, L_INNER)), full((s, L_INNER))],
        out_shape=[
            jax.ShapeDtypeStruct((s, M_INNER), BF16),
            jax.ShapeDtypeStruct((s, L_INNER), BF16),
            jax.ShapeDtypeStruct((s, L_INNER), F32),
        ],
        compiler_params=_params("arbitrary"),
        name="sample_post",
    )(y, xc, dvec, q, k, pbig, n_rows, num, wr, keep, mnew)


def _lanes(vec, first_lane):
    n = vec.shape[0]
    return jnp.pad(vec.astype(F32), (first_lane, LANES - first_lane - n)).reshape(1, LANES)


def _spread_matrix(first_lane, n_heads, width):
    r = lax.broadcasted_iota(jnp.int32, (2 * LANES, n_heads * width), 0) % LANES
    c = lax.broadcasted_iota(jnp.int32, (2 * LANES, n_heads * width), 1)
    return (r - first_lane == c // width).astype(BF16)


def _shift_matrix(n):
    r = lax.broadcasted_iota(jnp.int32, ((CONV_W - 1) * n, 2 * n), 0)
    c = lax.broadcasted_iota(jnp.int32, ((CONV_W - 1) * n, 2 * n), 1)
    return (c == n + r % n - (r // n + 1)).astype(BF16)


def kernel(x_prompt, x_sample, state_mamba_conv, state_mamba_ssm, state_mlstm_conv, state_mlstm_C, state_mlstm_n, state_mlstm_m, w_in, mamba_conv_w, mamba_conv_b, mamba_dt_bias, mamba_A_log, mamba_D, mamba_norm_w, w_branch_a, mlstm_conv_w, mlstm_conv_b, mlstm_wq, mlstm_wk, mlstm_i_bias, mlstm_f_bias, mlstm_norm_w, w_branch_b, w_out, norm_mix_w, norm_mlp_w, w_up, w_down, final_norm_w):
    depth = w_in.shape[0]
    assert depth == 1
    batch, seq, _ = x_prompt.shape
    n_samp, dec_seq, _ = x_sample.shape
    assert dec_seq == 1 and seq % CHUNK == 0 and seq >= SUBLANES
    assert batch % SCAN_NB == 0
    l = 0

    assert w_in.shape[2] == W_END
    w_big, w_small = _wprep(jnp.transpose(w_in[l]))
    g_mix = norm_mix_w[l].reshape(1, D_MODEL)
    cwm = mamba_conv_w[l]
    cbm = mamba_conv_b[l].reshape(1, M_CONV)
    cwl = mlstm_conv_w[l]
    cbl = mlstm_conv_b[l].reshape(1, L_INNER)
    dtb = _lanes(mamba_dt_bias[l], DT_LANE)
    alog = _lanes(mamba_A_log[l], DT_LANE)
    ib = _lanes(mlstm_i_bias[l], I_LANE)
    fb = _lanes(mlstm_f_bias[l], F_LANE)
    dvec = jnp.repeat(mamba_D[l].astype(F32), M_HDIM).reshape(1, M_INNER)
    mnw = mamba_norm_w[l].reshape(1, M_INNER)
    lnw = mlstm_norm_w[l].reshape(1, L_INNER)
    wq = mlstm_wq[l].astype(BF16)
    wk = mlstm_wk[l].astype(BF16)
    wa = w_branch_a[l].astype(BF16)
    wb = w_branch_b[l].astype(BF16)
    wo = w_out[l].astype(BF16)
    wup = w_up[l].astype(BF16)
    wdn = w_down[l].astype(BF16)
    nmw = norm_mlp_w[l].reshape(1, D_MODEL)
    fw = final_norm_w.reshape(1, D_MODEL)
    e64 = _spread_matrix(DT_LANE, M_HEADS, M_HDIM)
    el = _spread_matrix(F_LANE, L_HEADS, L_HDIM)
    shift = _shift_matrix(CHUNK)

    xp = x_prompt.reshape(batch * seq, D_MODEL)
    xs = x_sample.reshape(n_samp, D_MODEL)
    pbig, psmall, sbig, ssmall = _inproj(xp, xs, g_mix, w_big, w_small, tm=2048)

    m_lanes = jnp.pad(state_mlstm_m[l], ((0, 0), (F_LANE, LANES - F_LANE - L_HEADS)))
    (xc, bcc, q, k, dt, da, wr, keep, mnew, s_conv_m, s_conv_l) = _sample_pre(
        sbig, ssmall, jnp.transpose(state_mamba_conv[l], (1, 0, 2)),
        jnp.transpose(state_mlstm_conv[l], (1, 0, 2)), m_lanes,
        cwm, cbm, cwl, cbl, wq, wk, dtb, alog, ib, fb)
    samples = (
        dt[:, :M_HEADS], da[:, :M_HEADS],
        wr[:, F_LANE:F_LANE + L_HEADS], keep[:, F_LANE:F_LANE + L_HEADS], xc,
        bcc[:, :M_GROUPS * M_STATE].reshape(n_samp, M_GROUPS, M_STATE),
        bcc[:, M_GROUPS * M_STATE:].reshape(n_samp, M_GROUPS, M_STATE),
        q.reshape(n_samp * L_HEADS, L_HDIM), k.reshape(n_samp * L_HEADS, L_HDIM),
        sbig[:, 7 * PROJ_BLOCK:8 * PROJ_BLOCK].astype(F32).reshape(n_samp * L_HEADS, L_HDIM),
        state_mamba_ssm[l].reshape(n_samp, M_PAIRS, LANES, M_STATE), state_mlstm_C[l])

    ya, hb, p_ssm, p_c, p_n, p_m, s_ssm, s_c, y_rows, num_rows = _scan_prompt(
        pbig, psmall, cwm[:, :M_INNER], cbm[:, :M_INNER], cwm[:, M_INNER:], cbm[:, M_INNER:],
        cwl, cbl, dtb, alog, dvec, wq, wk, ib, fb, e64, el, shift, batch, seq, samples)
    y_prompt = _tail(ya.reshape(batch * seq, M_INNER), hb.reshape(batch * seq, L_INNER), pbig, xp,
                     mnw, lnw, wa, wb, wo, nmw, wup, wdn, fw, tm=512)
    p3 = pbig.reshape(batch, seq, BIG_WIDTH)
    p_conv_m = p3[:, seq - (CONV_W - 1):, 2 * PROJ_BLOCK:6 * PROJ_BLOCK].astype(F32)
    p_conv_l = p3[:, seq - (CONV_W - 1):, 6 * PROJ_BLOCK:7 * PROJ_BLOCK].astype(F32)
    p_ssm = p_ssm.reshape(batch, M_HEADS, M_HDIM, M_STATE)
    p_m = p_m[:, 0, F_LANE:F_LANE + L_HEADS]

    ya_s, hb_s, s_n = _sample_post(
        y_rows.reshape(n_samp, M_INNER), xc, sbig, dvec, q, k,
        state_mlstm_n[l].reshape(n_samp, L_INNER), num_rows.reshape(n_samp, L_INNER),
        wr, keep, mnew)
    y_sample = _tail(ya_s, hb_s, sbig, xs, mnw, lnw, wa, wb, wo, nmw, wup, wdn, fw, tm=n_samp)

    lead = lambda a: a[None]
    return (
        y_prompt.reshape(batch, seq, D_MODEL),
        y_sample.reshape(n_samp, 1, D_MODEL),
        lead(p_conv_m), lead(p_ssm), lead(p_conv_l), lead(p_c), lead(p_n), lead(p_m),
        lead(jnp.transpose(s_conv_m, (1, 0, 2))),
        lead(s_ssm.reshape(n_samp, M_HEADS, M_HDIM, M_STATE)),
        lead(jnp.transpose(s_conv_l, (1, 0, 2))),
        lead(s_c), lead(s_n.reshape(n_samp, L_HEADS, L_HDIM)), lead(mnew[:, F_LANE:F_LANE + L_HEADS]),
    )
```

```python
import jax
import jax.numpy as jnp
from jax import lax
from jax.experimental import pallas as pl
from jax.experimental.pallas import tpu as pltpu

F32 = jnp.float32
BF16 = jnp.bfloat16

D_MODEL = 1024
M_INNER = 2048
M_HEADS = 32
M_HDIM = 64
M_GROUPS = 8
M_PAIRS = M_HEADS // 2
M_STATE = 128
M_CONV = 4096
L_INNER = 1024
L_HEADS = 8
L_HDIM = 128
D_FF = 4096
CONV_W = 4
CHUNK = 128
EPS = 1e-6

LANES = 128
SUBLANES = 8
PROJ_BLOCK = 1024
N_PROJ_BLOCKS = 11
BIG_WIDTH = PROJ_BLOCK * N_PROJ_BLOCKS
DT_LANE = 0
I_LANE = 32
F_LANE = 40
VMEM_LIMIT = 56 * 1024 * 1024

LOG2_E = 1.4426950408889634
NT_DIMS = (((1,), (1,)), ((), ()))
TN_DIMS = (((0,), (0,)), ((), ()))


def _params(*sem):
    return pltpu.CompilerParams(dimension_semantics=sem, vmem_limit_bytes=VMEM_LIMIT)


def _sigmoid(x):
    return 0.5 * jnp.tanh(0.5 * x) + 0.5


def _silu(x):
    h = 0.5 * x
    return h + h * jnp.tanh(h)


def _log1p_exp_neg_abs(x):
    e = jnp.exp(-jnp.abs(x))
    u = 1.0 + e
    return jnp.where(u == 1.0, e, jnp.log(u) * (e / (u - 1.0)))


def _softplus(x):
    return jnp.maximum(x, 0.0) + _log1p_exp_neg_abs(x)


def _log_sigmoid(x):
    return jnp.minimum(x, 0.0) - _log1p_exp_neg_abs(x)


def _rms(x, w):
    return x * lax.rsqrt(jnp.mean(x * x, axis=-1, keepdims=True) + EPS) * w


def _lane_iota(shape):
    return lax.broadcasted_iota(jnp.int32, shape, len(shape) - 1)


def _tri(n):
    r = lax.broadcasted_iota(jnp.int32, (n, n), 0)
    c = lax.broadcasted_iota(jnp.int32, (n, n), 1)
    return r >= c


def _split3(a):
    hi = a.astype(BF16)
    r1 = a - hi.astype(F32)
    mid = r1.astype(BF16)
    lo = (r1 - mid.astype(F32)).astype(BF16)
    return hi, mid, lo


def _cumsum_rows(causal, a):
    tri01 = causal.astype(F32).astype(BF16)
    return jnp.dot(jnp.concatenate([tri01] * 3, axis=1), jnp.concatenate(_split3(a), axis=0),
                   preferred_element_type=F32)


def _spread_many(arrs, e2_ref):
    n = arrs[0].shape[0]
    pieces = [jnp.concatenate(_split3(a)[:2], axis=1) for a in arrs]
    out = jnp.dot(jnp.concatenate(pieces, axis=0), e2_ref[...], preferred_element_type=F32)
    return [out[i * n:(i + 1) * n] for i in range(len(arrs))]


def _spread(a, e2_ref):
    return _spread_many((a,), e2_ref)[0]


CONV_COLS = 512
BF16_ROWS = 16


def _conv_silu(x_ref, win_ref, shift_ref, w_ref, b_ref, out_ref):
    n, width = x_ref.shape
    win_ref[n:2 * n, :] = x_ref[...]
    for c0 in range(0, width, CONV_COLS):
        cs = slice(c0, c0 + CONV_COLS)
        sh = jnp.dot(shift_ref[...], win_ref[:, cs], preferred_element_type=F32)
        y = b_ref[:, cs] + w_ref[CONV_W - 1:CONV_W, cs] * x_ref[:, cs].astype(F32)
        for back in range(1, CONV_W):
            y = y + w_ref[CONV_W - 1 - back:CONV_W - back, cs] * sh[(back - 1) * n:back * n]
        out_ref[:, cs] = _silu(y).astype(out_ref.dtype)
    win_ref[n - BF16_ROWS:n, :] = x_ref[n - BF16_ROWS:n, :]


W_DT0 = M_INNER + M_CONV
W_U0 = W_DT0 + M_HEADS
W_I0 = W_U0 + 3 * L_INNER
W_G0 = W_I0 + 2 * L_HEADS
W_END = W_G0 + 2 * D_MODEL
W_FIRST_UVO = W_DT0 // PROJ_BLOCK
W_FIRST_GATE = W_FIRST_UVO + 3 * L_INNER // PROJ_BLOCK
W_SHIFT_UVO = W_U0 - W_DT0
W_SHIFT_GATE = W_G0 - (W_DT0 + 3 * L_INNER)
W_NEXT_ROWS = 64
assert W_DT0 % PROJ_BLOCK == 0 and DT_LANE == 0
assert (W_I0 - I_LANE) == W_FIRST_GATE * PROJ_BLOCK and F_LANE == I_LANE + L_HEADS
assert max(W_SHIFT_UVO, W_SHIFT_GATE) <= W_NEXT_ROWS and PROJ_BLOCK % W_NEXT_ROWS == 0
assert W_SHIFT_UVO % BF16_ROWS == 0 and W_SHIFT_GATE % BF16_ROWS == 0
assert M_HEADS % BF16_ROWS == 0 and I_LANE % BF16_ROWS == 0 and (2 * L_HEADS) % BF16_ROWS == 0


def _wprep_kernel(a_ref, b_ref, big_ref, small_ref):
    j = pl.program_id(0)

    def emit(shift):
        if shift == 0:
            big_ref[...] = a_ref[...].astype(BF16)
        else:
            big_ref[0:PROJ_BLOCK - shift, :] = a_ref[shift:PROJ_BLOCK, :].astype(BF16)
            big_ref[PROJ_BLOCK - shift:PROJ_BLOCK, :] = b_ref[0:shift, :].astype(BF16)

    @pl.when(j == 0)
    def _():
        small_ref[...] = jnp.zeros_like(small_ref)

    @pl.when(j < W_FIRST_UVO)
    def _():
        emit(0)

    @pl.when((j >= W_FIRST_UVO) & (j < W_FIRST_GATE))
    def _():
        emit(W_SHIFT_UVO)

    @pl.when(j >= W_FIRST_GATE)
    def _():
        emit(W_SHIFT_GATE)

    @pl.when(j == W_FIRST_UVO)
    def _():
        small_ref[DT_LANE:DT_LANE + M_HEADS, :] = a_ref[DT_LANE:DT_LANE + M_HEADS, :].astype(BF16)

    @pl.when(j == W_FIRST_GATE)
    def _():
        small_ref[I_LANE:I_LANE + 2 * L_HEADS, :] = a_ref[I_LANE:I_LANE + 2 * L_HEADS, :].astype(BF16)


def _wprep(w_t):
    return pl.pallas_call(
        _wprep_kernel,
        grid=(N_PROJ_BLOCKS,),
        in_specs=[
            pl.BlockSpec((PROJ_BLOCK, D_MODEL), lambda j: (j, 0)),
            pl.BlockSpec((W_NEXT_ROWS, D_MODEL), lambda j: ((j + 1) * (PROJ_BLOCK // W_NEXT_ROWS), 0)),
        ],
        out_specs=[
            pl.BlockSpec((PROJ_BLOCK, D_MODEL), lambda j: (j, 0)),
            pl.BlockSpec((LANES, D_MODEL), lambda j: (0, 0)),
        ],
        out_shape=[
            jax.ShapeDtypeStruct((BIG_WIDTH, D_MODEL), BF16),
            jax.ShapeDtypeStruct((LANES, D_MODEL), BF16),
        ],
        compiler_params=_params("arbitrary"),
        name="w_prep",
    )(w_t, w_t)


def _inproj_kernel(x_ref, xs_ref, g_ref, w_ref, ws_ref, o_ref, os_ref, so_ref, sos_ref, xn_ref, xsn_ref):
    i = pl.program_id(0)
    j = pl.program_id(1)

    @pl.when(j == 0)
    def _():
        xn_ref[...] = _rms(x_ref[...], g_ref[...]).astype(BF16)
        os_ref[...] = lax.dot_general(xn_ref[...], ws_ref[...], NT_DIMS, preferred_element_type=F32)

    @pl.when((i == 0) & (j == 0))
    def _():
        xsn_ref[...] = _rms(xs_ref[...], g_ref[...]).astype(BF16)
        sos_ref[...] = lax.dot_general(xsn_ref[...], ws_ref[...], NT_DIMS, preferred_element_type=F32)

    @pl.when(i == 0)
    def _():
        so_ref[...] = lax.dot_general(
            xsn_ref[...], w_ref[...], NT_DIMS, preferred_element_type=F32).astype(BF16)

    o_ref[...] = lax.dot_general(
        xn_ref[...], w_ref[...], NT_DIMS, preferred_element_type=F32).astype(BF16)


def _inproj(x, xs, g, w_big, w_small, tm):
    m = x.shape[0]
    s = xs.shape[0]
    tm = min(tm, m)
    last = N_PROJ_BLOCKS - 1
    sample_block = lambda i, j: (0, jnp.where(i == 0, j, last))
    return pl.pallas_call(
        _inproj_kernel,
        grid=(m // tm, N_PROJ_BLOCKS),
        in_specs=[
            pl.BlockSpec((tm, D_MODEL), lambda i, j: (i, 0)),
            pl.BlockSpec((s, D_MODEL), lambda i, j: (0, 0)),
            pl.BlockSpec((1, D_MODEL), lambda i, j: (0, 0)),
            pl.BlockSpec((PROJ_BLOCK, D_MODEL), lambda i, j: (j, 0)),
            pl.BlockSpec((LANES, D_MODEL), lambda i, j: (0, 0)),
        ],
        out_specs=[
            pl.BlockSpec((tm, PROJ_BLOCK), lambda i, j: (i, j)),
            pl.BlockSpec((tm, LANES), lambda i, j: (i, 0)),
            pl.BlockSpec((s, PROJ_BLOCK), sample_block),
            pl.BlockSpec((s, LANES), lambda i, j: (0, 0)),
        ],
        out_shape=[
            jax.ShapeDtypeStruct((m, BIG_WIDTH), BF16),
            jax.ShapeDtypeStruct((m, LANES), F32),
            jax.ShapeDtypeStruct((s, BIG_WIDTH), BF16),
            jax.ShapeDtypeStruct((s, LANES), F32),
        ],
        scratch_shapes=[pltpu.VMEM((tm, D_MODEL), BF16), pltpu.VMEM((s, D_MODEL), BF16)],
        compiler_params=_params("arbitrary", "arbitrary"),
        name="in_proj",
    )(x, xs, g, w_big, w_small)


SCAN_NB = 2


def _ssd_chunk(x_ref, bc_ref, sm, cwx_ref, cbx_ref, cwbc_ref, cbbc_ref, dtb_ref, alog_ref, dvec_ref,
               e64_ref, shift_ref, y_ref, h_ref, winx_ref, winbc_ref, xc_scr, bcc_scr):
    n = x_ref.shape[0]
    _conv_silu(x_ref, winx_ref, shift_ref, cwx_ref, cbx_ref, xc_scr)
    _conv_silu(bc_ref, winbc_ref, shift_ref, cwbc_ref, cbbc_ref, bcc_scr)
    xc = xc_scr[...]
    bcc = bcc_scr[...]

    lane = _lane_iota((n, LANES))
    dt = _softplus(sm + dtb_ref[...])
    da = jnp.where(lane < M_HEADS, dt * (-LOG2_E * jnp.exp(alog_ref[...])), 0.0)
    causal = _tri(n)
    acum = _cumsum_rows(causal, da)
    acum_t = acum.T
    last = acum[n - 1:n, :]
    exp_last = jnp.exp2(last)
    dt_x, wr_x, ea_x = _spread_many((dt, jnp.exp2(last - acum), jnp.exp2(acum)), e64_ref)

    xdt = xc * dt_x
    xw = (xdt * wr_x).astype(BF16)
    xb = xdt.astype(BF16)
    low_half = jnp.bitwise_and(_lane_iota((n, M_INNER)), LANES - 1) < M_HDIM
    zero = jnp.zeros((), BF16)
    x_lo = jnp.where(low_half, xb, zero)
    x_hi = jnp.where(low_half, zero, xb)
    first_rows = lax.broadcasted_iota(jnp.int32, (LANES, LANES), 0) < M_HDIM

    def weights(cb, h):
        seg = acum[:, h:h + 1] - acum_t[h:h + 1, :]
        return (cb * jnp.exp2(jnp.where(causal, seg, -jnp.inf))).astype(BF16)

    pairs_per_group = M_PAIRS // M_GROUPS
    groups = {}

    def group(g):
        if g not in groups:
            bg = bcc[:, g * M_STATE:(g + 1) * M_STATE]
            cg = bcc[:, (M_GROUPS + g) * M_STATE:(M_GROUPS + g + 1) * M_STATE]
            groups[g] = (bg, cg, lax.dot_general(cg, bg, NT_DIMS, preferred_element_type=F32))
        return groups[g]

    def pair_weights(hp):
        cb = group(hp // pairs_per_group)[2]
        return weights(cb, 2 * hp), weights(cb, 2 * hp + 1)

    w_next = pair_weights(0)
    for hp in range(M_PAIRS):
        bg, cg, _ = group(hp // pairs_per_group)
        w0, w1 = w_next
        if hp + 1 < M_PAIRS:
            w_next = pair_weights(hp + 1)
        h0, h1 = 2 * hp, 2 * hp + 1
        sl = slice(hp * LANES, (hp + 1) * LANES)
        hs = h_ref[hp]
        y = jnp.dot(w0, x_lo[:, sl], preferred_element_type=F32)
        y = y + jnp.dot(w1, x_hi[:, sl], preferred_element_type=F32)
        ys = lax.dot_general(cg, hs.astype(BF16), NT_DIMS, preferred_element_type=F32)
        y_ref[:, sl] = (y + ea_x[:, sl] * ys + dvec_ref[:, sl] * xc[:, sl]).astype(BF16)
        el = jnp.where(first_rows, exp_last[:, h0:h0 + 1], exp_last[:, h1:h1 + 1])
        h_ref[hp] = el * hs + lax.dot_general(xw[:, sl], bg, TN_DIMS, preferred_element_type=F32)


def _mlstm_chunks(seqs, wq_ref, wk_ref, ib_ref, fb_ref, el_ref):
    n = seqs[0][0].shape[0]
    causal = _tri(n)
    lane = _lane_iota((n, LANES))
    gate_lanes = (lane >= F_LANE) & (lane < F_LANE + L_HEADS)
    scale = L_HDIM ** -0.5
    lanes_of = [slice(h * L_HDIM, (h + 1) * L_HDIM) for h in range(L_HEADS)]

    work = [(i, h) for i in range(len(seqs)) for h in range(L_HEADS)]
    c_olds = {(i, h): seqs[i][4][h] for i, h in work}
    n_olds = {(i, h): seqs[i][5][h:h + 1, :] for i, h in work}
    qs, ks, qbs, kbs = {}, {}, {}, {}
    for i, h in work:
        ub = seqs[i][0][:, lanes_of[h]]
        qs[i, h] = jnp.dot(ub, wq_ref[h], preferred_element_type=F32)
        ks[i, h] = jnp.dot(ub, wk_ref[h], preferred_element_type=F32) * scale

    gates = []
    for uc, v_ref, sm, hh_ref, c_ref, n_ref, m_ref in seqs:
        ig = pltpu.roll(sm + ib_ref[...], F_LANE - I_LANE, axis=1)
        logf = jnp.where(gate_lanes, _log_sigmoid(sm + fb_ref[...]), 0.0)
        bcum = _cumsum_rows(causal, logf)
        m_old = m_ref[...]
        bl = bcum[n - 1:n, :]
        src = bl - bcum + ig
        m_new = jnp.maximum(bl + m_old, jnp.max(src, axis=0, keepdims=True))
        gates.append(dict(
            bcum=bcum, bcum_t=bcum.T, ig_t=ig.T, m_old=m_old, m_new=m_new,
            keep=jnp.exp(bl + m_old - m_new),
            wr_x=_spread(jnp.exp(src - m_new), el_ref)))

    logds, inters, rowmax = {}, {}, {}
    for i, h in work:
        g = gates[i]
        ln = F_LANE + h
        bcol = jnp.broadcast_to(g["bcum"][:, ln:ln + 1], (n, n))
        logd = jnp.where(causal, bcol - g["bcum_t"][ln:ln + 1, :] + g["ig_t"][ln:ln + 1, :], -jnp.inf)
        logds[i, h] = logd
        inters[i, h] = bcol + g["m_old"][:, ln:ln + 1]
        rowmax[i, h] = jnp.max(logd, axis=1, keepdims=True)
    ss, scs, floors = {}, {}, {}
    for i, h in work:
        qbs[i, h] = qs[i, h].astype(BF16)
        kbs[i, h] = ks[i, h].astype(BF16)
        m_s = jnp.maximum(inters[i, h], rowmax[i, h])
        dm = jnp.exp(logds[i, h] - m_s)
        scs[i, h] = jnp.exp(inters[i, h] - m_s)
        floors[i, h] = jnp.exp(-m_s)
        ss[i, h] = lax.dot_general(qbs[i, h], kbs[i, h], NT_DIMS, preferred_element_type=F32) * dm
    nums, dens = {}, {}
    for i, h in work:
        vb = seqs[i][1][:, lanes_of[h]]
        num = jnp.dot(ss[i, h].astype(BF16), vb, preferred_element_type=F32)
        nums[i, h] = num + scs[i, h] * jnp.dot(
            qbs[i, h], c_olds[i, h].astype(BF16), preferred_element_type=F32)
        dens[i, h] = (jnp.sum(ss[i, h], axis=1, keepdims=True)
                      + scs[i, h] * jnp.sum(qs[i, h] * n_olds[i, h], axis=1, keepdims=True))
    for i, h in work:
        hh = nums[i, h] / jnp.maximum(jnp.abs(dens[i, h]), floors[i, h])
        seqs[i][3][:, lanes_of[h]] = hh.astype(BF16)
    for i, h in work:
        g = gates[i]
        ln = F_LANE + h
        kw = ks[i, h] * g["wr_x"][:, lanes_of[h]]
        kp = g["keep"][:, ln:ln + 1]
        seqs[i][4][h] = kp * c_olds[i, h] + lax.dot_general(
            kw.astype(BF16), seqs[i][1][:, lanes_of[h]], TN_DIMS, preferred_element_type=F32)
        seqs[i][5][h:h + 1, :] = kp * n_olds[i, h] + jnp.sum(kw, axis=0, keepdims=True)
    for i in range(len(seqs)):
        seqs[i][6][...] = gates[i]["m_new"]


def _diag_rows(row, diag):
    return jnp.where(diag, jnp.broadcast_to(row, diag.shape), 0.0).astype(BF16)


def _state_spreads(x_ref, q_ref, k_ref, xs_scr, qs_scr, ks_scr):
    ns = x_ref.shape[0]
    rows = lax.broadcasted_iota(jnp.int32, (LANES, LANES), 0)
    diag = rows == _lane_iota((LANES, LANES))
    ones = jnp.ones((LANES, LANES), BF16)
    ones2 = jnp.ones((2 * LANES, LANES), BF16)
    for s in range(ns):
        for hp in range(M_PAIRS):
            xrow = x_ref[s:s + 1, hp * LANES:(hp + 1) * LANES]
            xs_scr[s * M_PAIRS + hp] = jnp.dot(_diag_rows(xrow, diag), ones, preferred_element_type=F32)
    for r in range(ns * L_HEADS):
        for src_ref, dst_scr in ((q_ref, qs_scr), (k_ref, ks_scr)):
            row = src_ref[r:r + 1, :]
            hi = row.astype(BF16).astype(F32)
            lhs = jnp.concatenate([_diag_rows(hi, diag), _diag_rows(row - hi, diag)], axis=1)
            dst_scr[r] = jnp.dot(lhs, ones2, preferred_element_type=F32)


def _state_updates(base, dt_ref, da_ref, wr_ref, keep_ref, b_ref, c_ref, v_ref, h_ref, cst_ref,
                   hn_ref, cn_ref, y_ref, num_ref, xs_scr, qs_scr, ks_scr):
    ns = b_ref.shape[0]
    rows = lax.broadcasted_iota(jnp.int32, (LANES, LANES), 0)
    lanes = _lane_iota((LANES, LANES))
    first_rows = rows < M_HDIM
    ones = jnp.ones((LANES, LANES), BF16)
    pairs_per_group = M_PAIRS // M_GROUPS
    acc = jnp.zeros((LANES, LANES), F32)
    for s in range(ns):
        for hp in range(M_PAIRS):
            g = hp // pairs_per_group
            h0, h1 = 2 * hp, 2 * hp + 1
            brow = b_ref[s, g:g + 1, :]
            crow = c_ref[s, g:g + 1, :]
            dav = jnp.where(first_rows, da_ref[base + s, h0], da_ref[base + s, h1])
            dtv = jnp.where(first_rows, dt_ref[base + s, h0], dt_ref[base + s, h1])
            hn = dav * h_ref[s, hp] + (dtv * xs_scr[s * M_PAIRS + hp]) * brow
            hn_ref[s, hp] = hn
            ysum = jnp.dot((hn * crow).astype(BF16), ones, preferred_element_type=F32)
            acc = jnp.where(lanes == s * M_PAIRS + hp, ysum, acc)
    y_ref[...] = acc.T[0:ns * M_PAIRS, :]
    for s in range(ns):
        for h in range(L_HEADS):
            r = s * L_HEADS + h
            c_old = cst_ref[s, h]
            num_ref[r:r + 1, :] = jnp.sum(qs_scr[r] * c_old, axis=0, keepdims=True)
            cn_ref[s, h] = (keep_ref[base + s, h] * c_old
                            + (ks_scr[r] * wr_ref[base + s, h]) * v_ref[r:r + 1, :])


def _scan_kernel(dt_ref, da_ref, wr_ref, keep_ref, sx_ref, sb_ref, sc_ref, sq_ref, sk_ref, sv_ref,
                 sh_ref, scst_ref, x_ref, bc_ref, u_ref, v_ref, sm_ref,
                 cwx_ref, cbx_ref, cwbc_ref, cbbc_ref, cwl_ref, cbl_ref, dtb_ref, alog_ref, dvec_ref,
                 wq_ref, wk_ref, ib_ref, fb_ref, e64_ref, el_ref, shift_ref,
                 y_ref, hh_ref, h_ref, c_ref, n_ref, m_ref, shn_ref, scn_ref, sy_ref, snum_ref,
                 winx_ref, winbc_ref, winu_ref, xc_scr, bcc_scr, uc_scr, xs_scr, qs_scr, ks_scr):
    n = x_ref.shape[1]
    step = pl.program_id(0) * pl.num_programs(1) + pl.program_id(1)

    @pl.when(pl.program_id(1) == 0)
    def _():
        h_ref[...] = jnp.zeros_like(h_ref)
        c_ref[...] = jnp.zeros_like(c_ref)
        n_ref[...] = jnp.zeros_like(n_ref)
        m_ref[...] = jnp.zeros_like(m_ref)
        winx_ref[:, 0:n, :] = jnp.zeros((SCAN_NB, n, M_INNER), BF16)
        winbc_ref[:, 0:n, :] = jnp.zeros((SCAN_NB, n, M_INNER), BF16)
        winu_ref[:, 0:n, :] = jnp.zeros((SCAN_NB, n, L_INNER), BF16)

    seqs = []
    for i in range(SCAN_NB):
        _ssd_chunk(x_ref.at[i], bc_ref.at[i], sm_ref[i], cwx_ref, cbx_ref, cwbc_ref, cbbc_ref,
                   dtb_ref, alog_ref, dvec_ref, e64_ref, shift_ref, y_ref.at[i], h_ref.at[i],
                   winx_ref.at[i], winbc_ref.at[i], xc_scr, bcc_scr)
        _conv_silu(u_ref.at[i], winu_ref.at[i], shift_ref, cwl_ref, cbl_ref, uc_scr.at[i])
        seqs.append((uc_scr.at[i], v_ref.at[i], sm_ref[i], hh_ref.at[i], c_ref.at[i], n_ref.at[i],
                     m_ref.at[i]))
        if i == 0:
            _state_spreads(sx_ref, sq_ref, sk_ref, xs_scr, qs_scr, ks_scr)
    _state_updates(step * sb_ref.shape[0], dt_ref, da_ref, wr_ref, keep_ref, sb_ref, sc_ref, sv_ref,
                   sh_ref, scst_ref, shn_ref, scn_ref, sy_ref, snum_ref, xs_scr, qs_scr, ks_scr)
    _mlstm_chunks(seqs, wq_ref, wk_ref, ib_ref, fb_ref, el_ref)


def _scan_prompt(pbig, psmall, cwx, cbx, cwbc, cbbc, cwl, cbl, dtb, alog, dvec, wq, wk, ib, fb,
                 e64, el, shift, batch, seq, samples):
    p3 = pbig.reshape(batch, seq, BIG_WIDTH)
    s3 = psmall.reshape(batch, seq, LANES)
    nb = SCAN_NB
    n_chunks = seq // CHUNK
    n_steps = (batch // nb) * n_chunks
    s_dt, s_da, s_wr, s_keep, s_x, s_b, s_c, s_q, s_k, s_v, s_h, s_cst = samples
    n_samp = s_h.shape[0]
    ns = n_samp // n_steps
    assert ns * n_steps == n_samp and ns * M_PAIRS <= LANES and (ns * L_HEADS) % SUBLANES == 0
    smem = pl.BlockSpec(memory_space=pltpu.SMEM)
    step = lambda b, c: b * n_chunks + c
    srows = lambda n_rows: pl.BlockSpec((n_rows, LANES), lambda b, c: (step(b, c), 0))
    sgrp = pl.BlockSpec((ns, M_GROUPS, M_STATE), lambda b, c: (step(b, c), 0, 0))
    sstate = lambda d1: pl.BlockSpec((ns, d1, LANES, LANES), lambda b, c: (step(b, c), 0, 0, 0))
    const2 = lambda b, c: (0, 0)
    const3 = lambda b, c: (0, 0, 0)
    rows = lambda width, col: pl.BlockSpec((nb, CHUNK, width), lambda b, c, col=col: (b, c, col))
    state4 = lambda d1, d2, d3: pl.BlockSpec((nb, d1, d2, d3), lambda b, c: (b, 0, 0, 0))
    return pl.pallas_call(
        _scan_kernel,
        grid=(batch // nb, seq // CHUNK),
        in_specs=[
            smem, smem, smem, smem,
            pl.BlockSpec((None, ns, M_INNER), lambda b, c: (step(b, c), 0, 0)),
            sgrp, sgrp, srows(ns * L_HEADS), srows(ns * L_HEADS), srows(ns * L_HEADS),
            sstate(M_PAIRS), sstate(L_HEADS),
            rows(M_INNER, 1),
            rows(M_INNER, 2),
            rows(L_INNER, 6),
            rows(L_INNER, 7),
            rows(LANES, 0),
            pl.BlockSpec((CONV_W, M_INNER), const2), pl.BlockSpec((1, M_INNER), const2),
            pl.BlockSpec((CONV_W, M_INNER), const2), pl.BlockSpec((1, M_INNER), const2),
            pl.BlockSpec((CONV_W, L_INNER), const2), pl.BlockSpec((1, L_INNER), const2),
            pl.BlockSpec((1, LANES), const2), pl.BlockSpec((1, LANES), const2),
            pl.BlockSpec((1, M_INNER), const2),
            pl.BlockSpec((L_HEADS, L_HDIM, L_HDIM), const3),
            pl.BlockSpec((L_HEADS, L_HDIM, L_HDIM), const3),
            pl.BlockSpec((1, LANES), const2), pl.BlockSpec((1, LANES), const2),
            pl.BlockSpec((2 * LANES, M_INNER), const2),
            pl.BlockSpec((2 * LANES, L_INNER), const2),
            pl.BlockSpec(((CONV_W - 1) * CHUNK, 2 * CHUNK), const2),
        ],
        out_specs=[
            rows(M_INNER, 0),
            rows(L_INNER, 0),
            state4(M_PAIRS, LANES, M_STATE),
            state4(L_HEADS, L_HDIM, L_HDIM),
            pl.BlockSpec((nb, L_HEADS, L_HDIM), lambda b, c: (b, 0, 0)),
            pl.BlockSpec((nb, 1, LANES), lambda b, c: (b, 0, 0)),
            sstate(M_PAIRS), sstate(L_HEADS), srows(ns * M_PAIRS), srows(ns * L_HEADS),
        ],
        out_shape=[
            jax.ShapeDtypeStruct((batch, seq, M_INNER), BF16),
            jax.ShapeDtypeStruct((batch, seq, L_INNER), BF16),
            jax.ShapeDtypeStruct((batch, M_PAIRS, LANES, M_STATE), F32),
            jax.ShapeDtypeStruct((batch, L_HEADS, L_HDIM, L_HDIM), F32),
            jax.ShapeDtypeStruct((batch, L_HEADS, L_HDIM), F32),
            jax.ShapeDtypeStruct((batch, 1, LANES), F32),
            jax.ShapeDtypeStruct(s_h.shape, F32),
            jax.ShapeDtypeStruct(s_cst.shape, F32),
            jax.ShapeDtypeStruct((n_samp * M_PAIRS, LANES), F32),
            jax.ShapeDtypeStruct((n_samp * L_HEADS, LANES), F32),
        ],
        scratch_shapes=[
            pltpu.VMEM((nb, 2 * CHUNK, M_INNER), BF16),
            pltpu.VMEM((nb, 2 * CHUNK, M_INNER), BF16),
            pltpu.VMEM((nb, 2 * CHUNK, L_INNER), BF16),
            pltpu.VMEM((CHUNK, M_INNER), F32),
            pltpu.VMEM((CHUNK, M_INNER), BF16),
            pltpu.VMEM((nb, CHUNK, L_INNER), BF16),
            pltpu.VMEM((ns * M_PAIRS, LANES, LANES), F32),
            pltpu.VMEM((ns * L_HEADS, LANES, LANES), F32),
            pltpu.VMEM((ns * L_HEADS, LANES, LANES), F32),
        ],
        compiler_params=_params("parallel", "arbitrary"),
        name="scan_prompt",
    )(s_dt, s_da, s_wr, s_keep, s_x.reshape(n_steps, ns, M_INNER), s_b, s_c, s_q, s_k, s_v, s_h, s_cst,
      p3, p3, p3, p3, s3, cwx, cbx, cwbc, cbbc, cwl, cbl, dtb, alog, dvec, wq, wk, ib, fb,
      e64, el, shift)


def _tail_kernel(ys_ref, hh_ref, z_ref, o_ref, ga_ref, gb_ref, x_ref, mnw_ref, lnw_ref,
                 wa_ref, wb_ref, wo_ref, nmw_ref, wup_ref, wdn_ref, fw_ref, y_ref):
    gw = M_INNER // M_GROUPS
    a = None
    for g in range(M_GROUPS):
        sl = slice(g * gw, (g + 1) * gw)
        yg = ys_ref[:, sl].astype(F32) * _silu(z_ref[:, sl].astype(F32))
        part = jnp.dot(_rms(yg, mnw_ref[:, sl]).astype(BF16), wa_ref[sl, :], preferred_element_type=F32)
        a = part if a is None else a + part
    b = None
    for hp in range(L_HEADS // 2):
        pieces = []
        for h in (2 * hp, 2 * hp + 1):
            sl = slice(h * L_HDIM, (h + 1) * L_HDIM)
            gate = _sigmoid(o_ref[:, sl].astype(F32))
            pieces.append((gate * _rms(hh_ref[:, sl].astype(F32), lnw_ref[:, sl])).astype(BF16))
        rows = slice(2 * hp * L_HDIM, (2 * hp + 2) * L_HDIM)
        part = jnp.dot(jnp.concatenate(pieces, axis=1), wb_ref[rows, :], preferred_element_type=F32)
        b = part if b is None else b + part
    t = _sigmoid(ga_ref[...].astype(F32)) * a + _sigmoid(gb_ref[...].astype(F32)) * b
    x1 = x_ref[...] + jnp.dot(t.astype(BF16), wo_ref[...], preferred_element_type=F32)
    hn = _rms(x1, nmw_ref[...]).astype(BF16)
    acc = x1
    for c in range(D_FF // PROJ_BLOCK):
        sl = slice(c * PROJ_BLOCK, (c + 1) * PROJ_BLOCK)
        up = jnp.dot(hn, wup_ref[:, sl], preferred_element_type=F32)
        act = jnp.square(jnp.maximum(up, 0.0)).astype(BF16)
        acc = acc + jnp.dot(act, wdn_ref[sl, :], preferred_element_type=F32)
    y_ref[...] = _rms(acc, fw_ref[...])


def _tail(ys, hh, pbig, x, mnw, lnw, wa, wb, wo, nmw, wup, wdn, fw, tm):
    m = x.shape[0]
    tm = min(tm, m)
    rows = lambda i: (i, 0)
    const = lambda i: (0, 0)

    def resident(shape):
        return pl.BlockSpec(shape, const, pipeline_mode=pl.Buffered(1))

    return pl.pallas_call(
        _tail_kernel,
        grid=(m // tm,),
        in_specs=[
            pl.BlockSpec((tm, M_INNER), rows),
            pl.BlockSpec((tm, L_INNER), rows),
            pl.BlockSpec((tm, M_INNER), rows),
            pl.BlockSpec((tm, PROJ_BLOCK), lambda i: (i, 8)),
            pl.BlockSpec((tm, PROJ_BLOCK), lambda i: (i, 9)),
            pl.BlockSpec((tm, PROJ_BLOCK), lambda i: (i, 10)),
            pl.BlockSpec((tm, D_MODEL), rows),
            resident((1, M_INNER)),
            resident((1, L_INNER)),
            resident((M_INNER, D_MODEL)),
            resident((L_INNER, D_MODEL)),
            resident((D_MODEL, D_MODEL)),
            resident((1, D_MODEL)),
            resident((D_MODEL, D_FF)),
            resident((D_FF, D_MODEL)),
            resident((1, D_MODEL)),
        ],
        out_specs=pl.BlockSpec((tm, D_MODEL), rows),
        out_shape=jax.ShapeDtypeStruct((m, D_MODEL), F32),
        compiler_params=_params("parallel"),
        name="tail",
    )(ys, hh, pbig, pbig, pbig, pbig, x, mnw, lnw, wa, wb, wo, nmw, wup, wdn, fw)


def _sample_pre_kernel(xbc_ref, u_ref, sm_ref, cm0_ref, cm1_ref, cm2_ref, cl0_ref, cl1_ref, cl2_ref,
                       m_ref, cwm_ref, cbm_ref, cwl_ref, cbl_ref, wq_ref, wk_ref,
                       dtb_ref, alog_ref, ib_ref, fb_ref,
                       xc_ref, bcc_ref, q_ref, k_ref, dt_ref, da_ref, wr_ref, keep_ref, mnew_ref,
                       cmn_ref, cln_ref):
    xbc = xbc_ref[...].astype(F32)
    conv_m = (cbm_ref[...] + cwm_ref[0:1, :] * cm0_ref[...] + cwm_ref[1:2, :] * cm1_ref[...]
              + cwm_ref[2:3, :] * cm2_ref[...] + cwm_ref[3:4, :] * xbc)
    act = _silu(conv_m)
    xc_ref[...] = act[:, :M_INNER]
    bcc_ref[...] = act[:, M_INNER:]
    cmn_ref[0] = cm1_ref[...]
    cmn_ref[1] = cm2_ref[...]
    cmn_ref[2] = xbc

    u = u_ref[...].astype(F32)
    conv_l = (cbl_ref[...] + cwl_ref[0:1, :] * cl0_ref[...] + cwl_ref[1:2, :] * cl1_ref[...]
              + cwl_ref[2:3, :] * cl2_ref[...] + cwl_ref[3:4, :] * u)
    uc = _silu(conv_l)
    cln_ref[0] = cl1_ref[...]
    cln_ref[1] = cl2_ref[...]
    cln_ref[2] = u
    scale = L_HDIM ** -0.5
    for h in range(L_HEADS):
        sl = slice(h * L_HDIM, (h + 1) * L_HDIM)
        ub = uc[:, sl].astype(BF16)
        q_ref[:, sl] = jnp.dot(ub, wq_ref[h], preferred_element_type=F32)
        k_ref[:, sl] = jnp.dot(ub, wk_ref[h], preferred_element_type=F32) * scale

    sm = sm_ref[...]
    dt = _softplus(sm + dtb_ref[...])
    dt_ref[...] = dt
    da_ref[...] = jnp.exp(dt * (-jnp.exp(alog_ref[...])))
    ig = pltpu.roll(sm + ib_ref[...], F_LANE - I_LANE, axis=1)
    logf = _log_sigmoid(sm + fb_ref[...])
    m_old = m_ref[...]
    m_new = jnp.maximum(logf + m_old, ig)
    mnew_ref[...] = m_new
    wr_ref[...] = jnp.exp(ig - m_new)
    keep_ref[...] = jnp.exp(logf + m_old - m_new)


def _sample_pre(pbig, psmall, conv_m, conv_l, m_lanes, cwm, cbm, cwl, cbl, wq, wk, dtb, alog, ib, fb):
    s = pbig.shape[0]
    f = lambda shape: jax.ShapeDtypeStruct(shape, F32)
    full2 = lambda shape: pl.BlockSpec(shape, lambda i: (0, 0))
    full3 = lambda shape: pl.BlockSpec(shape, lambda i: (0, 0, 0))
    state_row = lambda width, j: pl.BlockSpec((None, s, width), lambda i, j=j: (j, 0, 0))
    return pl.pallas_call(
        _sample_pre_kernel,
        grid=(1,),
        in_specs=[
            full2((s, M_CONV)),
            pl.BlockSpec((s, L_INNER), lambda i: (0, 6)),
            full2((s, LANES)),
            state_row(M_CONV, 0), state_row(M_CONV, 1), state_row(M_CONV, 2),
            state_row(L_INNER, 0), state_row(L_INNER, 1), state_row(L_INNER, 2),
            full2((s, LANES)),
            full2((CONV_W, M_CONV)), full2((1, M_CONV)),
            full2((CONV_W, L_INNER)), full2((1, L_INNER)),
            full3((L_HEADS, L_HDIM, L_HDIM)), full3((L_HEADS, L_HDIM, L_HDIM)),
            full2((1, LANES)), full2((1, LANES)), full2((1, LANES)), full2((1, LANES)),
        ],
        out_specs=[
            full2((s, M_INNER)), full2((s, M_INNER)), full2((s, L_INNER)), full2((s, L_INNER)),
            full2((s, LANES)), full2((s, LANES)), full2((s, LANES)), full2((s, LANES)), full2((s, LANES)),
            full3((CONV_W - 1, s, M_CONV)), full3((CONV_W - 1, s, L_INNER)),
        ],
        out_shape=[
            f((s, M_INNER)), f((s, M_INNER)), f((s, L_INNER)), f((s, L_INNER)),
            f((s, LANES)), f((s, LANES)), f((s, LANES)), f((s, LANES)), f((s, LANES)),
            f((CONV_W - 1, s, M_CONV)), f((CONV_W - 1, s, L_INNER)),
        ],
        compiler_params=_params("arbitrary"),
        name="sample_pre",
    )(pbig[:, 2 * PROJ_BLOCK:6 * PROJ_BLOCK], pbig, psmall, conv_m, conv_m, conv_m,
      conv_l, conv_l, conv_l, m_lanes, cwm, cbm, cwl, cbl, wq, wk, dtb, alog, ib, fb)


def _sample_post_kernel(y_ref, xc_ref, dvec_ref,
                        q_ref, k_ref, v_ref, n_ref, num_ref, wr_ref, keep_ref, mnew_ref,
                        ys_ref, hh_ref, nn_ref):
    ys_ref[...] = (y_ref[...] + dvec_ref[...] * xc_ref[...]).astype(BF16)

    wr = wr_ref[...]
    keep = keep_ref[...]
    floor = jnp.exp(-mnew_ref[...])
    for h in range(L_HEADS):
        ln = F_LANE + h
        sl = slice(h * L_HDIM, (h + 1) * L_HDIM)
        q = q_ref[:, sl]
        k = k_ref[:, sl]
        n_old = n_ref[:, sl]
        wrc = wr[:, ln:ln + 1]
        kpc = keep[:, ln:ln + 1]
        wgt = jnp.sum(q * k, axis=1, keepdims=True) * wrc
        num = wgt * v_ref[:, sl].astype(F32) + kpc * num_ref[:, sl]
        den = wgt + kpc * jnp.sum(q * n_old, axis=1, keepdims=True)
        hh_ref[:, sl] = (num / jnp.maximum(jnp.abs(den), floor[:, ln:ln + 1])).astype(BF16)
        nn_ref[:, sl] = kpc * n_old + wrc * k


def _sample_post(y, xc, pbig, dvec, q, k, n_rows, num, wr, keep, mnew):
    s = y.shape[0]
    full = lambda shape: pl.BlockSpec(shape, lambda i: (0, 0))
    blk = lambda width, j: pl.BlockSpec((s, width), lambda i, j=j: (0, j))
    return pl.pallas_call(
        _sample_post_kernel,
        grid=(1,),
        in_specs=[
            full((s, M_INNER)), full((s, M_INNER)), full((1, M_INNER)),
            full((s, L_INNER)), full((s, L_INNER)), blk(L_INNER, 7),
            full((s, L_INNER)), full((s, L_INNER)),
            full((s, LANES)), full((s, LANES)), full((s, LANES)),
        ],
        out_specs=[full((s, M_INNER)), full((s, L_INNER)), full((s, L_INNER))],
        out_shape=[
            jax.ShapeDtypeStruct((s, M_INNER), BF16),
            jax.ShapeDtypeStruct((s, L_INNER), BF16),
            jax.ShapeDtypeStruct((s, L_INNER), F32),
        ],
        compiler_params=_params("arbitrary"),
        name="sample_post",
    )(y, xc, dvec, q, k, pbig, n_rows, num, wr, keep, mnew)


def _lanes(vec, first_lane):
    n = vec.shape[0]
    return jnp.pad(vec.astype(F32), (first_lane, LANES - first_lane - n)).reshape(1, LANES)


def _spread_matrix(first_lane, n_heads, width):
    r = lax.broadcasted_iota(jnp.int32, (2 * LANES, n_heads * width), 0) % LANES
    c = lax.broadcasted_iota(jnp.int32, (2 * LANES, n_heads * width), 1)
    return (r - first_lane == c // width).astype(BF16)


def _shift_matrix(n):
    r = lax.broadcasted_iota(jnp.int32, ((CONV_W - 1) * n, 2 * n), 0)
    c = lax.broadcasted_iota(jnp.int32, ((CONV_W - 1) * n, 2 * n), 1)
    return (c == n + r % n - (r // n + 1)).astype(BF16)


def kernel(x_prompt, x_sample, state_mamba_conv, state_mamba_ssm, state_mlstm_conv, state_mlstm_C, state_mlstm_n, state_mlstm_m, w_in, mamba_conv_w, mamba_conv_b, mamba_dt_bias, mamba_A_log, mamba_D, mamba_norm_w, w_branch_a, mlstm_conv_w, mlstm_conv_b, mlstm_wq, mlstm_wk, mlstm_i_bias, mlstm_f_bias, mlstm_norm_w, w_branch_b, w_out, norm_mix_w, norm_mlp_w, w_up, w_down, final_norm_w):
    depth = w_in.shape[0]
    assert depth == 1
    batch, seq, _ = x_prompt.shape
    n_samp, dec_seq, _ = x_sample.shape
    assert dec_seq == 1 and seq % CHUNK == 0 and seq >= SUBLANES
    assert batch % SCAN_NB == 0
    l = 0

    assert w_in.shape[2] == W_END
    w_big, w_small = _wprep(jnp.transpose(w_in[l]))
    g_mix = norm_mix_w[l].reshape(1, D_MODEL)
    cwm = mamba_conv_w[l]
    cbm = mamba_conv_b[l].reshape(1, M_CONV)
    cwl = mlstm_conv_w[l]
    cbl = mlstm_conv_b[l].reshape(1, L_INNER)
    dtb = _lanes(mamba_dt_bias[l], DT_LANE)
    alog = _lanes(mamba_A_log[l], DT_LANE)
    ib = _lanes(mlstm_i_bias[l], I_LANE)
    fb = _lanes(mlstm_f_bias[l], F_LANE)
    dvec = jnp.repeat(mamba_D[l].astype(F32), M_HDIM).reshape(1, M_INNER)
    mnw = mamba_norm_w[l].reshape(1, M_INNER)
    lnw = mlstm_norm_w[l].reshape(1, L_INNER)
    wq = mlstm_wq[l].astype(BF16)
    wk = mlstm_wk[l].astype(BF16)
    wa = w_branch_a[l].astype(BF16)
    wb = w_branch_b[l].astype(BF16)
    wo = w_out[l].astype(BF16)
    wup = w_up[l].astype(BF16)
    wdn = w_down[l].astype(BF16)
    nmw = norm_mlp_w[l].reshape(1, D_MODEL)
    fw = final_norm_w.reshape(1, D_MODEL)
    e64 = _spread_matrix(DT_LANE, M_HEADS, M_HDIM)
    el = _spread_matrix(F_LANE, L_HEADS, L_HDIM)
    shift = _shift_matrix(CHUNK)

    xp = x_prompt.reshape(batch * seq, D_MODEL)
    xs = x_sample.reshape(n_samp, D_MODEL)
    pbig, psmall, sbig, ssmall = _inproj(xp, xs, g_mix, w_big, w_small, tm=2048)

    m_lanes = jnp.pad(state_mlstm_m[l], ((0, 0), (F_LANE, LANES - F_LANE - L_HEADS)))
    (xc, bcc, q, k, dt, da, wr, keep, mnew, s_conv_m, s_conv_l) = _sample_pre(
        sbig, ssmall, jnp.transpose(state_mamba_conv[l], (1, 0, 2)),
        jnp.transpose(state_mlstm_conv[l], (1, 0, 2)), m_lanes,
        cwm, cbm, cwl, cbl, wq, wk, dtb, alog, ib, fb)
    samples = (
        dt[:, :M_HEADS], da[:, :M_HEADS],
        wr[:, F_LANE:F_LANE + L_HEADS], keep[:, F_LANE:F_LANE + L_HEADS], xc,
        bcc[:, :M_GROUPS * M_STATE].reshape(n_samp, M_GROUPS, M_STATE),
        bcc[:, M_GROUPS * M_STATE:].reshape(n_samp, M_GROUPS, M_STATE),
        q.reshape(n_samp * L_HEADS, L_HDIM), k.reshape(n_samp * L_HEADS, L_HDIM),
        sbig[:, 7 * PROJ_BLOCK:8 * PROJ_BLOCK].astype(F32).reshape(n_samp * L_HEADS, L_HDIM),
        state_mamba_ssm[l].reshape(n_samp, M_PAIRS, LANES, M_STATE), state_mlstm_C[l])

    ya, hb, p_ssm, p_c, p_n, p_m, s_ssm, s_c, y_rows, num_rows = _scan_prompt(
        pbig, psmall, cwm[:, :M_INNER], cbm[:, :M_INNER], cwm[:, M_INNER:], cbm[:, M_INNER:],
        cwl, cbl, dtb, alog, dvec, wq, wk, ib, fb, e64, el, shift, batch, seq, samples)
    y_prompt = _tail(ya.reshape(batch * seq, M_INNER), hb.reshape(batch * seq, L_INNER), pbig, xp,
                     mnw, lnw, wa, wb, wo, nmw, wup, wdn, fw, tm=512)
    p3 = pbig.reshape(batch, seq, BIG_WIDTH)
    p_conv_m = p3[:, seq - (CONV_W - 1):, 2 * PROJ_BLOCK:6 * PROJ_BLOCK].astype(F32)
    p_conv_l = p3[:, seq - (CONV_W - 1):, 6 * PROJ_BLOCK:7 * PROJ_BLOCK].astype(F32)
    p_ssm = p_ssm.reshape(batch, M_HEADS, M_HDIM, M_STATE)
    p_m = p_m[:, 0, F_LANE:F_LANE + L_HEADS]

    ya_s, hb_s, s_n = _sample_post(
        y_rows.reshape(n_samp, M_INNER), xc, sbig, dvec, q, k,
        state_mlstm_n[l].reshape(n_samp, L_INNER), num_rows.reshape(n_samp, L_INNER),
        wr, keep, mnew)
    y_sample = _tail(ya_s, hb_s, sbig, xs, mnw, lnw, wa, wb, wo, nmw, wup, wdn, fw, tm=n_samp)

    lead = lambda a: a[None]
    return (
        y_prompt.reshape(batch, seq, D_MODEL),
        y_sample.reshape(n_samp, 1, D_MODEL),
        lead(p_conv_m), lead(p_ssm), lead(p_conv_l), lead(p_c), lead(p_n), lead(p_m),
        lead(jnp.transpose(s_conv_m, (1, 0, 2))),
        lead(s_ssm.reshape(n_samp, M_HEADS, M_HDIM, M_STATE)),
        lead(jnp.transpose(s_conv_l, (1, 0, 2))),
        lead(s_c), lead(s_n.reshape(n_samp, L_HEADS, L_HDIM)), lead(mnew[:, F_LANE:F_LANE + L_HEADS]),
    )
```

```python
import jax
import jax.numpy as jnp
from jax import lax
from jax.experimental import pallas as pl
from jax.experimental.pallas import tpu as pltpu

F32 = jnp.float32
BF16 = jnp.bfloat16

D_MODEL = 1024
M_INNER = 2048
M_HEADS = 32
M_HDIM = 64
M_GROUPS = 8
M_PAIRS = M_HEADS // 2
M_STATE = 128
M_CONV = 4096
L_INNER = 1024
L_HEADS = 8
L_HDIM = 128
D_FF = 4096
CONV_W = 4
CHUNK = 128
EPS = 1e-6

LANES = 128
SUBLANES = 8
PROJ_BLOCK = 1024
N_PROJ_BLOCKS = 11
BIG_WIDTH = PROJ_BLOCK * N_PROJ_BLOCKS
DT_LANE = 0
I_LANE = 32
F_LANE = 40
VMEM_LIMIT = 56 * 1024 * 1024

LOG2_E = 1.4426950408889634
NT_DIMS = (((1,), (1,)), ((), ()))
TN_DIMS = (((0,), (0,)), ((), ()))


def _params(*sem):
    return pltpu.CompilerParams(dimension_semantics=sem, vmem_limit_bytes=VMEM_LIMIT)


def _sigmoid(x):
    return 0.5 * jnp.tanh(0.5 * x) + 0.5


def _silu(x):
    h = 0.5 * x
    return h + h * jnp.tanh(h)


def _log1p_exp_neg_abs(x):
    e = jnp.exp(-jnp.abs(x))
    u = 1.0 + e
    return jnp.where(u == 1.0, e, jnp.log(u) * (e / (u - 1.0)))


def _softplus(x):
    return jnp.maximum(x, 0.0) + _log1p_exp_neg_abs(x)


def _log_sigmoid(x):
    return jnp.minimum(x, 0.0) - _log1p_exp_neg_abs(x)


def _rms(x, w):
    return x * lax.rsqrt(jnp.mean(x * x, axis=-1, keepdims=True) + EPS) * w


def _lane_iota(shape):
    return lax.broadcasted_iota(jnp.int32, shape, len(shape) - 1)


def _tri(n):
    r = lax.broadcasted_iota(jnp.int32, (n, n), 0)
    c = lax.broadcasted_iota(jnp.int32, (n, n), 1)
    return r >= c


def _split3(a):
    hi = a.astype(BF16)
    r1 = a - hi.astype(F32)
    mid = r1.astype(BF16)
    lo = (r1 - mid.astype(F32)).astype(BF16)
    return hi, mid, lo


def _cumsum_rows(causal, a):
    tri01 = causal.astype(F32).astype(BF16)
    return jnp.dot(jnp.concatenate([tri01] * 3, axis=1), jnp.concatenate(_split3(a), axis=0),
                   preferred_element_type=F32)


def _spread_many(arrs, e2_ref):
    n = arrs[0].shape[0]
    pieces = [jnp.concatenate(_split3(a)[:2], axis=1) for a in arrs]
    out = jnp.dot(jnp.concatenate(pieces, axis=0), e2_ref[...], preferred_element_type=F32)
    return [out[i * n:(i + 1) * n] for i in range(len(arrs))]


def _spread(a, e2_ref):
    return _spread_many((a,), e2_ref)[0]


CONV_COLS = 512
BF16_ROWS = 16


def _conv_silu(x_ref, win_ref, shift_ref, w_ref, b_ref, out_ref):
    n, width = x_ref.shape
    win_ref[n:2 * n, :] = x_ref[...]
    for c0 in range(0, width, CONV_COLS):
        cs = slice(c0, c0 + CONV_COLS)
        sh = jnp.dot(shift_ref[...], win_ref[:, cs], preferred_element_type=F32)
        y = b_ref[:, cs] + w_ref[CONV_W - 1:CONV_W, cs] * x_ref[:, cs].astype(F32)
        for back in range(1, CONV_W):
            y = y + w_ref[CONV_W - 1 - back:CONV_W - back, cs] * sh[(back - 1) * n:back * n]
        out_ref[:, cs] = _silu(y).astype(out_ref.dtype)
    win_ref[n - BF16_ROWS:n, :] = x_ref[n - BF16_ROWS:n, :]


W_DT0 = M_INNER + M_CONV
W_U0 = W_DT0 + M_HEADS
W_I0 = W_U0 + 3 * L_INNER
W_G0 = W_I0 + 2 * L_HEADS
W_END = W_G0 + 2 * D_MODEL
W_FIRST_UVO = W_DT0 // PROJ_BLOCK
W_FIRST_GATE = W_FIRST_UVO + 3 * L_INNER // PROJ_BLOCK
W_SHIFT_UVO = W_U0 - W_DT0
W_SHIFT_GATE = W_G0 - (W_DT0 + 3 * L_INNER)
W_NEXT_ROWS = 64
assert W_DT0 % PROJ_BLOCK == 0 and DT_LANE == 0
assert (W_I0 - I_LANE) == W_FIRST_GATE * PROJ_BLOCK and F_LANE == I_LANE + L_HEADS
assert max(W_SHIFT_UVO, W_SHIFT_GATE) <= W_NEXT_ROWS and PROJ_BLOCK % W_NEXT_ROWS == 0
assert W_SHIFT_UVO % BF16_ROWS == 0 and W_SHIFT_GATE % BF16_ROWS == 0
assert M_HEADS % BF16_ROWS == 0 and I_LANE % BF16_ROWS == 0 and (2 * L_HEADS) % BF16_ROWS == 0


def _wprep_kernel(a_ref, b_ref, big_ref, small_ref):
    j = pl.program_id(0)

    def emit(shift):
        if shift == 0:
            big_ref[...] = a_ref[...].astype(BF16)
        else:
            big_ref[0:PROJ_BLOCK - shift, :] = a_ref[shift:PROJ_BLOCK, :].astype(BF16)
            big_ref[PROJ_BLOCK - shift:PROJ_BLOCK, :] = b_ref[0:shift, :].astype(BF16)

    @pl.when(j == 0)
    def _():
        small_ref[...] = jnp.zeros_like(small_ref)

    @pl.when(j < W_FIRST_UVO)
    def _():
        emit(0)

    @pl.when((j >= W_FIRST_UVO) & (j < W_FIRST_GATE))
    def _():
        emit(W_SHIFT_UVO)

    @pl.when(j >= W_FIRST_GATE)
    def _():
        emit(W_SHIFT_GATE)

    @pl.when(j == W_FIRST_UVO)
    def _():
        small_ref[DT_LANE:DT_LANE + M_HEADS, :] = a_ref[DT_LANE:DT_LANE + M_HEADS, :].astype(BF16)

    @pl.when(j == W_FIRST_GATE)
    def _():
        small_ref[I_LANE:I_LANE + 2 * L_HEADS, :] = a_ref[I_LANE:I_LANE + 2 * L_HEADS, :].astype(BF16)


def _wprep(w_t):
    return pl.pallas_call(
        _wprep_kernel,
        grid=(N_PROJ_BLOCKS,),
        in_specs=[
            pl.BlockSpec((PROJ_BLOCK, D_MODEL), lambda j: (j, 0)),
            pl.BlockSpec((W_NEXT_ROWS, D_MODEL), lambda j: ((j + 1) * (PROJ_BLOCK // W_NEXT_ROWS), 0)),
        ],
        out_specs=[
            pl.BlockSpec((PROJ_BLOCK, D_MODEL), lambda j: (j, 0)),
            pl.BlockSpec((LANES, D_MODEL), lambda j: (0, 0)),
        ],
        out_shape=[
            jax.ShapeDtypeStruct((BIG_WIDTH, D_MODEL), BF16),
            jax.ShapeDtypeStruct((LANES, D_MODEL), BF16),
        ],
        compiler_params=_params("arbitrary"),
        name="w_prep",
    )(w_t, w_t)


def _inproj_kernel(x_ref, xs_ref, g_ref, w_ref, ws_ref, o_ref, os_ref, so_ref, sos_ref, xn_ref, xsn_ref):
    i = pl.program_id(0)
    j = pl.program_id(1)

    @pl.when(j == 0)
    def _():
        xn_ref[...] = _rms(x_ref[...], g_ref[...]).astype(BF16)
        os_ref[...] = lax.dot_general(xn_ref[...], ws_ref[...], NT_DIMS, preferred_element_type=F32)

    @pl.when((i == 0) & (j == 0))
    def _():
        xsn_ref[...] = _rms(xs_ref[...], g_ref[...]).astype(BF16)
        sos_ref[...] = lax.dot_general(xsn_ref[...], ws_ref[...], NT_DIMS, preferred_element_type=F32)

    @pl.when(i == 0)
    def _():
        so_ref[...] = lax.dot_general(
            xsn_ref[...], w_ref[...], NT_DIMS, preferred_element_type=F32).astype(BF16)

    o_ref[...] = lax.dot_general(
        xn_ref[...], w_ref[...], NT_DIMS, preferred_element_type=F32).astype(BF16)


def _inproj(x, xs, g, w_big, w_small, tm):
    m = x.shape[0]
    s = xs.shape[0]
    tm = min(tm, m)
    last = N_PROJ_BLOCKS - 1
    sample_block = lambda i, j: (0, jnp.where(i == 0, j, last))
    return pl.pallas_call(
        _inproj_kernel,
        grid=(m // tm, N_PROJ_BLOCKS),
        in_specs=[
            pl.BlockSpec((tm, D_MODEL), lambda i, j: (i, 0)),
            pl.BlockSpec((s, D_MODEL), lambda i, j: (0, 0)),
            pl.BlockSpec((1, D_MODEL), lambda i, j: (0, 0)),
            pl.BlockSpec((PROJ_BLOCK, D_MODEL), lambda i, j: (j, 0)),
            pl.BlockSpec((LANES, D_MODEL), lambda i, j: (0, 0)),
        ],
        out_specs=[
            pl.BlockSpec((tm, PROJ_BLOCK), lambda i, j: (i, j)),
            pl.BlockSpec((tm, LANES), lambda i, j: (i, 0)),
            pl.BlockSpec((s, PROJ_BLOCK), sample_block),
            pl.BlockSpec((s, LANES), lambda i, j: (0, 0)),
        ],
        out_shape=[
            jax.ShapeDtypeStruct((m, BIG_WIDTH), BF16),
            jax.ShapeDtypeStruct((m, LANES), F32),
            jax.ShapeDtypeStruct((s, BIG_WIDTH), BF16),
            jax.ShapeDtypeStruct((s, LANES), F32),
        ],
        scratch_shapes=[pltpu.VMEM((tm, D_MODEL), BF16), pltpu.VMEM((s, D_MODEL), BF16)],
        compiler_params=_params("arbitrary", "arbitrary"),
        name="in_proj",
    )(x, xs, g, w_big, w_small)


SCAN_NB = 2


def _ssd_chunk(x_ref, bc_ref, sm, cwx_ref, cbx_ref, cwbc_ref, cbbc_ref, dtb_ref, alog_ref, dvec_ref,
               e64_ref, shift_ref, y_ref, h_ref, winx_ref, winbc_ref, xc_scr, bcc_scr):
    n = x_ref.shape[0]
    _conv_silu(x_ref, winx_ref, shift_ref, cwx_ref, cbx_ref, xc_scr)
    _conv_silu(bc_ref, winbc_ref, shift_ref, cwbc_ref, cbbc_ref, bcc_scr)
    xc = xc_scr[...]
    bcc = bcc_scr[...]

    lane = _lane_iota((n, LANES))
    dt = _softplus(sm + dtb_ref[...])
    da = jnp.where(lane < M_HEADS, dt * (-LOG2_E * jnp.exp(alog_ref[...])), 0.0)
    causal = _tri(n)
    acum = _cumsum_rows(causal, da)
    acum_t = acum.T
    last = acum[n - 1:n, :]
    exp_last = jnp.exp2(last)
    dt_x, wr_x, ea_x = _spread_many((dt, jnp.exp2(last - acum), jnp.exp2(acum)), e64_ref)

    xdt = xc * dt_x
    xw = (xdt * wr_x).astype(BF16)
    xb = xdt.astype(BF16)
    low_half = jnp.bitwise_and(_lane_iota((n, M_INNER)), LANES - 1) < M_HDIM
    zero = jnp.zeros((), BF16)
    x_lo = jnp.where(low_half, xb, zero)
    x_hi = jnp.where(low_half, zero, xb)
    first_rows = lax.broadcasted_iota(jnp.int32, (LANES, LANES), 0) < M_HDIM

    def weights(cb, h):
        seg = acum[:, h:h + 1] - acum_t[h:h + 1, :]
        return (cb * jnp.exp2(jnp.where(causal, seg, -jnp.inf))).astype(BF16)

    pairs_per_group = M_PAIRS // M_GROUPS
    groups = {}

    def group(g):
        if g not in groups:
            bg = bcc[:, g * M_STATE:(g + 1) * M_STATE]
            cg = bcc[:, (M_GROUPS + g) * M_STATE:(M_GROUPS + g + 1) * M_STATE]
            groups[g] = (bg, cg, lax.dot_general(cg, bg, NT_DIMS, preferred_element_type=F32))
        return groups[g]

    def pair_weights(hp):
        cb = group(hp // pairs_per_group)[2]
        return weights(cb, 2 * hp), weights(cb, 2 * hp + 1)

    w_next = pair_weights(0)
    for hp in range(M_PAIRS):
        bg, cg, _ = group(hp // pairs_per_group)
        w0, w1 = w_next
        if hp + 1 < M_PAIRS:
            w_next = pair_weights(hp + 1)
        h0, h1 = 2 * hp, 2 * hp + 1
        sl = slice(hp * LANES, (hp + 1) * LANES)
        hs = h_ref[hp]
        y = jnp.dot(w0, x_lo[:, sl], preferred_element_type=F32)
        y = y + jnp.dot(w1, x_hi[:, sl], preferred_element_type=F32)
        ys = lax.dot_general(cg, hs.astype(BF16), NT_DIMS, preferred_element_type=F32)
        y_ref[:, sl] = (y + ea_x[:, sl] * ys + dvec_ref[:, sl] * xc[:, sl]).astype(BF16)
        el = jnp.where(first_rows, exp_last[:, h0:h0 + 1], exp_last[:, h1:h1 + 1])
        h_ref[hp] = el * hs + lax.dot_general(xw[:, sl], bg, TN_DIMS, preferred_element_type=F32)


def _mlstm_chunks(seqs, wq_ref, wk_ref, ib_ref, fb_ref, el_ref):
    n = seqs[0][0].shape[0]
    causal = _tri(n)
    lane = _lane_iota((n, LANES))
    gate_lanes = (lane >= F_LANE) & (lane < F_LANE + L_HEADS)
    scale = L_HDIM ** -0.5
    lanes_of = [slice(h * L_HDIM, (h + 1) * L_HDIM) for h in range(L_HEADS)]

    work = [(i, h) for i in range(len(seqs)) for h in range(L_HEADS)]
    c_olds = {(i, h): seqs[i][4][h] for i, h in work}
    n_olds = {(i, h): seqs[i][5][h:h + 1, :] for i, h in work}
    qs, ks, qbs, kbs = {}, {}, {}, {}
    for i, h in work:
        ub = seqs[i][0][:, lanes_of[h]]
        qs[i, h] = jnp.dot(ub, wq_ref[h], preferred_element_type=F32)
        ks[i, h] = jnp.dot(ub, wk_ref[h], preferred_element_type=F32) * scale

    gates = []
    for uc, v_ref, sm, hh_ref, c_ref, n_ref, m_ref in seqs:
        ig = pltpu.roll(sm + ib_ref[...], F_LANE - I_LANE, axis=1)
        logf = jnp.where(gate_lanes, _log_sigmoid(sm + fb_ref[...]), 0.0)
        bcum = _cumsum_rows(causal, logf)
        m_old = m_ref[...]
        bl = bcum[n - 1:n, :]
        src = bl - bcum + ig
        m_new = jnp.maximum(bl + m_old, jnp.max(src, axis=0, keepdims=True))
        gates.append(dict(
            bcum=bcum, bcum_t=bcum.T, ig_t=ig.T, m_old=m_old, m_new=m_new,
            keep=jnp.exp(bl + m_old - m_new),
            wr_x=_spread(jnp.exp(src - m_new), el_ref)))

    logds, inters, rowmax = {}, {}, {}
    for i, h in work:
        g = gates[i]
        ln = F_LANE + h
        bcol = jnp.broadcast_to(g["bcum"][:, ln:ln + 1], (n, n))
        logd = jnp.where(causal, bcol - g["bcum_t"][ln:ln + 1, :] + g["ig_t"][ln:ln + 1, :], -jnp.inf)
        logds[i, h] = logd
        inters[i, h] = bcol + g["m_old"][:, ln:ln + 1]
        rowmax[i, h] = jnp.max(logd, axis=1, keepdims=True)
    ss, scs, floors = {}, {}, {}
    for i, h in work:
        qbs[i, h] = qs[i, h].astype(BF16)
        kbs[i, h] = ks[i, h].astype(BF16)
        m_s = jnp.maximum(inters[i, h], rowmax[i, h])
        dm = jnp.exp(logds[i, h] - m_s)
        scs[i, h] = jnp.exp(inters[i, h] - m_s)
        floors[i, h] = jnp.exp(-m_s)
        ss[i, h] = lax.dot_general(qbs[i, h], kbs[i, h], NT_DIMS, preferred_element_type=F32) * dm
    nums, dens = {}, {}
    for i, h in work:
        vb = seqs[i][1][:, lanes_of[h]]
        num = jnp.dot(ss[i, h].astype(BF16), vb, preferred_element_type=F32)
        nums[i, h] = num + scs[i, h] * jnp.dot(
            qbs[i, h], c_olds[i, h].astype(BF16), preferred_element_type=F32)
        dens[i, h] = (jnp.sum(ss[i, h], axis=1, keepdims=True)
                      + scs[i, h] * jnp.sum(qs[i, h] * n_olds[i, h], axis=1, keepdims=True))
    for i, h in work:
        hh = nums[i, h] / jnp.maximum(jnp.abs(dens[i, h]), floors[i, h])
        seqs[i][3][:, lanes_of[h]] = hh.astype(BF16)
    for i, h in work:
        g = gates[i]
        ln = F_LANE + h
        kw = ks[i, h] * g["wr_x"][:, lanes_of[h]]
        kp = g["keep"][:, ln:ln + 1]
        seqs[i][4][h] = kp * c_olds[i, h] + lax.dot_general(
            kw.astype(BF16), seqs[i][1][:, lanes_of[h]], TN_DIMS, preferred_element_type=F32)
        seqs[i][5][h:h + 1, :] = kp * n_olds[i, h] + jnp.sum(kw, axis=0, keepdims=True)
    for i in range(len(seqs)):
        seqs[i][6][...] = gates[i]["m_new"]


def _diag_rows(row, diag):
    return jnp.where(diag, jnp.broadcast_to(row, diag.shape), 0.0).astype(BF16)


def _state_spreads(x_ref, q_ref, k_ref, xs_scr, qs_scr, ks_scr):
    ns = x_ref.shape[0]
    rows = lax.broadcasted_iota(jnp.int32, (LANES, LANES), 0)
    diag = rows == _lane_iota((LANES, LANES))
    ones = jnp.ones((LANES, LANES), BF16)
    ones2 = jnp.ones((2 * LANES, LANES), BF16)
    for s in range(ns):
        for hp in range(M_PAIRS):
            xrow = x_ref[s:s + 1, hp * LANES:(hp + 1) * LANES]
            xs_scr[s * M_PAIRS + hp] = jnp.dot(_diag_rows(xrow, diag), ones, preferred_element_type=F32)
    for r in range(ns * L_HEADS):
        for src_ref, dst_scr in ((q_ref, qs_scr), (k_ref, ks_scr)):
            row = src_ref[r:r + 1, :]
            hi = row.astype(BF16).astype(F32)
            lhs = jnp.concatenate([_diag_rows(hi, diag), _diag_rows(row - hi, diag)], axis=1)
            dst_scr[r] = jnp.dot(lhs, ones2, preferred_element_type=F32)


def _state_updates(base, dt_ref, da_ref, wr_ref, keep_ref, b_ref, c_ref, v_ref, h_ref, cst_ref,
                   hn_ref, cn_ref, y_ref, num_ref, xs_scr, qs_scr, ks_scr):
    ns = b_ref.shape[0]
    rows = lax.broadcasted_iota(jnp.int32, (LANES, LANES), 0)
    lanes = _lane_iota((LANES, LANES))
    first_rows = rows < M_HDIM
    ones = jnp.ones((LANES, LANES), BF16)
    pairs_per_group = M_PAIRS // M_GROUPS
    acc = jnp.zeros((LANES, LANES), F32)
    for s in range(ns):
        for hp in range(M_PAIRS):
            g = hp // pairs_per_group
            h0, h1 = 2 * hp, 2 * hp + 1
            brow = b_ref[s, g:g + 1, :]
            crow = c_ref[s, g:g + 1, :]
            dav = jnp.where(first_rows, da_ref[base + s, h0], da_ref[base + s, h1])
            dtv = jnp.where(first_rows, dt_ref[base + s, h0], dt_ref[base + s, h1])
            hn = dav * h_ref[s, hp] + (dtv * xs_scr[s * M_PAIRS + hp]) * brow
            hn_ref[s, hp] = hn
            ysum = jnp.dot((hn * crow).astype(BF16), ones, preferred_element_type=F32)
            acc = jnp.where(lanes == s * M_PAIRS + hp, ysum, acc)
    y_ref[...] = acc.T[0:ns * M_PAIRS, :]
    for s in range(ns):
        for h in range(L_HEADS):
            r = s * L_HEADS + h
            c_old = cst_ref[s, h]
            num_ref[r:r + 1, :] = jnp.sum(qs_scr[r] * c_old, axis=0, keepdims=True)
            cn_ref[s, h] = (keep_ref[base + s, h] * c_old
                            + (ks_scr[r] * wr_ref[base + s, h]) * v_ref[r:r + 1, :])


def _scan_kernel(dt_ref, da_ref, wr_ref, keep_ref, sx_ref, sb_ref, sc_ref, sq_ref, sk_ref, sv_ref,
                 sh_ref, scst_ref, x_ref, bc_ref, u_ref, v_ref, sm_ref,
                 cwx_ref, cbx_ref, cwbc_ref, cbbc_ref, cwl_ref, cbl_ref, dtb_ref, alog_ref, dvec_ref,
                 wq_ref, wk_ref, ib_ref, fb_ref, e64_ref, el_ref, shift_ref,
                 y_ref, hh_ref, h_ref, c_ref, n_ref, m_ref, shn_ref, scn_ref, sy_ref, snum_ref,
                 winx_ref, winbc_ref, winu_ref, xc_scr, bcc_scr, uc_scr, xs_scr, qs_scr, ks_scr):
    n = x_ref.shape[1]
    step = pl.program_id(0) * pl.num_programs(1) + pl.program_id(1)

    @pl.when(pl.program_id(1) == 0)
    def _():
        h_ref[...] = jnp.zeros_like(h_ref)
        c_ref[...] = jnp.zeros_like(c_ref)
        n_ref[...] = jnp.zeros_like(n_ref)
        m_ref[...] = jnp.zeros_like(m_ref)
        winx_ref[:, 0:n, :] = jnp.zeros((SCAN_NB, n, M_INNER), BF16)
        winbc_ref[:, 0:n, :] = jnp.zeros((SCAN_NB, n, M_INNER), BF16)
        winu_ref[:, 0:n, :] = jnp.zeros((SCAN_NB, n, L_INNER), BF16)

    seqs = []
    for i in range(SCAN_NB):
        _ssd_chunk(x_ref.at[i], bc_ref.at[i], sm_ref[i], cwx_ref, cbx_ref, cwbc_ref, cbbc_ref,
                   dtb_ref, alog_ref, dvec_ref, e64_ref, shift_ref, y_ref.at[i], h_ref.at[i],
                   winx_ref.at[i], winbc_ref.at[i], xc_scr, bcc_scr)
        _conv_silu(u_ref.at[i], winu_ref.at[i], shift_ref, cwl_ref, cbl_ref, uc_scr.at[i])
        seqs.append((uc_scr.at[i], v_ref.at[i], sm_ref[i], hh_ref.at[i], c_ref.at[i], n_ref.at[i],
                     m_ref.at[i]))
        if i == 0:
            _state_spreads(sx_ref, sq_ref, sk_ref, xs_scr, qs_scr, ks_scr)
    _state_updates(step * sb_ref.shape[0], dt_ref, da_ref, wr_ref, keep_ref, sb_ref, sc_ref, sv_ref,
                   sh_ref, scst_ref, shn_ref, scn_ref, sy_ref, snum_ref, xs_scr, qs_scr, ks_scr)
    _mlstm_chunks(seqs, wq_ref, wk_ref, ib_ref, fb_ref, el_ref)


def _scan_prompt(pbig, psmall, cwx, cbx, cwbc, cbbc, cwl, cbl, dtb, alog, dvec, wq, wk, ib, fb,
                 e64, el, shift, batch, seq, samples):
    p3 = pbig.reshape(batch, seq, BIG_WIDTH)
    s3 = psmall.reshape(batch, seq, LANES)
    nb = SCAN_NB
    n_chunks = seq // CHUNK
    n_steps = (batch // nb) * n_chunks
    s_dt, s_da, s_wr, s_keep, s_x, s_b, s_c, s_q, s_k, s_v, s_h, s_cst = samples
    n_samp = s_h.shape[0]
    ns = n_samp // n_steps
    assert ns * n_steps == n_samp and ns * M_PAIRS <= LANES and (ns * L_HEADS) % SUBLANES == 0
    smem = pl.BlockSpec(memory_space=pltpu.SMEM)
    step = lambda b, c: b * n_chunks + c
    srows = lambda n_rows: pl.BlockSpec((n_rows, LANES), lambda b, c: (step(b, c), 0))
    sgrp = pl.BlockSpec((ns, M_GROUPS, M_STATE), lambda b, c: (step(b, c), 0, 0))
    sstate = lambda d1: pl.BlockSpec((ns, d1, LANES, LANES), lambda b, c: (step(b, c), 0, 0, 0))
    const2 = lambda b, c: (0, 0)
    const3 = lambda b, c: (0, 0, 0)
    rows = lambda width, col: pl.BlockSpec((nb, CHUNK, width), lambda b, c, col=col: (b, c, col))
    state4 = lambda d1, d2, d3: pl.BlockSpec((nb, d1, d2, d3), lambda b, c: (b, 0, 0, 0))
    return pl.pallas_call(
        _scan_kernel,
        grid=(batch // nb, seq // CHUNK),
        in_specs=[
            smem, smem, smem, smem,
            pl.BlockSpec((None, ns, M_INNER), lambda b, c: (step(b, c), 0, 0)),
            sgrp, sgrp, srows(ns * L_HEADS), srows(ns * L_HEADS), srows(ns * L_HEADS),
            sstate(M_PAIRS), sstate(L_HEADS),
            rows(M_INNER, 1),
            rows(M_INNER, 2),
            rows(L_INNER, 6),
            rows(L_INNER, 7),
            rows(LANES, 0),
            pl.BlockSpec((CONV_W, M_INNER), const2), pl.BlockSpec((1, M_INNER), const2),
            pl.BlockSpec((CONV_W, M_INNER), const2), pl.BlockSpec((1, M_INNER), const2),
            pl.BlockSpec((CONV_W, L_INNER), const2), pl.BlockSpec((1, L_INNER), const2),
            pl.BlockSpec((1, LANES), const2), pl.BlockSpec((1, LANES), const2),
            pl.BlockSpec((1, M_INNER), const2),
            pl.BlockSpec((L_HEADS, L_HDIM, L_HDIM), const3),
            pl.BlockSpec((L_HEADS, L_HDIM, L_HDIM), const3),
            pl.BlockSpec((1, LANES), const2), pl.BlockSpec((1, LANES), const2),
            pl.BlockSpec((2 * LANES, M_INNER), const2),
            pl.BlockSpec((2 * LANES, L_INNER), const2),
            pl.BlockSpec(((CONV_W - 1) * CHUNK, 2 * CHUNK), const2),
        ],
        out_specs=[
            rows(M_INNER, 0),
            rows(L_INNER, 0),
            state4(M_PAIRS, LANES, M_STATE),
            state4(L_HEADS, L_HDIM, L_HDIM),
            pl.BlockSpec((nb, L_HEADS, L_HDIM), lambda b, c: (b, 0, 0)),
            pl.BlockSpec((nb, 1, LANES), lambda b, c: (b, 0, 0)),
            sstate(M_PAIRS), sstate(L_HEADS), srows(ns * M_PAIRS), srows(ns * L_HEADS),
        ],
        out_shape=[
            jax.ShapeDtypeStruct((batch, seq, M_INNER), BF16),
            jax.ShapeDtypeStruct((batch, seq, L_INNER), BF16),
            jax.ShapeDtypeStruct((batch, M_PAIRS, LANES, M_STATE), F32),
            jax.ShapeDtypeStruct((batch, L_HEADS, L_HDIM, L_HDIM), F32),
            jax.ShapeDtypeStruct((batch, L_HEADS, L_HDIM), F32),
            jax.ShapeDtypeStruct((batch, 1, LANES), F32),
            jax.ShapeDtypeStruct(s_h.shape, F32),
            jax.ShapeDtypeStruct(s_cst.shape, F32),
            jax.ShapeDtypeStruct((n_samp * M_PAIRS, LANES), F32),
            jax.ShapeDtypeStruct((n_samp * L_HEADS, LANES), F32),
        ],
        scratch_shapes=[
            pltpu.VMEM((nb, 2 * CHUNK, M_INNER), BF16),
            pltpu.VMEM((nb, 2 * CHUNK, M_INNER), BF16),
            pltpu.VMEM((nb, 2 * CHUNK, L_INNER), BF16),
            pltpu.VMEM((CHUNK, M_INNER), F32),
            pltpu.VMEM((CHUNK, M_INNER), BF16),
            pltpu.VMEM((nb, CHUNK, L_INNER), BF16),
            pltpu.VMEM((ns * M_PAIRS, LANES, LANES), F32),
            pltpu.VMEM((ns * L_HEADS, LANES, LANES), F32),
            pltpu.VMEM((ns * L_HEADS, LANES, LANES), F32),
        ],
        compiler_params=_params("parallel", "arbitrary"),
        name="scan_prompt",
    )(s_dt, s_da, s_wr, s_keep, s_x.reshape(n_steps, ns, M_INNER), s_b, s_c, s_q, s_k, s_v, s_h, s_cst,
      p3, p3, p3, p3, s3, cwx, cbx, cwbc, cbbc, cwl, cbl, dtb, alog, dvec, wq, wk, ib, fb,
      e64, el, shift)


FF_COLS = 512


def _tail_kernel(ys_ref, hh_ref, z_ref, o_ref, ga_ref, gb_ref, x_ref, mnw_ref, lnw_ref,
                 wa_ref, wb_ref, wo_ref, nmw_ref, wup_ref, wdn_ref, fw_ref, y_ref):
    gw = M_INNER // M_GROUPS
    a = None
    for g in range(M_GROUPS):
        sl = slice(g * gw, (g + 1) * gw)
        yg = ys_ref[:, sl].astype(F32) * _silu(z_ref[:, sl].astype(F32))
        part = jnp.dot(_rms(yg, mnw_ref[:, sl]).astype(BF16), wa_ref[sl, :], preferred_element_type=F32)
        a = part if a is None else a + part
    b = None
    for hp in range(L_HEADS // 2):
        pieces = []
        for h in (2 * hp, 2 * hp + 1):
            sl = slice(h * L_HDIM, (h + 1) * L_HDIM)
            gate = _sigmoid(o_ref[:, sl].astype(F32))
            pieces.append((gate * _rms(hh_ref[:, sl].astype(F32), lnw_ref[:, sl])).astype(BF16))
        rows = slice(2 * hp * L_HDIM, (2 * hp + 2) * L_HDIM)
        part = jnp.dot(jnp.concatenate(pieces, axis=1), wb_ref[rows, :], preferred_element_type=F32)
        b = part if b is None else b + part
    t = _sigmoid(ga_ref[...].astype(F32)) * a + _sigmoid(gb_ref[...].astype(F32)) * b
    x1 = x_ref[...] + jnp.dot(t.astype(BF16), wo_ref[...], preferred_element_type=F32)
    hn = _rms(x1, nmw_ref[...]).astype(BF16)
    acc = x1
    for c in range(D_FF // FF_COLS):
        sl = slice(c * FF_COLS, (c + 1) * FF_COLS)
        up = jnp.dot(hn, wup_ref[:, sl], preferred_element_type=F32)
        act = jnp.square(jnp.maximum(up, 0.0)).astype(BF16)
        acc = acc + jnp.dot(act, wdn_ref[sl, :], preferred_element_type=F32)
    y_ref[...] = _rms(acc, fw_ref[...])


def _tail(ys, hh, pbig, x, mnw, lnw, wa, wb, wo, nmw, wup, wdn, fw, tm):
    m = x.shape[0]
    tm = min(tm, m)
    rows = lambda i: (i, 0)
    const = lambda i: (0, 0)

    def resident(shape):
        return pl.BlockSpec(shape, const, pipeline_mode=pl.Buffered(1))

    return pl.pallas_call(
        _tail_kernel,
        grid=(m // tm,),
        in_specs=[
            pl.BlockSpec((tm, M_INNER), rows),
            pl.BlockSpec((tm, L_INNER), rows),
            pl.BlockSpec((tm, M_INNER), rows),
            pl.BlockSpec((tm, PROJ_BLOCK), lambda i: (i, 8)),
            pl.BlockSpec((tm, PROJ_BLOCK), lambda i: (i, 9)),
            pl.BlockSpec((tm, PROJ_BLOCK), lambda i: (i, 10)),
            pl.BlockSpec((tm, D_MODEL), rows),
            resident((1, M_INNER)),
            resident((1, L_INNER)),
            resident((M_INNER, D_MODEL)),
            resident((L_INNER, D_MODEL)),
            resident((D_MODEL, D_MODEL)),
            resident((1, D_MODEL)),
            resident((D_MODEL, D_FF)),
            resident((D_FF, D_MODEL)),
            resident((1, D_MODEL)),
        ],
        out_specs=pl.BlockSpec((tm, D_MODEL), rows),
        out_shape=jax.ShapeDtypeStruct((m, D_MODEL), F32),
        compiler_params=_params("parallel"),
        name="tail",
    )(ys, hh, pbig, pbig, pbig, pbig, x, mnw, lnw, wa, wb, wo, nmw, wup, wdn, fw)


def _sample_pre_kernel(xbc_ref, u_ref, sm_ref, cm0_ref, cm1_ref, cm2_ref, cl0_ref, cl1_ref, cl2_ref,
                       m_ref, cwm_ref, cbm_ref, cwl_ref, cbl_ref, wq_ref, wk_ref,
                       dtb_ref, alog_ref, ib_ref, fb_ref,
                       xc_ref, bcc_ref, q_ref, k_ref, dt_ref, da_ref, wr_ref, keep_ref, mnew_ref,
                       cmn_ref, cln_ref):
    xbc = xbc_ref[...].astype(F32)
    conv_m = (cbm_ref[...] + cwm_ref[0:1, :] * cm0_ref[...] + cwm_ref[1:2, :] * cm1_ref[...]
              + cwm_ref[2:3, :] * cm2_ref[...] + cwm_ref[3:4, :] * xbc)
    act = _silu(conv_m)
    xc_ref[...] = act[:, :M_INNER]
    bcc_ref[...] = act[:, M_INNER:]
    cmn_ref[0] = cm1_ref[...]
    cmn_ref[1] = cm2_ref[...]
    cmn_ref[2] = xbc

    u = u_ref[...].astype(F32)
    conv_l = (cbl_ref[...] + cwl_ref[0:1, :] * cl0_ref[...] + cwl_ref[1:2, :] * cl1_ref[...]
              + cwl_ref[2:3, :] * cl2_ref[...] + cwl_ref[3:4, :] * u)
    uc = _silu(conv_l)
    cln_ref[0] = cl1_ref[...]
    cln_ref[1] = cl2_ref[...]
    cln_ref[2] = u
    scale = L_HDIM ** -0.5
    for h in range(L_HEADS):
        sl = slice(h * L_HDIM, (h + 1) * L_HDIM)
        ub = uc[:, sl].astype(BF16)
        q_ref[:, sl] = jnp.dot(ub, wq_ref[h], preferred_element_type=F32)
        k_ref[:, sl] = jnp.dot(ub, wk_ref[h], preferred_element_type=F32) * scale

    sm = sm_ref[...]
    dt = _softplus(sm + dtb_ref[...])
    dt_ref[...] = dt
    da_ref[...] = jnp.exp(dt * (-jnp.exp(alog_ref[...])))
    ig = pltpu.roll(sm + ib_ref[...], F_LANE - I_LANE, axis=1)
    logf = _log_sigmoid(sm + fb_ref[...])
    m_old = m_ref[...]
    m_new = jnp.maximum(logf + m_old, ig)
    mnew_ref[...] = m_new
    wr_ref[...] = jnp.exp(ig - m_new)
    keep_ref[...] = jnp.exp(logf + m_old - m_new)


def _sample_pre(pbig, psmall, conv_m, conv_l, m_lanes, cwm, cbm, cwl, cbl, wq, wk, dtb, alog, ib, fb):
    s = pbig.shape[0]
    f = lambda shape: jax.ShapeDtypeStruct(shape, F32)
    full2 = lambda shape: pl.BlockSpec(shape, lambda i: (0, 0))
    full3 = lambda shape: pl.BlockSpec(shape, lambda i: (0, 0, 0))
    state_row = lambda width, j: pl.BlockSpec((None, s, width), lambda i, j=j: (j, 0, 0))
    return pl.pallas_call(
        _sample_pre_kernel,
        grid=(1,),
        in_specs=[
            full2((s, M_CONV)),
            pl.BlockSpec((s, L_INNER), lambda i: (0, 6)),
            full2((s, LANES)),
            state_row(M_CONV, 0), state_row(M_CONV, 1), state_row(M_CONV, 2),
            state_row(L_INNER, 0), state_row(L_INNER, 1), state_row(L_INNER, 2),
            full2((s, LANES)),
            full2((CONV_W, M_CONV)), full2((1, M_CONV)),
            full2((CONV_W, L_INNER)), full2((1, L_INNER)),
            full3((L_HEADS, L_HDIM, L_HDIM)), full3((L_HEADS, L_HDIM, L_HDIM)),
            full2((1, LANES)), full2((1, LANES)), full2((1, LANES)), full2((1, LANES)),
        ],
        out_specs=[
            full2((s, M_INNER)), full2((s, M_INNER)), full2((s, L_INNER)), full2((s, L_INNER)),
            full2((s, LANES)), full2((s, LANES)), full2((s, LANES)), full2((s, LANES)), full2((s, LANES)),
            full3((CONV_W - 1, s, M_CONV)), full3((CONV_W - 1, s, L_INNER)),
        ],
        out_shape=[
            f((s, M_INNER)), f((s, M_INNER)), f((s, L_INNER)), f((s, L_INNER)),
            f((s, LANES)), f((s, LANES)), f((s, LANES)), f((s, LANES)), f((s, LANES)),
            f((CONV_W - 1, s, M_CONV)), f((CONV_W - 1, s, L_INNER)),
        ],
        compiler_params=_params("arbitrary"),
        name="sample_pre",
    )(pbig[:, 2 * PROJ_BLOCK:6 * PROJ_BLOCK], pbig, psmall, conv_m, conv_m, conv_m,
      conv_l, conv_l, conv_l, m_lanes, cwm, cbm, cwl, cbl, wq, wk, dtb, alog, ib, fb)


def _sample_post_kernel(y_ref, xc_ref, dvec_ref,
                        q_ref, k_ref, v_ref, n_ref, num_ref, wr_ref, keep_ref, mnew_ref,
                        ys_ref, hh_ref, nn_ref):
    ys_ref[...] = (y_ref[...] + dvec_ref[...] * xc_ref[...]).astype(BF16)

    wr = wr_ref[...]
    keep = keep_ref[...]
    floor = jnp.exp(-mnew_ref[...])
    for h in range(L_HEADS):
        ln = F_LANE + h
        sl = slice(h * L_HDIM, (h + 1) * L_HDIM)
        q = q_ref[:, sl]
        k = k_ref[:, sl]
        n_old = n_ref[:, sl]
        wrc = wr[:, ln:ln + 1]
        kpc = keep[:, ln:ln + 1]
        wgt = jnp.sum(q * k, axis=1, keepdims=True) * wrc
        num = wgt * v_ref[:, sl].astype(F32) + kpc * num_ref[:, sl]
        den = wgt + kpc * jnp.sum(q * n_old, axis=1, keepdims=True)
        hh_ref[:, sl] = (num / jnp.maximum(jnp.abs(den), floor[:, ln:ln + 1])).astype(BF16)
        nn_ref[:, sl] = kpc * n_old + wrc * k


def _sample_post(y, xc, pbig, dvec, q, k, n_rows, num, wr, keep, mnew):
    s = y.shape[0]
    full = lambda shape: pl.BlockSpec(shape, lambda i: (0, 0))
    blk = lambda width, j: pl.BlockSpec((s, width), lambda i, j=j: (0, j))
    return pl.pallas_call(
        _sample_post_kernel,
        grid=(1,),
        in_specs=[
            full((s, M_INNER)), full((s, M_INNER)), full((1, M_INNER)),
            full((s, L_INNER)), full((s, L_INNER)), blk(L_INNER, 7),
            full((s, L_INNER)), full((s, L_INNER)),
            full((s, LANES)), full((s, LANES)), full((s, LANES)),
        ],
        out_specs=[full((s, M_INNER)), full((s, L_INNER)), full((s, L_INNER))],
        out_shape=[
            jax.ShapeDtypeStruct((s, M_INNER), BF16),
            jax.ShapeDtypeStruct((s, L_INNER), BF16),
            jax.ShapeDtypeStruct((s, L_INNER), F32),
        ],
        compiler_params=_params("arbitrary"),
        name="sample_post",
    )(y, xc, dvec, q, k, pbig, n_rows, num, wr, keep, mnew)


def _lanes(vec, first_lane):
    n = vec.shape[0]
    return jnp.pad(vec.astype(F32), (first_lane, LANES - first_lane - n)).reshape(1, LANES)


def _spread_matrix(first_lane, n_heads, width):
    r = lax.broadcasted_iota(jnp.int32, (2 * LANES, n_heads * width), 0) % LANES
    c = lax.broadcasted_iota(jnp.int32, (2 * LANES, n_heads * width), 1)
    return (r - first_lane == c // width).astype(BF16)


def _shift_matrix(n):
    r = lax.broadcasted_iota(jnp.int32, ((CONV_W - 1) * n, 2 * n), 0)
    c = lax.broadcasted_iota(jnp.int32, ((CONV_W - 1) * n, 2 * n), 1)
    return (c == n + r % n - (r // n + 1)).astype(BF16)


def kernel(x_prompt, x_sample, state_mamba_conv, state_mamba_ssm, state_mlstm_conv, state_mlstm_C, state_mlstm_n, state_mlstm_m, w_in, mamba_conv_w, mamba_conv_b, mamba_dt_bias, mamba_A_log, mamba_D, mamba_norm_w, w_branch_a, mlstm_conv_w, mlstm_conv_b, mlstm_wq, mlstm_wk, mlstm_i_bias, mlstm_f_bias, mlstm_norm_w, w_branch_b, w_out, norm_mix_w, norm_mlp_w, w_up, w_down, final_norm_w):
    depth = w_in.shape[0]
    assert depth == 1
    batch, seq, _ = x_prompt.shape
    n_samp, dec_seq, _ = x_sample.shape
    assert dec_seq == 1 and seq % CHUNK == 0 and seq >= SUBLANES
    assert batch % SCAN_NB == 0
    l = 0

    assert w_in.shape[2] == W_END
    w_big, w_small = _wprep(jnp.transpose(w_in[l]))
    g_mix = norm_mix_w[l].reshape(1, D_MODEL)
    cwm = mamba_conv_w[l]
    cbm = mamba_conv_b[l].reshape(1, M_CONV)
    cwl = mlstm_conv_w[l]
    cbl = mlstm_conv_b[l].reshape(1, L_INNER)
    dtb = _lanes(mamba_dt_bias[l], DT_LANE)
    alog = _lanes(mamba_A_log[l], DT_LANE)
    ib = _lanes(mlstm_i_bias[l], I_LANE)
    fb = _lanes(mlstm_f_bias[l], F_LANE)
    dvec = jnp.repeat(mamba_D[l].astype(F32), M_HDIM).reshape(1, M_INNER)
    mnw = mamba_norm_w[l].reshape(1, M_INNER)
    lnw = mlstm_norm_w[l].reshape(1, L_INNER)
    wq = mlstm_wq[l].astype(BF16)
    wk = mlstm_wk[l].astype(BF16)
    wa = w_branch_a[l].astype(BF16)
    wb = w_branch_b[l].astype(BF16)
    wo = w_out[l].astype(BF16)
    wup = w_up[l].astype(BF16)
    wdn = w_down[l].astype(BF16)
    nmw = norm_mlp_w[l].reshape(1, D_MODEL)
    fw = final_norm_w.reshape(1, D_MODEL)
    e64 = _spread_matrix(DT_LANE, M_HEADS, M_HDIM)
    el = _spread_matrix(F_LANE, L_HEADS, L_HDIM)
    shift = _shift_matrix(CHUNK)

    xp = x_prompt.reshape(batch * seq, D_MODEL)
    xs = x_sample.reshape(n_samp, D_MODEL)
    pbig, psmall, sbig, ssmall = _inproj(xp, xs, g_mix, w_big, w_small, tm=2048)

    m_lanes = jnp.pad(state_mlstm_m[l], ((0, 0), (F_LANE, LANES - F_LANE - L_HEADS)))
    (xc, bcc, q, k, dt, da, wr, keep, mnew, s_conv_m, s_conv_l) = _sample_pre(
        sbig, ssmall, jnp.transpose(state_mamba_conv[l], (1, 0, 2)),
        jnp.transpose(state_mlstm_conv[l], (1, 0, 2)), m_lanes,
        cwm, cbm, cwl, cbl, wq, wk, dtb, alog, ib, fb)
    samples = (
        dt[:, :M_HEADS], da[:, :M_HEADS],
        wr[:, F_LANE:F_LANE + L_HEADS], keep[:, F_LANE:F_LANE + L_HEADS], xc,
        bcc[:, :M_GROUPS * M_STATE].reshape(n_samp, M_GROUPS, M_STATE),
        bcc[:, M_GROUPS * M_STATE:].reshape(n_samp, M_GROUPS, M_STATE),
        q.reshape(n_samp * L_HEADS, L_HDIM), k.reshape(n_samp * L_HEADS, L_HDIM),
        sbig[:, 7 * PROJ_BLOCK:8 * PROJ_BLOCK].astype(F32).reshape(n_samp * L_HEADS, L_HDIM),
        state_mamba_ssm[l].reshape(n_samp, M_PAIRS, LANES, M_STATE), state_mlstm_C[l])

    ya, hb, p_ssm, p_c, p_n, p_m, s_ssm, s_c, y_rows, num_rows = _scan_prompt(
        pbig, psmall, cwm[:, :M_INNER], cbm[:, :M_INNER], cwm[:, M_INNER:], cbm[:, M_INNER:],
        cwl, cbl, dtb, alog, dvec, wq, wk, ib, fb, e64, el, shift, batch, seq, samples)
    y_prompt = _tail(ya.reshape(batch * seq, M_INNER), hb.reshape(batch * seq, L_INNER), pbig, xp,
                     mnw, lnw, wa, wb, wo, nmw, wup, wdn, fw, tm=512)
    p3 = pbig.reshape(batch, seq, BIG_WIDTH)
    p_conv_m = p3[:, seq - (CONV_W - 1):, 2 * PROJ_BLOCK:6 * PROJ_BLOCK].astype(F32)
    p_conv_l = p3[:, seq - (CONV_W - 1):, 6 * PROJ_BLOCK:7 * PROJ_BLOCK].astype(F32)
    p_ssm = p_ssm.reshape(batch, M_HEADS, M_HDIM, M_STATE)
    p_m = p_m[:, 0, F_LANE:F_LANE + L_HEADS]

    ya_s, hb_s, s_n = _sample_post(
        y_rows.reshape(n_samp, M_INNER), xc, sbig, dvec, q, k,
        state_mlstm_n[l].reshape(n_samp, L_INNER), num_rows.reshape(n_samp, L_INNER),
        wr, keep, mnew)
    y_sample = _tail(ya_s, hb_s, sbig, xs, mnw, lnw, wa, wb, wo, nmw, wup, wdn, fw, tm=n_samp)

    lead = lambda a: a[None]
    return (
        y_prompt.reshape(batch, seq, D_MODEL),
        y_sample.reshape(n_samp, 1, D_MODEL),
        lead(p_conv_m), lead(p_ssm), lead(p_conv_l), lead(p_c), lead(p_n), lead(p_m),
        lead(jnp.transpose(s_conv_m, (1, 0, 2))),
        lead(s_ssm.reshape(n_samp, M_HEADS, M_HDIM, M_STATE)),
        lead(jnp.transpose(s_conv_l, (1, 0, 2))),
        lead(s_c), lead(s_n.reshape(n_samp, L_HEADS, L_HDIM)), lead(mnew[:, F_LANE:F_LANE + L_HEADS]),
    )
```
